```python
import jax, jax.numpy as jnp
from jax import lax
import numpy as np

D_MODEL = 1024
BATCH = 8
SEQ = 4096
DEPTH = 1

CTX_LEN = 256
GRID_W = 64
MIX_WIDTH = D_MODEL
ATTN_WIDTH = MIX_WIDTH // 2
LRU_WIDTH = MIX_WIDTH - ATTN_WIDTH
HEAD_DIM = 64
N_ATTN_HEADS = ATTN_WIDTH // HEAD_DIM
LRU_BLOCK = 64
N_LRU_BLOCKS = LRU_WIDTH // LRU_BLOCK
IN_WIDTH = 4 * ATTN_WIDTH + 2 * LRU_WIDTH
NA_ROWS_MAX = 8
NA_COLS = 16
CONV_WIDTH = 4
LRU_C = 8.0
ROPE_BASE = 10000.0
EPS = 1e-6
NEG_INF = -1e30

kernel_name = "hymba_na_rglru_prefix_block"


def rms_norm(x, g):
    x32 = x.astype(jnp.float32)
    y = x32 * lax.rsqrt(jnp.mean(x32 * x32, axis=-1, keepdims=True) + EPS)
    return (y * g.astype(jnp.float32)).astype(x.dtype)


def split_heads(t):
    b, l, _ = t.shape
    return t.reshape(b, l, -1, HEAD_DIM).transpose(0, 2, 1, 3)


def merge_heads(t):
    b, h, l, d = t.shape
    return t.transpose(0, 2, 1, 3).reshape(b, l, h * d)


def _rotate(x, pos):
    half = x.shape[-1] // 2
    inv = ROPE_BASE ** (-jnp.arange(half, dtype=jnp.float32) / half)
    ang = pos.astype(jnp.float32)[:, None] * inv[None, :]
    cos, sin = jnp.cos(ang), jnp.sin(ang)
    x1, x2 = x[..., :half], x[..., half:]
    return jnp.concatenate([x1 * cos - x2 * sin, x2 * cos + x1 * sin], axis=-1).astype(x.dtype)


def axial_rope(x, pos_r, pos_c):
    d = x.shape[-1] // 2
    return jnp.concatenate([_rotate(x[..., :d], pos_r), _rotate(x[..., d:], pos_c)], axis=-1)


def neighbourhood_attention(q_rot, q_plain, k_rot, v, k_ctx, v_ctx, rpb):
    b, h, s, hd = q_rot.shape
    rows = s // GRID_W
    kh = min(NA_ROWS_MAX, rows)
    scale = hd ** -0.5
    grid = lambda t: t.reshape(b, h, rows, GRID_W, hd)
    qr, qp, kr, vg = grid(q_rot), grid(q_plain), grid(k_rot), grid(v)
    cols = jnp.arange(GRID_W)
    col_start = jnp.clip(cols - NA_COLS // 2, 0, GRID_W - NA_COLS)
    col_mask = (cols[None, :] >= col_start[:, None]) & (cols[None, :] < col_start[:, None] + NA_COLS)
    dc_idx = jnp.clip(cols[None, :] - cols[:, None], -(NA_COLS - 1), NA_COLS - 1) + NA_COLS - 1

    def row_step(r):
        r0 = jnp.clip(r - kh // 2, 0, rows - kh)
        q_r = lax.dynamic_index_in_dim(qr, r, axis=2, keepdims=False)
        qp_r = lax.dynamic_index_in_dim(qp, r, axis=2, keepdims=False)
        k_band = lax.dynamic_slice_in_dim(kr, r0, kh, axis=2)
        v_band = lax.dynamic_slice_in_dim(vg, r0, kh, axis=2)
        s_loc = jnp.einsum('bhqd,bhjkd->bhqjk', q_r, k_band,
                           preferred_element_type=jnp.float32) * scale
        dr_idx = r0 + jnp.arange(kh) - r + NA_ROWS_MAX - 1
        bias = rpb[:, dr_idx[None, :, None], dc_idx[:, None, :]].astype(jnp.float32)
        s_loc = jnp.where(col_mask[:, None, :], s_loc + bias, NEG_INF)
        s_ctx = jnp.einsum('bhqd,bhcd->bhqc', qp_r, k_ctx,
                           preferred_element_type=jnp.float32) * scale
        n_loc = kh * GRID_W
        sc = jnp.concatenate([s_loc.reshape(b, h, GRID_W, n_loc), s_ctx], axis=-1)
        p = jax.nn.softmax(sc, axis=-1).astype(v.dtype)
        p_loc = p[..., :n_loc].reshape(b, h, GRID_W, kh, GRID_W)
        return (jnp.einsum('bhqjk,bhjkd->bhqd', p_loc, v_band)
                + jnp.einsum('bhqc,bhcd->bhqd', p[..., n_loc:], v_ctx))

    out = lax.map(row_step, jnp.arange(rows))
    return out.transpose(1, 2, 0, 3, 4).reshape(b, h, s, hd)


def context_attention(q, k, v):
    sc = jnp.einsum('bhqd,bhkd->bhqk', q, k, preferred_element_type=jnp.float32) * (q.shape[-1] ** -0.5)
    p = jax.nn.softmax(sc, axis=-1).astype(v.dtype)
    return jnp.einsum('bhqk,bhkd->bhqd', p, v)


def depthwise_conv(u, w, bias):
    k = w.shape[0]
    lo = (k - 1) // 2
    y = lax.conv_general_dilated(u, w[:, None, :].astype(u.dtype), window_strides=(1,),
                                 padding=[(lo, k - 1 - lo)],
                                 dimension_numbers=('NWC', 'WIO', 'NWC'),
                                 feature_group_count=u.shape[-1])
    return y + bias


def rglru_coeffs(u, w_a, b_a, w_x, b_x, lam):
    bsz, l, ch = u.shape
    u32 = u.astype(jnp.float32)
    ub = u32.reshape(bsz, l, N_LRU_BLOCKS, LRU_BLOCK)
    r = jax.nn.sigmoid(jnp.einsum('blnc,ncd->blnd', ub, w_a.astype(jnp.float32)).reshape(bsz, l, ch) + b_a)
    i = jax.nn.sigmoid(jnp.einsum('blnc,ncd->blnd', ub, w_x.astype(jnp.float32)).reshape(bsz, l, ch) + b_x)
    log_a = -LRU_C * r * jax.nn.softplus(-lam.astype(jnp.float32))
    a = jnp.exp(log_a)
    mult = jnp.sqrt(-jnp.expm1(2.0 * log_a))
    return a, mult * (i * u32)


def linear_scan(a, b, h0, reverse):
    def comb(lhs, rhs):
        return lhs[0] * rhs[0], rhs[0] * lhs[1] + rhs[1]
    a_cum, h = lax.associative_scan(comb, (a, b), axis=1, reverse=reverse)
    return h if h0 is None else h + a_cum * h0[:, None, :]


def rglru_bidirectional(u, u_c, w_a, b_a, w_x, b_x, lam):
    ys, hcs = [], []
    for d, reverse in ((0, False), (1, True)):
        a_c, b_c = rglru_coeffs(u_c, w_a[d], b_a[d], w_x[d], b_x[d], lam[d])
        h_c = linear_scan(a_c, b_c, None, reverse)
        h_end = h_c[:, 0] if reverse else h_c[:, -1]
        a, b = rglru_coeffs(u, w_a[d], b_a[d], w_x[d], b_x[d], lam[d])
        ys.append(linear_scan(a, b, h_end, reverse))
        hcs.append(h_c)
    return (ys[0] + ys[1]).astype(u.dtype), hcs


def setup_inputs(seed: int = 0) -> dict:
    key = jax.random.key(seed)
    ks = jax.random.split(key, 24)
    nrm = lambda k, shape, s: jax.random.normal(k, shape, jnp.float32) * s
    u = jax.random.uniform(ks[18], (DEPTH, 2, LRU_WIDTH), jnp.float32, 0.9, 0.999)
    a0 = u ** (1.0 / LRU_C)
    return {
        "x": nrm(ks[0], (BATCH, SEQ, D_MODEL), 1.0),
        "c": nrm(ks[1], (BATCH, D_MODEL), 1.0),
        "ctx": nrm(ks[2], (BATCH, CTX_LEN, D_MODEL), 1.0),
        "c_ctx": nrm(ks[3], (D_MODEL,), 1.0),
        "norm_g": 1.0 + nrm(ks[4], (DEPTH, D_MODEL), 0.02),
        "w_mod": nrm(ks[5], (DEPTH, D_MODEL, 3 * D_MODEL), D_MODEL ** -0.5),
        "b_mod": nrm(ks[6], (DEPTH, 3 * D_MODEL), 0.02),
        "w_in": nrm(ks[7], (DEPTH, D_MODEL, IN_WIDTH), D_MODEL ** -0.5),
        "w_out": nrm(ks[8], (DEPTH, MIX_WIDTH, D_MODEL), MIX_WIDTH ** -0.5),
        "q_norm_g": 1.0 + nrm(ks[9], (DEPTH, HEAD_DIM), 0.02),
        "k_norm_g": 1.0 + nrm(ks[10], (DEPTH, HEAD_DIM), 0.02),
        "rpb": nrm(ks[11], (DEPTH, N_ATTN_HEADS, 2 * NA_ROWS_MAX - 1, 2 * NA_COLS - 1), 0.5),
        "conv_w": nrm(ks[12], (DEPTH, CONV_WIDTH, LRU_WIDTH), CONV_WIDTH ** -0.5),
        "conv_b": nrm(ks[13], (DEPTH, LRU_WIDTH), 0.02),
        "lru_wa": nrm(ks[14], (DEPTH, 2, N_LRU_BLOCKS, LRU_BLOCK, LRU_BLOCK), LRU_BLOCK ** -0.5),
        "lru_ba": nrm(ks[15], (DEPTH, 2, LRU_WIDTH), 0.1),
        "lru_wx": nrm(ks[16], (DEPTH, 2, N_LRU_BLOCKS, LRU_BLOCK, LRU_BLOCK), LRU_BLOCK ** -0.5),
        "lru_bx": nrm(ks[17], (DEPTH, 2, LRU_WIDTH), 0.1),
        "lru_lam": jnp.log(a0) - jnp.log1p(-a0),
    }


def reference(x, c, ctx, c_ctx, norm_g, w_mod, b_mod, w_in, w_out, q_norm_g, k_norm_g, rpb,
              conv_w, conv_b, lru_wa, lru_ba, lru_wx, lru_bx, lru_lam):
    bsz, s, _ = x.shape
    t = jnp.arange(s)
    pos_r, pos_c = t // GRID_W, t % GRID_W
    aw, lw = ATTN_WIDTH, LRU_WIDTH
    silu_c = jax.nn.silu(c)
    silu_cc = jax.nn.silu(c_ctx)
    for layer in range(DEPTH):
        shift, scale, gate = jnp.split(silu_c @ w_mod[layer] + b_mod[layer], 3, axis=-1)
        shift_c, scale_c, gate_c = jnp.split(silu_cc @ w_mod[layer] + b_mod[layer], 3, axis=-1)
        h = rms_norm(x, norm_g[layer]) * (1.0 + scale[:, None, :]) + shift[:, None, :]
        hc = rms_norm(ctx, norm_g[layer]) * (1.0 + scale_c) + shift_c
        w = w_in[layer]
        q, k, v, z_a, u, z_l = jnp.split(h @ w, [aw, 2 * aw, 3 * aw, 4 * aw, 4 * aw + lw], axis=-1)
        k_c, v_c = jnp.split(hc @ w[:, aw:3 * aw], 2, axis=-1)
        u_c = hc @ w[:, 4 * aw:4 * aw + lw]

        qn = rms_norm(split_heads(q), q_norm_g[layer])
        kn = rms_norm(split_heads(k), k_norm_g[layer])
        kcn = rms_norm(split_heads(k_c), k_norm_g[layer])
        vch = split_heads(v_c)
        attn = neighbourhood_attention(axial_rope(qn, pos_r, pos_c), qn, axial_rope(kn, pos_r, pos_c),
                                       split_heads(v), kcn, vch, rpb[layer])
        attn = merge_heads(attn)

        uc = depthwise_conv(u, conv_w[layer], conv_b[layer])
        ucc = depthwise_conv(u_c, conv_w[layer], conv_b[layer])
        y, hc_states = rglru_bidirectional(uc, ucc, lru_wa[layer], lru_ba[layer], lru_wx[layer],
                                           lru_bx[layer], lru_lam[layer])

        mix = jnp.concatenate([attn * jax.nn.silu(z_a), y * jax.nn.silu(z_l)], axis=-1) @ w_out[layer]

        if layer + 1 < DEPTH:
            q_c = hc @ w[:, :aw]
            z_a_c = hc @ w[:, 3 * aw:4 * aw]
            z_l_c = hc @ w[:, 4 * aw + lw:]
            attn_c = merge_heads(context_attention(rms_norm(split_heads(q_c), q_norm_g[layer]), kcn, vch))
            y_c = (hc_states[0] + hc_states[1]).astype(ctx.dtype)
            mix_c = jnp.concatenate([attn_c * jax.nn.silu(z_a_c), y_c * jax.nn.silu(z_l_c)], axis=-1) @ w_out[layer]
            ctx = ctx + gate_c * mix_c

        x = x + gate[:, None, :] * mix
    return x
```

```python
import functools
import math

import jax
import jax.numpy as jnp
from jax import lax
from jax.experimental import pallas as pl
from jax.experimental.pallas import tpu as pltpu

F32 = jnp.float32
BF16 = jnp.bfloat16

D_MODEL = 1024
GRID_W = 64
HEAD_DIM = 64
ATTN_WIDTH = 512
LRU_WIDTH = 512
N_HEADS = ATTN_WIDTH // HEAD_DIM
LRU_BLOCK = 64
NA_ROWS = 8
NA_COLS = 16
CONV_WIDTH = 4
LRU_C = 8.0
ROPE_BASE = 10000.0
EPS = 1e-6
NEG_INF = -1e30
LOG2E = math.log2(math.e)

LANES = 128
SUBLANES = 8
MXU_DIM = 256
VMEM_LIMIT_BYTES = 58 * 1024 * 1024

Q_ROWS = 4
BAND_ROWS = Q_ROWS + NA_ROWS - 1
Q_BLK = Q_ROWS * GRID_W
BAND = BAND_ROWS * GRID_W
N_BIAS_ROWS = 2 * NA_ROWS - 1
N_BIAS_COLS = 2 * NA_COLS - 1
TOE_W = 1024

LRU_CHUNK = 256


def _dot(a, b):
    return jnp.dot(a, b, preferred_element_type=F32)


def _dot_nt(a, b):
    return lax.dot_general(a, b, (((1,), (1,)), ((), ())), preferred_element_type=F32)


def _params(*semantics):
    return pltpu.CompilerParams(dimension_semantics=semantics, vmem_limit_bytes=VMEM_LIMIT_BYTES)


def _split_bf16(t):
    hi = t.astype(BF16)
    lo = (t - hi.astype(F32)).astype(BF16)
    return hi, lo


def _mod_kernel(cc_ref, w_ref, b_ref, o_ref):
    cc = cc_ref[...]
    s = cc * jax.nn.sigmoid(cc)
    s_hi, s_lo = _split_bf16(s)
    w_hi, w_lo = _split_bf16(w_ref[...])
    o_ref[...] = _dot(s_hi, w_hi) + _dot(s_lo, w_hi) + _dot(s_hi, w_lo) + b_ref[...]


def _modulation(cc, w_mod, b_mod):
    rows, d = cc.shape
    n = w_mod.shape[1]
    tn = 512
    return pl.pallas_call(
        _mod_kernel,
        grid=(n // tn,),
        in_specs=[pl.BlockSpec((rows, d), lambda j: (0, 0)),
                  pl.BlockSpec((d, tn), lambda j: (0, j)),
                  pl.BlockSpec((1, tn), lambda j: (0, j))],
        out_specs=pl.BlockSpec((rows, tn), lambda j: (0, j)),
        out_shape=jax.ShapeDtypeStruct((rows, n), F32),
        compiler_params=_params("arbitrary"),
        name="modulation",
    )(cc, w_mod, b_mod)


def _bias_kernel(rexp_ref, o_ref):
    q = lax.broadcasted_iota(jnp.int32, (GRID_W, TOE_W), 0)
    lane = lax.broadcasted_iota(jnp.int32, (GRID_W, TOE_W), 1)
    k = lane % GRID_W
    diff = k - q + (NA_COLS - 1)
    toe = jnp.zeros((GRID_W, TOE_W), F32)
    for d in range(N_BIAS_COLS):
        toe = jnp.where(diff == d, rexp_ref[0, d:d + 1, :], toe)
    col_start = jnp.clip(q - NA_COLS // 2, 0, GRID_W - NA_COLS)
    col_ok = (k >= col_start) & (k < col_start + NA_COLS)
    toe = jnp.where(col_ok, toe, NEG_INF)

    for typ in range(3):
        for ri in range(Q_ROWS):
            if typ == 0:
                ja, dr0 = 0, NA_ROWS - 1 - ri
            elif typ == 1:
                ja, dr0 = ri, NA_ROWS // 2 - 1
            else:
                ja, dr0 = BAND_ROWS - NA_ROWS, BAND_ROWS - NA_ROWS - ri
            shift = ((ja - dr0) * GRID_W) % TOE_W
            rolled = pltpu.roll(toe, shift, axis=1) if shift else toe
            ok = (lane >= ja * GRID_W) & (lane < (ja + NA_ROWS) * GRID_W)
            strip = jnp.where(ok, rolled, NEG_INF)
            o_ref[0, typ, ri * GRID_W:(ri + 1) * GRID_W, :] = strip[:, :BAND]


def _bias_tables(rpb):
    rexp = jnp.repeat(jnp.transpose(rpb, (0, 2, 1)) * LOG2E, GRID_W, axis=2)
    rexp = jnp.pad(rexp, ((0, 0), (0, 0), (0, TOE_W - N_BIAS_ROWS * GRID_W)))
    return pl.pallas_call(
        _bias_kernel,
        grid=(N_HEADS,),
        in_specs=[pl.BlockSpec((1, N_BIAS_COLS, TOE_W), lambda h: (h, 0, 0))],
        out_specs=pl.BlockSpec((1, 3, Q_BLK, BAND), lambda h: (h, 0, 0, 0)),
        out_shape=jax.ShapeDtypeStruct((N_HEADS, 3, Q_BLK, BAND), F32),
        compiler_params=_params("arbitrary"),
        name="bias_tables",
    )(rexp)


def _adaln(x, g_ref, scale_ref, shift_ref):
    ms = jnp.mean(x * x, axis=-1, keepdims=True)
    gm = g_ref[...] * (1.0 + scale_ref[0])
    return ((x * lax.rsqrt(ms + EPS)) * gm + shift_ref[0]).astype(BF16)


def _head_rms(t, bd_ref, gain):
    sq = (t * t).astype(BF16)
    bd = bd_ref[...]
    m = jnp.concatenate([_dot(sq[:, :MXU_DIM], bd), _dot(sq[:, MXU_DIM:], bd)], axis=-1)
    return t * lax.rsqrt(m + EPS) * gain


def _rope(t, cos_ref, sin_ref):
    lane = lax.broadcasted_iota(jnp.int32, (1, LANES), 1)
    first = (lane % 32) < 16
    parts = []
    for j in range(t.shape[1] // LANES):
        c = t[:, j * LANES:(j + 1) * LANES]
        parts.append(jnp.where(first, pltpu.roll(c, LANES - 16, axis=1), pltpu.roll(c, 16, axis=1)))
    swapped = jnp.concatenate(parts, axis=-1)
    return t * cos_ref[...] + swapped * sin_ref[...]


def _silu(t):
    return t * jax.nn.sigmoid(t)


def _inproj_kernel(x_ref, g_ref, scale_ref, shift_ref, w_ref, cos_ref, sin_ref, bd_ref, qg_ref, kg_ref,
                   qr_ref, qp_ref, kr_ref, v_ref, za_ref, u_ref, zl_ref):
    hb = _adaln(x_ref[0], g_ref, scale_ref, shift_ref)
    aw = ATTN_WIDTH

    def proj(i):
        return _dot(hb, w_ref[:, i * aw:(i + 1) * aw])

    qn = _head_rms(proj(0), bd_ref, qg_ref[...])
    qp_ref[0] = qn.astype(BF16)
    qr_ref[0] = _rope(qn, cos_ref, sin_ref).astype(BF16)
    kn = _head_rms(proj(1), bd_ref, kg_ref[...])
    kr_ref[0] = _rope(kn, cos_ref, sin_ref).astype(BF16)
    v_ref[0] = proj(2).astype(BF16)
    za_ref[0] = _silu(proj(3)).astype(BF16)
    u_ref[0] = proj(4)
    zl_ref[0] = _silu(proj(5)).astype(BF16)


def _inproj(x, g, scale, shift, w_bf, cos_t, sin_t, bd, qg, kg, tm=512):
    b, s, d = x.shape
    n = w_bf.shape[1]
    aw = ATTN_WIDTH
    tok = lambda i, j: (j, i, 0)
    per_b = lambda i, j: (j, 0, 0)
    const = lambda i, j: (0, 0)
    out_bf = jax.ShapeDtypeStruct((b, s, aw), BF16)
    out_f32 = jax.ShapeDtypeStruct((b, s, aw), F32)
    blk = pl.BlockSpec((1, tm, aw), tok)
    return pl.pallas_call(
        _inproj_kernel,
        grid=(s // tm, b),
        in_specs=[pl.BlockSpec((1, tm, d), tok),
                  pl.BlockSpec((1, d), const),
                  pl.BlockSpec((1, 1, d), per_b),
                  pl.BlockSpec((1, 1, d), per_b),
                  pl.BlockSpec((d, n), const),
                  pl.BlockSpec((tm, aw), lambda i, j: (i, 0)),
                  pl.BlockSpec((tm, aw), lambda i, j: (i, 0)),
                  pl.BlockSpec((MXU_DIM, MXU_DIM), const),
                  pl.BlockSpec((1, aw), const),
                  pl.BlockSpec((1, aw), const)],
        out_specs=[blk, blk, blk, blk, blk, blk, blk],
        out_shape=[out_bf, out_bf, out_bf, out_bf, out_bf, out_f32, out_bf],
        compiler_params=_params("arbitrary", "arbitrary"),
        name="inproj",
    )(x, g, scale, shift, w_bf, cos_t, sin_t, bd, qg, kg)


def _ctxproj_kernel(c_ref, g_ref, scale_ref, shift_ref, w_ref, bd_ref, kg_ref, kc_ref, vc_ref, uc_ref):
    hb = _adaln(c_ref[0], g_ref, scale_ref, shift_ref)
    aw = ATTN_WIDTH
    kc_ref[0] = _head_rms(_dot(hb, w_ref[:, :aw]), bd_ref, kg_ref[...]).astype(BF16)
    vc_ref[0] = _dot(hb, w_ref[:, aw:2 * aw]).astype(BF16)
    uc_ref[0] = _dot(hb, w_ref[:, 2 * aw:])


def _ctxproj(ctx, g, scale_c, shift_c, w_ctx_bf, bd, kg):
    b, l, d = ctx.shape
    aw = ATTN_WIDTH
    const = lambda i: (0, 0)
    blk = pl.BlockSpec((1, l, aw), lambda i: (i, 0, 0))
    return pl.pallas_call(
        _ctxproj_kernel,
        grid=(b,),
        in_specs=[pl.BlockSpec((1, l, d), lambda i: (i, 0, 0)),
                  pl.BlockSpec((1, d), const),
                  pl.BlockSpec((1, 1, d), lambda i: (0, 0, 0)),
                  pl.BlockSpec((1, 1, d), lambda i: (0, 0, 0)),
                  pl.BlockSpec((d, 3 * aw), const),
                  pl.BlockSpec((MXU_DIM, MXU_DIM), const),
                  pl.BlockSpec((1, aw), const)],
        out_specs=[blk, blk, blk],
        out_shape=[jax.ShapeDtypeStruct((b, l, aw), BF16),
                   jax.ShapeDtypeStruct((b, l, aw), BF16),
                   jax.ShapeDtypeStruct((b, l, aw), F32)],
        compiler_params=_params("arbitrary"),
        name="ctxproj",
    )(ctx, g, scale_c, shift_c, w_ctx_bf, bd, kg)


def _attn_kernel(qr_ref, qp_ref, kr_ref, v_ref, kc_ref, vc_ref, za_ref, bias_ref, o_ref):
    rows = qr_ref.shape[1] // GRID_W
    n_blk = rows // Q_ROWS
    lane = lax.broadcasted_iota(jnp.int32, (1, LANES), 1)
    head_lanes = [lane < HEAD_DIM, lane >= HEAD_DIM]
    kc = kc_ref[0]
    vc = vc_ref[0]

    def block(rb, typ):
        if isinstance(rb, int):
            q0 = rb * Q_BLK
            band0 = min(max(rb * Q_ROWS - NA_ROWS // 2, 0), rows - BAND_ROWS) * GRID_W
        else:
            q0 = pl.multiple_of(rb * Q_BLK, Q_BLK)
            band0 = pl.multiple_of(jnp.clip(rb * Q_ROWS - NA_ROWS // 2, 0, rows - BAND_ROWS) * GRID_W, GRID_W)
        qr = qr_ref[0, pl.ds(q0, Q_BLK), :]
        qp = qp_ref[0, pl.ds(q0, Q_BLK), :]
        kb = kr_ref[0, pl.ds(band0, BAND), :]
        vb = v_ref[0, pl.ds(band0, BAND), :]
        zero = jnp.zeros_like(qr)
        outs = []
        for h in range(2):
            s_loc = _dot_nt(jnp.where(head_lanes[h], qr, zero), kb) + bias_ref[h, typ]
            s_ctx = _dot_nt(jnp.where(head_lanes[h], qp, zero), kc)
            m = jnp.maximum(jnp.max(s_loc, axis=-1, keepdims=True), jnp.max(s_ctx, axis=-1, keepdims=True))
            p_loc = jnp.exp2(s_loc - m)
            p_ctx = jnp.exp2(s_ctx - m)
            denom = jnp.sum(p_loc, axis=-1, keepdims=True) + jnp.sum(p_ctx, axis=-1, keepdims=True)
            acc = _dot(p_loc.astype(BF16), vb) + _dot(p_ctx.astype(BF16), vc)
            outs.append(acc / denom)
        o = jnp.where(head_lanes[0], outs[0], outs[1])
        gated = o * za_ref[0, pl.ds(q0, Q_BLK), :].astype(F32)
        o_ref[0, pl.ds(q0, Q_BLK), :] = gated.astype(o_ref.dtype)

    block(0, 0)

    def body(rb, carry):
        block(rb, 1)
        return carry

    lax.fori_loop(1, n_blk - 1, body, 0)
    block(n_blk - 1, 2)


def _attention(qr, qp, kr, v, kc, vc, za, bias):
    b, s, aw = qr.shape
    l = kc.shape[1]
    n_pairs = aw // LANES
    lat = pl.BlockSpec((1, s, LANES), lambda p, i: (i, 0, p))
    cblk = pl.BlockSpec((1, l, LANES), lambda p, i: (i, 0, p))
    return pl.pallas_call(
        _attn_kernel,
        grid=(n_pairs, b),
        in_specs=[lat, lat, lat, lat, cblk, cblk, lat,
                  pl.BlockSpec((2, 3, Q_BLK, BAND), lambda p, i: (p, 0, 0, 0))],
        out_specs=lat,
        out_shape=jax.ShapeDtypeStruct((b, s, aw), BF16),
        compiler_params=_params("arbitrary", "arbitrary"),
        name="attention",
    )(qr, qp, kr, v, kc, vc, za, bias)


def _conv(prev8, mid, next8, w_ref, b_ref):
    n = mid.shape[0]
    ext = jnp.concatenate([prev8, mid, next8], axis=0)
    y = b_ref[...] + w_ref[0:1, :] * ext[SUBLANES - 1:SUBLANES - 1 + n]
    y = y + w_ref[1:2, :] * mid
    y = y + w_ref[2:3, :] * ext[SUBLANES + 1:SUBLANES + 1 + n]
    return y + w_ref[3:4, :] * ext[SUBLANES + 2:SUBLANES + 2 + n]


def _lru_kernel(u_ref, uc_ref, zl_ref, cw_ref, cb_ref, wg_ref, ba_ref, bx_ref, lam_ref, o_ref,
                a_s, b_s, h_s, yf_s):
    s = u_ref.shape[1]
    tc = LRU_CHUNK
    n_chunks = s // tc
    row = lax.broadcasted_iota(jnp.int32, (SUBLANES, LRU_WIDTH), 0)
    zeros8 = jnp.zeros((SUBLANES, LRU_WIDTH), F32)

    def coeffs(uc, d):
        ucb = uc.astype(BF16)

        def gate(gi, bias):
            pre = jnp.concatenate([_dot(ucb[:, :MXU_DIM], wg_ref[d, gi, 0]),
                                   _dot(ucb[:, MXU_DIM:], wg_ref[d, gi, 1])], axis=-1)
            return jax.nn.sigmoid(pre + bias)

        r = gate(0, ba_ref[d])
        i = gate(1, bx_ref[d])
        a = jnp.exp2(r * (jax.nn.softplus(-lam_ref[d]) * (-LRU_C * LOG2E)))
        return a, jnp.sqrt(1.0 - a * a) * (i * uc)

    def scan_chunk(uc, d, h):
        a, bb = coeffs(uc, d)
        a_s[...] = a
        b_s[...] = bb
        n_groups = uc.shape[0] // SUBLANES

        def body(gi, h):
            g = gi if d == 0 else n_groups - 1 - gi
            r0 = pl.multiple_of(g * SUBLANES, SUBLANES)
            av = a_s[pl.ds(r0, SUBLANES), :]
            bv = b_s[pl.ds(r0, SUBLANES), :]
            for k in (1, 2, 4):
                if d == 0:
                    keep = row >= k
                    sh = k
                else:
                    keep = row < SUBLANES - k
                    sh = SUBLANES - k
                a_sh = jnp.where(keep, pltpu.roll(av, sh, axis=0), 1.0)
                b_sh = jnp.where(keep, pltpu.roll(bv, sh, axis=0), 0.0)
                bv = bv + av * b_sh
                av = av * a_sh
            hg = bv + av * h
            h_s[pl.ds(r0, SUBLANES), :] = hg
            return hg[SUBLANES - 1:SUBLANES, :] if d == 0 else hg[0:1, :]

        return lax.fori_loop(0, n_groups, body, h)

    def latent_conv(c):
        t0 = pl.multiple_of(c * tc, tc)
        mid = u_ref[0, pl.ds(t0, tc), :]
        p0 = pl.multiple_of(jnp.maximum(t0 - SUBLANES, 0), SUBLANES)
        n0 = pl.multiple_of(jnp.minimum(t0 + tc, s - SUBLANES), SUBLANES)
        prev8 = jnp.where(c > 0, u_ref[0, pl.ds(p0, SUBLANES), :], zeros8)
        next8 = jnp.where(c < n_chunks - 1, u_ref[0, pl.ds(n0, SUBLANES), :], zeros8)
        return t0, _conv(prev8, mid, next8, cw_ref, cb_ref)

    h_zero = jnp.zeros((1, LRU_WIDTH), F32)
    ucc = _conv(zeros8, uc_ref[0], zeros8, cw_ref, cb_ref)
    h_fwd0 = scan_chunk(ucc, 0, h_zero)
    h_bwd0 = scan_chunk(ucc, 1, h_zero)

    def fwd_body(c, h):
        t0, uc = latent_conv(c)
        h = scan_chunk(uc, 0, h)
        yf_s[pl.ds(t0, tc), :] = h_s[...]
        return h

    lax.fori_loop(0, n_chunks, fwd_body, h_fwd0)

    def bwd_body(ci, h):
        c = n_chunks - 1 - ci
        t0, uc = latent_conv(c)
        h = scan_chunk(uc, 1, h)
        y = yf_s[pl.ds(t0, tc), :] + h_s[...]
        o_ref[0, pl.ds(t0, tc), :] = (y * zl_ref[0, pl.ds(t0, tc), :].astype(F32)).astype(o_ref.dtype)
        return h

    lax.fori_loop(0, n_chunks, bwd_body, h_bwd0)


def _rglru(u, u_c, zl, conv_w, conv_b, wg, ba, bx, lam):
    b, s, lw = u.shape
    l = u_c.shape[1]
    assert l == LRU_CHUNK
    full = lambda *shape: pl.BlockSpec(shape, lambda i: (0,) * len(shape))
    return pl.pallas_call(
        _lru_kernel,
        grid=(b,),
        in_specs=[pl.BlockSpec((1, s, lw), lambda i: (i, 0, 0)),
                  pl.BlockSpec((1, l, lw), lambda i: (i, 0, 0)),
                  pl.BlockSpec((1, s, lw), lambda i: (i, 0, 0)),
                  full(CONV_WIDTH, lw), full(1, lw),
                  full(2, 2, 2, MXU_DIM, MXU_DIM),
                  full(2, 1, lw), full(2, 1, lw), full(2, 1, lw)],
        out_specs=pl.BlockSpec((1, s, lw), lambda i: (i, 0, 0)),
        out_shape=jax.ShapeDtypeStruct((b, s, lw), BF16),
        scratch_shapes=[pltpu.VMEM((LRU_CHUNK, lw), F32),
                        pltpu.VMEM((LRU_CHUNK, lw), F32),
                        pltpu.VMEM((LRU_CHUNK, lw), F32),
                        pltpu.VMEM((s, lw), F32)],
        compiler_params=_params("arbitrary"),
        name="rglru",
    )(u, u_c, zl, conv_w, conv_b, wg, ba, bx, lam)


def _outproj_kernel(x_ref, a_ref, y_ref, w_ref, gate_ref, o_ref):
    mix = _dot(a_ref[0], w_ref[:ATTN_WIDTH, :]) + _dot(y_ref[0], w_ref[ATTN_WIDTH:, :])
    o_ref[0] = x_ref[0] + gate_ref[0] * mix


def _outproj(x, ag, yg, w_bf, gate, tm=1024):
    b, s, d = x.shape
    tok = lambda i, j: (j, i, 0)
    return pl.pallas_call(
        _outproj_kernel,
        grid=(s // tm, b),
        in_specs=[pl.BlockSpec((1, tm, d), tok),
                  pl.BlockSpec((1, tm, ATTN_WIDTH), tok),
                  pl.BlockSpec((1, tm, LRU_WIDTH), tok),
                  pl.BlockSpec((ATTN_WIDTH + LRU_WIDTH, d), lambda i, j: (0, 0)),
                  pl.BlockSpec((1, 1, d), lambda i, j: (j, 0, 0))],
        out_specs=pl.BlockSpec((1, tm, d), tok),
        out_shape=jax.ShapeDtypeStruct((b, s, d), x.dtype),
        compiler_params=_params("arbitrary", "arbitrary"),
        name="outproj",
    )(x, ag, yg, w_bf, gate)


def _rope_tables(s):
    t = jnp.arange(s)
    half = HEAD_DIM // 4
    inv = ROPE_BASE ** (-jnp.arange(half, dtype=F32) / half)
    ang_r = (t // GRID_W).astype(F32)[:, None] * inv[None, :]
    ang_c = (t % GRID_W).astype(F32)[:, None] * inv[None, :]
    cos = jnp.concatenate([jnp.cos(ang_r)] * 2 + [jnp.cos(ang_c)] * 2, axis=-1)
    sin = jnp.concatenate([-jnp.sin(ang_r), jnp.sin(ang_r), -jnp.sin(ang_c), jnp.sin(ang_c)], axis=-1)
    return jnp.tile(cos, (1, N_HEADS)), jnp.tile(sin, (1, N_HEADS))


def _block_diag_gates(w):
    n = MXU_DIM // LRU_BLOCK
    halves = []
    for hf in range(2):
        m = jnp.zeros((MXU_DIM, MXU_DIM), w.dtype)
        for j in range(n):
            m = lax.dynamic_update_slice(m, w[hf * n + j], (j * LRU_BLOCK, j * LRU_BLOCK))
        halves.append(m)
    return jnp.stack(halves)


def kernel(x, c, ctx, c_ctx, norm_g, w_mod, b_mod, w_in, w_out, q_norm_g, k_norm_g, rpb, conv_w, conv_b,
           lru_wa, lru_ba, lru_wx, lru_bx, lru_lam):
    bsz, s, d = x.shape
    assert w_in.shape[0] == 1 and d == D_MODEL and s % (GRID_W * Q_ROWS) == 0
    aw, lw = ATTN_WIDTH, LRU_WIDTH

    pad_rows = 2 * SUBLANES - bsz - 1
    cc = jnp.concatenate([c, c_ctx[None, :], jnp.zeros((pad_rows, d), F32)], axis=0)
    mod = _modulation(cc, w_mod[0], b_mod[0][None, :])
    shift, scale, gate = [mod[:bsz, i * d:(i + 1) * d][:, None, :] for i in range(3)]
    shift_c, scale_c = [mod[bsz:bsz + 1, i * d:(i + 1) * d][:, None, :] for i in range(2)]

    w = w_in[0]
    w_bf = w.astype(BF16)
    w_ctx_bf = jnp.concatenate([w[:, aw:3 * aw], w[:, 4 * aw:4 * aw + lw]], axis=1).astype(BF16)
    g = norm_g[0][None, :]
    cos_t, sin_t = _rope_tables(s)
    blk = jnp.arange(MXU_DIM) // HEAD_DIM
    bd = jnp.where(blk[:, None] == blk[None, :], 1.0 / HEAD_DIM, 0.0).astype(BF16)
    qg = jnp.tile(q_norm_g[0] * (HEAD_DIM ** -0.5 * LOG2E), N_HEADS)[None, :]
    kg = jnp.tile(k_norm_g[0], N_HEADS)[None, :]

    qr, qp, kr, v, za, u, zl = _inproj(x, g, scale, shift, w_bf, cos_t, sin_t, bd, qg, kg)
    kc, vc, u_c = _ctxproj(ctx, g, scale_c, shift_c, w_ctx_bf, bd, kg)

    bias = _bias_tables(rpb[0])
    ag = _attention(qr, qp, kr, v, kc, vc, za, bias)

    wg = jnp.stack([jnp.stack([_block_diag_gates(lru_wa[0, dd]), _block_diag_gates(lru_wx[0, dd])])
                    for dd in range(2)]).astype(BF16)
    yg = _rglru(u, u_c, zl, conv_w[0], conv_b[0][None, :], wg,
                lru_ba[0][:, None, :], lru_bx[0][:, None, :], lru_lam[0][:, None, :])

    return _outproj(x, ag, yg, w_out[0].astype(BF16), gate)
```

```python
import functools
import math

import jax
import jax.numpy as jnp
from jax import lax
from jax.experimental import pallas as pl
from jax.experimental.pallas import tpu as pltpu

F32 = jnp.float32
BF16 = jnp.bfloat16

D_MODEL = 1024
GRID_W = 64
HEAD_DIM = 64
ATTN_WIDTH = 512
LRU_WIDTH = 512
N_HEADS = ATTN_WIDTH // HEAD_DIM
LRU_BLOCK = 64
NA_ROWS = 8
NA_COLS = 16
CONV_WIDTH = 4
LRU_C = 8.0
ROPE_BASE = 10000.0
EPS = 1e-6
NEG_INF = -1e30
LOG2E = math.log2(math.e)

LANES = 128
SUBLANES = 8
MXU_DIM = 256
VMEM_LIMIT_BYTES = 58 * 1024 * 1024

Q_ROWS = 4
BAND_ROWS = Q_ROWS + NA_ROWS - 1
Q_BLK = Q_ROWS * GRID_W
BAND = BAND_ROWS * GRID_W
N_BIAS_ROWS = 2 * NA_ROWS - 1
N_BIAS_COLS = 2 * NA_COLS - 1
TOE_W = 1024

LRU_CHUNK = 256
LRU_PITCH = LRU_CHUNK + SUBLANES
LRU_SLABS = LRU_WIDTH // LANES


def _dot(a, b):
    return jnp.dot(a, b, preferred_element_type=F32)


def _dot_nt(a, b):
    return lax.dot_general(a, b, (((1,), (1,)), ((), ())), preferred_element_type=F32)


def _params(*semantics):
    return pltpu.CompilerParams(dimension_semantics=semantics, vmem_limit_bytes=VMEM_LIMIT_BYTES)


def _split_bf16(t):
    hi = t.astype(BF16)
    lo = (t - hi.astype(F32)).astype(BF16)
    return hi, lo


def _mod_kernel(cc_ref, w_ref, b_ref, o_ref):
    cc = cc_ref[...]
    s = cc * jax.nn.sigmoid(cc)
    s_hi, s_lo = _split_bf16(s)
    w_hi, w_lo = _split_bf16(w_ref[...])
    o_ref[...] = _dot(s_hi, w_hi) + _dot(s_lo, w_hi) + _dot(s_hi, w_lo) + b_ref[...]


def _modulation(cc, w_mod, b_mod):
    rows, d = cc.shape
    n = w_mod.shape[1]
    tn = 512
    return pl.pallas_call(
        _mod_kernel,
        grid=(n // tn,),
        in_specs=[pl.BlockSpec((rows, d), lambda j: (0, 0)),
                  pl.BlockSpec((d, tn), lambda j: (0, j)),
                  pl.BlockSpec((1, tn), lambda j: (0, j))],
        out_specs=pl.BlockSpec((rows, tn), lambda j: (0, j)),
        out_shape=jax.ShapeDtypeStruct((rows, n), F32),
        compiler_params=_params("arbitrary"),
        name="modulation",
    )(cc, w_mod, b_mod)


def _bias_kernel(rexp_ref, o_ref):
    q = lax.broadcasted_iota(jnp.int32, (GRID_W, TOE_W), 0)
    lane = lax.broadcasted_iota(jnp.int32, (GRID_W, TOE_W), 1)
    k = lane % GRID_W
    diff = k - q + (NA_COLS - 1)
    toe = jnp.zeros((GRID_W, TOE_W), F32)
    for d in range(N_BIAS_COLS):
        toe = jnp.where(diff == d, rexp_ref[0, d:d + 1, :], toe)
    col_start = jnp.clip(q - NA_COLS // 2, 0, GRID_W - NA_COLS)
    col_ok = (k >= col_start) & (k < col_start + NA_COLS)
    toe = jnp.where(col_ok, toe, NEG_INF)

    for typ in range(3):
        for ri in range(Q_ROWS):
            if typ == 0:
                ja, dr0 = 0, NA_ROWS - 1 - ri
            elif typ == 1:
                ja, dr0 = ri, NA_ROWS // 2 - 1
            else:
                ja, dr0 = BAND_ROWS - NA_ROWS, BAND_ROWS - NA_ROWS - ri
            shift = ((ja - dr0) * GRID_W) % TOE_W
            rolled = pltpu.roll(toe, shift, axis=1) if shift else toe
            ok = (lane >= ja * GRID_W) & (lane < (ja + NA_ROWS) * GRID_W)
            strip = jnp.where(ok, rolled, NEG_INF)
            o_ref[0, typ, ri * GRID_W:(ri + 1) * GRID_W, :] = strip[:, :BAND]


def _bias_tables(rpb):
    rexp = jnp.repeat(jnp.transpose(rpb, (0, 2, 1)) * LOG2E, GRID_W, axis=2)
    rexp = jnp.pad(rexp, ((0, 0), (0, 0), (0, TOE_W - N_BIAS_ROWS * GRID_W)))
    return pl.pallas_call(
        _bias_kernel,
        grid=(N_HEADS,),
        in_specs=[pl.BlockSpec((1, N_BIAS_COLS, TOE_W), lambda h: (h, 0, 0))],
        out_specs=pl.BlockSpec((1, 3, Q_BLK, BAND), lambda h: (h, 0, 0, 0)),
        out_shape=jax.ShapeDtypeStruct((N_HEADS, 3, Q_BLK, BAND), F32),
        compiler_params=_params("arbitrary"),
        name="bias_tables",
    )(rexp)


def _adaln(x, g_ref, scale_ref, shift_ref):
    ms = jnp.mean(x * x, axis=-1, keepdims=True)
    gm = g_ref[...] * (1.0 + scale_ref[0])
    return ((x * lax.rsqrt(ms + EPS)) * gm + shift_ref[0]).astype(BF16)


def _head_rms(t, bd_ref, gain):
    sq = (t * t).astype(BF16)
    bd = bd_ref[...]
    m = jnp.concatenate([_dot(sq[:, :MXU_DIM], bd), _dot(sq[:, MXU_DIM:], bd)], axis=-1)
    return t * lax.rsqrt(m + EPS) * gain


def _rope(t, cos_ref, sin_ref):
    lane = lax.broadcasted_iota(jnp.int32, (1, LANES), 1)
    first = (lane % 32) < 16
    parts = []
    for j in range(t.shape[1] // LANES):
        c = t[:, j * LANES:(j + 1) * LANES]
        parts.append(jnp.where(first, pltpu.roll(c, LANES - 16, axis=1), pltpu.roll(c, 16, axis=1)))
    swapped = jnp.concatenate(parts, axis=-1)
    return t * cos_ref[...] + swapped * sin_ref[...]


def _silu(t):
    return t * jax.nn.sigmoid(t)


def _inproj_kernel(x_ref, g_ref, scale_ref, shift_ref, w_ref, cos_ref, sin_ref, bd_ref, qg_ref, kg_ref,
                   qr_ref, qp_ref, kr_ref, v_ref, za_ref, u_ref, zl_ref):
    hb = _adaln(x_ref[0], g_ref, scale_ref, shift_ref)
    aw = ATTN_WIDTH

    def proj(i):
        return _dot(hb, w_ref[:, i * aw:(i + 1) * aw])

    qn = _head_rms(proj(0), bd_ref, qg_ref[...])
    qp_ref[0] = qn.astype(BF16)
    qr_ref[0] = _rope(qn, cos_ref, sin_ref).astype(BF16)
    kn = _head_rms(proj(1), bd_ref, kg_ref[...])
    kr_ref[0] = _rope(kn, cos_ref, sin_ref).astype(BF16)
    v_ref[0] = proj(2).astype(BF16)
    za_ref[0] = _silu(proj(3)).astype(BF16)
    u_ref[0] = proj(4)
    zl_ref[0] = _silu(proj(5)).astype(BF16)


def _inproj(x, g, scale, shift, w_bf, cos_t, sin_t, bd, qg, kg, tm=512):
    b, s, d = x.shape
    n = w_bf.shape[1]
    aw = ATTN_WIDTH
    tok = lambda i, j: (j, i, 0)
    per_b = lambda i, j: (j, 0, 0)
    const = lambda i, j: (0, 0)
    out_bf = jax.ShapeDtypeStruct((b, s, aw), BF16)
    out_f32 = jax.ShapeDtypeStruct((b, s, aw), F32)
    blk = pl.BlockSpec((1, tm, aw), tok)
    return pl.pallas_call(
        _inproj_kernel,
        grid=(s // tm, b),
        in_specs=[pl.BlockSpec((1, tm, d), tok),
                  pl.BlockSpec((1, d), const),
                  pl.BlockSpec((1, 1, d), per_b),
                  pl.BlockSpec((1, 1, d), per_b),
                  pl.BlockSpec((d, n), const),
                  pl.BlockSpec((tm, aw), lambda i, j: (i, 0)),
                  pl.BlockSpec((tm, aw), lambda i, j: (i, 0)),
                  pl.BlockSpec((MXU_DIM, MXU_DIM), const),
                  pl.BlockSpec((1, aw), const),
                  pl.BlockSpec((1, aw), const)],
        out_specs=[blk, blk, blk, blk, blk, blk, blk],
        out_shape=[out_bf, out_bf, out_bf, out_bf, out_bf, out_f32, out_bf],
        compiler_params=_params("arbitrary", "arbitrary"),
        name="inproj",
    )(x, g, scale, shift, w_bf, cos_t, sin_t, bd, qg, kg)


def _ctxproj_kernel(c_ref, g_ref, scale_ref, shift_ref, w_ref, bd_ref, kg_ref, kc_ref, vc_ref, uc_ref):
    hb = _adaln(c_ref[0], g_ref, scale_ref, shift_ref)
    aw = ATTN_WIDTH
    kc_ref[0] = _head_rms(_dot(hb, w_ref[:, :aw]), bd_ref, kg_ref[...]).astype(BF16)
    vc_ref[0] = _dot(hb, w_ref[:, aw:2 * aw]).astype(BF16)
    uc_ref[0] = _dot(hb, w_ref[:, 2 * aw:])


def _ctxproj(ctx, g, scale_c, shift_c, w_ctx_bf, bd, kg):
    b, l, d = ctx.shape
    aw = ATTN_WIDTH
    const = lambda i: (0, 0)
    blk = pl.BlockSpec((1, l, aw), lambda i: (i, 0, 0))
    return pl.pallas_call(
        _ctxproj_kernel,
        grid=(b,),
        in_specs=[pl.BlockSpec((1, l, d), lambda i: (i, 0, 0)),
                  pl.BlockSpec((1, d), const),
                  pl.BlockSpec((1, 1, d), lambda i: (0, 0, 0)),
                  pl.BlockSpec((1, 1, d), lambda i: (0, 0, 0)),
                  pl.BlockSpec((d, 3 * aw), const),
                  pl.BlockSpec((MXU_DIM, MXU_DIM), const),
                  pl.BlockSpec((1, aw), const)],
        out_specs=[blk, blk, blk],
        out_shape=[jax.ShapeDtypeStruct((b, l, aw), BF16),
                   jax.ShapeDtypeStruct((b, l, aw), BF16),
                   jax.ShapeDtypeStruct((b, l, aw), F32)],
        compiler_params=_params("arbitrary"),
        name="ctxproj",
    )(ctx, g, scale_c, shift_c, w_ctx_bf, bd, kg)


def _attn_kernel(qr_ref, qp_ref, kr_ref, v_ref, kc_ref, vc_ref, za_ref, bias_ref, o_ref):
    rows = qr_ref.shape[1] // GRID_W
    n_blk = rows // Q_ROWS
    lane = lax.broadcasted_iota(jnp.int32, (1, LANES), 1)
    head_lanes = [lane < HEAD_DIM, lane >= HEAD_DIM]
    kc = kc_ref[0]
    vc = vc_ref[0]

    def block(rb, typ):
        if isinstance(rb, int):
            q0 = rb * Q_BLK
            band0 = min(max(rb * Q_ROWS - NA_ROWS // 2, 0), rows - BAND_ROWS) * GRID_W
        else:
            q0 = pl.multiple_of(rb * Q_BLK, Q_BLK)
            band0 = pl.multiple_of(jnp.clip(rb * Q_ROWS - NA_ROWS // 2, 0, rows - BAND_ROWS) * GRID_W, GRID_W)
        qr = qr_ref[0, pl.ds(q0, Q_BLK), :]
        qp = qp_ref[0, pl.ds(q0, Q_BLK), :]
        kb = kr_ref[0, pl.ds(band0, BAND), :]
        vb = v_ref[0, pl.ds(band0, BAND), :]
        zero = jnp.zeros_like(qr)
        outs = []
        for h in range(2):
            s_loc = _dot_nt(jnp.where(head_lanes[h], qr, zero), kb) + bias_ref[h, typ]
            s_ctx = _dot_nt(jnp.where(head_lanes[h], qp, zero), kc)
            m = jnp.maximum(jnp.max(s_loc, axis=-1, keepdims=True), jnp.max(s_ctx, axis=-1, keepdims=True))
            p_loc = jnp.exp2(s_loc - m)
            p_ctx = jnp.exp2(s_ctx - m)
            denom = jnp.sum(p_loc, axis=-1, keepdims=True) + jnp.sum(p_ctx, axis=-1, keepdims=True)
            acc = _dot(p_loc.astype(BF16), vb) + _dot(p_ctx.astype(BF16), vc)
            outs.append(acc / denom)
        o = jnp.where(head_lanes[0], outs[0], outs[1])
        gated = o * za_ref[0, pl.ds(q0, Q_BLK), :].astype(F32)
        o_ref[0, pl.ds(q0, Q_BLK), :] = gated.astype(o_ref.dtype)

    block(0, 0)

    def body(rb, carry):
        block(rb, 1)
        return carry

    lax.fori_loop(1, n_blk - 1, body, 0)
    block(n_blk - 1, 2)


def _attention(qr, qp, kr, v, kc, vc, za, bias):
    b, s, aw = qr.shape
    l = kc.shape[1]
    n_pairs = aw // LANES
    lat = pl.BlockSpec((1, s, LANES), lambda p, i: (i, 0, p))
    cblk = pl.BlockSpec((1, l, LANES), lambda p, i: (i, 0, p))
    return pl.pallas_call(
        _attn_kernel,
        grid=(n_pairs, b),
        in_specs=[lat, lat, lat, lat, cblk, cblk, lat,
                  pl.BlockSpec((2, 3, Q_BLK, BAND), lambda p, i: (p, 0, 0, 0))],
        out_specs=lat,
        out_shape=jax.ShapeDtypeStruct((b, s, aw), BF16),
        compiler_params=_params("arbitrary", "arbitrary"),
        name="attention",
    )(qr, qp, kr, v, kc, vc, za, bias)


def _lru_pass_kernel(direction, n_chunks, *refs):
    if direction == 0:
        (u_ref, up_ref, un_ref, uc_ref, cw_ref, cb_ref, wg_ref, ba_ref, bx_ref, lam_ref,
         o_ref, a_s, b_s, h_s, carry_s) = refs
    else:
        (u_ref, up_ref, un_ref, uc_ref, hf_ref, zl_ref, cw_ref, cb_ref, wg_ref, ba_ref, bx_ref, lam_ref,
         o_ref, a_s, b_s, h_s, carry_s) = refs
    nb = u_ref.shape[0]
    tc = LRU_CHUNK
    step = pl.program_id(0)
    c = step if direction == 0 else n_chunks - 1 - step
    zeros8 = jnp.zeros((SUBLANES, LRU_WIDTH), F32)
    half_log2a = jax.nn.softplus(-lam_ref[...]) * (-0.5 * LRU_C * LOG2E)
    half_ba = 0.5 * ba_ref[...]
    half_bx = 0.5 * bx_ref[...]

    def gate_tanh(ucb, gi, half_bias):
        pre = jnp.concatenate([_dot(ucb[:, :MXU_DIM], wg_ref[gi, 0]),
                               _dot(ucb[:, MXU_DIM:], wg_ref[gi, 1])], axis=-1)
        return jnp.tanh(pre + half_bias)

    def conv(bi, prev8, mid, next8):
        ext = jnp.concatenate([prev8, mid, next8], axis=0)
        n = ext.shape[0]

        def shifted(k):
            return pltpu.roll(ext, (n - k) % n, axis=0)[SUBLANES:SUBLANES + tc]

        y = cb_ref[...] + cw_ref[0:1, :] * shifted(-1)
        y = y + cw_ref[1:2, :] * mid
        y = y + cw_ref[2:3, :] * shifted(1)
        return y + cw_ref[3:4, :] * shifted(2)

    def fill(bi, uc):
        ucb = uc.astype(BF16)
        tr = gate_tanh(ucb, 0, half_ba)
        ti = gate_tanh(ucb, 1, half_bx)
        a = jnp.exp2(tr * half_log2a + half_log2a)
        gap = 1.0 - a * a
        mult = gap * lax.rsqrt(jnp.maximum(gap, 1e-30))
        bb = mult * ((0.5 * ti + 0.5) * uc)
        r0 = pl.multiple_of(bi * LRU_PITCH, SUBLANES)
        for j in range(LRU_SLABS):
            a_s[j, pl.ds(r0, tc), :] = a[:, j * LANES:(j + 1) * LANES]
            b_s[j, pl.ds(r0, tc), :] = bb[:, j * LANES:(j + 1) * LANES]

    def scan():
        def body(i, hs):
            t = i if direction == 0 else tc - 1 - i
            rows = pl.ds(t, nb, stride=LRU_PITCH)
            out = []
            for j in range(LRU_SLABS):
                h = a_s[j, rows, :] * hs[j] + b_s[j, rows, :]
                h_s[j, rows, :] = h
                out.append(h)
            return tuple(out)

        hs = tuple(carry_s[:, j * LANES:(j + 1) * LANES] for j in range(LRU_SLABS))
        hs = lax.fori_loop(0, tc, body, hs, unroll=8)
        for j in range(LRU_SLABS):
            carry_s[:, j * LANES:(j + 1) * LANES] = hs[j]

    @pl.when(step == 0)
    def _():
        carry_s[...] = jnp.zeros_like(carry_s)

        def ctx_fill(bi, carry):
            fill(bi, conv(bi, zeros8, uc_ref[bi], zeros8))
            return carry

        lax.fori_loop(0, nb, ctx_fill, 0)
        scan()

    def lat_fill(bi, carry):
        prev8 = jnp.where(c > 0, up_ref[bi], zeros8)
        next8 = jnp.where(c < n_chunks - 1, un_ref[bi], zeros8)
        fill(bi, conv(bi, prev8, u_ref[bi], next8))
        return carry

    lax.fori_loop(0, nb, lat_fill, 0)
    scan()

    def emit(bi, carry):
        r0 = pl.multiple_of(bi * LRU_PITCH, SUBLANES)
        h = jnp.concatenate([h_s[j, pl.ds(r0, tc), :] for j in range(LRU_SLABS)], axis=-1)
        if direction == 1:
            h = (hf_ref[bi].astype(F32) + h) * zl_ref[bi].astype(F32)
        o_ref[bi] = h.astype(o_ref.dtype)
        return carry

    lax.fori_loop(0, nb, emit, 0)


def _lru_pass(direction, u, u_c, extra, conv_w, conv_b, wg, ba, bx, lam):
    b, s, lw = u.shape
    l = u_c.shape[1]
    tc = LRU_CHUNK
    assert l == tc and s % tc == 0
    n_chunks = s // tc
    halo_per_chunk = tc // SUBLANES
    n_halo = s // SUBLANES
    chunk = (lambda i: i) if direction == 0 else (lambda i: n_chunks - 1 - i)
    full = lambda *shape: pl.BlockSpec(shape, lambda i: (0,) * len(shape))
    cblk = pl.BlockSpec((b, tc, lw), lambda i: (0, chunk(i), 0))
    in_specs = [cblk,
                pl.BlockSpec((b, SUBLANES, lw), lambda i: (0, jnp.maximum(chunk(i) * halo_per_chunk - 1, 0), 0)),
                pl.BlockSpec((b, SUBLANES, lw),
                             lambda i: (0, jnp.minimum((chunk(i) + 1) * halo_per_chunk, n_halo - 1), 0)),
                full(b, l, lw)]
    in_specs += [cblk] * len(extra)
    in_specs += [full(CONV_WIDTH, lw), full(1, lw), full(2, 2, MXU_DIM, MXU_DIM),
                 full(1, lw), full(1, lw), full(1, lw)]
    slab = pltpu.VMEM((LRU_SLABS, b * LRU_PITCH, LANES), F32)
    return pl.pallas_call(
        functools.partial(_lru_pass_kernel, direction, n_chunks),
        grid=(n_chunks,),
        in_specs=in_specs,
        out_specs=cblk,
        out_shape=jax.ShapeDtypeStruct((b, s, lw), BF16),
        scratch_shapes=[slab, slab, slab, pltpu.VMEM((b, lw), F32)],
        compiler_params=_params("arbitrary"),
        name="rglru_fwd" if direction == 0 else "rglru_bwd",
    )(u, u, u, u_c, *extra, conv_w, conv_b, wg, ba, bx, lam)


def _rglru(u, u_c, zl, conv_w, conv_b, wg, ba, bx, lam):
    hf = _lru_pass(0, u, u_c, (), conv_w, conv_b, wg[0], ba[0], bx[0], lam[0])
    return _lru_pass(1, u, u_c, (hf, zl), conv_w, conv_b, wg[1], ba[1], bx[1], lam[1])


def _outproj_kernel(x_ref, a_ref, y_ref, w_ref, gate_ref, o_ref):
    mix = _dot(a_ref[0], w_ref[:ATTN_WIDTH, :]) + _dot(y_ref[0], w_ref[ATTN_WIDTH:, :])
    o_ref[0] = x_ref[0] + gate_ref[0] * mix


def _outproj(x, ag, yg, w_bf, gate, tm=1024):
    b, s, d = x.shape
    tok = lambda i, j: (j, i, 0)
    return pl.pallas_call(
        _outproj_kernel,
        grid=(s // tm, b),
        in_specs=[pl.BlockSpec((1, tm, d), tok),
                  pl.BlockSpec((1, tm, ATTN_WIDTH), tok),
                  pl.BlockSpec((1, tm, LRU_WIDTH), tok),
                  pl.BlockSpec((ATTN_WIDTH + LRU_WIDTH, d), lambda i, j: (0, 0)),
                  pl.BlockSpec((1, 1, d), lambda i, j: (j, 0, 0))],
        out_specs=pl.BlockSpec((1, tm, d), tok),
        out_shape=jax.ShapeDtypeStruct((b, s, d), x.dtype),
        compiler_params=_params("arbitrary", "arbitrary"),
        name="outproj",
    )(x, ag, yg, w_bf, gate)


def _rope_tables(s):
    t = jnp.arange(s)
    half = HEAD_DIM // 4
    inv = ROPE_BASE ** (-jnp.arange(half, dtype=F32) / half)
    ang_r = (t // GRID_W).astype(F32)[:, None] * inv[None, :]
    ang_c = (t % GRID_W).astype(F32)[:, None] * inv[None, :]
    cos = jnp.concatenate([jnp.cos(ang_r)] * 2 + [jnp.cos(ang_c)] * 2, axis=-1)
    sin = jnp.concatenate([-jnp.sin(ang_r), jnp.sin(ang_r), -jnp.sin(ang_c), jnp.sin(ang_c)], axis=-1)
    return jnp.tile(cos, (1, N_HEADS)), jnp.tile(sin, (1, N_HEADS))


def _block_diag_gates(w):
    n = MXU_DIM // LRU_BLOCK
    halves = []
    for hf in range(2):
        m = jnp.zeros((MXU_DIM, MXU_DIM), w.dtype)
        for j in range(n):
            m = lax.dynamic_update_slice(m, w[hf * n + j], (j * LRU_BLOCK, j * LRU_BLOCK))
        halves.append(m)
    return jnp.stack(halves)


def kernel(x, c, ctx, c_ctx, norm_g, w_mod, b_mod, w_in, w_out, q_norm_g, k_norm_g, rpb, conv_w, conv_b,
           lru_wa, lru_ba, lru_wx, lru_bx, lru_lam):
    bsz, s, d = x.shape
    assert w_in.shape[0] == 1 and d == D_MODEL and s % (GRID_W * Q_ROWS) == 0
    aw, lw = ATTN_WIDTH, LRU_WIDTH

    pad_rows = 2 * SUBLANES - bsz - 1
    cc = jnp.concatenate([c, c_ctx[None, :], jnp.zeros((pad_rows, d), F32)], axis=0)
    mod = _modulation(cc, w_mod[0], b_mod[0][None, :])
    shift, scale, gate = [mod[:bsz, i * d:(i + 1) * d][:, None, :] for i in range(3)]
    shift_c, scale_c = [mod[bsz:bsz + 1, i * d:(i + 1) * d][:, None, :] for i in range(2)]

    w = w_in[0]
    w_bf = w.astype(BF16)
    w_ctx_bf = jnp.concatenate([w[:, aw:3 * aw], w[:, 4 * aw:4 * aw + lw]], axis=1).astype(BF16)
    g = norm_g[0][None, :]
    cos_t, sin_t = _rope_tables(s)
    blk = jnp.arange(MXU_DIM) // HEAD_DIM
    bd = jnp.where(blk[:, None] == blk[None, :], 1.0 / HEAD_DIM, 0.0).astype(BF16)
    qg = jnp.tile(q_norm_g[0] * (HEAD_DIM ** -0.5 * LOG2E), N_HEADS)[None, :]
    kg = jnp.tile(k_norm_g[0], N_HEADS)[None, :]

    qr, qp, kr, v, za, u, zl = _inproj(x, g, scale, shift, w_bf, cos_t, sin_t, bd, qg, kg)
    kc, vc, u_c = _ctxproj(ctx, g, scale_c, shift_c, w_ctx_bf, bd, kg)

    bias = _bias_tables(rpb[0])
    ag = _attention(qr, qp, kr, v, kc, vc, za, bias)

    wg = jnp.stack([jnp.stack([_block_diag_gates(lru_wa[0, dd]), _block_diag_gates(lru_wx[0, dd])])
                    for dd in range(2)])
    wg = (0.5 * wg).astype(BF16)
    yg = _rglru(u, u_c, zl, conv_w[0], conv_b[0][None, :], wg,
                lru_ba[0][:, None, :], lru_bx[0][:, None, :], lru_lam[0][:, None, :])

    return _outproj(x, ag, yg, w_out[0].astype(BF16), gate)
```

```python
import functools
import math

import jax
import jax.numpy as jnp
from jax import lax
from jax.experimental import pallas as pl
from jax.experimental.pallas import tpu as pltpu

F32 = jnp.float32
BF16 = jnp.bfloat16

D_MODEL = 1024
GRID_W = 64
HEAD_DIM = 64
ATTN_WIDTH = 512
LRU_WIDTH = 512
N_HEADS = ATTN_WIDTH // HEAD_DIM
LRU_BLOCK = 64
NA_ROWS = 8
NA_COLS = 16
CONV_WIDTH = 4
LRU_C = 8.0
ROPE_BASE = 10000.0
EPS = 1e-6
NEG_INF = -1e30
LOG2E = math.log2(math.e)

LANES = 128
SUBLANES = 8
MXU_DIM = 256
VMEM_LIMIT_BYTES = 58 * 1024 * 1024

Q_ROWS = 4
BAND_ROWS = Q_ROWS + NA_ROWS
Q_BLK = Q_ROWS * GRID_W
BAND = BAND_ROWS * GRID_W
KEY_CHUNK = 256
N_BIAS_ROWS = 2 * NA_ROWS - 1
N_BIAS_COLS = 2 * NA_COLS - 1
TOE_W = 1024

LRU_CHUNK = 256
LRU_PITCH = LRU_CHUNK + SUBLANES
LRU_SLABS = LRU_WIDTH // LANES


def _dot(a, b):
    return jnp.dot(a, b, preferred_element_type=F32)


def _dot_nt(a, b):
    return lax.dot_general(a, b, (((1,), (1,)), ((), ())), preferred_element_type=F32)


def _params(*semantics):
    return pltpu.CompilerParams(dimension_semantics=semantics, vmem_limit_bytes=VMEM_LIMIT_BYTES)


def _split_bf16(t):
    hi = t.astype(BF16)
    lo = (t - hi.astype(F32)).astype(BF16)
    return hi, lo


def _mod_kernel(cc_ref, w_ref, b_ref, o_ref):
    cc = cc_ref[...]
    s = cc * jax.nn.sigmoid(cc)
    s_hi, s_lo = _split_bf16(s)
    w_hi, w_lo = _split_bf16(w_ref[...])
    o_ref[...] = _dot(s_hi, w_hi) + _dot(s_lo, w_hi) + _dot(s_hi, w_lo) + b_ref[...]


def _modulation(cc, w_mod, b_mod):
    rows, d = cc.shape
    n = w_mod.shape[1]
    tn = 512
    return pl.pallas_call(
        _mod_kernel,
        grid=(n // tn,),
        in_specs=[pl.BlockSpec((rows, d), lambda j: (0, 0)),
                  pl.BlockSpec((d, tn), lambda j: (0, j)),
                  pl.BlockSpec((1, tn), lambda j: (0, j))],
        out_specs=pl.BlockSpec((rows, tn), lambda j: (0, j)),
        out_shape=jax.ShapeDtypeStruct((rows, n), F32),
        compiler_params=_params("arbitrary"),
        name="modulation",
    )(cc, w_mod, b_mod)


def _bias_kernel(rexp_ref, o_ref):
    k = lax.broadcasted_iota(jnp.int32, (GRID_W, TOE_W), 0)
    q = lax.broadcasted_iota(jnp.int32, (GRID_W, TOE_W), 1) % GRID_W
    diff = k - q + (NA_COLS - 1)
    toe = jnp.zeros((GRID_W, TOE_W), F32)
    for d in range(N_BIAS_COLS):
        toe = jnp.where(diff == d, rexp_ref[0, d:d + 1, :], toe)
    col_start = jnp.clip(q - NA_COLS // 2, 0, GRID_W - NA_COLS)
    col_ok = (k >= col_start) & (k < col_start + NA_COLS)
    toe = jnp.where(col_ok, toe, NEG_INF)
    ri = lax.broadcasted_iota(jnp.int32, (GRID_W, Q_BLK), 1) // GRID_W
    masked = jnp.full((GRID_W, Q_BLK), NEG_INF, F32)

    for typ in range(3):
        off = (NA_ROWS - 1, NA_ROWS // 2 - 1, NA_ROWS - BAND_ROWS + Q_ROWS - 1)[typ]
        for jj in range(BAND_ROWS):
            if typ == 0:
                lo, hi = (0, Q_ROWS) if jj < NA_ROWS else (0, 0)
            elif typ == 1:
                lo, hi = max(jj - NA_ROWS + 1, 0), min(jj, Q_ROWS - 1) + 1
            else:
                lo, hi = (0, Q_ROWS) if jj >= BAND_ROWS - NA_ROWS else (0, 0)
            strip = masked
            if lo < hi:
                e0 = N_BIAS_ROWS - 1 - jj - off
                rolled = pltpu.roll(toe, (-e0 * GRID_W) % TOE_W, axis=1)[:, :Q_BLK]
                strip = jnp.where((ri >= lo) & (ri < hi), rolled, NEG_INF)
            o_ref[0, typ, jj * GRID_W:(jj + 1) * GRID_W, :] = strip


def _bias_tables(rpb):
    rexp = jnp.repeat(jnp.transpose(rpb[:, ::-1, :], (0, 2, 1)) * LOG2E, GRID_W, axis=2)
    rexp = jnp.pad(rexp, ((0, 0), (0, 0), (0, TOE_W - N_BIAS_ROWS * GRID_W)))
    return pl.pallas_call(
        _bias_kernel,
        grid=(N_HEADS,),
        in_specs=[pl.BlockSpec((1, N_BIAS_COLS, TOE_W), lambda h: (h, 0, 0))],
        out_specs=pl.BlockSpec((1, 3, BAND, Q_BLK), lambda h: (h, 0, 0, 0)),
        out_shape=jax.ShapeDtypeStruct((N_HEADS, 3, BAND, Q_BLK), F32),
        compiler_params=_params("arbitrary"),
        name="bias_tables",
    )(rexp)


def _adaln(x, g_ref, scale_ref, shift_ref):
    ms = jnp.mean(x * x, axis=-1, keepdims=True)
    gm = g_ref[...] * (1.0 + scale_ref[0])
    return ((x * lax.rsqrt(ms + EPS)) * gm + shift_ref[0]).astype(BF16)


def _head_rms(t, bd_ref, gain):
    sq = (t * t).astype(BF16)
    bd = bd_ref[...]
    m = jnp.concatenate([_dot(sq[:, :MXU_DIM], bd), _dot(sq[:, MXU_DIM:], bd)], axis=-1)
    return t * lax.rsqrt(m + EPS) * gain


def _rope(t, cos_ref, sin_ref):
    lane = lax.broadcasted_iota(jnp.int32, (1, LANES), 1)
    first = (lane % 32) < 16
    parts = []
    for j in range(t.shape[1] // LANES):
        c = t[:, j * LANES:(j + 1) * LANES]
        parts.append(jnp.where(first, pltpu.roll(c, LANES - 16, axis=1), pltpu.roll(c, 16, axis=1)))
    swapped = jnp.concatenate(parts, axis=-1)
    return t * cos_ref[...] + swapped * sin_ref[...]


def _silu(t):
    return t * jax.nn.sigmoid(t)


def _inproj_kernel(x_ref, g_ref, scale_ref, shift_ref, w_ref, wt_ref, cos_ref, sin_ref, bd_ref, qg_ref, kg_ref,
                   qr_ref, qp_ref, kr_ref, vt_ref, zat_ref, u_ref, zl_ref):
    hb = _adaln(x_ref[0], g_ref, scale_ref, shift_ref)
    aw = ATTN_WIDTH

    def proj(i):
        return _dot(hb, w_ref[:, i * aw:(i + 1) * aw])

    qn = _head_rms(proj(0), bd_ref, qg_ref[...])
    qp_ref[0] = qn.astype(BF16)
    qr_ref[0] = _rope(qn, cos_ref, sin_ref).astype(BF16)
    kn = _head_rms(proj(1), bd_ref, kg_ref[...])
    kr_ref[0] = _rope(kn, cos_ref, sin_ref).astype(BF16)
    u_ref[0] = proj(2)
    zl_ref[0] = _silu(proj(3)).astype(BF16)
    vt_ref[0] = _dot_nt(wt_ref[:aw, :], hb).astype(BF16)
    zat_ref[0] = _silu(_dot_nt(wt_ref[aw:, :], hb)).astype(BF16)


def _inproj(x, g, scale, shift, w_bf, wt_bf, cos_t, sin_t, bd, qg, kg, tm=512):
    b, s, d = x.shape
    aw = ATTN_WIDTH
    tok = lambda i, j: (j, i, 0)
    per_b = lambda i, j: (j, 0, 0)
    const = lambda i, j: (0, 0)
    out_bf = jax.ShapeDtypeStruct((b, s, aw), BF16)
    out_t = jax.ShapeDtypeStruct((b, aw, s), BF16)
    out_f32 = jax.ShapeDtypeStruct((b, s, aw), F32)
    blk = pl.BlockSpec((1, tm, aw), tok)
    blk_t = pl.BlockSpec((1, aw, tm), lambda i, j: (j, 0, i))
    return pl.pallas_call(
        _inproj_kernel,
        grid=(s // tm, b),
        in_specs=[pl.BlockSpec((1, tm, d), tok),
                  pl.BlockSpec((1, d), const),
                  pl.BlockSpec((1, 1, d), per_b),
                  pl.BlockSpec((1, 1, d), per_b),
                  pl.BlockSpec(w_bf.shape, const),
                  pl.BlockSpec(wt_bf.shape, const),
                  pl.BlockSpec((tm, aw), lambda i, j: (i, 0)),
                  pl.BlockSpec((tm, aw), lambda i, j: (i, 0)),
                  pl.BlockSpec((MXU_DIM, MXU_DIM), const),
                  pl.BlockSpec((1, aw), const),
                  pl.BlockSpec((1, aw), const)],
        out_specs=[blk, blk, blk, blk_t, blk_t, blk, blk],
        out_shape=[out_bf, out_bf, out_bf, out_t, out_t, out_f32, out_bf],
        compiler_params=_params("arbitrary", "arbitrary"),
        name="inproj",
    )(x, g, scale, shift, w_bf, wt_bf, cos_t, sin_t, bd, qg, kg)


def _ctxproj_kernel(c_ref, g_ref, scale_ref, shift_ref, w_ref, wt_ref, bd_ref, kg_ref, kc_ref, vct_ref, uc_ref):
    hb = _adaln(c_ref[0], g_ref, scale_ref, shift_ref)
    aw = ATTN_WIDTH
    kc_ref[0] = _head_rms(_dot(hb, w_ref[:, :aw]), bd_ref, kg_ref[...]).astype(BF16)
    uc_ref[0] = _dot(hb, w_ref[:, aw:])
    vct_ref[0] = _dot_nt(wt_ref[...], hb).astype(BF16)


def _ctxproj(ctx, g, scale_c, shift_c, w_ctx_bf, wt_ctx_bf, bd, kg):
    b, l, d = ctx.shape
    aw = ATTN_WIDTH
    const = lambda i: (0, 0)
    blk = pl.BlockSpec((1, l, aw), lambda i: (i, 0, 0))
    return pl.pallas_call(
        _ctxproj_kernel,
        grid=(b,),
        in_specs=[pl.BlockSpec((1, l, d), lambda i: (i, 0, 0)),
                  pl.BlockSpec((1, d), const),
                  pl.BlockSpec((1, 1, d), lambda i: (0, 0, 0)),
                  pl.BlockSpec((1, 1, d), lambda i: (0, 0, 0)),
                  pl.BlockSpec(w_ctx_bf.shape, const),
                  pl.BlockSpec(wt_ctx_bf.shape, const),
                  pl.BlockSpec((MXU_DIM, MXU_DIM), const),
                  pl.BlockSpec((1, aw), const)],
        out_specs=[blk, pl.BlockSpec((1, aw, l), lambda i: (i, 0, 0)), blk],
        out_shape=[jax.ShapeDtypeStruct((b, l, aw), BF16),
                   jax.ShapeDtypeStruct((b, aw, l), BF16),
                   jax.ShapeDtypeStruct((b, l, aw), F32)],
        compiler_params=_params("arbitrary"),
        name="ctxproj",
    )(ctx, g, scale_c, shift_c, w_ctx_bf, wt_ctx_bf, bd, kg)


def _attn_kernel(qr_ref, qp_ref, kr_ref, vt_ref, kc_ref, vct_ref, zat_ref, bias_ref, o_ref, s_buf0, s_buf1):
    rows = qr_ref.shape[1] // GRID_W
    n_blk = rows // Q_ROWS
    n_chunks = BAND // KEY_CHUNK
    lane_head = lax.broadcasted_iota(jnp.int32, (1, LANES), 1) // HEAD_DIM
    s_bufs = (s_buf0, s_buf1)

    def item(rb):
        rb = jnp.asarray(rb, jnp.int32)
        q0 = pl.multiple_of(rb * Q_BLK, Q_BLK)
        band0 = pl.multiple_of(jnp.clip(rb * Q_ROWS - NA_ROWS // 2, 0, rows - BAND_ROWS) * GRID_W, Q_BLK)
        typ = jnp.where(rb == 0, 0, jnp.where(rb == n_blk - 1, 2, 1))
        return q0, band0, typ

    def score_steps(rb, h):
        q0, band0, typ = item(rb)
        s_buf = s_bufs[h]
        mine = lane_head == h
        qr = jnp.where(mine, qr_ref[0, pl.ds(q0, Q_BLK), :], jnp.zeros((), BF16))
        qp = jnp.where(mine, qp_ref[0, pl.ds(q0, Q_BLK), :], jnp.zeros((), BF16))

        def local(c, m):
            rows_c = pl.ds(c * KEY_CHUNK, KEY_CHUNK)
            s = _dot_nt(kr_ref[0, pl.ds(band0 + c * KEY_CHUNK, KEY_CHUNK), :], qr) + bias_ref[h, typ, rows_c, :]
            s_buf[rows_c, :] = s
            return jnp.maximum(m, jnp.max(s, axis=0, keepdims=True))

        def context(m):
            s = _dot_nt(kc_ref[0], qp)
            s_buf[pl.ds(BAND, KEY_CHUNK), :] = s
            return jnp.maximum(m, jnp.max(s, axis=0, keepdims=True))

        return [functools.partial(local, c) for c in range(n_chunks)] + [context]

    def value_steps(rb, h, m):
        q0, band0, _ = item(rb)
        s_buf = s_bufs[h]
        hd = pl.ds(h * HEAD_DIM, HEAD_DIM)

        def step(c, carry):
            denom, acc = carry
            p = jnp.exp2(s_buf[pl.ds(c * KEY_CHUNK, KEY_CHUNK), :] - m)
            if c < n_chunks:
                vt = vt_ref[0, hd, pl.ds(band0 + c * KEY_CHUNK, KEY_CHUNK)]
            else:
                vt = vct_ref[0, hd, :]
            return denom + jnp.sum(p, axis=0, keepdims=True), acc + _dot(vt, p.astype(BF16))

        def finish(carry):
            denom, acc = carry
            gated = (acc / denom) * zat_ref[0, hd, pl.ds(q0, Q_BLK)].astype(F32)
            o_ref[0, hd, pl.ds(q0, Q_BLK)] = gated.astype(o_ref.dtype)

        return [functools.partial(step, c) for c in range(n_chunks + 1)], finish

    m_init = jnp.full((1, Q_BLK), -jnp.inf, F32)
    sums_init = (jnp.zeros((1, Q_BLK), F32), jnp.zeros((HEAD_DIM, Q_BLK), F32))

    def overlapped(score_item, value_item, m_prev):
        s_steps = score_steps(*score_item) if score_item else [None] * (n_chunks + 1)
        v_steps, finish = value_steps(*value_item, m_prev) if value_item else ([None] * (n_chunks + 1), None)
        m, carry = m_init, sums_init
        for s_step, v_step in zip(s_steps, v_steps):
            if s_step:
                m = s_step(m)
            if v_step:
                carry = v_step(carry)
        if finish:
            finish(carry)
        return m

    def body(rb, m0):
        m1 = overlapped((rb, 1), (rb, 0), m0)
        return overlapped((rb + 1, 0), (rb, 1), m1)

    m0 = lax.fori_loop(0, n_blk - 1, body, overlapped((0, 0), None, None))
    m1 = overlapped((n_blk - 1, 1), (n_blk - 1, 0), m0)
    overlapped(None, (n_blk - 1, 1), m1)


def _attention(qr, qp, kr, vt, kc, vct, zat, bias):
    b, s, aw = qr.shape
    l = kc.shape[1]
    n_pairs = aw // LANES
    lat = pl.BlockSpec((1, s, LANES), lambda p, i: (i, 0, p))
    lat_t = pl.BlockSpec((1, LANES, s), lambda p, i: (i, p, 0))
    return pl.pallas_call(
        _attn_kernel,
        grid=(n_pairs, b),
        in_specs=[lat, lat, lat, lat_t,
                  pl.BlockSpec((1, l, LANES), lambda p, i: (i, 0, p)),
                  pl.BlockSpec((1, LANES, l), lambda p, i: (i, p, 0)),
                  lat_t,
                  pl.BlockSpec((2, 3, BAND, Q_BLK), lambda p, i: (p, 0, 0, 0))],
        out_specs=lat_t,
        out_shape=jax.ShapeDtypeStruct((b, aw, s), BF16),
        scratch_shapes=[pltpu.VMEM((BAND + l, Q_BLK), F32), pltpu.VMEM((BAND + l, Q_BLK), F32)],
        compiler_params=_params("arbitrary", "arbitrary"),
        name="attention",
    )(qr, qp, kr, vt, kc, vct, zat, bias)


def _lru_pass_kernel(direction, n_chunks, *refs):
    if direction == 0:
        (u_ref, up_ref, un_ref, uc_ref, cw_ref, cb_ref, wg_ref, ba_ref, bx_ref, lam_ref,
         o_ref, a_s, b_s, h_s, carry_s) = refs
    else:
        (u_ref, up_ref, un_ref, uc_ref, hf_ref, zl_ref, cw_ref, cb_ref, wg_ref, ba_ref, bx_ref, lam_ref,
         o_ref, a_s, b_s, h_s, carry_s) = refs
    nb = u_ref.shape[0]
    tc = LRU_CHUNK
    step = pl.program_id(0)
    c = step if direction == 0 else n_chunks - 1 - step
    zeros8 = jnp.zeros((SUBLANES, LRU_WIDTH), F32)
    half_log2a = jax.nn.softplus(-lam_ref[...]) * (-0.5 * LRU_C * LOG2E)
    half_ba = 0.5 * ba_ref[...]
    half_bx = 0.5 * bx_ref[...]

    def gate_tanh(ucb, gi, half_bias):
        pre = jnp.concatenate([_dot(ucb[:, :MXU_DIM], wg_ref[gi, 0]),
                               _dot(ucb[:, MXU_DIM:], wg_ref[gi, 1])], axis=-1)
        return jnp.tanh(pre + half_bias)

    def conv(bi, prev8, mid, next8):
        ext = jnp.concatenate([prev8, mid, next8], axis=0)
        n = ext.shape[0]

        def shifted(k):
            return pltpu.roll(ext, (n - k) % n, axis=0)[SUBLANES:SUBLANES + tc]

        y = cb_ref[...] + cw_ref[0:1, :] * shifted(-1)
        y = y + cw_ref[1:2, :] * mid
        y = y + cw_ref[2:3, :] * shifted(1)
        return y + cw_ref[3:4, :] * shifted(2)

    def fill(bi, uc):
        ucb = uc.astype(BF16)
        tr = gate_tanh(ucb, 0, half_ba)
        ti = gate_tanh(ucb, 1, half_bx)
        a = jnp.exp2(tr * half_log2a + half_log2a)
        gap = 1.0 - a * a
        mult = gap * lax.rsqrt(jnp.maximum(gap, 1e-30))
        bb = mult * ((0.5 * ti + 0.5) * uc)
        r0 = pl.multiple_of(bi * LRU_PITCH, SUBLANES)
        for j in range(LRU_SLABS):
            a_s[j, pl.ds(r0, tc), :] = a[:, j * LANES:(j + 1) * LANES]
            b_s[j, pl.ds(r0, tc), :] = bb[:, j * LANES:(j + 1) * LANES]

    def scan():
        def body(i, hs):
            t = i if direction == 0 else tc - 1 - i
            rows = pl.ds(t, nb, stride=LRU_PITCH)
            out = []
            for j in range(LRU_SLABS):
                h = a_s[j, rows, :] * hs[j] + b_s[j, rows, :]
                h_s[j, rows, :] = h
                out.append(h)
            return tuple(out)

        hs = tuple(carry_s[:, j * LANES:(j + 1) * LANES] for j in range(LRU_SLABS))
        hs = lax.fori_loop(0, tc, body, hs, unroll=8)
        for j in range(LRU_SLABS):
            carry_s[:, j * LANES:(j + 1) * LANES] = hs[j]

    @pl.when(step == 0)
    def _():
        carry_s[...] = jnp.zeros_like(carry_s)

        def ctx_fill(bi, carry):
            fill(bi, conv(bi, zeros8, uc_ref[bi], zeros8))
            return carry

        lax.fori_loop(0, nb, ctx_fill, 0)
        scan()

    def lat_fill(bi, carry):
        prev8 = jnp.where(c > 0, up_ref[bi], zeros8)
        next8 = jnp.where(c < n_chunks - 1, un_ref[bi], zeros8)
        fill(bi, conv(bi, prev8, u_ref[bi], next8))
        return carry

    lax.fori_loop(0, nb, lat_fill, 0)
    scan()

    def emit(bi, carry):
        r0 = pl.multiple_of(bi * LRU_PITCH, SUBLANES)
        h = jnp.concatenate([h_s[j, pl.ds(r0, tc), :] for j in range(LRU_SLABS)], axis=-1)
        if direction == 1:
            h = (hf_ref[bi].astype(F32) + h) * zl_ref[bi].astype(F32)
        o_ref[bi] = h.astype(o_ref.dtype)
        return carry

    lax.fori_loop(0, nb, emit, 0)


def _lru_pass(direction, u, u_c, extra, conv_w, conv_b, wg, ba, bx, lam):
    b, s, lw = u.shape
    l = u_c.shape[1]
    tc = LRU_CHUNK
    assert l == tc and s % tc == 0
    n_chunks = s // tc
    halo_per_chunk = tc // SUBLANES
    n_halo = s // SUBLANES
    chunk = (lambda i: i) if direction == 0 else (lambda i: n_chunks - 1 - i)
    full = lambda *shape: pl.BlockSpec(shape, lambda i: (0,) * len(shape))
    cblk = pl.BlockSpec((b, tc, lw), lambda i: (0, chunk(i), 0))
    in_specs = [cblk,
                pl.BlockSpec((b, SUBLANES, lw), lambda i: (0, jnp.maximum(chunk(i) * halo_per_chunk - 1, 0), 0)),
                pl.BlockSpec((b, SUBLANES, lw),
                             lambda i: (0, jnp.minimum((chunk(i) + 1) * halo_per_chunk, n_halo - 1), 0)),
                full(b, l, lw)]
    in_specs += [cblk] * len(extra)
    in_specs += [full(CONV_WIDTH, lw), full(1, lw), full(2, 2, MXU_DIM, MXU_DIM),
                 full(1, lw), full(1, lw), full(1, lw)]
    slab = pltpu.VMEM((LRU_SLABS, b * LRU_PITCH, LANES), F32)
    return pl.pallas_call(
        functools.partial(_lru_pass_kernel, direction, n_chunks),
        grid=(n_chunks,),
        in_specs=in_specs,
        out_specs=cblk,
        out_shape=jax.ShapeDtypeStruct((b, s, lw), BF16),
        scratch_shapes=[slab, slab, slab, pltpu.VMEM((b, lw), F32)],
        compiler_params=_params("arbitrary"),
        name="rglru_fwd" if direction == 0 else "rglru_bwd",
    )(u, u, u, u_c, *extra, conv_w, conv_b, wg, ba, bx, lam)


def _rglru(u, u_c, zl, conv_w, conv_b, wg, ba, bx, lam):
    hf = _lru_pass(0, u, u_c, (), conv_w, conv_b, wg[0], ba[0], bx[0], lam[0])
    return _lru_pass(1, u, u_c, (hf, zl), conv_w, conv_b, wg[1], ba[1], bx[1], lam[1])


def _outproj_kernel(x_ref, at_ref, y_ref, w_ref, gate_ref, o_ref):
    mix = lax.dot_general(at_ref[0], w_ref[:ATTN_WIDTH, :], (((0,), (0,)), ((), ())), preferred_element_type=F32)
    mix = mix + _dot(y_ref[0], w_ref[ATTN_WIDTH:, :])
    o_ref[0] = x_ref[0] + gate_ref[0] * mix


def _outproj(x, agt, yg, w_bf, gate, tm=1024):
    b, s, d = x.shape
    tok = lambda i, j: (j, i, 0)
    return pl.pallas_call(
        _outproj_kernel,
        grid=(s // tm, b),
        in_specs=[pl.BlockSpec((1, tm, d), tok),
                  pl.BlockSpec((1, ATTN_WIDTH, tm), lambda i, j: (j, 0, i)),
                  pl.BlockSpec((1, tm, LRU_WIDTH), tok),
                  pl.BlockSpec((ATTN_WIDTH + LRU_WIDTH, d), lambda i, j: (0, 0)),
                  pl.BlockSpec((1, 1, d), lambda i, j: (j, 0, 0))],
        out_specs=pl.BlockSpec((1, tm, d), tok),
        out_shape=jax.ShapeDtypeStruct((b, s, d), x.dtype),
        compiler_params=_params("arbitrary", "arbitrary"),
        name="outproj",
    )(x, agt, yg, w_bf, gate)


def _rope_tables(s):
    t = jnp.arange(s)
    half = HEAD_DIM // 4
    inv = ROPE_BASE ** (-jnp.arange(half, dtype=F32) / half)
    ang_r = (t // GRID_W).astype(F32)[:, None] * inv[None, :]
    ang_c = (t % GRID_W).astype(F32)[:, None] * inv[None, :]
    cos = jnp.concatenate([jnp.cos(ang_r)] * 2 + [jnp.cos(ang_c)] * 2, axis=-1)
    sin = jnp.concatenate([-jnp.sin(ang_r), jnp.sin(ang_r), -jnp.sin(ang_c), jnp.sin(ang_c)], axis=-1)
    return jnp.tile(cos, (1, N_HEADS)), jnp.tile(sin, (1, N_HEADS))


def _block_diag_gates(w):
    n = MXU_DIM // LRU_BLOCK
    halves = []
    for hf in range(2):
        m = jnp.zeros((MXU_DIM, MXU_DIM), w.dtype)
        for j in range(n):
            m = lax.dynamic_update_slice(m, w[hf * n + j], (j * LRU_BLOCK, j * LRU_BLOCK))
        halves.append(m)
    return jnp.stack(halves)


def kernel(x, c, ctx, c_ctx, norm_g, w_mod, b_mod, w_in, w_out, q_norm_g, k_norm_g, rpb, conv_w, conv_b,
           lru_wa, lru_ba, lru_wx, lru_bx, lru_lam):
    bsz, s, d = x.shape
    assert w_in.shape[0] == 1 and d == D_MODEL and s % (GRID_W * Q_ROWS) == 0
    aw, lw = ATTN_WIDTH, LRU_WIDTH

    pad_rows = 2 * SUBLANES - bsz - 1
    cc = jnp.concatenate([c, c_ctx[None, :], jnp.zeros((pad_rows, d), F32)], axis=0)
    mod = _modulation(cc, w_mod[0], b_mod[0][None, :])
    shift, scale, gate = [mod[:bsz, i * d:(i + 1) * d][:, None, :] for i in range(3)]
    shift_c, scale_c = [mod[bsz:bsz + 1, i * d:(i + 1) * d][:, None, :] for i in range(2)]

    w = w_in[0]
    w_q, w_k, w_v, w_za, w_u, w_zl = [w[:, i * aw:(i + 1) * aw].astype(BF16) for i in range(6)]
    w_bf = jnp.concatenate([w_q, w_k, w_u, w_zl], axis=1)
    wt_bf = jnp.concatenate([w_v, w_za], axis=1).T
    w_ctx_bf = jnp.concatenate([w_k, w_u], axis=1)
    wt_ctx_bf = w_v.T
    g = norm_g[0][None, :]
    cos_t, sin_t = _rope_tables(s)
    blk = jnp.arange(MXU_DIM) // HEAD_DIM
    bd = jnp.where(blk[:, None] == blk[None, :], 1.0 / HEAD_DIM, 0.0).astype(BF16)
    qg = jnp.tile(q_norm_g[0] * (HEAD_DIM ** -0.5 * LOG2E), N_HEADS)[None, :]
    kg = jnp.tile(k_norm_g[0], N_HEADS)[None, :]

    qr, qp, kr, vt, zat, u, zl = _inproj(x, g, scale, shift, w_bf, wt_bf, cos_t, sin_t, bd, qg, kg)
    kc, vct, u_c = _ctxproj(ctx, g, scale_c, shift_c, w_ctx_bf, wt_ctx_bf, bd, kg)

    bias = _bias_tables(rpb[0])
    agt = _attention(qr, qp, kr, vt, kc, vct, zat, bias)

    wg = jnp.stack([jnp.stack([_block_diag_gates(lru_wa[0, dd]), _block_diag_gates(lru_wx[0, dd])])
                    for dd in range(2)])
    wg = (0.5 * wg).astype(BF16)
    yg = _rglru(u, u_c, zl, conv_w[0], conv_b[0][None, :], wg,
                lru_ba[0][:, None, :], lru_bx[0][:, None, :], lru_lam[0][:, None, :])

    return _outproj(x, agt, yg, w_out[0].astype(BF16), gate)
```

```python
import functools
import itertools
import math

import jax
import jax.numpy as jnp
from jax import lax
from jax.experimental import pallas as pl
from jax.experimental.pallas import tpu as pltpu

F32 = jnp.float32
BF16 = jnp.bfloat16

D_MODEL = 1024
GRID_W = 64
HEAD_DIM = 64
ATTN_WIDTH = 512
LRU_WIDTH = 512
N_HEADS = ATTN_WIDTH // HEAD_DIM
LRU_BLOCK = 64
NA_ROWS = 8
NA_COLS = 16
CONV_WIDTH = 4
LRU_C = 8.0
ROPE_BASE = 10000.0
EPS = 1e-6
NEG_INF = -1e30
LOG2E = math.log2(math.e)

LANES = 128
SUBLANES = 8
MXU_DIM = 256
VMEM_LIMIT_BYTES = 58 * 1024 * 1024

Q_ROWS = 4
BAND_ROWS = Q_ROWS + NA_ROWS
Q_BLK = Q_ROWS * GRID_W
BAND = BAND_ROWS * GRID_W
KEY_CHUNK = 256
ONES_ROWS = 16
N_BIAS_ROWS = 2 * NA_ROWS - 1
N_BIAS_COLS = 2 * NA_COLS - 1
TOE_W = 1024

LRU_CHUNK = 256
LRU_PITCH = LRU_CHUNK + SUBLANES
LRU_SLABS = LRU_WIDTH // LANES


def _dot(a, b):
    return jnp.dot(a, b, preferred_element_type=F32)


def _dot_nt(a, b):
    return lax.dot_general(a, b, (((1,), (1,)), ((), ())), preferred_element_type=F32)


def _params(*semantics):
    return pltpu.CompilerParams(dimension_semantics=semantics, vmem_limit_bytes=VMEM_LIMIT_BYTES)


def _split_bf16(t):
    hi = t.astype(BF16)
    lo = (t - hi.astype(F32)).astype(BF16)
    return hi, lo


def _mod_kernel(cc_ref, w_ref, b_ref, o_ref):
    cc = cc_ref[...]
    s = cc * jax.nn.sigmoid(cc)
    s_hi, s_lo = _split_bf16(s)
    w_hi, w_lo = _split_bf16(w_ref[...])
    o_ref[...] = _dot(s_hi, w_hi) + _dot(s_lo, w_hi) + _dot(s_hi, w_lo) + b_ref[...]


def _modulation(cc, w_mod, b_mod):
    rows, d = cc.shape
    n = w_mod.shape[1]
    tn = 512
    return pl.pallas_call(
        _mod_kernel,
        grid=(n // tn,),
        in_specs=[pl.BlockSpec((rows, d), lambda j: (0, 0)),
                  pl.BlockSpec((d, tn), lambda j: (0, j)),
                  pl.BlockSpec((1, tn), lambda j: (0, j))],
        out_specs=pl.BlockSpec((rows, tn), lambda j: (0, j)),
        out_shape=jax.ShapeDtypeStruct((rows, n), F32),
        compiler_params=_params("arbitrary"),
        name="modulation",
    )(cc, w_mod, b_mod)


def _bias_kernel(rexp_ref, o_ref):
    k = lax.broadcasted_iota(jnp.int32, (GRID_W, TOE_W), 0)
    q = lax.broadcasted_iota(jnp.int32, (GRID_W, TOE_W), 1) % GRID_W
    diff = k - q + (NA_COLS - 1)
    toe = jnp.zeros((GRID_W, TOE_W), F32)
    for d in range(N_BIAS_COLS):
        toe = jnp.where(diff == d, rexp_ref[0, d:d + 1, :], toe)
    col_start = jnp.clip(q - NA_COLS // 2, 0, GRID_W - NA_COLS)
    col_ok = (k >= col_start) & (k < col_start + NA_COLS)
    toe = jnp.where(col_ok, toe, NEG_INF)
    ri = lax.broadcasted_iota(jnp.int32, (GRID_W, Q_BLK), 1) // GRID_W
    masked = jnp.full((GRID_W, Q_BLK), NEG_INF, F32)

    for typ in range(3):
        off = (NA_ROWS - 1, NA_ROWS // 2 - 1, NA_ROWS - BAND_ROWS + Q_ROWS - 1)[typ]
        for jj in range(BAND_ROWS):
            if typ == 0:
                lo, hi = (0, Q_ROWS) if jj < NA_ROWS else (0, 0)
            elif typ == 1:
                lo, hi = max(jj - NA_ROWS + 1, 0), min(jj, Q_ROWS - 1) + 1
            else:
                lo, hi = (0, Q_ROWS) if jj >= BAND_ROWS - NA_ROWS else (0, 0)
            strip = masked
            if lo < hi:
                e0 = N_BIAS_ROWS - 1 - jj - off
                rolled = pltpu.roll(toe, (-e0 * GRID_W) % TOE_W, axis=1)[:, :Q_BLK]
                strip = jnp.where((ri >= lo) & (ri < hi), rolled, NEG_INF)
            o_ref[0, typ, jj * GRID_W:(jj + 1) * GRID_W, :] = strip


def _bias_tables(rpb):
    rexp = jnp.repeat(jnp.transpose(rpb[:, ::-1, :], (0, 2, 1)) * LOG2E, GRID_W, axis=2)
    rexp = jnp.pad(rexp, ((0, 0), (0, 0), (0, TOE_W - N_BIAS_ROWS * GRID_W)))
    return pl.pallas_call(
        _bias_kernel,
        grid=(N_HEADS,),
        in_specs=[pl.BlockSpec((1, N_BIAS_COLS, TOE_W), lambda h: (h, 0, 0))],
        out_specs=pl.BlockSpec((1, 3, BAND, Q_BLK), lambda h: (h, 0, 0, 0)),
        out_shape=jax.ShapeDtypeStruct((N_HEADS, 3, BAND, Q_BLK), F32),
        compiler_params=_params("arbitrary"),
        name="bias_tables",
    )(rexp)


def _adaln(x, g_ref, scale_ref, shift_ref):
    ms = jnp.mean(x * x, axis=-1, keepdims=True)
    gm = g_ref[...] * (1.0 + scale_ref[0])
    return ((x * lax.rsqrt(ms + EPS)) * gm + shift_ref[0]).astype(BF16)


def _head_rms(t, bd_ref, gain):
    sq = (t * t).astype(BF16)
    bd = bd_ref[...]
    m = jnp.concatenate([_dot(sq[:, :MXU_DIM], bd), _dot(sq[:, MXU_DIM:], bd)], axis=-1)
    return t * lax.rsqrt(m + EPS) * gain


def _rope(t, cos_ref, sin_ref):
    lane = lax.broadcasted_iota(jnp.int32, (1, LANES), 1)
    first = (lane % 32) < 16
    cos, sin = cos_ref[...], sin_ref[...]
    parts = []
    for j in range(t.shape[1] // LANES):
        c = t[:, j * LANES:(j + 1) * LANES]
        swapped = jnp.where(first, pltpu.roll(c, LANES - 16, axis=1), pltpu.roll(c, 16, axis=1))
        parts.append(c * cos + swapped * sin)
    return jnp.concatenate(parts, axis=-1)


def _silu(t):
    return t * jax.nn.sigmoid(t)


def _inproj_kernel(x_ref, g_ref, scale_ref, shift_ref, w_ref, wt_ref, cos_ref, sin_ref, bd_ref, qg_ref, kg_ref,
                   qr_ref, qp_ref, kr_ref, vt_ref, zat_ref, u_ref, zl_ref):
    hb = _adaln(x_ref[0], g_ref, scale_ref, shift_ref)
    aw = ATTN_WIDTH

    def proj(i):
        return _dot(hb, w_ref[:, i * aw:(i + 1) * aw])

    qn = _head_rms(proj(0), bd_ref, qg_ref[...])
    qp_ref[0] = qn.astype(BF16)
    qr_ref[0] = _rope(qn, cos_ref, sin_ref).astype(BF16)
    kn = _head_rms(proj(1), bd_ref, kg_ref[...])
    kr_ref[0] = _rope(kn, cos_ref, sin_ref).astype(BF16)
    u_ref[0] = proj(2)
    zl_ref[0] = _silu(proj(3)).astype(BF16)
    vt_ref[0] = _dot_nt(wt_ref[:aw, :], hb).astype(BF16)
    zat_ref[0] = _silu(_dot_nt(wt_ref[aw:, :], hb)).astype(BF16)


def _inproj(x, g, scale, shift, w_bf, wt_bf, cos_t, sin_t, bd, qg, kg, tm=512):
    b, s, d = x.shape
    aw = ATTN_WIDTH
    tok = lambda i, j: (j, i, 0)
    per_b = lambda i, j: (j, 0, 0)
    const = lambda i, j: (0, 0)
    out_bf = jax.ShapeDtypeStruct((b, s, aw), BF16)
    out_t = jax.ShapeDtypeStruct((b, aw, s), BF16)
    out_f32 = jax.ShapeDtypeStruct((b, s, aw), F32)
    blk = pl.BlockSpec((1, tm, aw), tok)
    blk_t = pl.BlockSpec((1, aw, tm), lambda i, j: (j, 0, i))
    return pl.pallas_call(
        _inproj_kernel,
        grid=(s // tm, b),
        in_specs=[pl.BlockSpec((1, tm, d), tok),
                  pl.BlockSpec((1, d), const),
                  pl.BlockSpec((1, 1, d), per_b),
                  pl.BlockSpec((1, 1, d), per_b),
                  pl.BlockSpec(w_bf.shape, const),
                  pl.BlockSpec(wt_bf.shape, const),
                  pl.BlockSpec((tm, LANES), lambda i, j: (i, 0)),
                  pl.BlockSpec((tm, LANES), lambda i, j: (i, 0)),
                  pl.BlockSpec((MXU_DIM, MXU_DIM), const),
                  pl.BlockSpec((1, aw), const),
                  pl.BlockSpec((1, aw), const)],
        out_specs=[blk, blk, blk, blk_t, blk_t, blk, blk],
        out_shape=[out_bf, out_bf, out_bf, out_t, out_t, out_f32, out_bf],
        compiler_params=_params("arbitrary", "arbitrary"),
        name="inproj",
    )(x, g, scale, shift, w_bf, wt_bf, cos_t, sin_t, bd, qg, kg)


def _ctxproj_kernel(c_ref, g_ref, scale_ref, shift_ref, w_ref, wt_ref, bd_ref, kg_ref, kc_ref, vct_ref, uc_ref):
    hb = _adaln(c_ref[0], g_ref, scale_ref, shift_ref)
    aw = ATTN_WIDTH
    kc_ref[0] = _head_rms(_dot(hb, w_ref[:, :aw]), bd_ref, kg_ref[...]).astype(BF16)
    uc_ref[0] = _dot(hb, w_ref[:, aw:])
    vct_ref[0] = _dot_nt(wt_ref[...], hb).astype(BF16)


def _ctxproj(ctx, g, scale_c, shift_c, w_ctx_bf, wt_ctx_bf, bd, kg):
    b, l, d = ctx.shape
    aw = ATTN_WIDTH
    const = lambda i: (0, 0)
    blk = pl.BlockSpec((1, l, aw), lambda i: (i, 0, 0))
    return pl.pallas_call(
        _ctxproj_kernel,
        grid=(b,),
        in_specs=[pl.BlockSpec((1, l, d), lambda i: (i, 0, 0)),
                  pl.BlockSpec((1, d), const),
                  pl.BlockSpec((1, 1, d), lambda i: (0, 0, 0)),
                  pl.BlockSpec((1, 1, d), lambda i: (0, 0, 0)),
                  pl.BlockSpec(w_ctx_bf.shape, const),
                  pl.BlockSpec(wt_ctx_bf.shape, const),
                  pl.BlockSpec((MXU_DIM, MXU_DIM), const),
                  pl.BlockSpec((1, aw), const)],
        out_specs=[blk, pl.BlockSpec((1, aw, l), lambda i: (i, 0, 0)), blk],
        out_shape=[jax.ShapeDtypeStruct((b, l, aw), BF16),
                   jax.ShapeDtypeStruct((b, aw, l), BF16),
                   jax.ShapeDtypeStruct((b, l, aw), F32)],
        compiler_params=_params("arbitrary"),
        name="ctxproj",
    )(ctx, g, scale_c, shift_c, w_ctx_bf, wt_ctx_bf, bd, kg)


def _attn_kernel(qr_ref, qp_ref, kr_ref, vt_ref, kc_ref, vct_ref, zat_ref, bias_ref, o_ref, s_buf0, s_buf1):
    rows = qr_ref.shape[1] // GRID_W
    n_blk = rows // Q_ROWS
    chunk_rows = KEY_CHUNK // GRID_W
    n_tiles = Q_BLK // LANES
    lane_head = lax.broadcasted_iota(jnp.int32, (1, LANES), 1) // HEAD_DIM
    s_bufs = (s_buf0, s_buf1)
    ones_rows = jnp.ones((ONES_ROWS, KEY_CHUNK), BF16)

    def in_window(typ, jj, tile):
        if typ == 0:
            return jj < NA_ROWS
        if typ == 2:
            return jj >= BAND_ROWS - NA_ROWS
        return any(0 <= jj - ri < NA_ROWS for ri in range(tile * Q_ROWS // n_tiles, (tile + 1) * Q_ROWS // n_tiles))

    def local_chunks(typ):
        return [c for c in range(BAND // KEY_CHUNK)
                if any(in_window(typ, c * chunk_rows + r, t) for r in range(chunk_rows) for t in range(n_tiles))]

    def sub_blocks(c):
        for r in range(chunk_rows):
            for t in range(n_tiles):
                yield c * chunk_rows + r, t, slice(r * GRID_W, (r + 1) * GRID_W), slice(t * LANES, (t + 1) * LANES)

    def item(rb):
        rb = jnp.asarray(rb, jnp.int32)
        q0 = pl.multiple_of(rb * Q_BLK, Q_BLK)
        band0 = pl.multiple_of(jnp.clip(rb * Q_ROWS - NA_ROWS // 2, 0, rows - BAND_ROWS) * GRID_W, Q_BLK)
        return q0, band0

    def col_max(m, blk, t):
        m = list(m)
        m[t] = jnp.maximum(m[t], jnp.max(blk, axis=0, keepdims=True))
        return tuple(m)

    def score_steps(rb, h, typ):
        q0, band0 = item(rb)
        s_buf = s_bufs[h]
        mine = lane_head == h
        qr = jnp.where(mine, qr_ref[0, pl.ds(q0, Q_BLK), :], jnp.zeros((), BF16))
        qp = jnp.where(mine, qp_ref[0, pl.ds(q0, Q_BLK), :], jnp.zeros((), BF16))

        def local(c, m):
            s = _dot_nt(kr_ref[0, pl.ds(band0 + c * KEY_CHUNK, KEY_CHUNK), :], qr)
            for jj, t, r_sl, l_sl in sub_blocks(c):
                if in_window(typ, jj, t):
                    k_sl = slice(jj * GRID_W, (jj + 1) * GRID_W)
                    blk = s[r_sl, l_sl] + bias_ref[h, typ, k_sl, l_sl]
                    s_buf[k_sl, l_sl] = blk
                    m = col_max(m, blk, t)
            return m

        def context(m):
            s = _dot_nt(kc_ref[0], qp)
            s_buf[pl.ds(BAND, KEY_CHUNK), :] = s
            for t in range(n_tiles):
                m = col_max(m, s[:, t * LANES:(t + 1) * LANES], t)
            return m

        return [functools.partial(local, c) for c in local_chunks(typ)] + [context]

    def value_steps(rb, h, typ, m):
        q0, band0 = item(rb)
        s_buf = s_bufs[h]
        hd = pl.ds(h * HEAD_DIM, HEAD_DIM)

        def probs(k_sl, t):
            return jnp.exp2(s_buf[k_sl, t * LANES:(t + 1) * LANES] - m[t]).astype(BF16)

        def local(c, acc):
            zero_blk = jnp.zeros((GRID_W, LANES), BF16)
            row_blks = []
            for r in range(chunk_rows):
                jj = c * chunk_rows + r
                k_sl = slice(jj * GRID_W, (jj + 1) * GRID_W)
                row_blks.append(jnp.concatenate(
                    [probs(k_sl, t) if in_window(typ, jj, t) else zero_blk for t in range(n_tiles)], axis=1))
            p = jnp.concatenate(row_blks, axis=0)
            vt = jnp.concatenate([vt_ref[0, hd, pl.ds(band0 + c * KEY_CHUNK, KEY_CHUNK)], ones_rows], axis=0)
            return acc + _dot(vt, p)

        def context(acc):
            k_sl = slice(BAND, BAND + KEY_CHUNK)
            p = jnp.concatenate([probs(k_sl, t) for t in range(n_tiles)], axis=1)
            vt = jnp.concatenate([vct_ref[0, hd, :], ones_rows], axis=0)
            return acc + _dot(vt, p)

        return [functools.partial(local, c) for c in local_chunks(typ)] + [context]

    def write_out(rb, h, acc):
        q0, _ = item(rb)
        hd = pl.ds(h * HEAD_DIM, HEAD_DIM)
        gated = (acc[:HEAD_DIM] / acc[HEAD_DIM:HEAD_DIM + 1]) * zat_ref[0, hd, pl.ds(q0, Q_BLK)].astype(F32)
        o_ref[0, hd, pl.ds(q0, Q_BLK)] = gated.astype(o_ref.dtype)

    m_init = (jnp.full((1, LANES), -jnp.inf, F32),) * n_tiles
    acc_init = jnp.zeros((HEAD_DIM + ONES_ROWS, Q_BLK), F32)

    def overlapped(score_item, value_item, out_item, carry):
        m_prev, acc_prev = carry
        s_steps = score_steps(*score_item) if score_item else []
        v_steps = value_steps(*value_item, m_prev) if value_item else []
        m, acc = m_init, acc_init
        for i, (s_step, v_step) in enumerate(itertools.zip_longest(s_steps, v_steps)):
            if s_step:
                m = s_step(m)
            if i == 0 and out_item:
                write_out(*out_item, acc_prev)
            if v_step:
                acc = v_step(acc)
        return m, acc

    first, mid, last = 0, 1, 2
    carry = overlapped((0, 0, first), None, None, (None, None))
    carry = overlapped((0, 1, first), (0, 0, first), None, carry)
    carry = overlapped((1, 0, mid), (0, 1, first), (0, 0), carry)

    def body(rb, carry):
        carry = overlapped((rb, 1, mid), (rb, 0, mid), (rb - 1, 1), carry)
        return overlapped((rb + 1, 0, mid), (rb, 1, mid), (rb, 0), carry)

    carry = lax.fori_loop(1, n_blk - 2, body, carry, unroll=2)
    carry = overlapped((n_blk - 2, 1, mid), (n_blk - 2, 0, mid), (n_blk - 3, 1), carry)
    carry = overlapped((n_blk - 1, 0, last), (n_blk - 2, 1, mid), (n_blk - 2, 0), carry)
    carry = overlapped((n_blk - 1, 1, last), (n_blk - 1, 0, last), (n_blk - 2, 1), carry)
    carry = overlapped(None, (n_blk - 1, 1, last), (n_blk - 1, 0), carry)
    write_out(n_blk - 1, 1, carry[1])


def _attention(qr, qp, kr, vt, kc, vct, zat, bias):
    b, s, aw = qr.shape
    l = kc.shape[1]
    n_pairs = aw // LANES
    lat = pl.BlockSpec((1, s, LANES), lambda p, i: (i, 0, p))
    lat_t = pl.BlockSpec((1, LANES, s), lambda p, i: (i, p, 0))
    return pl.pallas_call(
        _attn_kernel,
        grid=(n_pairs, b),
        in_specs=[lat, lat, lat, lat_t,
                  pl.BlockSpec((1, l, LANES), lambda p, i: (i, 0, p)),
                  pl.BlockSpec((1, LANES, l), lambda p, i: (i, p, 0)),
                  lat_t,
                  pl.BlockSpec((2, 3, BAND, Q_BLK), lambda p, i: (p, 0, 0, 0))],
        out_specs=lat_t,
        out_shape=jax.ShapeDtypeStruct((b, aw, s), BF16),
        scratch_shapes=[pltpu.VMEM((BAND + l, Q_BLK), F32), pltpu.VMEM((BAND + l, Q_BLK), F32)],
        compiler_params=_params("arbitrary", "arbitrary"),
        name="attention",
    )(qr, qp, kr, vt, kc, vct, zat, bias)


def _lru_pass_kernel(direction, n_chunks, *refs):
    if direction == 0:
        (u_ref, up_ref, un_ref, uc_ref, cw_ref, cb_ref, wg_ref, ba_ref, bx_ref, lam_ref,
         o_ref, a_s, b_s, h_s, carry_s) = refs
    else:
        (u_ref, up_ref, un_ref, uc_ref, hf_ref, zl_ref, cw_ref, cb_ref, wg_ref, ba_ref, bx_ref, lam_ref,
         o_ref, a_s, b_s, h_s, carry_s) = refs
    nb = u_ref.shape[0]
    tc = LRU_CHUNK
    step = pl.program_id(0)
    c = step if direction == 0 else n_chunks - 1 - step
    zeros8 = jnp.zeros((SUBLANES, LRU_WIDTH), F32)
    half_log2a = jax.nn.softplus(-lam_ref[...]) * (-0.5 * LRU_C * LOG2E)
    half_ba = 0.5 * ba_ref[...]
    half_bx = 0.5 * bx_ref[...]

    def gate_tanh(ucb, gi, half_bias):
        pre = jnp.concatenate([_dot(ucb[:, :MXU_DIM], wg_ref[gi, 0]),
                               _dot(ucb[:, MXU_DIM:], wg_ref[gi, 1])], axis=-1)
        return jnp.tanh(pre + half_bias)

    def conv(bi, prev8, mid, next8):
        ext = jnp.concatenate([prev8, mid, next8], axis=0)
        n = ext.shape[0]

        def shifted(k):
            return pltpu.roll(ext, (n - k) % n, axis=0)[SUBLANES:SUBLANES + tc]

        y = cb_ref[...] + cw_ref[0:1, :] * shifted(-1)
        y = y + cw_ref[1:2, :] * mid
        y = y + cw_ref[2:3, :] * shifted(1)
        return y + cw_ref[3:4, :] * shifted(2)

    def fill(bi, uc):
        ucb = uc.astype(BF16)
        tr = gate_tanh(ucb, 0, half_ba)
        ti = gate_tanh(ucb, 1, half_bx)
        a = jnp.exp2(tr * half_log2a + half_log2a)
        gap = 1.0 - a * a
        mult = gap * lax.rsqrt(jnp.maximum(gap, 1e-30))
        bb = mult * ((0.5 * ti + 0.5) * uc)
        r0 = pl.multiple_of(bi * LRU_PITCH, SUBLANES)
        for j in range(LRU_SLABS):
            a_s[j, pl.ds(r0, tc), :] = a[:, j * LANES:(j + 1) * LANES]
            b_s[j, pl.ds(r0, tc), :] = bb[:, j * LANES:(j + 1) * LANES]

    def scan():
        def body(i, hs):
            t = i if direction == 0 else tc - 1 - i
            rows = pl.ds(t, nb, stride=LRU_PITCH)
            out = []
            for j in range(LRU_SLABS):
                h = a_s[j, rows, :] * hs[j] + b_s[j, rows, :]
                h_s[j, rows, :] = h
                out.append(h)
            return tuple(out)

        hs = tuple(carry_s[:, j * LANES:(j + 1) * LANES] for j in range(LRU_SLABS))
        hs = lax.fori_loop(0, tc, body, hs, unroll=8)
        for j in range(LRU_SLABS):
            carry_s[:, j * LANES:(j + 1) * LANES] = hs[j]

    @pl.when(step == 0)
    def _():
        carry_s[...] = jnp.zeros_like(carry_s)

        def ctx_fill(bi, carry):
            fill(bi, conv(bi, zeros8, uc_ref[bi], zeros8))
            return carry

        lax.fori_loop(0, nb, ctx_fill, 0)
        scan()

    def lat_fill(bi, carry):
        prev8 = jnp.where(c > 0, up_ref[bi], zeros8)
        next8 = jnp.where(c < n_chunks - 1, un_ref[bi], zeros8)
        fill(bi, conv(bi, prev8, u_ref[bi], next8))
        return carry

    lax.fori_loop(0, nb, lat_fill, 0)
    scan()

    def emit(bi, carry):
        r0 = pl.multiple_of(bi * LRU_PITCH, SUBLANES)
        h = jnp.concatenate([h_s[j, pl.ds(r0, tc), :] for j in range(LRU_SLABS)], axis=-1)
        if direction == 1:
            h = (hf_ref[bi].astype(F32) + h) * zl_ref[bi].astype(F32)
        o_ref[bi] = h.astype(o_ref.dtype)
        return carry

    lax.fori_loop(0, nb, emit, 0)


def _lru_pass(direction, u, u_c, extra, conv_w, conv_b, wg, ba, bx, lam):
    b, s, lw = u.shape
    l = u_c.shape[1]
    tc = LRU_CHUNK
    assert l == tc and s % tc == 0
    n_chunks = s // tc
    halo_per_chunk = tc // SUBLANES
    n_halo = s // SUBLANES
    chunk = (lambda i: i) if direction == 0 else (lambda i: n_chunks - 1 - i)
    full = lambda *shape: pl.BlockSpec(shape, lambda i: (0,) * len(shape))
    cblk = pl.BlockSpec((b, tc, lw), lambda i: (0, chunk(i), 0))
    in_specs = [cblk,
                pl.BlockSpec((b, SUBLANES, lw), lambda i: (0, jnp.maximum(chunk(i) * halo_per_chunk - 1, 0), 0)),
                pl.BlockSpec((b, SUBLANES, lw),
                             lambda i: (0, jnp.minimum((chunk(i) + 1) * halo_per_chunk, n_halo - 1), 0)),
                full(b, l, lw)]
    in_specs += [cblk] * len(extra)
    in_specs += [full(CONV_WIDTH, lw), full(1, lw), full(2, 2, MXU_DIM, MXU_DIM),
                 full(1, lw), full(1, lw), full(1, lw)]
    slab = pltpu.VMEM((LRU_SLABS, b * LRU_PITCH, LANES), F32)
    return pl.pallas_call(
        functools.partial(_lru_pass_kernel, direction, n_chunks),
        grid=(n_chunks,),
        in_specs=in_specs,
        out_specs=cblk,
        out_shape=jax.ShapeDtypeStruct((b, s, lw), BF16),
        scratch_shapes=[slab, slab, slab, pltpu.VMEM((b, lw), F32)],
        compiler_params=_params("arbitrary"),
        name="rglru_fwd" if direction == 0 else "rglru_bwd",
    )(u, u, u, u_c, *extra, conv_w, conv_b, wg, ba, bx, lam)


def _rglru(u, u_c, zl, conv_w, conv_b, wg, ba, bx, lam):
    hf = _lru_pass(0, u, u_c, (), conv_w, conv_b, wg[0], ba[0], bx[0], lam[0])
    return _lru_pass(1, u, u_c, (hf, zl), conv_w, conv_b, wg[1], ba[1], bx[1], lam[1])


def _outproj_kernel(x_ref, at_ref, y_ref, w_ref, gate_ref, o_ref):
    mix = lax.dot_general(at_ref[0], w_ref[:ATTN_WIDTH, :], (((0,), (0,)), ((), ())), preferred_element_type=F32)
    mix = mix + _dot(y_ref[0], w_ref[ATTN_WIDTH:, :])
    o_ref[0] = x_ref[0] + gate_ref[0] * mix


def _outproj(x, agt, yg, w_bf, gate, tm=1024):
    b, s, d = x.shape
    tok = lambda i, j: (j, i, 0)
    return pl.pallas_call(
        _outproj_kernel,
        grid=(s // tm, b),
        in_specs=[pl.BlockSpec((1, tm, d), tok),
                  pl.BlockSpec((1, ATTN_WIDTH, tm), lambda i, j: (j, 0, i)),
                  pl.BlockSpec((1, tm, LRU_WIDTH), tok),
                  pl.BlockSpec((ATTN_WIDTH + LRU_WIDTH, d), lambda i, j: (0, 0)),
                  pl.BlockSpec((1, 1, d), lambda i, j: (j, 0, 0))],
        out_specs=pl.BlockSpec((1, tm, d), tok),
        out_shape=jax.ShapeDtypeStruct((b, s, d), x.dtype),
        compiler_params=_params("arbitrary", "arbitrary"),
        name="outproj",
    )(x, agt, yg, w_bf, gate)


def _rope_tables(s):
    t = jnp.arange(s)
    half = HEAD_DIM // 4
    inv = ROPE_BASE ** (-jnp.arange(half, dtype=F32) / half)
    ang_r = (t // GRID_W).astype(F32)[:, None] * inv[None, :]
    ang_c = (t % GRID_W).astype(F32)[:, None] * inv[None, :]
    cos = jnp.concatenate([jnp.cos(ang_r)] * 2 + [jnp.cos(ang_c)] * 2, axis=-1)
    sin = jnp.concatenate([-jnp.sin(ang_r), jnp.sin(ang_r), -jnp.sin(ang_c), jnp.sin(ang_c)], axis=-1)
    return jnp.tile(cos, (1, LANES // HEAD_DIM)), jnp.tile(sin, (1, LANES // HEAD_DIM))


def _block_diag_gates(w):
    n = MXU_DIM // LRU_BLOCK
    lead = w.shape[:-3]
    w = w.reshape(lead + (2, n, LRU_BLOCK, LRU_BLOCK))
    on_diag = jnp.eye(n, dtype=bool)[:, None, :, None]
    blocks = jnp.where(on_diag, w[..., :, :, None, :], jnp.zeros((), w.dtype))
    return blocks.reshape(lead + (2, MXU_DIM, MXU_DIM))


def kernel(x, c, ctx, c_ctx, norm_g, w_mod, b_mod, w_in, w_out, q_norm_g, k_norm_g, rpb, conv_w, conv_b,
           lru_wa, lru_ba, lru_wx, lru_bx, lru_lam):
    bsz, s, d = x.shape
    assert w_in.shape[0] == 1 and d == D_MODEL and s % (GRID_W * Q_ROWS) == 0
    aw, lw = ATTN_WIDTH, LRU_WIDTH

    pad_rows = 2 * SUBLANES - bsz - 1
    cc = jnp.concatenate([c, c_ctx[None, :], jnp.zeros((pad_rows, d), F32)], axis=0)
    mod = _modulation(cc, w_mod[0], b_mod[0][None, :])
    shift, scale, gate = [mod[:bsz, i * d:(i + 1) * d][:, None, :] for i in range(3)]
    shift_c, scale_c = [mod[bsz:bsz + 1, i * d:(i + 1) * d][:, None, :] for i in range(2)]

    w = w_in[0]
    w_q, w_k, w_v, w_za, w_u, w_zl = [w[:, i * aw:(i + 1) * aw].astype(BF16) for i in range(6)]
    w_bf = jnp.concatenate([w_q, w_k, w_u, w_zl], axis=1)
    wt_bf = jnp.concatenate([w_v, w_za], axis=1).T
    w_ctx_bf = jnp.concatenate([w_k, w_u], axis=1)
    wt_ctx_bf = w_v.T
    g = norm_g[0][None, :]
    cos_t, sin_t = _rope_tables(s)
    blk = jnp.arange(MXU_DIM) // HEAD_DIM
    bd = jnp.where(blk[:, None] == blk[None, :], 1.0 / HEAD_DIM, 0.0).astype(BF16)
    qg = jnp.tile(q_norm_g[0] * (HEAD_DIM ** -0.5 * LOG2E), N_HEADS)[None, :]
    kg = jnp.tile(k_norm_g[0], N_HEADS)[None, :]

    qr, qp, kr, vt, zat, u, zl = _inproj(x, g, scale, shift, w_bf, wt_bf, cos_t, sin_t, bd, qg, kg)
    kc, vct, u_c = _ctxproj(ctx, g, scale_c, shift_c, w_ctx_bf, wt_ctx_bf, bd, kg)

    bias = _bias_tables(rpb[0])
    agt = _attention(qr, qp, kr, vt, kc, vct, zat, bias)

    wg = _block_diag_gates(jnp.stack([lru_wa[0], lru_wx[0]], axis=1))
    wg = (0.5 * wg).astype(BF16)
    yg = _rglru(u, u_c, zl, conv_w[0], conv_b[0][None, :], wg,
                lru_ba[0][:, None, :], lru_bx[0][:, None, :], lru_lam[0][:, None, :])

    return _outproj(x, agt, yg, w_out[0].astype(BF16), gate)
```

```python
import functools
import itertools
import math

import jax
import jax.numpy as jnp
from jax import lax
from jax.experimental import pallas as pl
from jax.experimental.pallas import tpu as pltpu

F32 = jnp.float32
BF16 = jnp.bfloat16

D_MODEL = 1024
GRID_W = 64
HEAD_DIM = 64
ATTN_WIDTH = 512
LRU_WIDTH = 512
N_HEADS = ATTN_WIDTH // HEAD_DIM
LRU_BLOCK = 64
NA_ROWS = 8
NA_COLS = 16
CONV_WIDTH = 4
LRU_C = 8.0
ROPE_BASE = 10000.0
EPS = 1e-6
NEG_INF = -1e30
LOG2E = math.log2(math.e)

LANES = 128
SUBLANES = 8
MXU_DIM = 256
VMEM_LIMIT_BYTES = 58 * 1024 * 1024

Q_ROWS = 4
BAND_ROWS = Q_ROWS + NA_ROWS
Q_BLK = Q_ROWS * GRID_W
BAND = BAND_ROWS * GRID_W
KEY_CHUNK = 256
ONES_ROWS = 16
N_BIAS_ROWS = 2 * NA_ROWS - 1
N_BIAS_COLS = 2 * NA_COLS - 1
TOE_W = 1024

LRU_CHUNK = 256
LRU_PITCH = LRU_CHUNK + SUBLANES
LRU_SLABS = LRU_WIDTH // LANES
CONV_ROWS = 128
HALO = 16


def _dot(a, b):
    return jnp.dot(a, b, preferred_element_type=F32)


def _dot_nt(a, b):
    return lax.dot_general(a, b, (((1,), (1,)), ((), ())), preferred_element_type=F32)


def _params(*semantics):
    return pltpu.CompilerParams(dimension_semantics=semantics, vmem_limit_bytes=VMEM_LIMIT_BYTES)


def _split_bf16(t):
    hi = t.astype(BF16)
    lo = (t - hi.astype(F32)).astype(BF16)
    return hi, lo


def _mod_kernel(cc_ref, w_ref, b_ref, o_ref):
    cc = cc_ref[...]
    s = cc * jax.nn.sigmoid(cc)
    s_hi, s_lo = _split_bf16(s)
    w_hi, w_lo = _split_bf16(w_ref[...])
    o_ref[...] = _dot(s_hi, w_hi) + _dot(s_lo, w_hi) + _dot(s_hi, w_lo) + b_ref[...]


def _modulation(cc, w_mod, b_mod):
    rows, d = cc.shape
    n = w_mod.shape[1]
    tn = 512
    return pl.pallas_call(
        _mod_kernel,
        grid=(n // tn,),
        in_specs=[pl.BlockSpec((rows, d), lambda j: (0, 0)),
                  pl.BlockSpec((d, tn), lambda j: (0, j)),
                  pl.BlockSpec((1, tn), lambda j: (0, j))],
        out_specs=pl.BlockSpec((rows, tn), lambda j: (0, j)),
        out_shape=jax.ShapeDtypeStruct((rows, n), F32),
        compiler_params=_params("arbitrary"),
        name="modulation",
    )(cc, w_mod, b_mod)


def _bias_kernel(rexp_ref, o_ref):
    k = lax.broadcasted_iota(jnp.int32, (GRID_W, TOE_W), 0)
    q = lax.broadcasted_iota(jnp.int32, (GRID_W, TOE_W), 1) % GRID_W
    diff = k - q + (NA_COLS - 1)
    toe = jnp.zeros((GRID_W, TOE_W), F32)
    for d in range(N_BIAS_COLS):
        toe = jnp.where(diff == d, rexp_ref[0, d:d + 1, :], toe)
    col_start = jnp.clip(q - NA_COLS // 2, 0, GRID_W - NA_COLS)
    col_ok = (k >= col_start) & (k < col_start + NA_COLS)
    toe = jnp.where(col_ok, toe, NEG_INF)
    ri = lax.broadcasted_iota(jnp.int32, (GRID_W, Q_BLK), 1) // GRID_W
    masked = jnp.full((GRID_W, Q_BLK), NEG_INF, F32)

    for typ in range(3):
        off = (NA_ROWS - 1, NA_ROWS // 2 - 1, NA_ROWS - BAND_ROWS + Q_ROWS - 1)[typ]
        for jj in range(BAND_ROWS):
            if typ == 0:
                lo, hi = (0, Q_ROWS) if jj < NA_ROWS else (0, 0)
            elif typ == 1:
                lo, hi = max(jj - NA_ROWS + 1, 0), min(jj, Q_ROWS - 1) + 1
            else:
                lo, hi = (0, Q_ROWS) if jj >= BAND_ROWS - NA_ROWS else (0, 0)
            strip = masked
            if lo < hi:
                e0 = N_BIAS_ROWS - 1 - jj - off
                rolled = pltpu.roll(toe, (-e0 * GRID_W) % TOE_W, axis=1)[:, :Q_BLK]
                strip = jnp.where((ri >= lo) & (ri < hi), rolled, NEG_INF)
            o_ref[0, typ, jj * GRID_W:(jj + 1) * GRID_W, :] = strip


def _bias_tables(rpb):
    rexp = jnp.repeat(jnp.transpose(rpb[:, ::-1, :], (0, 2, 1)) * LOG2E, GRID_W, axis=2)
    rexp = jnp.pad(rexp, ((0, 0), (0, 0), (0, TOE_W - N_BIAS_ROWS * GRID_W)))
    return pl.pallas_call(
        _bias_kernel,
        grid=(N_HEADS,),
        in_specs=[pl.BlockSpec((1, N_BIAS_COLS, TOE_W), lambda h: (h, 0, 0))],
        out_specs=pl.BlockSpec((1, 3, BAND, Q_BLK), lambda h: (h, 0, 0, 0)),
        out_shape=jax.ShapeDtypeStruct((N_HEADS, 3, BAND, Q_BLK), F32),
        compiler_params=_params("arbitrary"),
        name="bias_tables",
    )(rexp)


def _adaln(x, g_ref, scale_ref, shift_ref):
    ms = jnp.mean(x * x, axis=-1, keepdims=True)
    gm = g_ref[...] * (1.0 + scale_ref[0])
    return ((x * lax.rsqrt(ms + EPS)) * gm + shift_ref[0]).astype(BF16)


def _head_rms(t, bd_ref, gain):
    sq = (t * t).astype(BF16)
    bd = bd_ref[...]
    m = jnp.concatenate([_dot(sq[:, :MXU_DIM], bd), _dot(sq[:, MXU_DIM:], bd)], axis=-1)
    return t * lax.rsqrt(m + EPS) * gain


def _rope(t, cos_ref, sin_ref):
    lane = lax.broadcasted_iota(jnp.int32, (1, LANES), 1)
    first = (lane % 32) < 16
    cos, sin = cos_ref[...], sin_ref[...]
    parts = []
    for j in range(t.shape[1] // LANES):
        c = t[:, j * LANES:(j + 1) * LANES]
        swapped = jnp.where(first, pltpu.roll(c, LANES - 16, axis=1), pltpu.roll(c, 16, axis=1))
        parts.append(c * cos + swapped * sin)
    return jnp.concatenate(parts, axis=-1)


def _silu(t):
    return t * jax.nn.sigmoid(t)


def _conv_window(ext, n, cw_ref, cb_ref):
    rows = ext.shape[0]

    def shifted(k):
        return pltpu.roll(ext, (rows - k) % rows, axis=0)[HALO:HALO + n]

    y = cb_ref[...] + cw_ref[0:1, :] * shifted(-1)
    y = y + cw_ref[1:2, :] * ext[HALO:HALO + n]
    y = y + cw_ref[2:3, :] * shifted(1)
    return y + cw_ref[3:4, :] * shifted(2)


def _inproj_kernel(x_ref, xp_ref, xn_ref, g_ref, scale_ref, shift_ref, w_ref, wt_ref, cos_ref, sin_ref, bd_ref,
                   qg_ref, kg_ref, cw_ref, cb_ref, qr_ref, qp_ref, kr_ref, vt_ref, zat_ref, uc_ref, zl_ref):
    tm = x_ref.shape[1]
    aw = ATTN_WIDTH
    i, n_tiles = pl.program_id(0), pl.num_programs(0)
    hb_ext = jnp.concatenate([_adaln(xp_ref[0], g_ref, scale_ref, shift_ref),
                              _adaln(x_ref[0], g_ref, scale_ref, shift_ref),
                              _adaln(xn_ref[0], g_ref, scale_ref, shift_ref)], axis=0)
    hb = hb_ext[HALO:HALO + tm]

    def proj(j):
        return _dot(hb, w_ref[:, j * aw:(j + 1) * aw])

    u_ext = _dot(hb_ext, w_ref[:, 2 * aw:3 * aw])
    row = lax.broadcasted_iota(jnp.int32, (tm + 2 * HALO, 1), 0)
    inside = ((row >= HALO) | (i > 0)) & ((row < HALO + tm) | (i < n_tiles - 1))
    u_ext = jnp.where(inside, u_ext, 0.0)

    qn = _head_rms(proj(0), bd_ref, qg_ref[...])
    qp_ref[0] = qn.astype(BF16)
    qr_ref[0] = _rope(qn, cos_ref, sin_ref).astype(BF16)
    kn = _head_rms(proj(1), bd_ref, kg_ref[...])
    kr_ref[0] = _rope(kn, cos_ref, sin_ref).astype(BF16)

    zl_ref[0] = _silu(proj(3)).astype(BF16)
    vt_ref[0] = _dot_nt(wt_ref[:aw, :], hb).astype(BF16)
    zat_ref[0] = _silu(_dot_nt(wt_ref[aw:, :], hb)).astype(BF16)
    for r in range(0, tm, CONV_ROWS):
        uc_ref[0, r:r + CONV_ROWS, :] = _conv_window(u_ext[r:r + CONV_ROWS + 2 * HALO], CONV_ROWS, cw_ref, cb_ref)


def _inproj(x, g, scale, shift, w_bf, wt_bf, cos_t, sin_t, bd, qg, kg, conv_w, conv_b, tm=512):
    b, s, d = x.shape
    aw = ATTN_WIDTH
    tok = lambda i, j: (j, i, 0)
    per_b = lambda i, j: (j, 0, 0)
    const = lambda i, j: (0, 0)
    halo_per_tile = tm // HALO
    n_halo = s // HALO
    out_bf = jax.ShapeDtypeStruct((b, s, aw), BF16)
    out_t = jax.ShapeDtypeStruct((b, aw, s), BF16)
    out_f32 = jax.ShapeDtypeStruct((b, s, aw), F32)
    blk = pl.BlockSpec((1, tm, aw), tok)
    blk_t = pl.BlockSpec((1, aw, tm), lambda i, j: (j, 0, i))
    return pl.pallas_call(
        _inproj_kernel,
        grid=(s // tm, b),
        in_specs=[pl.BlockSpec((1, tm, d), tok),
                  pl.BlockSpec((1, HALO, d), lambda i, j: (j, jnp.maximum(i * halo_per_tile - 1, 0), 0)),
                  pl.BlockSpec((1, HALO, d), lambda i, j: (j, jnp.minimum((i + 1) * halo_per_tile, n_halo - 1), 0)),
                  pl.BlockSpec((1, d), const),
                  pl.BlockSpec((1, 1, d), per_b),
                  pl.BlockSpec((1, 1, d), per_b),
                  pl.BlockSpec(w_bf.shape, const),
                  pl.BlockSpec(wt_bf.shape, const),
                  pl.BlockSpec((tm, LANES), lambda i, j: (i, 0)),
                  pl.BlockSpec((tm, LANES), lambda i, j: (i, 0)),
                  pl.BlockSpec((MXU_DIM, MXU_DIM), const),
                  pl.BlockSpec((1, aw), const),
                  pl.BlockSpec((1, aw), const),
                  pl.BlockSpec((CONV_WIDTH, aw), const),
                  pl.BlockSpec((1, aw), const)],
        out_specs=[blk, blk, blk, blk_t, blk_t, blk, blk],
        out_shape=[out_bf, out_bf, out_bf, out_t, out_t, out_f32, out_bf],
        compiler_params=_params("arbitrary", "arbitrary"),
        name="inproj",
    )(x, x, x, g, scale, shift, w_bf, wt_bf, cos_t, sin_t, bd, qg, kg, conv_w, conv_b)


def _ctxproj_kernel(c_ref, g_ref, scale_ref, shift_ref, w_ref, wt_ref, bd_ref, kg_ref, cw_ref, cb_ref,
                    kc_ref, vct_ref, uc_ref):
    hb = _adaln(c_ref[0], g_ref, scale_ref, shift_ref)
    aw = ATTN_WIDTH
    kc_ref[0] = _head_rms(_dot(hb, w_ref[:, :aw]), bd_ref, kg_ref[...]).astype(BF16)
    u = _dot(hb, w_ref[:, aw:])
    pad = jnp.zeros((HALO, aw), F32)
    uc_ref[0] = _conv_window(jnp.concatenate([pad, u, pad], axis=0), u.shape[0], cw_ref, cb_ref)
    vct_ref[0] = _dot_nt(wt_ref[...], hb).astype(BF16)


def _ctxproj(ctx, g, scale_c, shift_c, w_ctx_bf, wt_ctx_bf, bd, kg, conv_w, conv_b):
    b, l, d = ctx.shape
    aw = ATTN_WIDTH
    const = lambda i: (0, 0)
    blk = pl.BlockSpec((1, l, aw), lambda i: (i, 0, 0))
    return pl.pallas_call(
        _ctxproj_kernel,
        grid=(b,),
        in_specs=[pl.BlockSpec((1, l, d), lambda i: (i, 0, 0)),
                  pl.BlockSpec((1, d), const),
                  pl.BlockSpec((1, 1, d), lambda i: (0, 0, 0)),
                  pl.BlockSpec((1, 1, d), lambda i: (0, 0, 0)),
                  pl.BlockSpec(w_ctx_bf.shape, const),
                  pl.BlockSpec(wt_ctx_bf.shape, const),
                  pl.BlockSpec((MXU_DIM, MXU_DIM), const),
                  pl.BlockSpec((1, aw), const),
                  pl.BlockSpec((CONV_WIDTH, aw), const),
                  pl.BlockSpec((1, aw), const)],
        out_specs=[blk, pl.BlockSpec((1, aw, l), lambda i: (i, 0, 0)), blk],
        out_shape=[jax.ShapeDtypeStruct((b, l, aw), BF16),
                   jax.ShapeDtypeStruct((b, aw, l), BF16),
                   jax.ShapeDtypeStruct((b, l, aw), F32)],
        compiler_params=_params("arbitrary"),
        name="ctxproj",
    )(ctx, g, scale_c, shift_c, w_ctx_bf, wt_ctx_bf, bd, kg, conv_w, conv_b)


def _attn_kernel(qr_ref, qp_ref, kr_ref, vt_ref, kc_ref, vct_ref, zat_ref, bias_ref, o_ref, s_buf0, s_buf1):
    rows = qr_ref.shape[1] // GRID_W
    n_blk = rows // Q_ROWS
    chunk_rows = KEY_CHUNK // GRID_W
    n_tiles = Q_BLK // LANES
    lane_head = lax.broadcasted_iota(jnp.int32, (1, LANES), 1) // HEAD_DIM
    s_bufs = (s_buf0, s_buf1)
    ones_rows = jnp.ones((ONES_ROWS, KEY_CHUNK), BF16)

    def in_window(typ, jj, tile):
        if typ == 0:
            return jj < NA_ROWS
        if typ == 2:
            return jj >= BAND_ROWS - NA_ROWS
        return any(0 <= jj - ri < NA_ROWS for ri in range(tile * Q_ROWS // n_tiles, (tile + 1) * Q_ROWS // n_tiles))

    def local_chunks(typ):
        return [c for c in range(BAND // KEY_CHUNK)
                if any(in_window(typ, c * chunk_rows + r, t) for r in range(chunk_rows) for t in range(n_tiles))]

    def sub_blocks(c):
        for r in range(chunk_rows):
            for t in range(n_tiles):
                yield c * chunk_rows + r, t, slice(r * GRID_W, (r + 1) * GRID_W), slice(t * LANES, (t + 1) * LANES)

    def item(rb):
        rb = jnp.asarray(rb, jnp.int32)
        q0 = pl.multiple_of(rb * Q_BLK, Q_BLK)
        band0 = pl.multiple_of(jnp.clip(rb * Q_ROWS - NA_ROWS // 2, 0, rows - BAND_ROWS) * GRID_W, Q_BLK)
        return q0, band0

    def col_max(m, blk, t):
        m = list(m)
        m[t] = jnp.maximum(m[t], jnp.max(blk, axis=0, keepdims=True))
        return tuple(m)

    def score_steps(rb, h, typ):
        q0, band0 = item(rb)
        s_buf = s_bufs[h]
        mine = lane_head == h
        qr = jnp.where(mine, qr_ref[0, pl.ds(q0, Q_BLK), :], jnp.zeros((), BF16))
        qp = jnp.where(mine, qp_ref[0, pl.ds(q0, Q_BLK), :], jnp.zeros((), BF16))

        def local(c, m):
            s = _dot_nt(kr_ref[0, pl.ds(band0 + c * KEY_CHUNK, KEY_CHUNK), :], qr)
            for jj, t, r_sl, l_sl in sub_blocks(c):
                if in_window(typ, jj, t):
                    k_sl = slice(jj * GRID_W, (jj + 1) * GRID_W)
                    blk = s[r_sl, l_sl] + bias_ref[h, typ, k_sl, l_sl]
                    s_buf[k_sl, l_sl] = blk
                    m = col_max(m, blk, t)
            return m

        def context(m):
            s = _dot_nt(kc_ref[0], qp)
            s_buf[pl.ds(BAND, KEY_CHUNK), :] = s
            for t in range(n_tiles):
                m = col_max(m, s[:, t * LANES:(t + 1) * LANES], t)
            return m

        return [functools.partial(local, c) for c in local_chunks(typ)] + [context]

    def value_steps(rb, h, typ, m):
        q0, band0 = item(rb)
        s_buf = s_bufs[h]
        hd = pl.ds(h * HEAD_DIM, HEAD_DIM)

        def probs(k_sl, t):
            return jnp.exp2(s_buf[k_sl, t * LANES:(t + 1) * LANES] - m[t]).astype(BF16)

        def local(c, acc):
            zero_blk = jnp.zeros((GRID_W, LANES), BF16)
            row_blks = []
            for r in range(chunk_rows):
                jj = c * chunk_rows + r
                k_sl = slice(jj * GRID_W, (jj + 1) * GRID_W)
                row_blks.append(jnp.concatenate(
                    [probs(k_sl, t) if in_window(typ, jj, t) else zero_blk for t in range(n_tiles)], axis=1))
            p = jnp.concatenate(row_blks, axis=0)
            vt = jnp.concatenate([vt_ref[0, hd, pl.ds(band0 + c * KEY_CHUNK, KEY_CHUNK)], ones_rows], axis=0)
            return acc + _dot(vt, p)

        def context(acc):
            k_sl = slice(BAND, BAND + KEY_CHUNK)
            p = jnp.concatenate([probs(k_sl, t) for t in range(n_tiles)], axis=1)
            vt = jnp.concatenate([vct_ref[0, hd, :], ones_rows], axis=0)
            return acc + _dot(vt, p)

        return [functools.partial(local, c) for c in local_chunks(typ)] + [context]

    def write_out(rb, h, acc):
        q0, _ = item(rb)
        hd = pl.ds(h * HEAD_DIM, HEAD_DIM)
        gated = (acc[:HEAD_DIM] / acc[HEAD_DIM:HEAD_DIM + 1]) * zat_ref[0, hd, pl.ds(q0, Q_BLK)].astype(F32)
        o_ref[0, hd, pl.ds(q0, Q_BLK)] = gated.astype(o_ref.dtype)

    m_init = (jnp.full((1, LANES), -jnp.inf, F32),) * n_tiles
    acc_init = jnp.zeros((HEAD_DIM + ONES_ROWS, Q_BLK), F32)

    def overlapped(score_item, value_item, out_item, carry):
        m_prev, acc_prev = carry
        s_steps = score_steps(*score_item) if score_item else []
        v_steps = value_steps(*value_item, m_prev) if value_item else []
        m, acc = m_init, acc_init
        for i, (s_step, v_step) in enumerate(itertools.zip_longest(s_steps, v_steps)):
            if s_step:
                m = s_step(m)
            if i == 0 and out_item:
                write_out(*out_item, acc_prev)
            if v_step:
                acc = v_step(acc)
        return m, acc

    first, mid, last = 0, 1, 2
    carry = overlapped((0, 0, first), None, None, (None, None))
    carry = overlapped((0, 1, first), (0, 0, first), None, carry)
    carry = overlapped((1, 0, mid), (0, 1, first), (0, 0), carry)

    def body(rb, carry):
        carry = overlapped((rb, 1, mid), (rb, 0, mid), (rb - 1, 1), carry)
        return overlapped((rb + 1, 0, mid), (rb, 1, mid), (rb, 0), carry)

    carry = lax.fori_loop(1, n_blk - 2, body, carry, unroll=2)
    carry = overlapped((n_blk - 2, 1, mid), (n_blk - 2, 0, mid), (n_blk - 3, 1), carry)
    carry = overlapped((n_blk - 1, 0, last), (n_blk - 2, 1, mid), (n_blk - 2, 0), carry)
    carry = overlapped((n_blk - 1, 1, last), (n_blk - 1, 0, last), (n_blk - 2, 1), carry)
    carry = overlapped(None, (n_blk - 1, 1, last), (n_blk - 1, 0), carry)
    write_out(n_blk - 1, 1, carry[1])


def _attention(qr, qp, kr, vt, kc, vct, zat, bias):
    b, s, aw = qr.shape
    l = kc.shape[1]
    n_pairs = aw // LANES
    lat = pl.BlockSpec((1, s, LANES), lambda p, i: (i, 0, p))
    lat_t = pl.BlockSpec((1, LANES, s), lambda p, i: (i, p, 0))
    return pl.pallas_call(
        _attn_kernel,
        grid=(n_pairs, b),
        in_specs=[lat, lat, lat, lat_t,
                  pl.BlockSpec((1, l, LANES), lambda p, i: (i, 0, p)),
                  pl.BlockSpec((1, LANES, l), lambda p, i: (i, p, 0)),
                  lat_t,
                  pl.BlockSpec((2, 3, BAND, Q_BLK), lambda p, i: (p, 0, 0, 0))],
        out_specs=lat_t,
        out_shape=jax.ShapeDtypeStruct((b, aw, s), BF16),
        scratch_shapes=[pltpu.VMEM((BAND + l, Q_BLK), F32), pltpu.VMEM((BAND + l, Q_BLK), F32)],
        compiler_params=_params("arbitrary", "arbitrary"),
        name="attention",
    )(qr, qp, kr, vt, kc, vct, zat, bias)


def _lru_pass_kernel(direction, n_chunks, *refs):
    if direction == 0:
        u_ref, uc_ref, wg_ref, ba_ref, bx_ref, lam_ref, o_ref, a_s, b_s, h_s, carry_s = refs
    else:
        u_ref, uc_ref, hf_ref, zl_ref, wg_ref, ba_ref, bx_ref, lam_ref, o_ref, a_s, b_s, h_s, carry_s = refs
    nb = u_ref.shape[0]
    tc = LRU_CHUNK
    step = pl.program_id(0)
    half_log2a = jax.nn.softplus(-lam_ref[...]) * (-0.5 * LRU_C * LOG2E)
    half_ba = 0.5 * ba_ref[...]
    half_bx = 0.5 * bx_ref[...]

    def gate_tanh(ucb, gi, half_bias):
        pre = jnp.concatenate([_dot(ucb[:, :MXU_DIM], wg_ref[gi, 0]),
                               _dot(ucb[:, MXU_DIM:], wg_ref[gi, 1])], axis=-1)
        return jnp.tanh(pre + half_bias)

    def fill(bi, uc):
        ucb = uc.astype(BF16)
        tr = gate_tanh(ucb, 0, half_ba)
        ti = gate_tanh(ucb, 1, half_bx)
        a = jnp.exp2(tr * half_log2a + half_log2a)
        gap = 1.0 - a * a
        mult = gap * lax.rsqrt(jnp.maximum(gap, 1e-30))
        bb = mult * ((0.5 * ti + 0.5) * uc)
        r0 = pl.multiple_of(bi * LRU_PITCH, SUBLANES)
        for j in range(LRU_SLABS):
            a_s[j, pl.ds(r0, tc), :] = a[:, j * LANES:(j + 1) * LANES]
            b_s[j, pl.ds(r0, tc), :] = bb[:, j * LANES:(j + 1) * LANES]

    def scan():
        def body(i, hs):
            t = i if direction == 0 else tc - 1 - i
            rows = pl.ds(t, nb, stride=LRU_PITCH)
            out = []
            for j in range(LRU_SLABS):
                h = a_s[j, rows, :] * hs[j] + b_s[j, rows, :]
                h_s[j, rows, :] = h
                out.append(h)
            return tuple(out)

        hs = tuple(carry_s[:, j * LANES:(j + 1) * LANES] for j in range(LRU_SLABS))
        hs = lax.fori_loop(0, tc, body, hs, unroll=8)
        for j in range(LRU_SLABS):
            carry_s[:, j * LANES:(j + 1) * LANES] = hs[j]

    @pl.when(step == 0)
    def _():
        carry_s[...] = jnp.zeros_like(carry_s)

        def ctx_fill(bi, carry):
            fill(bi, uc_ref[bi])
            return carry

        lax.fori_loop(0, nb, ctx_fill, 0)
        scan()

    def lat_fill(bi, carry):
        fill(bi, u_ref[bi])
        return carry

    lax.fori_loop(0, nb, lat_fill, 0, unroll=4)
    scan()

    def emit(bi, carry):
        r0 = pl.multiple_of(bi * LRU_PITCH, SUBLANES)
        h = jnp.concatenate([h_s[j, pl.ds(r0, tc), :] for j in range(LRU_SLABS)], axis=-1)
        if direction == 1:
            h = (hf_ref[bi].astype(F32) + h) * zl_ref[bi].astype(F32)
        o_ref[bi] = h.astype(o_ref.dtype)
        return carry

    lax.fori_loop(0, nb, emit, 0)


def _lru_pass(direction, u, u_c, extra, wg, ba, bx, lam):
    b, s, lw = u.shape
    l = u_c.shape[1]
    tc = LRU_CHUNK
    assert l == tc and s % tc == 0
    n_chunks = s // tc
    chunk = (lambda i: i) if direction == 0 else (lambda i: n_chunks - 1 - i)
    full = lambda *shape: pl.BlockSpec(shape, lambda i: (0,) * len(shape))
    cblk = pl.BlockSpec((b, tc, lw), lambda i: (0, chunk(i), 0))
    in_specs = [cblk, full(b, l, lw)]
    in_specs += [cblk] * len(extra)
    in_specs += [full(2, 2, MXU_DIM, MXU_DIM), full(1, lw), full(1, lw), full(1, lw)]
    slab = pltpu.VMEM((LRU_SLABS, b * LRU_PITCH, LANES), F32)
    return pl.pallas_call(
        functools.partial(_lru_pass_kernel, direction, n_chunks),
        grid=(n_chunks,),
        in_specs=in_specs,
        out_specs=cblk,
        out_shape=jax.ShapeDtypeStruct((b, s, lw), BF16),
        scratch_shapes=[slab, slab, slab, pltpu.VMEM((b, lw), F32)],
        compiler_params=_params("arbitrary"),
        name="rglru_fwd" if direction == 0 else "rglru_bwd",
    )(u, u_c, *extra, wg, ba, bx, lam)


def _rglru(u, u_c, zl, wg, ba, bx, lam):
    hf = _lru_pass(0, u, u_c, (), wg[0], ba[0], bx[0], lam[0])
    return _lru_pass(1, u, u_c, (hf, zl), wg[1], ba[1], bx[1], lam[1])


def _outproj_kernel(x_ref, at_ref, y_ref, w_ref, gate_ref, o_ref):
    mix = lax.dot_general(at_ref[0], w_ref[:ATTN_WIDTH, :], (((0,), (0,)), ((), ())), preferred_element_type=F32)
    mix = mix + _dot(y_ref[0], w_ref[ATTN_WIDTH:, :])
    o_ref[0] = x_ref[0] + gate_ref[0] * mix


def _outproj(x, agt, yg, w_bf, gate, tm=1024):
    b, s, d = x.shape
    tok = lambda i, j: (j, i, 0)
    return pl.pallas_call(
        _outproj_kernel,
        grid=(s // tm, b),
        in_specs=[pl.BlockSpec((1, tm, d), tok),
                  pl.BlockSpec((1, ATTN_WIDTH, tm), lambda i, j: (j, 0, i)),
                  pl.BlockSpec((1, tm, LRU_WIDTH), tok),
                  pl.BlockSpec((ATTN_WIDTH + LRU_WIDTH, d), lambda i, j: (0, 0)),
                  pl.BlockSpec((1, 1, d), lambda i, j: (j, 0, 0))],
        out_specs=pl.BlockSpec((1, tm, d), tok),
        out_shape=jax.ShapeDtypeStruct((b, s, d), x.dtype),
        compiler_params=_params("arbitrary", "arbitrary"),
        name="outproj",
    )(x, agt, yg, w_bf, gate)


def _rope_tables(s):
    t = jnp.arange(s)
    half = HEAD_DIM // 4
    inv = ROPE_BASE ** (-jnp.arange(half, dtype=F32) / half)
    ang_r = (t // GRID_W).astype(F32)[:, None] * inv[None, :]
    ang_c = (t % GRID_W).astype(F32)[:, None] * inv[None, :]
    cos = jnp.concatenate([jnp.cos(ang_r)] * 2 + [jnp.cos(ang_c)] * 2, axis=-1)
    sin = jnp.concatenate([-jnp.sin(ang_r), jnp.sin(ang_r), -jnp.sin(ang_c), jnp.sin(ang_c)], axis=-1)
    return jnp.tile(cos, (1, LANES // HEAD_DIM)), jnp.tile(sin, (1, LANES // HEAD_DIM))


def _block_diag_gates(w):
    n = MXU_DIM // LRU_BLOCK
    lead = w.shape[:-3]
    w = w.reshape(lead + (2, n, LRU_BLOCK, LRU_BLOCK))
    on_diag = jnp.eye(n, dtype=bool)[:, None, :, None]
    blocks = jnp.where(on_diag, w[..., :, :, None, :], jnp.zeros((), w.dtype))
    return blocks.reshape(lead + (2, MXU_DIM, MXU_DIM))


def kernel(x, c, ctx, c_ctx, norm_g, w_mod, b_mod, w_in, w_out, q_norm_g, k_norm_g, rpb, conv_w, conv_b,
           lru_wa, lru_ba, lru_wx, lru_bx, lru_lam):
    bsz, s, d = x.shape
    assert w_in.shape[0] == 1 and d == D_MODEL and s % (GRID_W * Q_ROWS) == 0
    aw, lw = ATTN_WIDTH, LRU_WIDTH

    pad_rows = 2 * SUBLANES - bsz - 1
    cc = jnp.concatenate([c, c_ctx[None, :], jnp.zeros((pad_rows, d), F32)], axis=0)
    mod = _modulation(cc, w_mod[0], b_mod[0][None, :])
    shift, scale, gate = [mod[:bsz, i * d:(i + 1) * d][:, None, :] for i in range(3)]
    shift_c, scale_c = [mod[bsz:bsz + 1, i * d:(i + 1) * d][:, None, :] for i in range(2)]

    w = w_in[0]
    w_q, w_k, w_v, w_za, w_u, w_zl = [w[:, i * aw:(i + 1) * aw].astype(BF16) for i in range(6)]
    w_bf = jnp.concatenate([w_q, w_k, w_u, w_zl], axis=1)
    wt_bf = jnp.concatenate([w_v, w_za], axis=1).T
    w_ctx_bf = jnp.concatenate([w_k, w_u], axis=1)
    wt_ctx_bf = w_v.T
    g = norm_g[0][None, :]
    cos_t, sin_t = _rope_tables(s)
    blk = jnp.arange(MXU_DIM) // HEAD_DIM
    bd = jnp.where(blk[:, None] == blk[None, :], 1.0 / HEAD_DIM, 0.0).astype(BF16)
    qg = jnp.tile(q_norm_g[0] * (HEAD_DIM ** -0.5 * LOG2E), N_HEADS)[None, :]
    kg = jnp.tile(k_norm_g[0], N_HEADS)[None, :]

    cw, cb = conv_w[0], conv_b[0][None, :]
    qr, qp, kr, vt, zat, u, zl = _inproj(x, g, scale, shift, w_bf, wt_bf, cos_t, sin_t, bd, qg, kg, cw, cb)
    kc, vct, u_c = _ctxproj(ctx, g, scale_c, shift_c, w_ctx_bf, wt_ctx_bf, bd, kg, cw, cb)

    bias = _bias_tables(rpb[0])
    agt = _attention(qr, qp, kr, vt, kc, vct, zat, bias)

    wg = _block_diag_gates(jnp.stack([lru_wa[0], lru_wx[0]], axis=1))
    wg = (0.5 * wg).astype(BF16)
    yg = _rglru(u, u_c, zl, wg, lru_ba[0][:, None, :], lru_bx[0][:, None, :], lru_lam[0][:, None, :])

    return _outproj(x, agt, yg, w_out[0].astype(BF16), gate)
```

```python
import functools
import itertools
import math

import jax
import jax.numpy as jnp
from jax import lax
from jax.experimental import pallas as pl
from jax.experimental.pallas import tpu as pltpu

F32 = jnp.float32
BF16 = jnp.bfloat16

D_MODEL = 1024
GRID_W = 64
HEAD_DIM = 64
ATTN_WIDTH = 512
LRU_WIDTH = 512
N_HEADS = ATTN_WIDTH // HEAD_DIM
LRU_BLOCK = 64
NA_ROWS = 8
NA_COLS = 16
CONV_WIDTH = 4
LRU_C = 8.0
ROPE_BASE = 10000.0
EPS = 1e-6
NEG_INF = -1e30
LOG2E = math.log2(math.e)

LANES = 128
SUBLANES = 8
MXU_DIM = 256
VMEM_LIMIT_BYTES = 58 * 1024 * 1024

Q_ROWS = 4
BAND_ROWS = Q_ROWS + NA_ROWS
Q_BLK = Q_ROWS * GRID_W
BAND = BAND_ROWS * GRID_W
KEY_CHUNK = 256
ONES_ROWS = 16
N_BIAS_ROWS = 2 * NA_ROWS - 1
N_BIAS_COLS = 2 * NA_COLS - 1
TOE_W = 1024

LRU_CHUNK = 256
LRU_BWD_CHUNK = 128
LRU_SLABS = LRU_WIDTH // LANES
CONV_ROWS = 128
HALO = 16


def _dot(a, b):
    return jnp.dot(a, b, preferred_element_type=F32)


def _dot_nt(a, b):
    return lax.dot_general(a, b, (((1,), (1,)), ((), ())), preferred_element_type=F32)


def _params(*semantics):
    return pltpu.CompilerParams(dimension_semantics=semantics, vmem_limit_bytes=VMEM_LIMIT_BYTES)


def _split_bf16(t):
    hi = t.astype(BF16)
    lo = (t - hi.astype(F32)).astype(BF16)
    return hi, lo


def _mod_kernel(cc_ref, w_ref, b_ref, o_ref):
    cc = cc_ref[...]
    s = cc * jax.nn.sigmoid(cc)
    s_hi, s_lo = _split_bf16(s)
    w_hi, w_lo = _split_bf16(w_ref[...])
    o_ref[...] = _dot(s_hi, w_hi) + _dot(s_lo, w_hi) + _dot(s_hi, w_lo) + b_ref[...]


def _modulation(cc, w_mod, b_mod):
    rows, d = cc.shape
    n = w_mod.shape[1]
    tn = 512
    return pl.pallas_call(
        _mod_kernel,
        grid=(n // tn,),
        in_specs=[pl.BlockSpec((rows, d), lambda j: (0, 0)),
                  pl.BlockSpec((d, tn), lambda j: (0, j)),
                  pl.BlockSpec((1, tn), lambda j: (0, j))],
        out_specs=pl.BlockSpec((rows, tn), lambda j: (0, j)),
        out_shape=jax.ShapeDtypeStruct((rows, n), F32),
        compiler_params=_params("arbitrary"),
        name="modulation",
    )(cc, w_mod, b_mod)


def _bias_kernel(rexp_ref, o_ref):
    k = lax.broadcasted_iota(jnp.int32, (GRID_W, TOE_W), 0)
    q = lax.broadcasted_iota(jnp.int32, (GRID_W, TOE_W), 1) % GRID_W
    diff = k - q + (NA_COLS - 1)
    toe = jnp.zeros((GRID_W, TOE_W), F32)
    for d in range(N_BIAS_COLS):
        toe = jnp.where(diff == d, rexp_ref[0, d:d + 1, :], toe)
    col_start = jnp.clip(q - NA_COLS // 2, 0, GRID_W - NA_COLS)
    col_ok = (k >= col_start) & (k < col_start + NA_COLS)
    toe = jnp.where(col_ok, toe, NEG_INF)
    ri = lax.broadcasted_iota(jnp.int32, (GRID_W, Q_BLK), 1) // GRID_W
    masked = jnp.full((GRID_W, Q_BLK), NEG_INF, F32)

    for typ in range(3):
        off = (NA_ROWS - 1, NA_ROWS // 2 - 1, NA_ROWS - BAND_ROWS + Q_ROWS - 1)[typ]
        for jj in range(BAND_ROWS):
            if typ == 0:
                lo, hi = (0, Q_ROWS) if jj < NA_ROWS else (0, 0)
            elif typ == 1:
                lo, hi = max(jj - NA_ROWS + 1, 0), min(jj, Q_ROWS - 1) + 1
            else:
                lo, hi = (0, Q_ROWS) if jj >= BAND_ROWS - NA_ROWS else (0, 0)
            strip = masked
            if lo < hi:
                e0 = N_BIAS_ROWS - 1 - jj - off
                rolled = pltpu.roll(toe, (-e0 * GRID_W) % TOE_W, axis=1)[:, :Q_BLK]
                strip = jnp.where((ri >= lo) & (ri < hi), rolled, NEG_INF)
            o_ref[0, typ, jj * GRID_W:(jj + 1) * GRID_W, :] = strip


def _bias_tables(rpb):
    rexp = jnp.repeat(jnp.transpose(rpb[:, ::-1, :], (0, 2, 1)) * LOG2E, GRID_W, axis=2)
    rexp = jnp.pad(rexp, ((0, 0), (0, 0), (0, TOE_W - N_BIAS_ROWS * GRID_W)))
    return pl.pallas_call(
        _bias_kernel,
        grid=(N_HEADS,),
        in_specs=[pl.BlockSpec((1, N_BIAS_COLS, TOE_W), lambda h: (h, 0, 0))],
        out_specs=pl.BlockSpec((1, 3, BAND, Q_BLK), lambda h: (h, 0, 0, 0)),
        out_shape=jax.ShapeDtypeStruct((N_HEADS, 3, BAND, Q_BLK), F32),
        compiler_params=_params("arbitrary"),
        name="bias_tables",
    )(rexp)


def _adaln(x, g_ref, scale_ref, shift_ref):
    ms = jnp.mean(x * x, axis=-1, keepdims=True)
    gm = g_ref[...] * (1.0 + scale_ref[0])
    return ((x * lax.rsqrt(ms + EPS)) * gm + shift_ref[0]).astype(BF16)


def _head_rms(t, bd_ref, gain):
    sq = (t * t).astype(BF16)
    bd = bd_ref[...]
    m = jnp.concatenate([_dot(sq[:, :MXU_DIM], bd), _dot(sq[:, MXU_DIM:], bd)], axis=-1)
    return t * lax.rsqrt(m + EPS) * gain


def _rope(t, cos_ref, sin_ref):
    lane = lax.broadcasted_iota(jnp.int32, (1, LANES), 1)
    first = (lane % 32) < 16
    cos, sin = cos_ref[...], sin_ref[...]
    parts = []
    for j in range(t.shape[1] // LANES):
        c = t[:, j * LANES:(j + 1) * LANES]
        swapped = jnp.where(first, pltpu.roll(c, LANES - 16, axis=1), pltpu.roll(c, 16, axis=1))
        parts.append(c * cos + swapped * sin)
    return jnp.concatenate(parts, axis=-1)


def _silu(t):
    return t * jax.nn.sigmoid(t)


def _conv_window(ext, n, cw_ref, cb_ref):
    rows = ext.shape[0]

    def shifted(k):
        return pltpu.roll(ext, (rows - k) % rows, axis=0)[HALO:HALO + n]

    y = cb_ref[...] + cw_ref[0:1, :] * shifted(-1)
    y = y + cw_ref[1:2, :] * ext[HALO:HALO + n]
    y = y + cw_ref[2:3, :] * shifted(1)
    return y + cw_ref[3:4, :] * shifted(2)


def _inproj_kernel(x_ref, xp_ref, xn_ref, g_ref, scale_ref, shift_ref, w_ref, wt_ref, cos_ref, sin_ref, bd_ref,
                   qg_ref, kg_ref, cw_ref, cb_ref, qr_ref, qp_ref, kr_ref, vt_ref, zat_ref, uc_ref, zl_ref):
    tm = x_ref.shape[1]
    aw = ATTN_WIDTH
    i, n_tiles = pl.program_id(0), pl.num_programs(0)
    hb_ext = jnp.concatenate([_adaln(xp_ref[0], g_ref, scale_ref, shift_ref),
                              _adaln(x_ref[0], g_ref, scale_ref, shift_ref),
                              _adaln(xn_ref[0], g_ref, scale_ref, shift_ref)], axis=0)
    hb = hb_ext[HALO:HALO + tm]

    def proj(j):
        return _dot(hb, w_ref[:, j * aw:(j + 1) * aw])

    u_ext = _dot(hb_ext, w_ref[:, 2 * aw:3 * aw])
    row = lax.broadcasted_iota(jnp.int32, (tm + 2 * HALO, 1), 0)
    inside = ((row >= HALO) | (i > 0)) & ((row < HALO + tm) | (i < n_tiles - 1))
    u_ext = jnp.where(inside, u_ext, 0.0)

    qn = _head_rms(proj(0), bd_ref, qg_ref[...])
    qp_ref[0] = qn.astype(BF16)
    qr_ref[0] = _rope(qn, cos_ref, sin_ref).astype(BF16)
    kn = _head_rms(proj(1), bd_ref, kg_ref[...])
    kr_ref[0] = _rope(kn, cos_ref, sin_ref).astype(BF16)

    zl_ref[0] = _silu(proj(3)).astype(BF16)
    vt_ref[0] = _dot_nt(wt_ref[:aw, :], hb).astype(BF16)
    zat_ref[0] = _silu(_dot_nt(wt_ref[aw:, :], hb)).astype(BF16)
    for r in range(0, tm, CONV_ROWS):
        uc_ref[0, r:r + CONV_ROWS, :] = _conv_window(u_ext[r:r + CONV_ROWS + 2 * HALO], CONV_ROWS, cw_ref, cb_ref)


def _inproj(x, g, scale, shift, w_bf, wt_bf, cos_t, sin_t, bd, qg, kg, conv_w, conv_b, tm=512):
    b, s, d = x.shape
    aw = ATTN_WIDTH
    tok = lambda i, j: (j, i, 0)
    per_b = lambda i, j: (j, 0, 0)
    const = lambda i, j: (0, 0)
    halo_per_tile = tm // HALO
    n_halo = s // HALO
    out_bf = jax.ShapeDtypeStruct((b, s, aw), BF16)
    out_t = jax.ShapeDtypeStruct((b, aw, s), BF16)
    out_f32 = jax.ShapeDtypeStruct((b, s, aw), F32)
    blk = pl.BlockSpec((1, tm, aw), tok)
    blk_t = pl.BlockSpec((1, aw, tm), lambda i, j: (j, 0, i))
    return pl.pallas_call(
        _inproj_kernel,
        grid=(s // tm, b),
        in_specs=[pl.BlockSpec((1, tm, d), tok),
                  pl.BlockSpec((1, HALO, d), lambda i, j: (j, jnp.maximum(i * halo_per_tile - 1, 0), 0)),
                  pl.BlockSpec((1, HALO, d), lambda i, j: (j, jnp.minimum((i + 1) * halo_per_tile, n_halo - 1), 0)),
                  pl.BlockSpec((1, d), const),
                  pl.BlockSpec((1, 1, d), per_b),
                  pl.BlockSpec((1, 1, d), per_b),
                  pl.BlockSpec(w_bf.shape, const),
                  pl.BlockSpec(wt_bf.shape, const),
                  pl.BlockSpec((tm, LANES), lambda i, j: (i, 0)),
                  pl.BlockSpec((tm, LANES), lambda i, j: (i, 0)),
                  pl.BlockSpec((MXU_DIM, MXU_DIM), const),
                  pl.BlockSpec((1, aw), const),
                  pl.BlockSpec((1, aw), const),
                  pl.BlockSpec((CONV_WIDTH, aw), const),
                  pl.BlockSpec((1, aw), const)],
        out_specs=[blk, blk, blk, blk_t, blk_t, blk, blk],
        out_shape=[out_bf, out_bf, out_bf, out_t, out_t, out_f32, out_bf],
        compiler_params=_params("arbitrary", "arbitrary"),
        name="inproj",
    )(x, x, x, g, scale, shift, w_bf, wt_bf, cos_t, sin_t, bd, qg, kg, conv_w, conv_b)


def _ctxproj_kernel(c_ref, g_ref, scale_ref, shift_ref, w_ref, wt_ref, bd_ref, kg_ref, cw_ref, cb_ref,
                    kc_ref, vct_ref, uc_ref):
    hb = _adaln(c_ref[0], g_ref, scale_ref, shift_ref)
    aw = ATTN_WIDTH
    kc_ref[0] = _head_rms(_dot(hb, w_ref[:, :aw]), bd_ref, kg_ref[...]).astype(BF16)
    u = _dot(hb, w_ref[:, aw:])
    pad = jnp.zeros((HALO, aw), F32)
    uc_ref[0] = _conv_window(jnp.concatenate([pad, u, pad], axis=0), u.shape[0], cw_ref, cb_ref)
    vct_ref[0] = _dot_nt(wt_ref[...], hb).astype(BF16)


def _ctxproj(ctx, g, scale_c, shift_c, w_ctx_bf, wt_ctx_bf, bd, kg, conv_w, conv_b):
    b, l, d = ctx.shape
    aw = ATTN_WIDTH
    const = lambda i: (0, 0)
    blk = pl.BlockSpec((1, l, aw), lambda i: (i, 0, 0))
    return pl.pallas_call(
        _ctxproj_kernel,
        grid=(b,),
        in_specs=[pl.BlockSpec((1, l, d), lambda i: (i, 0, 0)),
                  pl.BlockSpec((1, d), const),
                  pl.BlockSpec((1, 1, d), lambda i: (0, 0, 0)),
                  pl.BlockSpec((1, 1, d), lambda i: (0, 0, 0)),
                  pl.BlockSpec(w_ctx_bf.shape, const),
                  pl.BlockSpec(wt_ctx_bf.shape, const),
                  pl.BlockSpec((MXU_DIM, MXU_DIM), const),
                  pl.BlockSpec((1, aw), const),
                  pl.BlockSpec((CONV_WIDTH, aw), const),
                  pl.BlockSpec((1, aw), const)],
        out_specs=[blk, pl.BlockSpec((1, aw, l), lambda i: (i, 0, 0)), blk],
        out_shape=[jax.ShapeDtypeStruct((b, l, aw), BF16),
                   jax.ShapeDtypeStruct((b, aw, l), BF16),
                   jax.ShapeDtypeStruct((b, l, aw), F32)],
        compiler_params=_params("arbitrary"),
        name="ctxproj",
    )(ctx, g, scale_c, shift_c, w_ctx_bf, wt_ctx_bf, bd, kg, conv_w, conv_b)


def _attn_kernel(qr_ref, qp_ref, kr_ref, vt_ref, kc_ref, vct_ref, zat_ref, bias_ref, o_ref, s_buf0, s_buf1):
    rows = qr_ref.shape[1] // GRID_W
    n_blk = rows // Q_ROWS
    chunk_rows = KEY_CHUNK // GRID_W
    n_tiles = Q_BLK // LANES
    lane_head = lax.broadcasted_iota(jnp.int32, (1, LANES), 1) // HEAD_DIM
    s_bufs = (s_buf0, s_buf1)
    ones_rows = jnp.ones((ONES_ROWS, KEY_CHUNK), BF16)

    def in_window(typ, jj, tile):
        if typ == 0:
            return jj < NA_ROWS
        if typ == 2:
            return jj >= BAND_ROWS - NA_ROWS
        return any(0 <= jj - ri < NA_ROWS for ri in range(tile * Q_ROWS // n_tiles, (tile + 1) * Q_ROWS // n_tiles))

    def local_chunks(typ):
        return [c for c in range(BAND // KEY_CHUNK)
                if any(in_window(typ, c * chunk_rows + r, t) for r in range(chunk_rows) for t in range(n_tiles))]

    def sub_blocks(c):
        for r in range(chunk_rows):
            for t in range(n_tiles):
                yield c * chunk_rows + r, t, slice(r * GRID_W, (r + 1) * GRID_W), slice(t * LANES, (t + 1) * LANES)

    def item(rb):
        rb = jnp.asarray(rb, jnp.int32)
        q0 = pl.multiple_of(rb * Q_BLK, Q_BLK)
        band0 = pl.multiple_of(jnp.clip(rb * Q_ROWS - NA_ROWS // 2, 0, rows - BAND_ROWS) * GRID_W, Q_BLK)
        return q0, band0

    def col_max(m, blk, t):
        m = list(m)
        m[t] = jnp.maximum(m[t], jnp.max(blk, axis=0, keepdims=True))
        return tuple(m)

    def score_steps(rb, h, typ):
        q0, band0 = item(rb)
        s_buf = s_bufs[h]
        mine = lane_head == h
        qr = jnp.where(mine, qr_ref[0, pl.ds(q0, Q_BLK), :], jnp.zeros((), BF16))
        qp = jnp.where(mine, qp_ref[0, pl.ds(q0, Q_BLK), :], jnp.zeros((), BF16))

        def local(c, m):
            n_rows = 1 + max(r for r in range(chunk_rows)
                             if any(in_window(typ, c * chunk_rows + r, t) for t in range(n_tiles)))
            s = _dot_nt(kr_ref[0, pl.ds(band0 + c * KEY_CHUNK, n_rows * GRID_W), :], qr)
            for jj, t, r_sl, l_sl in sub_blocks(c):
                if in_window(typ, jj, t):
                    k_sl = slice(jj * GRID_W, (jj + 1) * GRID_W)
                    blk = s[r_sl, l_sl] + bias_ref[h, typ, k_sl, l_sl]
                    s_buf[k_sl, l_sl] = blk
                    m = col_max(m, blk, t)
            return m

        def context(m):
            s = _dot_nt(kc_ref[0], qp)
            s_buf[pl.ds(BAND, KEY_CHUNK), :] = s
            for t in range(n_tiles):
                m = col_max(m, s[:, t * LANES:(t + 1) * LANES], t)
            return m

        return [functools.partial(local, c) for c in local_chunks(typ)] + [context]

    def value_steps(rb, h, typ, m):
        q0, band0 = item(rb)
        s_buf = s_bufs[h]
        hd = pl.ds(h * HEAD_DIM, HEAD_DIM)

        def probs(k_sl, t):
            return jnp.exp2(s_buf[k_sl, t * LANES:(t + 1) * LANES] - m[t]).astype(BF16)

        def local(c, acc):
            zero_blk = jnp.zeros((GRID_W, LANES), BF16)
            row_blks = []
            for r in range(chunk_rows):
                jj = c * chunk_rows + r
                k_sl = slice(jj * GRID_W, (jj + 1) * GRID_W)
                row_blks.append(jnp.concatenate(
                    [probs(k_sl, t) if in_window(typ, jj, t) else zero_blk for t in range(n_tiles)], axis=1))
            p = jnp.concatenate(row_blks, axis=0)
            vt = jnp.concatenate([vt_ref[0, hd, pl.ds(band0 + c * KEY_CHUNK, KEY_CHUNK)], ones_rows], axis=0)
            return acc + _dot(vt, p)

        def context(acc):
            k_sl = slice(BAND, BAND + KEY_CHUNK)
            p = jnp.concatenate([probs(k_sl, t) for t in range(n_tiles)], axis=1)
            vt = jnp.concatenate([vct_ref[0, hd, :], ones_rows], axis=0)
            return acc + _dot(vt, p)

        return [functools.partial(local, c) for c in local_chunks(typ)] + [context]

    def write_out(rb, h, acc):
        q0, _ = item(rb)
        hd = pl.ds(h * HEAD_DIM, HEAD_DIM)
        gated = (acc[:HEAD_DIM] / acc[HEAD_DIM:HEAD_DIM + 1]) * zat_ref[0, hd, pl.ds(q0, Q_BLK)].astype(F32)
        o_ref[0, hd, pl.ds(q0, Q_BLK)] = gated.astype(o_ref.dtype)

    m_init = (jnp.full((1, LANES), -jnp.inf, F32),) * n_tiles
    acc_init = jnp.zeros((HEAD_DIM + ONES_ROWS, Q_BLK), F32)

    def overlapped(score_item, value_item, out_item, carry):
        m_prev, acc_prev = carry
        s_steps = score_steps(*score_item) if score_item else []
        v_steps = value_steps(*value_item, m_prev) if value_item else []
        m, acc = m_init, acc_init
        for i, (s_step, v_step) in enumerate(itertools.zip_longest(s_steps, v_steps)):
            if s_step:
                m = s_step(m)
            if i == 0 and out_item:
                write_out(*out_item, acc_prev)
            if v_step:
                acc = v_step(acc)
        return m, acc

    first, mid, last = 0, 1, 2
    carry = overlapped((0, 0, first), None, None, (None, None))
    carry = overlapped((0, 1, first), (0, 0, first), None, carry)
    carry = overlapped((1, 0, mid), (0, 1, first), (0, 0), carry)

    def body(rb, carry):
        carry = overlapped((rb, 1, mid), (rb, 0, mid), (rb - 1, 1), carry)
        return overlapped((rb + 1, 0, mid), (rb, 1, mid), (rb, 0), carry)

    carry = lax.fori_loop(1, n_blk - 2, body, carry, unroll=3)
    carry = overlapped((n_blk - 2, 1, mid), (n_blk - 2, 0, mid), (n_blk - 3, 1), carry)
    carry = overlapped((n_blk - 1, 0, last), (n_blk - 2, 1, mid), (n_blk - 2, 0), carry)
    carry = overlapped((n_blk - 1, 1, last), (n_blk - 1, 0, last), (n_blk - 2, 1), carry)
    carry = overlapped(None, (n_blk - 1, 1, last), (n_blk - 1, 0), carry)
    write_out(n_blk - 1, 1, carry[1])


def _attention(qr, qp, kr, vt, kc, vct, zat, bias):
    b, s, aw = qr.shape
    l = kc.shape[1]
    n_pairs = aw // LANES
    lat = pl.BlockSpec((1, s, LANES), lambda p, i: (i, 0, p))
    lat_t = pl.BlockSpec((1, LANES, s), lambda p, i: (i, p, 0))
    return pl.pallas_call(
        _attn_kernel,
        grid=(n_pairs, b),
        in_specs=[lat, lat, lat, lat_t,
                  pl.BlockSpec((1, l, LANES), lambda p, i: (i, 0, p)),
                  pl.BlockSpec((1, LANES, l), lambda p, i: (i, p, 0)),
                  lat_t,
                  pl.BlockSpec((2, 3, BAND, Q_BLK), lambda p, i: (p, 0, 0, 0))],
        out_specs=lat_t,
        out_shape=jax.ShapeDtypeStruct((b, aw, s), BF16),
        scratch_shapes=[pltpu.VMEM((BAND + l, Q_BLK), F32), pltpu.VMEM((BAND + l, Q_BLK), F32)],
        compiler_params=_params("arbitrary", "arbitrary"),
        name="attention",
    )(qr, qp, kr, vt, kc, vct, zat, bias)


def _lru_pitch(tc):
    pitch = tc + SUBLANES
    assert pitch % (2 * SUBLANES) == SUBLANES
    return pitch


def _lru_gate_consts(ba, bx, lam):
    return 0.5 * ba, 0.5 * bx, jax.nn.softplus(-lam) * (-0.5 * LRU_C * LOG2E)


def _lru_fill(bi, uc, wg, consts, a_s, b_s):
    half_ba, half_bx, half_log2a = consts
    tc = uc.shape[0]
    ucb = uc.astype(BF16)

    def gate_tanh(gi, half_bias):
        pre = jnp.concatenate([_dot(ucb[:, :MXU_DIM], wg(gi, 0)), _dot(ucb[:, MXU_DIM:], wg(gi, 1))], axis=-1)
        return jnp.tanh(pre + half_bias)

    tr = gate_tanh(0, half_ba)
    ti = gate_tanh(1, half_bx)
    a = jnp.exp2(tr * half_log2a + half_log2a)
    gap = 1.0 - a * a
    mult = gap * lax.rsqrt(jnp.maximum(gap, 1e-30))
    bb = mult * ((0.5 * ti + 0.5) * uc)
    r0 = pl.multiple_of(bi * _lru_pitch(tc), SUBLANES)
    for j in range(LRU_SLABS):
        a_s[j, pl.ds(r0, tc), :] = a[:, j * LANES:(j + 1) * LANES]
        b_s[j, pl.ds(r0, tc), :] = bb[:, j * LANES:(j + 1) * LANES]


def _lru_scan(reverse, tc, nb, a_s, b_s, h_s, carry_s):
    pitch = _lru_pitch(tc)

    n_groups = tc // SUBLANES

    def body(g, hs):
        base = pl.multiple_of(((n_groups - 1 - g) if reverse else g) * SUBLANES, SUBLANES)
        hs = list(hs)
        for k in (reversed(range(SUBLANES)) if reverse else range(SUBLANES)):
            rows = pl.ds(base + k, nb, stride=pitch)
            for j in range(LRU_SLABS):
                hs[j] = a_s[j, rows, :] * hs[j] + b_s[j, rows, :]
                h_s[j, rows, :] = hs[j]
        return tuple(hs)

    hs = tuple(carry_s[:, j * LANES:(j + 1) * LANES] for j in range(LRU_SLABS))
    hs = lax.fori_loop(0, n_groups, body, hs)
    for j in range(LRU_SLABS):
        carry_s[:, j * LANES:(j + 1) * LANES] = hs[j]


def _lru_rows(bi, tc, h_s):
    r0 = pl.multiple_of(bi * _lru_pitch(tc), SUBLANES)
    return jnp.concatenate([h_s[j, pl.ds(r0, tc), :] for j in range(LRU_SLABS)], axis=-1)


def _lru_fwd_kernel(u_ref, uc_ref, wg_ref, ba_ref, bx_ref, lam_ref, hf_ref, hb0_ref, a_s, b_s, h_s, carry_s):
    nb, tc = u_ref.shape[0], u_ref.shape[1]
    consts = [_lru_gate_consts(ba_ref[d], bx_ref[d], lam_ref[d]) for d in range(2)]

    def fill_all(src_ref, d, unroll):
        def step(bi, carry):
            _lru_fill(bi, src_ref[bi], lambda gi, hf: wg_ref[d, gi, hf], consts[d], a_s, b_s)
            return carry

        lax.fori_loop(0, nb, step, 0, unroll=unroll)

    @pl.when(pl.program_id(0) == 0)
    def _():
        carry_s[...] = jnp.zeros_like(carry_s)
        fill_all(uc_ref, 1, 1)
        _lru_scan(True, tc, nb, a_s, b_s, h_s, carry_s)
        hb0_ref[...] = carry_s[...]
        carry_s[...] = jnp.zeros_like(carry_s)
        fill_all(uc_ref, 0, 1)
        _lru_scan(False, tc, nb, a_s, b_s, h_s, carry_s)

    fill_all(u_ref, 0, 4)
    _lru_scan(False, tc, nb, a_s, b_s, h_s, carry_s)

    def emit(bi, carry):
        hf_ref[bi] = _lru_rows(bi, tc, h_s).astype(hf_ref.dtype)
        return carry

    lax.fori_loop(0, nb, emit, 0)


def _lru_fwd(u, u_c, wg, ba, bx, lam):
    b, s, lw = u.shape
    tc = LRU_CHUNK
    assert u_c.shape[1] == tc and s % tc == 0
    full = lambda *shape: pl.BlockSpec(shape, lambda i: (0,) * len(shape))
    cblk = pl.BlockSpec((b, tc, lw), lambda i: (0, i, 0))
    slab = pltpu.VMEM((LRU_SLABS, b * _lru_pitch(tc), LANES), F32)
    return pl.pallas_call(
        _lru_fwd_kernel,
        grid=(s // tc,),
        in_specs=[cblk, full(b, tc, lw), full(2, 2, 2, MXU_DIM, MXU_DIM), full(2, 1, lw), full(2, 1, lw), full(2, 1, lw)],
        out_specs=[cblk, full(b, lw)],
        out_shape=[jax.ShapeDtypeStruct((b, s, lw), BF16), jax.ShapeDtypeStruct((b, lw), F32)],
        scratch_shapes=[slab, slab, slab, pltpu.VMEM((b, lw), F32)],
        compiler_params=_params("arbitrary"),
        name="rglru_fwd",
    )(u, u_c, wg, ba, bx, lam)


def _lru_bwd_out_kernel(u_ref, hf_ref, zl_ref, agt_ref, x_ref, hb0_ref, gate_ref, wg_ref, ba_ref, bx_ref, lam_ref,
                        wo_ref, o_ref, a_s, b_s, h_s, carry_s):
    nb, tc = u_ref.shape[0], u_ref.shape[1]
    consts = _lru_gate_consts(ba_ref[...], bx_ref[...], lam_ref[...])

    @pl.when(pl.program_id(0) == 0)
    def _():
        carry_s[...] = hb0_ref[...]

    def fill(bi, carry):
        _lru_fill(bi, u_ref[bi], lambda gi, hf: wg_ref[gi, hf], consts, a_s, b_s)
        return carry

    lax.fori_loop(0, nb, fill, 0, unroll=4)
    _lru_scan(True, tc, nb, a_s, b_s, h_s, carry_s)

    def emit(pi, carry):
        pair = (2 * pi, 2 * pi + 1)
        y = jnp.concatenate([((hf_ref[bi].astype(F32) + _lru_rows(bi, tc, h_s)) * zl_ref[bi].astype(F32)).astype(BF16)
                             for bi in pair], axis=0)
        at = jnp.concatenate([agt_ref[bi] for bi in pair], axis=1)
        mix = lax.dot_general(at, wo_ref[:ATTN_WIDTH, :], (((0,), (0,)), ((), ())), preferred_element_type=F32)
        mix = mix + _dot(y, wo_ref[ATTN_WIDTH:, :])
        for k, bi in enumerate(pair):
            o_ref[bi] = x_ref[bi] + gate_ref[bi] * mix[k * tc:(k + 1) * tc]
        return carry

    lax.fori_loop(0, nb // 2, emit, 0)


def _lru_bwd_out(u, hf, zl, agt, x, hb0, gate, wg, ba, bx, lam, wo_bf):
    b, s, lw = u.shape
    d = x.shape[2]
    tc = LRU_BWD_CHUNK
    n_chunks = s // tc
    assert s % tc == 0 and b % 2 == 0
    full = lambda *shape: pl.BlockSpec(shape, lambda i: (0,) * len(shape))
    rev = lambda i: n_chunks - 1 - i
    cblk = pl.BlockSpec((b, tc, lw), lambda i: (0, rev(i), 0))
    xblk = pl.BlockSpec((b, tc, d), lambda i: (0, rev(i), 0))
    slab = pltpu.VMEM((LRU_SLABS, b * _lru_pitch(tc), LANES), F32)
    return pl.pallas_call(
        _lru_bwd_out_kernel,
        grid=(n_chunks,),
        in_specs=[cblk, cblk, cblk,
                  pl.BlockSpec((b, ATTN_WIDTH, tc), lambda i: (0, 0, rev(i))),
                  xblk, full(b, lw), full(b, 1, d),
                  full(2, 2, MXU_DIM, MXU_DIM), full(1, lw), full(1, lw), full(1, lw),
                  full(ATTN_WIDTH + lw, d)],
        out_specs=xblk,
        out_shape=jax.ShapeDtypeStruct(x.shape, x.dtype),
        scratch_shapes=[slab, slab, slab, pltpu.VMEM((b, lw), F32)],
        compiler_params=_params("arbitrary"),
        name="rglru_bwd_outproj",
    )(u, hf, zl, agt, x, hb0, gate, wg, ba, bx, lam, wo_bf)


def _rope_tables(s):
    t = jnp.arange(s)
    half = HEAD_DIM // 4
    inv = ROPE_BASE ** (-jnp.arange(half, dtype=F32) / half)
    ang_r = (t // GRID_W).astype(F32)[:, None] * inv[None, :]
    ang_c = (t % GRID_W).astype(F32)[:, None] * inv[None, :]
    cos = jnp.concatenate([jnp.cos(ang_r)] * 2 + [jnp.cos(ang_c)] * 2, axis=-1)
    sin = jnp.concatenate([-jnp.sin(ang_r), jnp.sin(ang_r), -jnp.sin(ang_c), jnp.sin(ang_c)], axis=-1)
    return jnp.tile(cos, (1, LANES // HEAD_DIM)), jnp.tile(sin, (1, LANES // HEAD_DIM))


def _block_diag_gates(w):
    n = MXU_DIM // LRU_BLOCK
    lead = w.shape[:-3]
    w = w.reshape(lead + (2, n, LRU_BLOCK, LRU_BLOCK))
    on_diag = jnp.eye(n, dtype=bool)[:, None, :, None]
    blocks = jnp.where(on_diag, w[..., :, :, None, :], jnp.zeros((), w.dtype))
    return blocks.reshape(lead + (2, MXU_DIM, MXU_DIM))


def kernel(x, c, ctx, c_ctx, norm_g, w_mod, b_mod, w_in, w_out, q_norm_g, k_norm_g, rpb, conv_w, conv_b,
           lru_wa, lru_ba, lru_wx, lru_bx, lru_lam):
    bsz, s, d = x.shape
    assert w_in.shape[0] == 1 and d == D_MODEL and s % (GRID_W * Q_ROWS) == 0
    aw, lw = ATTN_WIDTH, LRU_WIDTH

    pad_rows = 2 * SUBLANES - bsz - 1
    cc = jnp.concatenate([c, c_ctx[None, :], jnp.zeros((pad_rows, d), F32)], axis=0)
    mod = _modulation(cc, w_mod[0], b_mod[0][None, :])
    shift, scale, gate = [mod[:bsz, i * d:(i + 1) * d][:, None, :] for i in range(3)]
    shift_c, scale_c = [mod[bsz:bsz + 1, i * d:(i + 1) * d][:, None, :] for i in range(2)]

    w = w_in[0]
    w_q, w_k, w_v, w_za, w_u, w_zl = [w[:, i * aw:(i + 1) * aw].astype(BF16) for i in range(6)]
    w_bf = jnp.concatenate([w_q, w_k, w_u, w_zl], axis=1)
    wt_bf = jnp.concatenate([w_v, w_za], axis=1).T
    w_ctx_bf = jnp.concatenate([w_k, w_u], axis=1)
    wt_ctx_bf = w_v.T
    g = norm_g[0][None, :]
    cos_t, sin_t = _rope_tables(s)
    blk = jnp.arange(MXU_DIM) // HEAD_DIM
    bd = jnp.where(blk[:, None] == blk[None, :], 1.0 / HEAD_DIM, 0.0).astype(BF16)
    qg = jnp.tile(q_norm_g[0] * (HEAD_DIM ** -0.5 * LOG2E), N_HEADS)[None, :]
    kg = jnp.tile(k_norm_g[0], N_HEADS)[None, :]

    cw, cb = conv_w[0], conv_b[0][None, :]
    qr, qp, kr, vt, zat, u, zl = _inproj(x, g, scale, shift, w_bf, wt_bf, cos_t, sin_t, bd, qg, kg, cw, cb)
    kc, vct, u_c = _ctxproj(ctx, g, scale_c, shift_c, w_ctx_bf, wt_ctx_bf, bd, kg, cw, cb)

    bias = _bias_tables(rpb[0])
    agt = _attention(qr, qp, kr, vt, kc, vct, zat, bias)

    wg = _block_diag_gates(jnp.stack([lru_wa[0], lru_wx[0]], axis=1))
    wg = (0.5 * wg).astype(BF16)
    ba, bx, lam = lru_ba[0][:, None, :], lru_bx[0][:, None, :], lru_lam[0][:, None, :]
    hf, hb0 = _lru_fwd(u, u_c, wg, ba, bx, lam)
    return _lru_bwd_out(u, hf, zl, agt, x, hb0, gate, wg[1], ba[1], bx[1], lam[1], w_out[0].astype(BF16))
```

```python
import functools
import itertools
import math

import jax
import jax.numpy as jnp
from jax import lax
from jax.experimental import pallas as pl
from jax.experimental.pallas import tpu as pltpu

F32 = jnp.float32
BF16 = jnp.bfloat16

D_MODEL = 1024
GRID_W = 64
HEAD_DIM = 64
ATTN_WIDTH = 512
LRU_WIDTH = 512
N_HEADS = ATTN_WIDTH // HEAD_DIM
LRU_BLOCK = 64
NA_ROWS = 8
NA_COLS = 16
CONV_WIDTH = 4
LRU_C = 8.0
ROPE_BASE = 10000.0
EPS = 1e-6
NEG_INF = -1e30
LOG2E = math.log2(math.e)

LANES = 128
SUBLANES = 8
MXU_DIM = 256
VMEM_LIMIT_BYTES = 58 * 1024 * 1024

Q_ROWS = 4
BAND_ROWS = Q_ROWS + NA_ROWS
Q_BLK = Q_ROWS * GRID_W
BAND = BAND_ROWS * GRID_W
KEY_CHUNK = 256
ONES_ROWS = 16
N_BIAS_ROWS = 2 * NA_ROWS - 1
N_BIAS_COLS = 2 * NA_COLS - 1
TOE_W = 1024

LRU_CHUNK = 256
LRU_BWD_CHUNK = 128
LRU_SLABS = LRU_WIDTH // LANES
CONV_ROWS = 128
COL_Q, COL_K, COL_V, COL_ZA, COL_U, COL_ZL = range(6)
HALO = 16


def _dot(a, b):
    return jnp.dot(a, b, preferred_element_type=F32)


def _dot_nt(a, b):
    return lax.dot_general(a, b, (((1,), (1,)), ((), ())), preferred_element_type=F32)


def _params(*semantics):
    return pltpu.CompilerParams(dimension_semantics=semantics, vmem_limit_bytes=VMEM_LIMIT_BYTES)


def _split_bf16(t):
    hi = t.astype(BF16)
    lo = (t - hi.astype(F32)).astype(BF16)
    return hi, lo


def _mod_kernel(cc_ref, w_ref, b_ref, o_ref):
    cc = cc_ref[...]
    s = cc * jax.nn.sigmoid(cc)
    s_hi, s_lo = _split_bf16(s)
    w_hi, w_lo = _split_bf16(w_ref[...])
    o_ref[...] = _dot(s_hi, w_hi) + _dot(s_lo, w_hi) + _dot(s_hi, w_lo) + b_ref[...]


def _modulation(cc, w_mod, b_mod):
    rows, d = cc.shape
    n = w_mod.shape[1]
    tn = 512
    return pl.pallas_call(
        _mod_kernel,
        grid=(n // tn,),
        in_specs=[pl.BlockSpec((rows, d), lambda j: (0, 0)),
                  pl.BlockSpec((d, tn), lambda j: (0, j)),
                  pl.BlockSpec((1, tn), lambda j: (0, j))],
        out_specs=pl.BlockSpec((rows, tn), lambda j: (0, j)),
        out_shape=jax.ShapeDtypeStruct((rows, n), F32),
        compiler_params=_params("arbitrary"),
        name="modulation",
    )(cc, w_mod, b_mod)


def _fill_bias_tables(rexp_ref, h, o_ref):
    k = lax.broadcasted_iota(jnp.int32, (GRID_W, TOE_W), 0)
    q = lax.broadcasted_iota(jnp.int32, (GRID_W, TOE_W), 1) % GRID_W
    diff = k - q + (NA_COLS - 1)
    toe = jnp.zeros((GRID_W, TOE_W), F32)
    for d in range(N_BIAS_COLS):
        toe = jnp.where(diff == d, rexp_ref[h, d:d + 1, :], toe)
    col_start = jnp.clip(q - NA_COLS // 2, 0, GRID_W - NA_COLS)
    col_ok = (k >= col_start) & (k < col_start + NA_COLS)
    toe = jnp.where(col_ok, toe, NEG_INF)
    ri = lax.broadcasted_iota(jnp.int32, (GRID_W, Q_BLK), 1) // GRID_W
    masked = jnp.full((GRID_W, Q_BLK), NEG_INF, F32)

    for typ in range(3):
        off = (NA_ROWS - 1, NA_ROWS // 2 - 1, NA_ROWS - BAND_ROWS + Q_ROWS - 1)[typ]
        for jj in range(BAND_ROWS):
            if typ == 0:
                lo, hi = (0, Q_ROWS) if jj < NA_ROWS else (0, 0)
            elif typ == 1:
                lo, hi = max(jj - NA_ROWS + 1, 0), min(jj, Q_ROWS - 1) + 1
            else:
                lo, hi = (0, Q_ROWS) if jj >= BAND_ROWS - NA_ROWS else (0, 0)
            strip = masked
            if lo < hi:
                e0 = N_BIAS_ROWS - 1 - jj - off
                rolled = pltpu.roll(toe, (-e0 * GRID_W) % TOE_W, axis=1)[:, :Q_BLK]
                strip = jnp.where((ri >= lo) & (ri < hi), rolled, NEG_INF)
            o_ref[h, typ, jj * GRID_W:(jj + 1) * GRID_W, :] = strip


def _bias_rows(rpb):
    rexp = jnp.repeat(jnp.transpose(rpb[:, ::-1, :], (0, 2, 1)) * LOG2E, GRID_W, axis=2)
    return jnp.pad(rexp, ((0, 0), (0, 0), (0, TOE_W - N_BIAS_ROWS * GRID_W)))


def _adaln(x, g_ref, scale_ref, shift_ref):
    ms = jnp.mean(x * x, axis=-1, keepdims=True)
    gm = g_ref[...] * (1.0 + scale_ref[0])
    return ((x * lax.rsqrt(ms + EPS)) * gm + shift_ref[0]).astype(BF16)


def _head_rms(t, bd_ref, gain):
    sq = (t * t).astype(BF16)
    bd = bd_ref[...]
    m = jnp.concatenate([_dot(sq[:, :MXU_DIM], bd), _dot(sq[:, MXU_DIM:], bd)], axis=-1)
    return t * lax.rsqrt(m + EPS) * gain


def _rope(t, cos_ref, sin_ref):
    lane = lax.broadcasted_iota(jnp.int32, (1, LANES), 1)
    first = (lane % 32) < 16
    cos, sin = cos_ref[...], sin_ref[...]
    parts = []
    for j in range(t.shape[1] // LANES):
        c = t[:, j * LANES:(j + 1) * LANES]
        swapped = jnp.where(first, pltpu.roll(c, LANES - 16, axis=1), pltpu.roll(c, 16, axis=1))
        parts.append(c * cos + swapped * sin)
    return jnp.concatenate(parts, axis=-1)


def _silu(t):
    return t * jax.nn.sigmoid(t)


def _conv_window(ext, n, cw_ref, cb_ref):
    rows = ext.shape[0]

    def shifted(k):
        return pltpu.roll(ext, (rows - k) % rows, axis=0)[HALO:HALO + n]

    y = cb_ref[...] + cw_ref[0:1, :] * shifted(-1)
    y = y + cw_ref[1:2, :] * ext[HALO:HALO + n]
    y = y + cw_ref[2:3, :] * shifted(1)
    return y + cw_ref[3:4, :] * shifted(2)


def _inproj_kernel(x_ref, xp_ref, xn_ref, g_ref, scale_ref, shift_ref, w_ref, wt_ref, cos_ref, sin_ref, bd_ref,
                   qg_ref, kg_ref, cw_ref, cb_ref, qr_ref, qp_ref, kr_ref, vt_ref, zat_ref, uc_ref, zl_ref):
    tm = x_ref.shape[1]
    aw = ATTN_WIDTH
    i, n_tiles = pl.program_id(0), pl.num_programs(0)
    hb_ext = jnp.concatenate([_adaln(xp_ref[0], g_ref, scale_ref, shift_ref),
                              _adaln(x_ref[0], g_ref, scale_ref, shift_ref),
                              _adaln(xn_ref[0], g_ref, scale_ref, shift_ref)], axis=0)
    hb = hb_ext[HALO:HALO + tm]

    def cols(j):
        return w_ref[:, j * aw:(j + 1) * aw]

    u_ext = _dot(hb_ext, cols(COL_U))
    row = lax.broadcasted_iota(jnp.int32, (tm + 2 * HALO, 1), 0)
    inside = ((row >= HALO) | (i > 0)) & ((row < HALO + tm) | (i < n_tiles - 1))
    u_ext = jnp.where(inside, u_ext, 0.0)

    qn = _head_rms(_dot(hb, cols(COL_Q)), bd_ref, qg_ref[...])
    qp_ref[0] = qn.astype(BF16)
    qr_ref[0] = _rope(qn, cos_ref, sin_ref).astype(BF16)
    kn = _head_rms(_dot(hb, cols(COL_K)), bd_ref, kg_ref[...])
    kr_ref[0] = _rope(kn, cos_ref, sin_ref).astype(BF16)

    zl_ref[0] = _silu(_dot(hb, cols(COL_ZL))).astype(BF16)
    vt_ref[0] = _dot_nt(wt_ref[:aw, :], hb).astype(BF16)
    zat_ref[0] = _silu(_dot_nt(wt_ref[aw:, :], hb)).astype(BF16)
    for r in range(0, tm, CONV_ROWS):
        uc_ref[0, r:r + CONV_ROWS, :] = _conv_window(u_ext[r:r + CONV_ROWS + 2 * HALO], CONV_ROWS, cw_ref, cb_ref)


def _inproj(x, g, scale, shift, w_bf, wt_bf, cos_t, sin_t, bd, qg, kg, conv_w, conv_b, tm=512):
    b, s, d = x.shape
    aw = ATTN_WIDTH
    tok = lambda i, j: (j, i, 0)
    per_b = lambda i, j: (j, 0, 0)
    const = lambda i, j: (0, 0)
    halo_per_tile = tm // HALO
    n_halo = s // HALO
    out_bf = jax.ShapeDtypeStruct((b, s, aw), BF16)
    out_t = jax.ShapeDtypeStruct((b, aw, s), BF16)
    out_f32 = jax.ShapeDtypeStruct((b, s, aw), F32)
    blk = pl.BlockSpec((1, tm, aw), tok)
    blk_t = pl.BlockSpec((1, aw, tm), lambda i, j: (j, 0, i))
    return pl.pallas_call(
        _inproj_kernel,
        grid=(s // tm, b),
        in_specs=[pl.BlockSpec((1, tm, d), tok),
                  pl.BlockSpec((1, HALO, d), lambda i, j: (j, jnp.maximum(i * halo_per_tile - 1, 0), 0)),
                  pl.BlockSpec((1, HALO, d), lambda i, j: (j, jnp.minimum((i + 1) * halo_per_tile, n_halo - 1), 0)),
                  pl.BlockSpec((1, d), const),
                  pl.BlockSpec((1, 1, d), per_b),
                  pl.BlockSpec((1, 1, d), per_b),
                  pl.BlockSpec(w_bf.shape, const),
                  pl.BlockSpec(wt_bf.shape, const),
                  pl.BlockSpec((tm, LANES), lambda i, j: (i, 0)),
                  pl.BlockSpec((tm, LANES), lambda i, j: (i, 0)),
                  pl.BlockSpec((MXU_DIM, MXU_DIM), const),
                  pl.BlockSpec((1, aw), const),
                  pl.BlockSpec((1, aw), const),
                  pl.BlockSpec((CONV_WIDTH, aw), const),
                  pl.BlockSpec((1, aw), const)],
        out_specs=[blk, blk, blk, blk_t, blk_t, blk, blk],
        out_shape=[out_bf, out_bf, out_bf, out_t, out_t, out_f32, out_bf],
        compiler_params=_params("arbitrary", "arbitrary"),
        name="inproj",
    )(x, x, x, g, scale, shift, w_bf, wt_bf, cos_t, sin_t, bd, qg, kg, conv_w, conv_b)


def _ctxproj_kernel(c_ref, g_ref, scale_ref, shift_ref, w_ref, wt_ref, bd_ref, kg_ref, cw_ref, cb_ref,
                    kc_ref, vct_ref, uc_ref):
    nb, l, d = c_ref.shape
    aw = ATTN_WIDTH
    hb = _adaln(c_ref[...].reshape(nb * l, d), g_ref, scale_ref, shift_ref)
    kc = _head_rms(_dot(hb, w_ref[:, COL_K * aw:(COL_K + 1) * aw]), bd_ref, kg_ref[...]).astype(BF16)
    u = _dot(hb, w_ref[:, COL_U * aw:(COL_U + 1) * aw])
    vct = _dot_nt(wt_ref[:aw, :], hb).astype(BF16)
    pad = jnp.zeros((HALO, aw), F32)
    for bi in range(nb):
        kc_ref[bi] = kc[bi * l:(bi + 1) * l]
        uc_ref[bi] = _conv_window(jnp.concatenate([pad, u[bi * l:(bi + 1) * l], pad], axis=0), l, cw_ref, cb_ref)
        vct_ref[bi] = vct[:, bi * l:(bi + 1) * l]


def _ctxproj(ctx, g, scale_c, shift_c, w_ctx_bf, wt_ctx_bf, bd, kg, conv_w, conv_b, nb=4):
    b, l, d = ctx.shape
    aw = ATTN_WIDTH
    const = lambda i: (0, 0)
    blk = pl.BlockSpec((nb, l, aw), lambda i: (i, 0, 0))
    return pl.pallas_call(
        _ctxproj_kernel,
        grid=(b // nb,),
        in_specs=[pl.BlockSpec((nb, l, d), lambda i: (i, 0, 0)),
                  pl.BlockSpec((1, d), const),
                  pl.BlockSpec((1, 1, d), lambda i: (0, 0, 0)),
                  pl.BlockSpec((1, 1, d), lambda i: (0, 0, 0)),
                  pl.BlockSpec(w_ctx_bf.shape, const),
                  pl.BlockSpec(wt_ctx_bf.shape, const),
                  pl.BlockSpec((MXU_DIM, MXU_DIM), const),
                  pl.BlockSpec((1, aw), const),
                  pl.BlockSpec((CONV_WIDTH, aw), const),
                  pl.BlockSpec((1, aw), const)],
        out_specs=[blk, pl.BlockSpec((nb, aw, l), lambda i: (i, 0, 0)), blk],
        out_shape=[jax.ShapeDtypeStruct((b, l, aw), BF16),
                   jax.ShapeDtypeStruct((b, aw, l), BF16),
                   jax.ShapeDtypeStruct((b, l, aw), F32)],
        compiler_params=_params("arbitrary"),
        name="ctxproj",
    )(ctx, g, scale_c, shift_c, w_ctx_bf, wt_ctx_bf, bd, kg, conv_w, conv_b)


def _attn_kernel(qr_ref, qp_ref, kr_ref, vt_ref, kc_ref, vct_ref, zat_ref, rexp_ref, o_ref, bias_ref, s_buf0, s_buf1):
    @pl.when(pl.program_id(1) == 0)
    def _():
        for h in range(2):
            _fill_bias_tables(rexp_ref, h, bias_ref)

    rows = qr_ref.shape[1] // GRID_W
    n_blk = rows // Q_ROWS
    chunk_rows = KEY_CHUNK // GRID_W
    n_tiles = Q_BLK // LANES
    lane_head = lax.broadcasted_iota(jnp.int32, (1, LANES), 1) // HEAD_DIM
    s_bufs = (s_buf0, s_buf1)
    ones_rows = jnp.ones((ONES_ROWS, KEY_CHUNK), BF16)

    def in_window(typ, jj, tile):
        if typ == 0:
            return jj < NA_ROWS
        if typ == 2:
            return jj >= BAND_ROWS - NA_ROWS
        return any(0 <= jj - ri < NA_ROWS for ri in range(tile * Q_ROWS // n_tiles, (tile + 1) * Q_ROWS // n_tiles))

    def local_chunks(typ):
        return [c for c in range(BAND // KEY_CHUNK)
                if any(in_window(typ, c * chunk_rows + r, t) for r in range(chunk_rows) for t in range(n_tiles))]

    def sub_blocks(c):
        for r in range(chunk_rows):
            for t in range(n_tiles):
                yield c * chunk_rows + r, t, slice(r * GRID_W, (r + 1) * GRID_W), slice(t * LANES, (t + 1) * LANES)

    def item(rb):
        rb = jnp.asarray(rb, jnp.int32)
        q0 = pl.multiple_of(rb * Q_BLK, Q_BLK)
        band0 = pl.multiple_of(jnp.clip(rb * Q_ROWS - NA_ROWS // 2, 0, rows - BAND_ROWS) * GRID_W, Q_BLK)
        return q0, band0

    def col_max(m, blk, t):
        m = list(m)
        m[t] = jnp.maximum(m[t], jnp.max(blk, axis=0, keepdims=True))
        return tuple(m)

    def score_steps(rb, h, typ):
        q0, band0 = item(rb)
        s_buf = s_bufs[h]
        mine = lane_head == h
        qr = jnp.where(mine, qr_ref[0, pl.ds(q0, Q_BLK), :], jnp.zeros((), BF16))
        qp = jnp.where(mine, qp_ref[0, pl.ds(q0, Q_BLK), :], jnp.zeros((), BF16))

        def local(c, m):
            n_rows = 1 + max(r for r in range(chunk_rows)
                             if any(in_window(typ, c * chunk_rows + r, t) for t in range(n_tiles)))
            s = _dot_nt(kr_ref[0, pl.ds(band0 + c * KEY_CHUNK, n_rows * GRID_W), :], qr)
            for jj, t, r_sl, l_sl in sub_blocks(c):
                if in_window(typ, jj, t):
                    k_sl = slice(jj * GRID_W, (jj + 1) * GRID_W)
                    blk = s[r_sl, l_sl] + bias_ref[h, typ, k_sl, l_sl]
                    s_buf[k_sl, l_sl] = blk
                    m = col_max(m, blk, t)
            return m

        def context(m):
            s = _dot_nt(kc_ref[0], qp)
            s_buf[pl.ds(BAND, KEY_CHUNK), :] = s
            for t in range(n_tiles):
                m = col_max(m, s[:, t * LANES:(t + 1) * LANES], t)
            return m

        return [functools.partial(local, c) for c in local_chunks(typ)] + [context]

    def value_steps(rb, h, typ, m):
        q0, band0 = item(rb)
        s_buf = s_bufs[h]
        hd = pl.ds(h * HEAD_DIM, HEAD_DIM)

        def probs(k_sl, t):
            return jnp.exp2(s_buf[k_sl, t * LANES:(t + 1) * LANES] - m[t]).astype(BF16)

        def local(c, acc):
            zero_blk = jnp.zeros((GRID_W, LANES), BF16)
            row_blks = []
            for r in range(chunk_rows):
                jj = c * chunk_rows + r
                k_sl = slice(jj * GRID_W, (jj + 1) * GRID_W)
                row_blks.append(jnp.concatenate(
                    [probs(k_sl, t) if in_window(typ, jj, t) else zero_blk for t in range(n_tiles)], axis=1))
            p = jnp.concatenate(row_blks, axis=0)
            vt = jnp.concatenate([vt_ref[0, hd, pl.ds(band0 + c * KEY_CHUNK, KEY_CHUNK)], ones_rows], axis=0)
            return acc + _dot(vt, p)

        def context(acc):
            k_sl = slice(BAND, BAND + KEY_CHUNK)
            p = jnp.concatenate([probs(k_sl, t) for t in range(n_tiles)], axis=1)
            vt = jnp.concatenate([vct_ref[0, hd, :], ones_rows], axis=0)
            return acc + _dot(vt, p)

        return [functools.partial(local, c) for c in local_chunks(typ)] + [context]

    def write_out(rb, h, acc):
        q0, _ = item(rb)
        hd = pl.ds(h * HEAD_DIM, HEAD_DIM)
        gated = (acc[:HEAD_DIM] / acc[HEAD_DIM:HEAD_DIM + 1]) * zat_ref[0, hd, pl.ds(q0, Q_BLK)].astype(F32)
        o_ref[0, hd, pl.ds(q0, Q_BLK)] = gated.astype(o_ref.dtype)

    m_init = (jnp.full((1, LANES), -jnp.inf, F32),) * n_tiles
    acc_init = jnp.zeros((HEAD_DIM + ONES_ROWS, Q_BLK), F32)

    def overlapped(score_item, value_item, out_item, carry):
        m_prev, acc_prev = carry
        s_steps = score_steps(*score_item) if score_item else []
        v_steps = value_steps(*value_item, m_prev) if value_item else []
        m, acc = m_init, acc_init
        for i, (s_step, v_step) in enumerate(itertools.zip_longest(s_steps, v_steps)):
            if s_step:
                m = s_step(m)
            if i == 0 and out_item:
                write_out(*out_item, acc_prev)
            if v_step:
                acc = v_step(acc)
        return m, acc

    first, mid, last = 0, 1, 2
    carry = overlapped((0, 0, first), None, None, (None, None))
    carry = overlapped((0, 1, first), (0, 0, first), None, carry)
    carry = overlapped((1, 0, mid), (0, 1, first), (0, 0), carry)

    def body(rb, carry):
        carry = overlapped((rb, 1, mid), (rb, 0, mid), (rb - 1, 1), carry)
        return overlapped((rb + 1, 0, mid), (rb, 1, mid), (rb, 0), carry)

    carry = lax.fori_loop(1, n_blk - 2, body, carry, unroll=3)
    carry = overlapped((n_blk - 2, 1, mid), (n_blk - 2, 0, mid), (n_blk - 3, 1), carry)
    carry = overlapped((n_blk - 1, 0, last), (n_blk - 2, 1, mid), (n_blk - 2, 0), carry)
    carry = overlapped((n_blk - 1, 1, last), (n_blk - 1, 0, last), (n_blk - 2, 1), carry)
    carry = overlapped(None, (n_blk - 1, 1, last), (n_blk - 1, 0), carry)
    write_out(n_blk - 1, 1, carry[1])


def _attention(qr, qp, kr, vt, kc, vct, zat, rexp):
    b, s, aw = qr.shape
    l = kc.shape[1]
    n_pairs = aw // LANES
    lat = pl.BlockSpec((1, s, LANES), lambda p, i: (i, 0, p))
    lat_t = pl.BlockSpec((1, LANES, s), lambda p, i: (i, p, 0))
    return pl.pallas_call(
        _attn_kernel,
        grid=(n_pairs, b),
        in_specs=[lat, lat, lat, lat_t,
                  pl.BlockSpec((1, l, LANES), lambda p, i: (i, 0, p)),
                  pl.BlockSpec((1, LANES, l), lambda p, i: (i, p, 0)),
                  lat_t,
                  pl.BlockSpec((2, N_BIAS_COLS, TOE_W), lambda p, i: (p, 0, 0))],
        out_specs=lat_t,
        out_shape=jax.ShapeDtypeStruct((b, aw, s), BF16),
        scratch_shapes=[pltpu.VMEM((2, 3, BAND, Q_BLK), F32),
                        pltpu.VMEM((BAND + l, Q_BLK), F32), pltpu.VMEM((BAND + l, Q_BLK), F32)],
        compiler_params=_params("arbitrary", "arbitrary"),
        name="attention",
    )(qr, qp, kr, vt, kc, vct, zat, rexp)


def _lru_pitch(tc):
    pitch = tc + SUBLANES
    assert pitch % (2 * SUBLANES) == SUBLANES
    return pitch


def _lru_gate_consts(ba, bx, lam):
    return 0.5 * ba, 0.5 * bx, jax.nn.softplus(-lam) * (-0.5 * LRU_C * LOG2E)


def _lru_fill(bi, uc, wg, consts, a_s, b_s):
    half_ba, half_bx, half_log2a = consts
    tc = uc.shape[0]
    ucb = uc.astype(BF16)

    def gate_tanh(gi, half_bias):
        pre = jnp.concatenate([_dot(ucb[:, :MXU_DIM], wg(gi, 0)), _dot(ucb[:, MXU_DIM:], wg(gi, 1))], axis=-1)
        return jnp.tanh(pre + half_bias)

    tr = gate_tanh(0, half_ba)
    ti = gate_tanh(1, half_bx)
    a = jnp.exp2(tr * half_log2a + half_log2a)
    gap = 1.0 - a * a
    mult = gap * lax.rsqrt(jnp.maximum(gap, 1e-30))
    bb = mult * ((0.5 * ti + 0.5) * uc)
    r0 = pl.multiple_of(bi * _lru_pitch(tc), SUBLANES)
    for j in range(LRU_SLABS):
        a_s[j, pl.ds(r0, tc), :] = a[:, j * LANES:(j + 1) * LANES]
        b_s[j, pl.ds(r0, tc), :] = bb[:, j * LANES:(j + 1) * LANES]


def _lru_scan(reverse, tc, nb, a_s, b_s, h_s, carry_s):
    pitch = _lru_pitch(tc)

    n_groups = tc // SUBLANES

    def body(g, hs):
        base = pl.multiple_of(((n_groups - 1 - g) if reverse else g) * SUBLANES, SUBLANES)
        hs = list(hs)
        for k in (reversed(range(SUBLANES)) if reverse else range(SUBLANES)):
            rows = pl.ds(base + k, nb, stride=pitch)
            for j in range(LRU_SLABS):
                hs[j] = a_s[j, rows, :] * hs[j] + b_s[j, rows, :]
                h_s[j, rows, :] = hs[j]
        return tuple(hs)

    hs = tuple(carry_s[:, j * LANES:(j + 1) * LANES] for j in range(LRU_SLABS))
    hs = lax.fori_loop(0, n_groups, body, hs)
    for j in range(LRU_SLABS):
        carry_s[:, j * LANES:(j + 1) * LANES] = hs[j]


def _lru_rows(bi, tc, h_s):
    r0 = pl.multiple_of(bi * _lru_pitch(tc), SUBLANES)
    return jnp.concatenate([h_s[j, pl.ds(r0, tc), :] for j in range(LRU_SLABS)], axis=-1)


def _lru_fwd_kernel(u_ref, uc_ref, wg_ref, ba_ref, bx_ref, lam_ref, hf_ref, hb0_ref, a_s, b_s, h_s, carry_s):
    nb, tc = u_ref.shape[0], u_ref.shape[1]
    consts = [_lru_gate_consts(ba_ref[d], bx_ref[d], lam_ref[d]) for d in range(2)]

    def fill_all(src_ref, d, unroll):
        def step(bi, carry):
            _lru_fill(bi, src_ref[bi], lambda gi, hf: wg_ref[d, gi, hf], consts[d], a_s, b_s)
            return carry

        lax.fori_loop(0, nb, step, 0, unroll=unroll)

    @pl.when(pl.program_id(0) == 0)
    def _():
        carry_s[...] = jnp.zeros_like(carry_s)
        fill_all(uc_ref, 1, 1)
        _lru_scan(True, tc, nb, a_s, b_s, h_s, carry_s)
        hb0_ref[...] = carry_s[...]
        carry_s[...] = jnp.zeros_like(carry_s)
        fill_all(uc_ref, 0, 1)
        _lru_scan(False, tc, nb, a_s, b_s, h_s, carry_s)

    fill_all(u_ref, 0, 4)
    _lru_scan(False, tc, nb, a_s, b_s, h_s, carry_s)

    def emit(bi, carry):
        hf_ref[bi] = _lru_rows(bi, tc, h_s).astype(hf_ref.dtype)
        return carry

    lax.fori_loop(0, nb, emit, 0)


def _lru_fwd(u, u_c, wg, ba, bx, lam):
    b, s, lw = u.shape
    tc = LRU_CHUNK
    assert u_c.shape[1] == tc and s % tc == 0
    full = lambda *shape: pl.BlockSpec(shape, lambda i: (0,) * len(shape))
    cblk = pl.BlockSpec((b, tc, lw), lambda i: (0, i, 0))
    slab = pltpu.VMEM((LRU_SLABS, b * _lru_pitch(tc), LANES), F32)
    return pl.pallas_call(
        _lru_fwd_kernel,
        grid=(s // tc,),
        in_specs=[cblk, full(b, tc, lw), full(2, 2, 2, MXU_DIM, MXU_DIM), full(2, 1, lw), full(2, 1, lw), full(2, 1, lw)],
        out_specs=[cblk, full(b, lw)],
        out_shape=[jax.ShapeDtypeStruct((b, s, lw), BF16), jax.ShapeDtypeStruct((b, lw), F32)],
        scratch_shapes=[slab, slab, slab, pltpu.VMEM((b, lw), F32)],
        compiler_params=_params("arbitrary"),
        name="rglru_fwd",
    )(u, u_c, wg, ba, bx, lam)


def _lru_bwd_out_kernel(u_ref, hf_ref, zl_ref, agt_ref, x_ref, hb0_ref, gate_ref, wg_ref, ba_ref, bx_ref, lam_ref,
                        wo_ref, o_ref, a_s, b_s, h_s, carry_s):
    nb, tc = u_ref.shape[0], u_ref.shape[1]
    consts = _lru_gate_consts(ba_ref[...], bx_ref[...], lam_ref[...])

    @pl.when(pl.program_id(0) == 0)
    def _():
        carry_s[...] = hb0_ref[...]

    def fill(bi, carry):
        _lru_fill(bi, u_ref[bi], lambda gi, hf: wg_ref[gi, hf], consts, a_s, b_s)
        return carry

    lax.fori_loop(0, nb, fill, 0, unroll=4)
    _lru_scan(True, tc, nb, a_s, b_s, h_s, carry_s)

    def emit(pi, carry):
        pair = (2 * pi, 2 * pi + 1)
        y = jnp.concatenate([((hf_ref[bi].astype(F32) + _lru_rows(bi, tc, h_s)) * zl_ref[bi].astype(F32)).astype(BF16)
                             for bi in pair], axis=0)
        at = jnp.concatenate([agt_ref[bi] for bi in pair], axis=1)
        mix = lax.dot_general(at, wo_ref[:ATTN_WIDTH, :], (((0,), (0,)), ((), ())), preferred_element_type=F32)
        mix = mix + _dot(y, wo_ref[ATTN_WIDTH:, :])
        for k, bi in enumerate(pair):
            o_ref[bi] = x_ref[bi] + gate_ref[bi] * mix[k * tc:(k + 1) * tc]
        return carry

    lax.fori_loop(0, nb // 2, emit, 0)


def _lru_bwd_out(u, hf, zl, agt, x, hb0, gate, wg, ba, bx, lam, wo_bf):
    b, s, lw = u.shape
    d = x.shape[2]
    tc = LRU_BWD_CHUNK
    n_chunks = s // tc
    assert s % tc == 0 and b % 2 == 0
    full = lambda *shape: pl.BlockSpec(shape, lambda i: (0,) * len(shape))
    rev = lambda i: n_chunks - 1 - i
    cblk = pl.BlockSpec((b, tc, lw), lambda i: (0, rev(i), 0))
    xblk = pl.BlockSpec((b, tc, d), lambda i: (0, rev(i), 0))
    slab = pltpu.VMEM((LRU_SLABS, b * _lru_pitch(tc), LANES), F32)
    return pl.pallas_call(
        _lru_bwd_out_kernel,
        grid=(n_chunks,),
        in_specs=[cblk, cblk, cblk,
                  pl.BlockSpec((b, ATTN_WIDTH, tc), lambda i: (0, 0, rev(i))),
                  xblk, full(b, lw), full(b, 1, d),
                  full(2, 2, MXU_DIM, MXU_DIM), full(1, lw), full(1, lw), full(1, lw),
                  full(ATTN_WIDTH + lw, d)],
        out_specs=xblk,
        out_shape=jax.ShapeDtypeStruct(x.shape, x.dtype),
        scratch_shapes=[slab, slab, slab, pltpu.VMEM((b, lw), F32)],
        compiler_params=_params("arbitrary"),
        name="rglru_bwd_outproj",
    )(u, hf, zl, agt, x, hb0, gate, wg, ba, bx, lam, wo_bf)


def _rope_tables(s):
    t = jnp.arange(s)
    half = HEAD_DIM // 4
    inv = ROPE_BASE ** (-jnp.arange(half, dtype=F32) / half)
    ang_r = (t // GRID_W).astype(F32)[:, None] * inv[None, :]
    ang_c = (t % GRID_W).astype(F32)[:, None] * inv[None, :]
    cos = jnp.concatenate([jnp.cos(ang_r)] * 2 + [jnp.cos(ang_c)] * 2, axis=-1)
    sin = jnp.concatenate([-jnp.sin(ang_r), jnp.sin(ang_r), -jnp.sin(ang_c), jnp.sin(ang_c)], axis=-1)
    return jnp.tile(cos, (1, LANES // HEAD_DIM)), jnp.tile(sin, (1, LANES // HEAD_DIM))


def _block_diag_gates(w):
    n = MXU_DIM // LRU_BLOCK
    lead = w.shape[:-3]
    w = w.reshape(lead + (2, n, LRU_BLOCK, LRU_BLOCK))
    on_diag = jnp.eye(n, dtype=bool)[:, None, :, None]
    blocks = jnp.where(on_diag, w[..., :, :, None, :], jnp.zeros((), w.dtype))
    return blocks.reshape(lead + (2, MXU_DIM, MXU_DIM))


def kernel(x, c, ctx, c_ctx, norm_g, w_mod, b_mod, w_in, w_out, q_norm_g, k_norm_g, rpb, conv_w, conv_b,
           lru_wa, lru_ba, lru_wx, lru_bx, lru_lam):
    bsz, s, d = x.shape
    assert w_in.shape[0] == 1 and d == D_MODEL and s % (GRID_W * Q_ROWS) == 0
    aw, lw = ATTN_WIDTH, LRU_WIDTH

    pad_rows = 2 * SUBLANES - bsz - 1
    cc = jnp.concatenate([c, c_ctx[None, :], jnp.zeros((pad_rows, d), F32)], axis=0)
    mod = _modulation(cc, w_mod[0], b_mod[0][None, :])
    shift, scale, gate = [mod[:bsz, i * d:(i + 1) * d][:, None, :] for i in range(3)]
    shift_c, scale_c = [mod[bsz:bsz + 1, i * d:(i + 1) * d][:, None, :] for i in range(2)]

    w = w_in[0]
    w_bf = w.astype(BF16)
    wt_bf = w_bf[:, COL_V * aw:(COL_ZA + 1) * aw].T
    w_ctx_bf, wt_ctx_bf = w_bf, wt_bf
    g = norm_g[0][None, :]
    cos_t, sin_t = _rope_tables(s)
    blk = jnp.arange(MXU_DIM) // HEAD_DIM
    bd = jnp.where(blk[:, None] == blk[None, :], 1.0 / HEAD_DIM, 0.0).astype(BF16)
    qg = jnp.tile(q_norm_g[0] * (HEAD_DIM ** -0.5 * LOG2E), N_HEADS)[None, :]
    kg = jnp.tile(k_norm_g[0], N_HEADS)[None, :]

    cw, cb = conv_w[0], conv_b[0][None, :]
    qr, qp, kr, vt, zat, u, zl = _inproj(x, g, scale, shift, w_bf, wt_bf, cos_t, sin_t, bd, qg, kg, cw, cb)
    kc, vct, u_c = _ctxproj(ctx, g, scale_c, shift_c, w_ctx_bf, wt_ctx_bf, bd, kg, cw, cb)

    agt = _attention(qr, qp, kr, vt, kc, vct, zat, _bias_rows(rpb[0]))

    wg = _block_diag_gates(jnp.stack([lru_wa[0], lru_wx[0]], axis=1))
    wg = (0.5 * wg).astype(BF16)
    ba, bx, lam = lru_ba[0][:, None, :], lru_bx[0][:, None, :], lru_lam[0][:, None, :]
    hf, hb0 = _lru_fwd(u, u_c, wg, ba, bx, lam)
    return _lru_bwd_out(u, hf, zl, agt, x, hb0, gate, wg[1], ba[1], bx[1], lam[1], w_out[0].astype(BF16))
```

```python
import functools
import itertools
import math

import jax
import jax.numpy as jnp
from jax import lax
from jax.experimental import pallas as pl
from jax.experimental.pallas import tpu as pltpu

F32 = jnp.float32
BF16 = jnp.bfloat16

D_MODEL = 1024
GRID_W = 64
HEAD_DIM = 64
ATTN_WIDTH = 512
LRU_WIDTH = 512
N_HEADS = ATTN_WIDTH // HEAD_DIM
LRU_BLOCK = 64
NA_ROWS = 8
NA_COLS = 16
CONV_WIDTH = 4
LRU_C = 8.0
ROPE_BASE = 10000.0
EPS = 1e-6
NEG_INF = -1e30
LOG2E = math.log2(math.e)

LANES = 128
SUBLANES = 8
MXU_DIM = 256
VMEM_LIMIT_BYTES = 58 * 1024 * 1024

Q_ROWS = 4
BAND_ROWS = Q_ROWS + NA_ROWS
Q_BLK = Q_ROWS * GRID_W
BAND = BAND_ROWS * GRID_W
KEY_CHUNK = 256
ONES_ROWS = 16
N_BIAS_ROWS = 2 * NA_ROWS - 1
N_BIAS_COLS = 2 * NA_COLS - 1
TOE_W = 1024

LRU_CHUNK = 256
LRU_BWD_CHUNK = 128
LRU_SLABS = LRU_WIDTH // LANES
CONV_ROWS = 128
COL_Q, COL_K, COL_V, COL_ZA, COL_U, COL_ZL = range(6)
HALO = 16


def _dot(a, b):
    return jnp.dot(a, b, preferred_element_type=F32)


def _dot_nt(a, b):
    return lax.dot_general(a, b, (((1,), (1,)), ((), ())), preferred_element_type=F32)


def _params(*semantics):
    return pltpu.CompilerParams(dimension_semantics=semantics, vmem_limit_bytes=VMEM_LIMIT_BYTES)


def _split_bf16(t):
    hi = t.astype(BF16)
    lo = (t - hi.astype(F32)).astype(BF16)
    return hi, lo


def _mod_kernel(cc_ref, w_ref, b_ref, o_ref):
    cc = cc_ref[...]
    s = cc * jax.nn.sigmoid(cc)
    s_hi, s_lo = _split_bf16(s)
    w_hi, w_lo = _split_bf16(w_ref[...])
    o_ref[...] = _dot(s_hi, w_hi) + _dot(s_lo, w_hi) + _dot(s_hi, w_lo) + b_ref[...]


def _modulation(cc, w_mod, b_mod):
    rows, d = cc.shape
    n = w_mod.shape[1]
    tn = 512
    return pl.pallas_call(
        _mod_kernel,
        grid=(n // tn,),
        in_specs=[pl.BlockSpec((rows, d), lambda j: (0, 0)),
                  pl.BlockSpec((d, tn), lambda j: (0, j)),
                  pl.BlockSpec((1, tn), lambda j: (0, j))],
        out_specs=pl.BlockSpec((rows, tn), lambda j: (0, j)),
        out_shape=jax.ShapeDtypeStruct((rows, n), F32),
        compiler_params=_params("arbitrary"),
        name="modulation",
    )(cc, w_mod, b_mod)


def _fill_bias_tables(rexp_ref, h, o_ref):
    k = lax.broadcasted_iota(jnp.int32, (GRID_W, TOE_W), 0)
    q = lax.broadcasted_iota(jnp.int32, (GRID_W, TOE_W), 1) % GRID_W
    diff = k - q + (NA_COLS - 1)
    toe = jnp.zeros((GRID_W, TOE_W), F32)
    for d in range(N_BIAS_COLS):
        toe = jnp.where(diff == d, rexp_ref[h, d:d + 1, :], toe)
    col_start = jnp.clip(q - NA_COLS // 2, 0, GRID_W - NA_COLS)
    col_ok = (k >= col_start) & (k < col_start + NA_COLS)
    toe = jnp.where(col_ok, toe, NEG_INF)
    ri = lax.broadcasted_iota(jnp.int32, (GRID_W, Q_BLK), 1) // GRID_W
    masked = jnp.full((GRID_W, Q_BLK), NEG_INF, F32)

    for typ in range(3):
        off = (NA_ROWS - 1, NA_ROWS // 2 - 1, NA_ROWS - BAND_ROWS + Q_ROWS - 1)[typ]
        for jj in range(BAND_ROWS):
            if typ == 0:
                lo, hi = (0, Q_ROWS) if jj < NA_ROWS else (0, 0)
            elif typ == 1:
                lo, hi = max(jj - NA_ROWS + 1, 0), min(jj, Q_ROWS - 1) + 1
            else:
                lo, hi = (0, Q_ROWS) if jj >= BAND_ROWS - NA_ROWS else (0, 0)
            strip = masked
            if lo < hi:
                e0 = N_BIAS_ROWS - 1 - jj - off
                rolled = pltpu.roll(toe, (-e0 * GRID_W) % TOE_W, axis=1)[:, :Q_BLK]
                strip = jnp.where((ri >= lo) & (ri < hi), rolled, NEG_INF)
            o_ref[h, typ, jj * GRID_W:(jj + 1) * GRID_W, :] = strip


def _bias_rows(rpb):
    rexp = jnp.repeat(jnp.transpose(rpb[:, ::-1, :], (0, 2, 1)) * LOG2E, GRID_W, axis=2)
    return jnp.pad(rexp, ((0, 0), (0, 0), (0, TOE_W - N_BIAS_ROWS * GRID_W)))


def _adaln(x, g_ref, scale_ref, shift_ref):
    ms = jnp.mean(x * x, axis=-1, keepdims=True)
    gm = g_ref[...] * (1.0 + scale_ref[0])
    return ((x * lax.rsqrt(ms + EPS)) * gm + shift_ref[0]).astype(BF16)


def _head_rms(t, bd_ref, gain):
    sq = (t * t).astype(BF16)
    bd = bd_ref[...]
    m = jnp.concatenate([_dot(sq[:, :MXU_DIM], bd), _dot(sq[:, MXU_DIM:], bd)], axis=-1)
    return t * lax.rsqrt(m + EPS) * gain


def _rope(t, cos_ref, sin_ref):
    lane = lax.broadcasted_iota(jnp.int32, (1, LANES), 1)
    first = (lane % 32) < 16
    cos, sin = cos_ref[...], sin_ref[...]
    parts = []
    for j in range(t.shape[1] // LANES):
        c = t[:, j * LANES:(j + 1) * LANES]
        swapped = jnp.where(first, pltpu.roll(c, LANES - 16, axis=1), pltpu.roll(c, 16, axis=1))
        parts.append(c * cos + swapped * sin)
    return jnp.concatenate(parts, axis=-1)


def _silu(t):
    return t * jax.nn.sigmoid(t)


def _conv_window(ext, n, cw_ref, cb_ref):
    rows = ext.shape[0]

    def shifted(k):
        return pltpu.roll(ext, (rows - k) % rows, axis=0)[HALO:HALO + n]

    y = cb_ref[...] + cw_ref[0:1, :] * shifted(-1)
    y = y + cw_ref[1:2, :] * ext[HALO:HALO + n]
    y = y + cw_ref[2:3, :] * shifted(1)
    return y + cw_ref[3:4, :] * shifted(2)


def _inproj_kernel(x_ref, xp_ref, xn_ref, g_ref, scale_ref, shift_ref, w_ref, wt_ref, cos_ref, sin_ref, bd_ref,
                   qg_ref, kg_ref, cw_ref, cb_ref, qr_ref, qp_ref, kr_ref, vt_ref, zat_ref, uc_ref, zl_ref):
    tm = x_ref.shape[1]
    aw = ATTN_WIDTH
    i, n_tiles = pl.program_id(0), pl.num_programs(0)
    hb_ext = jnp.concatenate([_adaln(xp_ref[0], g_ref, scale_ref, shift_ref),
                              _adaln(x_ref[0], g_ref, scale_ref, shift_ref),
                              _adaln(xn_ref[0], g_ref, scale_ref, shift_ref)], axis=0)
    hb = hb_ext[HALO:HALO + tm]

    def cols(j):
        return w_ref[:, j * aw:(j + 1) * aw]

    u_ext = _dot(hb_ext, cols(COL_U))
    row = lax.broadcasted_iota(jnp.int32, (tm + 2 * HALO, 1), 0)
    inside = ((row >= HALO) | (i > 0)) & ((row < HALO + tm) | (i < n_tiles - 1))
    u_ext = jnp.where(inside, u_ext, 0.0)

    qn = _head_rms(_dot(hb, cols(COL_Q)), bd_ref, qg_ref[...])
    qp_ref[0] = qn.astype(BF16)
    qr_ref[0] = _rope(qn, cos_ref, sin_ref).astype(BF16)
    kn = _head_rms(_dot(hb, cols(COL_K)), bd_ref, kg_ref[...])
    kr_ref[0] = _rope(kn, cos_ref, sin_ref).astype(BF16)

    zl_ref[0] = _silu(_dot(hb, cols(COL_ZL))).astype(BF16)
    vt_ref[0] = _dot_nt(wt_ref[:aw, :], hb).astype(BF16)
    zat_ref[0] = _silu(_dot_nt(wt_ref[aw:, :], hb)).astype(BF16)
    for r in range(0, tm, CONV_ROWS):
        uc_ref[0, r:r + CONV_ROWS, :] = _conv_window(u_ext[r:r + CONV_ROWS + 2 * HALO], CONV_ROWS, cw_ref, cb_ref)


def _inproj(x, g, scale, shift, w_bf, wt_bf, cos_t, sin_t, bd, qg, kg, conv_w, conv_b, tm=512):
    b, s, d = x.shape
    aw = ATTN_WIDTH
    tok = lambda i, j: (j, i, 0)
    per_b = lambda i, j: (j, 0, 0)
    const = lambda i, j: (0, 0)
    halo_per_tile = tm // HALO
    n_halo = s // HALO
    out_bf = jax.ShapeDtypeStruct((b, s, aw), BF16)
    out_t = jax.ShapeDtypeStruct((b, aw, s), BF16)
    out_f32 = jax.ShapeDtypeStruct((b, s, aw), F32)
    blk = pl.BlockSpec((1, tm, aw), tok)
    blk_t = pl.BlockSpec((1, aw, tm), lambda i, j: (j, 0, i))
    return pl.pallas_call(
        _inproj_kernel,
        grid=(s // tm, b),
        in_specs=[pl.BlockSpec((1, tm, d), tok),
                  pl.BlockSpec((1, HALO, d), lambda i, j: (j, jnp.maximum(i * halo_per_tile - 1, 0), 0)),
                  pl.BlockSpec((1, HALO, d), lambda i, j: (j, jnp.minimum((i + 1) * halo_per_tile, n_halo - 1), 0)),
                  pl.BlockSpec((1, d), const),
                  pl.BlockSpec((1, 1, d), per_b),
                  pl.BlockSpec((1, 1, d), per_b),
                  pl.BlockSpec(w_bf.shape, const),
                  pl.BlockSpec(wt_bf.shape, const),
                  pl.BlockSpec((tm, LANES), lambda i, j: (i, 0)),
                  pl.BlockSpec((tm, LANES), lambda i, j: (i, 0)),
                  pl.BlockSpec((MXU_DIM, MXU_DIM), const),
                  pl.BlockSpec((1, aw), const),
                  pl.BlockSpec((1, aw), const),
                  pl.BlockSpec((CONV_WIDTH, aw), const),
                  pl.BlockSpec((1, aw), const)],
        out_specs=[blk, blk, blk, blk_t, blk_t, blk, blk],
        out_shape=[out_bf, out_bf, out_bf, out_t, out_t, out_f32, out_bf],
        compiler_params=_params("arbitrary", "arbitrary"),
        name="inproj",
    )(x, x, x, g, scale, shift, w_bf, wt_bf, cos_t, sin_t, bd, qg, kg, conv_w, conv_b)


def _ctxproj_kernel(c_ref, g_ref, scale_ref, shift_ref, w_ref, wt_ref, bd_ref, kg_ref, cw_ref, cb_ref,
                    kc_ref, vct_ref, uc_ref):
    nb, l, d = c_ref.shape
    aw = ATTN_WIDTH
    hb = _adaln(c_ref[...].reshape(nb * l, d), g_ref, scale_ref, shift_ref)
    kc = _head_rms(_dot(hb, w_ref[:, COL_K * aw:(COL_K + 1) * aw]), bd_ref, kg_ref[...]).astype(BF16)
    u = _dot(hb, w_ref[:, COL_U * aw:(COL_U + 1) * aw])
    vct = _dot_nt(wt_ref[:aw, :], hb).astype(BF16)
    pad = jnp.zeros((HALO, aw), F32)
    for bi in range(nb):
        kc_ref[bi] = kc[bi * l:(bi + 1) * l]
        uc_ref[bi] = _conv_window(jnp.concatenate([pad, u[bi * l:(bi + 1) * l], pad], axis=0), l, cw_ref, cb_ref)
        vct_ref[bi] = vct[:, bi * l:(bi + 1) * l]


def _ctxproj(ctx, g, scale_c, shift_c, w_ctx_bf, wt_ctx_bf, bd, kg, conv_w, conv_b, nb=4):
    b, l, d = ctx.shape
    aw = ATTN_WIDTH
    const = lambda i: (0, 0)
    blk = pl.BlockSpec((nb, l, aw), lambda i: (i, 0, 0))
    return pl.pallas_call(
        _ctxproj_kernel,
        grid=(b // nb,),
        in_specs=[pl.BlockSpec((nb, l, d), lambda i: (i, 0, 0)),
                  pl.BlockSpec((1, d), const),
                  pl.BlockSpec((1, 1, d), lambda i: (0, 0, 0)),
                  pl.BlockSpec((1, 1, d), lambda i: (0, 0, 0)),
                  pl.BlockSpec(w_ctx_bf.shape, const),
                  pl.BlockSpec(wt_ctx_bf.shape, const),
                  pl.BlockSpec((MXU_DIM, MXU_DIM), const),
                  pl.BlockSpec((1, aw), const),
                  pl.BlockSpec((CONV_WIDTH, aw), const),
                  pl.BlockSpec((1, aw), const)],
        out_specs=[blk, pl.BlockSpec((nb, aw, l), lambda i: (i, 0, 0)), blk],
        out_shape=[jax.ShapeDtypeStruct((b, l, aw), BF16),
                   jax.ShapeDtypeStruct((b, aw, l), BF16),
                   jax.ShapeDtypeStruct((b, l, aw), F32)],
        compiler_params=_params("arbitrary"),
        name="ctxproj",
    )(ctx, g, scale_c, shift_c, w_ctx_bf, wt_ctx_bf, bd, kg, conv_w, conv_b)


def _attn_kernel(qr_ref, qp_ref, kr_ref, vt_ref, kc_ref, vct_ref, zat_ref, rexp_ref, o_ref,
                 bias_ref, s_buf0, s_buf1, p_buf0, p_buf1):
    @pl.when(pl.program_id(1) == 0)
    def _():
        for h in range(2):
            _fill_bias_tables(rexp_ref, h, bias_ref)

    rows = qr_ref.shape[1] // GRID_W
    n_blk = rows // Q_ROWS
    chunk_rows = KEY_CHUNK // GRID_W
    n_tiles = Q_BLK // LANES
    lane_head = lax.broadcasted_iota(jnp.int32, (1, LANES), 1) // HEAD_DIM
    s_bufs = (s_buf0, s_buf1)
    p_bufs = (p_buf0, p_buf1)
    ones_rows = jnp.ones((ONES_ROWS, KEY_CHUNK), BF16)

    def in_window(typ, jj, tile):
        if typ == 0:
            return jj < NA_ROWS
        if typ == 2:
            return jj >= BAND_ROWS - NA_ROWS
        return any(0 <= jj - ri < NA_ROWS for ri in range(tile * Q_ROWS // n_tiles, (tile + 1) * Q_ROWS // n_tiles))

    def local_chunks(typ):
        return [c for c in range(BAND // KEY_CHUNK)
                if any(in_window(typ, c * chunk_rows + r, t) for r in range(chunk_rows) for t in range(n_tiles))]

    def sub_blocks(c):
        for r in range(chunk_rows):
            for t in range(n_tiles):
                yield c * chunk_rows + r, t, slice(r * GRID_W, (r + 1) * GRID_W), slice(t * LANES, (t + 1) * LANES)

    def item(rb):
        rb = jnp.asarray(rb, jnp.int32)
        q0 = pl.multiple_of(rb * Q_BLK, Q_BLK)
        band0 = pl.multiple_of(jnp.clip(rb * Q_ROWS - NA_ROWS // 2, 0, rows - BAND_ROWS) * GRID_W, Q_BLK)
        return q0, band0

    def col_max(m, blk, t):
        m = list(m)
        m[t] = jnp.maximum(m[t], jnp.max(blk, axis=0, keepdims=True))
        return tuple(m)

    def score_steps(rb, h, typ):
        q0, band0 = item(rb)
        s_buf = s_bufs[h]
        mine = lane_head == h
        qr = jnp.where(mine, qr_ref[0, pl.ds(q0, Q_BLK), :], jnp.zeros((), BF16))
        qp = jnp.where(mine, qp_ref[0, pl.ds(q0, Q_BLK), :], jnp.zeros((), BF16))

        def local(c, m):
            n_rows = 1 + max(r for r in range(chunk_rows)
                             if any(in_window(typ, c * chunk_rows + r, t) for t in range(n_tiles)))
            s = _dot_nt(kr_ref[0, pl.ds(band0 + c * KEY_CHUNK, n_rows * GRID_W), :], qr)
            for jj, t, r_sl, l_sl in sub_blocks(c):
                if in_window(typ, jj, t):
                    k_sl = slice(jj * GRID_W, (jj + 1) * GRID_W)
                    blk = s[r_sl, l_sl] + bias_ref[h, typ, k_sl, l_sl]
                    s_buf[k_sl, l_sl] = blk
                    m = col_max(m, blk, t)
            return m

        def context(m):
            s = _dot_nt(kc_ref[0], qp)
            s_buf[pl.ds(BAND, KEY_CHUNK), :] = s
            for t in range(n_tiles):
                m = col_max(m, s[:, t * LANES:(t + 1) * LANES], t)
            return m

        return [functools.partial(local, c) for c in local_chunks(typ)] + [context]

    def prob_steps(rb, h, typ, m):
        s_buf, p_buf = s_bufs[h], p_bufs[h]

        def probs(k_sl, t):
            return jnp.exp2(s_buf[k_sl, t * LANES:(t + 1) * LANES] - m[t]).astype(BF16)

        def local(c):
            zero_blk = jnp.zeros((GRID_W, LANES), BF16)
            for r in range(chunk_rows):
                jj = c * chunk_rows + r
                k_sl = slice(jj * GRID_W, (jj + 1) * GRID_W)
                p_buf[k_sl, :] = jnp.concatenate(
                    [probs(k_sl, t) if in_window(typ, jj, t) else zero_blk for t in range(n_tiles)], axis=1)

        def context():
            k_sl = slice(BAND, BAND + KEY_CHUNK)
            p_buf[k_sl, :] = jnp.concatenate([probs(k_sl, t) for t in range(n_tiles)], axis=1)

        return [functools.partial(local, c) for c in local_chunks(typ)] + [context]

    def value_steps(rb, h, typ):
        _, band0 = item(rb)
        p_buf = p_bufs[h]
        hd = pl.ds(h * HEAD_DIM, HEAD_DIM)

        def local(c, acc):
            vt = jnp.concatenate([vt_ref[0, hd, pl.ds(band0 + c * KEY_CHUNK, KEY_CHUNK)], ones_rows], axis=0)
            return acc + _dot(vt, p_buf[c * KEY_CHUNK:(c + 1) * KEY_CHUNK, :])

        def context(acc):
            vt = jnp.concatenate([vct_ref[0, hd, :], ones_rows], axis=0)
            return acc + _dot(vt, p_buf[BAND:BAND + KEY_CHUNK, :])

        return [functools.partial(local, c) for c in local_chunks(typ)] + [context]

    def write_out(rb, h, acc):
        q0, _ = item(rb)
        hd = pl.ds(h * HEAD_DIM, HEAD_DIM)
        gated = (acc[:HEAD_DIM] / acc[HEAD_DIM:HEAD_DIM + 1]) * zat_ref[0, hd, pl.ds(q0, Q_BLK)].astype(F32)
        o_ref[0, hd, pl.ds(q0, Q_BLK)] = gated.astype(o_ref.dtype)

    m_init = (jnp.full((1, LANES), -jnp.inf, F32),) * n_tiles
    acc_init = jnp.zeros((HEAD_DIM + ONES_ROWS, Q_BLK), F32)

    def unit(score_item, prob_item, value_item, out_item, carry):
        m_prev, acc_prev = carry
        s_steps = score_steps(*score_item) if score_item else []
        e_steps = prob_steps(*prob_item, m_prev) if prob_item else []
        v_steps = value_steps(*value_item) if value_item else []
        m, acc = m_init, acc_init
        if out_item:
            write_out(*out_item, acc_prev)
        for s_step, e_step, v_step in itertools.zip_longest(s_steps, e_steps, v_steps):
            if s_step:
                m = s_step(m)
            if v_step:
                acc = v_step(acc)
            if e_step:
                e_step()
        return m, acc

    n_items = 2 * n_blk

    def typ_of(rb):
        return 0 if rb == 0 else (2 if rb == n_blk - 1 else 1)

    def static_unit(j, carry):
        def it(k, with_typ=True):
            if not 0 <= k < n_items:
                return None
            return (k // 2, k % 2, typ_of(k // 2)) if with_typ else (k // 2, k % 2)

        return unit(it(j), it(j - 1), it(j - 2), it(j - 3, False), carry)

    carry = (None, None)
    for j in range(5):
        carry = static_unit(j, carry)

    def body(rb, carry):
        carry = unit((rb, 1, 1), (rb, 0, 1), (rb - 1, 1, 1), (rb - 1, 0), carry)
        return unit((rb + 1, 0, 1), (rb, 1, 1), (rb, 0, 1), (rb - 1, 1), carry)

    carry = lax.fori_loop(2, n_blk - 2, body, carry, unroll=3)
    for j in range(2 * (n_blk - 2) + 1, n_items + 3):
        carry = static_unit(j, carry)


def _attention(qr, qp, kr, vt, kc, vct, zat, rexp):
    b, s, aw = qr.shape
    l = kc.shape[1]
    n_pairs = aw // LANES
    lat = pl.BlockSpec((1, s, LANES), lambda p, i: (i, 0, p))
    lat_t = pl.BlockSpec((1, LANES, s), lambda p, i: (i, p, 0))
    return pl.pallas_call(
        _attn_kernel,
        grid=(n_pairs, b),
        in_specs=[lat, lat, lat, lat_t,
                  pl.BlockSpec((1, l, LANES), lambda p, i: (i, 0, p)),
                  pl.BlockSpec((1, LANES, l), lambda p, i: (i, p, 0)),
                  lat_t,
                  pl.BlockSpec((2, N_BIAS_COLS, TOE_W), lambda p, i: (p, 0, 0))],
        out_specs=lat_t,
        out_shape=jax.ShapeDtypeStruct((b, aw, s), BF16),
        scratch_shapes=[pltpu.VMEM((2, 3, BAND, Q_BLK), F32),
                        pltpu.VMEM((BAND + l, Q_BLK), F32), pltpu.VMEM((BAND + l, Q_BLK), F32),
                        pltpu.VMEM((BAND + l, Q_BLK), BF16), pltpu.VMEM((BAND + l, Q_BLK), BF16)],
        compiler_params=_params("arbitrary", "arbitrary"),
        name="attention",
    )(qr, qp, kr, vt, kc, vct, zat, rexp)


def _lru_pitch(tc):
    pitch = tc + SUBLANES
    assert pitch % (2 * SUBLANES) == SUBLANES
    return pitch


def _lru_gate_consts(ba, bx, lam):
    return 0.5 * ba, 0.5 * bx, jax.nn.softplus(-lam) * (-0.5 * LRU_C * LOG2E)


def _lru_fill(bi, uc, wg, consts, a_s, b_s):
    half_ba, half_bx, half_log2a = consts
    tc = uc.shape[0]
    ucb = uc.astype(BF16)

    def gate_tanh(gi, half_bias):
        pre = jnp.concatenate([_dot(ucb[:, :MXU_DIM], wg(gi, 0)), _dot(ucb[:, MXU_DIM:], wg(gi, 1))], axis=-1)
        return jnp.tanh(pre + half_bias)

    tr = gate_tanh(0, half_ba)
    ti = gate_tanh(1, half_bx)
    a = jnp.exp2(tr * half_log2a + half_log2a)
    gap = 1.0 - a * a
    mult = gap * lax.rsqrt(jnp.maximum(gap, 1e-30))
    bb = mult * ((0.5 * ti + 0.5) * uc)
    r0 = pl.multiple_of(bi * _lru_pitch(tc), SUBLANES)
    for j in range(LRU_SLABS):
        a_s[j, pl.ds(r0, tc), :] = a[:, j * LANES:(j + 1) * LANES]
        b_s[j, pl.ds(r0, tc), :] = bb[:, j * LANES:(j + 1) * LANES]


def _lru_scan(reverse, tc, nb, a_s, b_s, h_s, carry_s):
    pitch = _lru_pitch(tc)

    n_groups = tc // SUBLANES

    def body(g, hs):
        base = pl.multiple_of(((n_groups - 1 - g) if reverse else g) * SUBLANES, SUBLANES)
        hs = list(hs)
        for k in (reversed(range(SUBLANES)) if reverse else range(SUBLANES)):
            rows = pl.ds(base + k, nb, stride=pitch)
            for j in range(LRU_SLABS):
                hs[j] = a_s[j, rows, :] * hs[j] + b_s[j, rows, :]
                h_s[j, rows, :] = hs[j]
        return tuple(hs)

    hs = tuple(carry_s[:, j * LANES:(j + 1) * LANES] for j in range(LRU_SLABS))
    hs = lax.fori_loop(0, n_groups, body, hs)
    for j in range(LRU_SLABS):
        carry_s[:, j * LANES:(j + 1) * LANES] = hs[j]


def _lru_rows(bi, tc, h_s):
    r0 = pl.multiple_of(bi * _lru_pitch(tc), SUBLANES)
    return jnp.concatenate([h_s[j, pl.ds(r0, tc), :] for j in range(LRU_SLABS)], axis=-1)


def _lru_fwd_kernel(u_ref, uc_ref, wg_ref, ba_ref, bx_ref, lam_ref, hf_ref, hb0_ref, a_s, b_s, h_s, carry_s):
    nb, tc = u_ref.shape[0], u_ref.shape[1]
    consts = [_lru_gate_consts(ba_ref[d], bx_ref[d], lam_ref[d]) for d in range(2)]

    def fill_all(src_ref, d, unroll):
        def step(bi, carry):
            _lru_fill(bi, src_ref[bi], lambda gi, hf: wg_ref[d, gi, hf], consts[d], a_s, b_s)
            return carry

        lax.fori_loop(0, nb, step, 0, unroll=unroll)

    @pl.when(pl.program_id(0) == 0)
    def _():
        carry_s[...] = jnp.zeros_like(carry_s)
        fill_all(uc_ref, 1, 1)
        _lru_scan(True, tc, nb, a_s, b_s, h_s, carry_s)
        hb0_ref[...] = carry_s[...]
        carry_s[...] = jnp.zeros_like(carry_s)
        fill_all(uc_ref, 0, 1)
        _lru_scan(False, tc, nb, a_s, b_s, h_s, carry_s)

    fill_all(u_ref, 0, 4)
    _lru_scan(False, tc, nb, a_s, b_s, h_s, carry_s)

    def emit(bi, carry):
        hf_ref[bi] = _lru_rows(bi, tc, h_s).astype(hf_ref.dtype)
        return carry

    lax.fori_loop(0, nb, emit, 0)


def _lru_fwd(u, u_c, wg, ba, bx, lam):
    b, s, lw = u.shape
    tc = LRU_CHUNK
    assert u_c.shape[1] == tc and s % tc == 0
    full = lambda *shape: pl.BlockSpec(shape, lambda i: (0,) * len(shape))
    cblk = pl.BlockSpec((b, tc, lw), lambda i: (0, i, 0))
    slab = pltpu.VMEM((LRU_SLABS, b * _lru_pitch(tc), LANES), F32)
    return pl.pallas_call(
        _lru_fwd_kernel,
        grid=(s // tc,),
        in_specs=[cblk, full(b, tc, lw), full(2, 2, 2, MXU_DIM, MXU_DIM), full(2, 1, lw), full(2, 1, lw), full(2, 1, lw)],
        out_specs=[cblk, full(b, lw)],
        out_shape=[jax.ShapeDtypeStruct((b, s, lw), BF16), jax.ShapeDtypeStruct((b, lw), F32)],
        scratch_shapes=[slab, slab, slab, pltpu.VMEM((b, lw), F32)],
        compiler_params=_params("arbitrary"),
        name="rglru_fwd",
    )(u, u_c, wg, ba, bx, lam)


def _lru_bwd_out_kernel(u_ref, hf_ref, zl_ref, agt_ref, x_ref, hb0_ref, gate_ref, wg_ref, ba_ref, bx_ref, lam_ref,
                        wo_ref, o_ref, a_s, b_s, h_s, carry_s):
    nb, tc = u_ref.shape[0], u_ref.shape[1]
    consts = _lru_gate_consts(ba_ref[...], bx_ref[...], lam_ref[...])

    @pl.when(pl.program_id(0) == 0)
    def _():
        carry_s[...] = hb0_ref[...]

    def fill(bi, carry):
        _lru_fill(bi, u_ref[bi], lambda gi, hf: wg_ref[gi, hf], consts, a_s, b_s)
        return carry

    lax.fori_loop(0, nb, fill, 0, unroll=4)
    _lru_scan(True, tc, nb, a_s, b_s, h_s, carry_s)

    def emit(pi, carry):
        pair = (2 * pi, 2 * pi + 1)
        y = jnp.concatenate([((hf_ref[bi].astype(F32) + _lru_rows(bi, tc, h_s)) * zl_ref[bi].astype(F32)).astype(BF16)
                             for bi in pair], axis=0)
        at = jnp.concatenate([agt_ref[bi] for bi in pair], axis=1)
        mix = lax.dot_general(at, wo_ref[:ATTN_WIDTH, :], (((0,), (0,)), ((), ())), preferred_element_type=F32)
        mix = mix + _dot(y, wo_ref[ATTN_WIDTH:, :])
        for k, bi in enumerate(pair):
            o_ref[bi] = x_ref[bi] + gate_ref[bi] * mix[k * tc:(k + 1) * tc]
        return carry

    lax.fori_loop(0, nb // 2, emit, 0)


def _lru_bwd_out(u, hf, zl, agt, x, hb0, gate, wg, ba, bx, lam, wo_bf):
    b, s, lw = u.shape
    d = x.shape[2]
    tc = LRU_BWD_CHUNK
    n_chunks = s // tc
    assert s % tc == 0 and b % 2 == 0
    full = lambda *shape: pl.BlockSpec(shape, lambda i: (0,) * len(shape))
    rev = lambda i: n_chunks - 1 - i
    cblk = pl.BlockSpec((b, tc, lw), lambda i: (0, rev(i), 0))
    xblk = pl.BlockSpec((b, tc, d), lambda i: (0, rev(i), 0))
    slab = pltpu.VMEM((LRU_SLABS, b * _lru_pitch(tc), LANES), F32)
    return pl.pallas_call(
        _lru_bwd_out_kernel,
        grid=(n_chunks,),
        in_specs=[cblk, cblk, cblk,
                  pl.BlockSpec((b, ATTN_WIDTH, tc), lambda i: (0, 0, rev(i))),
                  xblk, full(b, lw), full(b, 1, d),
                  full(2, 2, MXU_DIM, MXU_DIM), full(1, lw), full(1, lw), full(1, lw),
                  full(ATTN_WIDTH + lw, d)],
        out_specs=xblk,
        out_shape=jax.ShapeDtypeStruct(x.shape, x.dtype),
        scratch_shapes=[slab, slab, slab, pltpu.VMEM((b, lw), F32)],
        compiler_params=_params("arbitrary"),
        name="rglru_bwd_outproj",
    )(u, hf, zl, agt, x, hb0, gate, wg, ba, bx, lam, wo_bf)


def _rope_tables(s):
    t = jnp.arange(s)
    half = HEAD_DIM // 4
    inv = ROPE_BASE ** (-jnp.arange(half, dtype=F32) / half)
    ang_r = (t // GRID_W).astype(F32)[:, None] * inv[None, :]
    ang_c = (t % GRID_W).astype(F32)[:, None] * inv[None, :]
    cos = jnp.concatenate([jnp.cos(ang_r)] * 2 + [jnp.cos(ang_c)] * 2, axis=-1)
    sin = jnp.concatenate([-jnp.sin(ang_r), jnp.sin(ang_r), -jnp.sin(ang_c), jnp.sin(ang_c)], axis=-1)
    return jnp.tile(cos, (1, LANES // HEAD_DIM)), jnp.tile(sin, (1, LANES // HEAD_DIM))


def _block_diag_gates(w):
    n = MXU_DIM // LRU_BLOCK
    lead = w.shape[:-3]
    w = w.reshape(lead + (2, n, LRU_BLOCK, LRU_BLOCK))
    on_diag = jnp.eye(n, dtype=bool)[:, None, :, None]
    blocks = jnp.where(on_diag, w[..., :, :, None, :], jnp.zeros((), w.dtype))
    return blocks.reshape(lead + (2, MXU_DIM, MXU_DIM))


def kernel(x, c, ctx, c_ctx, norm_g, w_mod, b_mod, w_in, w_out, q_norm_g, k_norm_g, rpb, conv_w, conv_b,
           lru_wa, lru_ba, lru_wx, lru_bx, lru_lam):
    bsz, s, d = x.shape
    assert w_in.shape[0] == 1 and d == D_MODEL and s % (GRID_W * Q_ROWS) == 0
    aw, lw = ATTN_WIDTH, LRU_WIDTH

    pad_rows = 2 * SUBLANES - bsz - 1
    cc = jnp.concatenate([c, c_ctx[None, :], jnp.zeros((pad_rows, d), F32)], axis=0)
    mod = _modulation(cc, w_mod[0], b_mod[0][None, :])
    shift, scale, gate = [mod[:bsz, i * d:(i + 1) * d][:, None, :] for i in range(3)]
    shift_c, scale_c = [mod[bsz:bsz + 1, i * d:(i + 1) * d][:, None, :] for i in range(2)]

    w = w_in[0]
    w_bf = w.astype(BF16)
    wt_bf = w[:, COL_V * aw:(COL_ZA + 1) * aw].T.astype(BF16)
    w_ctx_bf, wt_ctx_bf = w_bf, wt_bf
    g = norm_g[0][None, :]
    cos_t, sin_t = _rope_tables(s)
    blk = jnp.arange(MXU_DIM) // HEAD_DIM
    bd = jnp.where(blk[:, None] == blk[None, :], 1.0 / HEAD_DIM, 0.0).astype(BF16)
    qg = jnp.tile(q_norm_g[0] * (HEAD_DIM ** -0.5 * LOG2E), N_HEADS)[None, :]
    kg = jnp.tile(k_norm_g[0], N_HEADS)[None, :]

    cw, cb = conv_w[0], conv_b[0][None, :]
    qr, qp, kr, vt, zat, u, zl = _inproj(x, g, scale, shift, w_bf, wt_bf, cos_t, sin_t, bd, qg, kg, cw, cb)
    kc, vct, u_c = _ctxproj(ctx, g, scale_c, shift_c, w_ctx_bf, wt_ctx_bf, bd, kg, cw, cb)

    agt = _attention(qr, qp, kr, vt, kc, vct, zat, _bias_rows(rpb[0]))

    wg = _block_diag_gates(jnp.stack([lru_wa[0], lru_wx[0]], axis=1))
    wg = (0.5 * wg).astype(BF16)
    ba, bx, lam = lru_ba[0][:, None, :], lru_bx[0][:, None, :], lru_lam[0][:, None, :]
    hf, hb0 = _lru_fwd(u, u_c, wg, ba, bx, lam)
    return _lru_bwd_out(u, hf, zl, agt, x, hb0, gate, wg[1], ba[1], bx[1], lam[1], w_out[0].astype(BF16))
```

```python
import functools
import itertools
import math

import jax
import jax.numpy as jnp
from jax import lax
from jax.experimental import pallas as pl
from jax.experimental.pallas import tpu as pltpu

F32 = jnp.float32
BF16 = jnp.bfloat16

D_MODEL = 1024
GRID_W = 64
HEAD_DIM = 64
ATTN_WIDTH = 512
LRU_WIDTH = 512
N_HEADS = ATTN_WIDTH // HEAD_DIM
LRU_BLOCK = 64
NA_ROWS = 8
NA_COLS = 16
CONV_WIDTH = 4
LRU_C = 8.0
ROPE_BASE = 10000.0
EPS = 1e-6
NEG_INF = -1e30
LOG2E = math.log2(math.e)

LANES = 128
SUBLANES = 8
MXU_DIM = 256
VMEM_LIMIT_BYTES = 58 * 1024 * 1024

Q_ROWS = 4
BAND_ROWS = Q_ROWS + NA_ROWS
Q_BLK = Q_ROWS * GRID_W
BAND = BAND_ROWS * GRID_W
KEY_CHUNK = 256
ONES_ROWS = 16
N_BIAS_ROWS = 2 * NA_ROWS - 1
N_BIAS_COLS = 2 * NA_COLS - 1
TOE_W = 1024

LRU_CHUNK = 256
LRU_BWD_CHUNK = 128
LRU_SLABS = LRU_WIDTH // LANES
CONV_ROWS = 128
COL_Q, COL_K, COL_V, COL_ZA, COL_U, COL_ZL = range(6)
HALO = 16


def _dot(a, b):
    return jnp.dot(a, b, preferred_element_type=F32)


def _dot_nt(a, b):
    return lax.dot_general(a, b, (((1,), (1,)), ((), ())), preferred_element_type=F32)


def _params(*semantics):
    return pltpu.CompilerParams(dimension_semantics=semantics, vmem_limit_bytes=VMEM_LIMIT_BYTES)


def _split_bf16(t):
    hi = t.astype(BF16)
    lo = (t - hi.astype(F32)).astype(BF16)
    return hi, lo


def _mod_kernel(cc_ref, w_ref, b_ref, o_ref):
    cc = cc_ref[...]
    s = cc * jax.nn.sigmoid(cc)
    s_hi, s_lo = _split_bf16(s)
    w_hi, w_lo = _split_bf16(w_ref[...])
    o_ref[...] = _dot(s_hi, w_hi) + _dot(s_lo, w_hi) + _dot(s_hi, w_lo) + b_ref[...]


def _modulation(cc, w_mod, b_mod):
    rows, d = cc.shape
    n = w_mod.shape[1]
    tn = 512
    return pl.pallas_call(
        _mod_kernel,
        grid=(n // tn,),
        in_specs=[pl.BlockSpec((rows, d), lambda j: (0, 0)),
                  pl.BlockSpec((d, tn), lambda j: (0, j)),
                  pl.BlockSpec((1, tn), lambda j: (0, j))],
        out_specs=pl.BlockSpec((rows, tn), lambda j: (0, j)),
        out_shape=jax.ShapeDtypeStruct((rows, n), F32),
        compiler_params=_params("arbitrary"),
        name="modulation",
    )(cc, w_mod, b_mod)


def _fill_bias_tables(rexp_ref, h, o_ref):
    k = lax.broadcasted_iota(jnp.int32, (GRID_W, TOE_W), 0)
    q = lax.broadcasted_iota(jnp.int32, (GRID_W, TOE_W), 1) % GRID_W
    diff = k - q + (NA_COLS - 1)
    toe = jnp.zeros((GRID_W, TOE_W), F32)
    for d in range(N_BIAS_COLS):
        toe = jnp.where(diff == d, rexp_ref[h, d:d + 1, :], toe)
    col_start = jnp.clip(q - NA_COLS // 2, 0, GRID_W - NA_COLS)
    col_ok = (k >= col_start) & (k < col_start + NA_COLS)
    toe = jnp.where(col_ok, toe, NEG_INF)
    ri = lax.broadcasted_iota(jnp.int32, (GRID_W, Q_BLK), 1) // GRID_W
    masked = jnp.full((GRID_W, Q_BLK), NEG_INF, F32)

    for typ in range(3):
        off = (NA_ROWS - 1, NA_ROWS // 2 - 1, NA_ROWS - BAND_ROWS + Q_ROWS - 1)[typ]
        for jj in range(BAND_ROWS):
            if typ == 0:
                lo, hi = (0, Q_ROWS) if jj < NA_ROWS else (0, 0)
            elif typ == 1:
                lo, hi = max(jj - NA_ROWS + 1, 0), min(jj, Q_ROWS - 1) + 1
            else:
                lo, hi = (0, Q_ROWS) if jj >= BAND_ROWS - NA_ROWS else (0, 0)
            strip = masked
            if lo < hi:
                e0 = N_BIAS_ROWS - 1 - jj - off
                rolled = pltpu.roll(toe, (-e0 * GRID_W) % TOE_W, axis=1)[:, :Q_BLK]
                strip = jnp.where((ri >= lo) & (ri < hi), rolled, NEG_INF)
            o_ref[h, typ, jj * GRID_W:(jj + 1) * GRID_W, :] = strip


def _bias_rows(rpb):
    rexp = jnp.repeat(jnp.transpose(rpb[:, ::-1, :], (0, 2, 1)) * LOG2E, GRID_W, axis=2)
    return jnp.pad(rexp, ((0, 0), (0, 0), (0, TOE_W - N_BIAS_ROWS * GRID_W)))


def _transpose_kernel(w_ref, o_ref):
    o_ref[...] = w_ref[...].T


def _transpose_cols(w, lo, hi, tn=MXU_DIM):
    rows = w.shape[0]
    assert lo % tn == 0 and hi % tn == 0
    return pl.pallas_call(
        _transpose_kernel,
        grid=((hi - lo) // tn,),
        in_specs=[pl.BlockSpec((rows, tn), lambda i: (0, lo // tn + i))],
        out_specs=pl.BlockSpec((tn, rows), lambda i: (i, 0)),
        out_shape=jax.ShapeDtypeStruct((hi - lo, rows), w.dtype),
        compiler_params=_params("arbitrary"),
        name="weight_transpose",
    )(w)


def _adaln(x, g_ref, scale_ref, shift_ref):
    ms = jnp.mean(x * x, axis=-1, keepdims=True)
    gm = g_ref[...] * (1.0 + scale_ref[0])
    return ((x * lax.rsqrt(ms + EPS)) * gm + shift_ref[0]).astype(BF16)


def _head_rms(t, bd_ref, gain):
    sq = (t * t).astype(BF16)
    bd = bd_ref[...]
    m = jnp.concatenate([_dot(sq[:, :MXU_DIM], bd), _dot(sq[:, MXU_DIM:], bd)], axis=-1)
    return t * lax.rsqrt(m + EPS) * gain


def _rope(t, cos_ref, sin_ref):
    lane = lax.broadcasted_iota(jnp.int32, (1, LANES), 1)
    first = (lane % 32) < 16
    cos, sin = cos_ref[...], sin_ref[...]
    parts = []
    for j in range(t.shape[1] // LANES):
        c = t[:, j * LANES:(j + 1) * LANES]
        swapped = jnp.where(first, pltpu.roll(c, LANES - 16, axis=1), pltpu.roll(c, 16, axis=1))
        parts.append(c * cos + swapped * sin)
    return jnp.concatenate(parts, axis=-1)


def _silu(t):
    return t * jax.nn.sigmoid(t)


def _conv_window(ext, n, cw_ref, cb_ref):
    rows = ext.shape[0]

    def shifted(k):
        return pltpu.roll(ext, (rows - k) % rows, axis=0)[HALO:HALO + n]

    y = cb_ref[...] + cw_ref[0:1, :] * shifted(-1)
    y = y + cw_ref[1:2, :] * ext[HALO:HALO + n]
    y = y + cw_ref[2:3, :] * shifted(1)
    return y + cw_ref[3:4, :] * shifted(2)


def _inproj_kernel(x_ref, xp_ref, xn_ref, g_ref, scale_ref, shift_ref, w_ref, wt_ref, cos_ref, sin_ref, bd_ref,
                   qg_ref, kg_ref, cw_ref, cb_ref, qr_ref, qp_ref, kr_ref, vt_ref, zat_ref, uc_ref, zl_ref):
    tm = x_ref.shape[1]
    aw = ATTN_WIDTH
    i, n_tiles = pl.program_id(0), pl.num_programs(0)
    hb_ext = jnp.concatenate([_adaln(xp_ref[0], g_ref, scale_ref, shift_ref),
                              _adaln(x_ref[0], g_ref, scale_ref, shift_ref),
                              _adaln(xn_ref[0], g_ref, scale_ref, shift_ref)], axis=0)
    hb = hb_ext[HALO:HALO + tm]

    def cols(j):
        return w_ref[:, j * aw:(j + 1) * aw]

    u_ext = _dot(hb_ext, cols(COL_U))
    row = lax.broadcasted_iota(jnp.int32, (tm + 2 * HALO, 1), 0)
    inside = ((row >= HALO) | (i > 0)) & ((row < HALO + tm) | (i < n_tiles - 1))
    u_ext = jnp.where(inside, u_ext, 0.0)

    qn = _head_rms(_dot(hb, cols(COL_Q)), bd_ref, qg_ref[...])
    qp_ref[0] = qn.astype(BF16)
    qr_ref[0] = _rope(qn, cos_ref, sin_ref).astype(BF16)
    kn = _head_rms(_dot(hb, cols(COL_K)), bd_ref, kg_ref[...])
    kr_ref[0] = _rope(kn, cos_ref, sin_ref).astype(BF16)

    zl_ref[0] = _silu(_dot(hb, cols(COL_ZL))).astype(BF16)
    vt_ref[0] = _dot_nt(wt_ref[:aw, :], hb).astype(BF16)
    zat_ref[0] = _silu(_dot_nt(wt_ref[aw:, :], hb)).astype(BF16)
    for r in range(0, tm, CONV_ROWS):
        uc_ref[0, r:r + CONV_ROWS, :] = _conv_window(u_ext[r:r + CONV_ROWS + 2 * HALO], CONV_ROWS, cw_ref, cb_ref)


def _inproj(x, g, scale, shift, w_bf, wt_bf, cos_t, sin_t, bd, qg, kg, conv_w, conv_b, tm=512):
    b, s, d = x.shape
    aw = ATTN_WIDTH
    tok = lambda i, j: (j, i, 0)
    per_b = lambda i, j: (j, 0, 0)
    const = lambda i, j: (0, 0)
    halo_per_tile = tm // HALO
    n_halo = s // HALO
    out_bf = jax.ShapeDtypeStruct((b, s, aw), BF16)
    out_t = jax.ShapeDtypeStruct((b, aw, s), BF16)
    out_f32 = jax.ShapeDtypeStruct((b, s, aw), F32)
    blk = pl.BlockSpec((1, tm, aw), tok)
    blk_t = pl.BlockSpec((1, aw, tm), lambda i, j: (j, 0, i))
    return pl.pallas_call(
        _inproj_kernel,
        grid=(s // tm, b),
        in_specs=[pl.BlockSpec((1, tm, d), tok),
                  pl.BlockSpec((1, HALO, d), lambda i, j: (j, jnp.maximum(i * halo_per_tile - 1, 0), 0)),
                  pl.BlockSpec((1, HALO, d), lambda i, j: (j, jnp.minimum((i + 1) * halo_per_tile, n_halo - 1), 0)),
                  pl.BlockSpec((1, d), const),
                  pl.BlockSpec((1, 1, d), per_b),
                  pl.BlockSpec((1, 1, d), per_b),
                  pl.BlockSpec(w_bf.shape, const),
                  pl.BlockSpec(wt_bf.shape, const),
                  pl.BlockSpec((tm, LANES), lambda i, j: (i, 0)),
                  pl.BlockSpec((tm, LANES), lambda i, j: (i, 0)),
                  pl.BlockSpec((MXU_DIM, MXU_DIM), const),
                  pl.BlockSpec((1, aw), const),
                  pl.BlockSpec((1, aw), const),
                  pl.BlockSpec((CONV_WIDTH, aw), const),
                  pl.BlockSpec((1, aw), const)],
        out_specs=[blk, blk, blk, blk_t, blk_t, blk, blk],
        out_shape=[out_bf, out_bf, out_bf, out_t, out_t, out_f32, out_bf],
        compiler_params=_params("arbitrary", "arbitrary"),
        name="inproj",
    )(x, x, x, g, scale, shift, w_bf, wt_bf, cos_t, sin_t, bd, qg, kg, conv_w, conv_b)


def _ctxproj_kernel(c_ref, g_ref, scale_ref, shift_ref, w_ref, wt_ref, bd_ref, kg_ref, cw_ref, cb_ref,
                    kc_ref, vct_ref, uc_ref):
    nb, l, d = c_ref.shape
    aw = ATTN_WIDTH
    hb = _adaln(c_ref[...].reshape(nb * l, d), g_ref, scale_ref, shift_ref)
    kc = _head_rms(_dot(hb, w_ref[:, COL_K * aw:(COL_K + 1) * aw]), bd_ref, kg_ref[...]).astype(BF16)
    u = _dot(hb, w_ref[:, COL_U * aw:(COL_U + 1) * aw])
    vct = _dot_nt(wt_ref[:aw, :], hb).astype(BF16)
    pad = jnp.zeros((HALO, aw), F32)
    for bi in range(nb):
        kc_ref[bi] = kc[bi * l:(bi + 1) * l]
        uc_ref[bi] = _conv_window(jnp.concatenate([pad, u[bi * l:(bi + 1) * l], pad], axis=0), l, cw_ref, cb_ref)
        vct_ref[bi] = vct[:, bi * l:(bi + 1) * l]


def _ctxproj(ctx, g, scale_c, shift_c, w_ctx_bf, wt_ctx_bf, bd, kg, conv_w, conv_b, nb=4):
    b, l, d = ctx.shape
    aw = ATTN_WIDTH
    const = lambda i: (0, 0)
    blk = pl.BlockSpec((nb, l, aw), lambda i: (i, 0, 0))
    return pl.pallas_call(
        _ctxproj_kernel,
        grid=(b // nb,),
        in_specs=[pl.BlockSpec((nb, l, d), lambda i: (i, 0, 0)),
                  pl.BlockSpec((1, d), const),
                  pl.BlockSpec((1, 1, d), lambda i: (0, 0, 0)),
                  pl.BlockSpec((1, 1, d), lambda i: (0, 0, 0)),
                  pl.BlockSpec(w_ctx_bf.shape, const),
                  pl.BlockSpec(wt_ctx_bf.shape, const),
                  pl.BlockSpec((MXU_DIM, MXU_DIM), const),
                  pl.BlockSpec((1, aw), const),
                  pl.BlockSpec((CONV_WIDTH, aw), const),
                  pl.BlockSpec((1, aw), const)],
        out_specs=[blk, pl.BlockSpec((nb, aw, l), lambda i: (i, 0, 0)), blk],
        out_shape=[jax.ShapeDtypeStruct((b, l, aw), BF16),
                   jax.ShapeDtypeStruct((b, aw, l), BF16),
                   jax.ShapeDtypeStruct((b, l, aw), F32)],
        compiler_params=_params("arbitrary"),
        name="ctxproj",
    )(ctx, g, scale_c, shift_c, w_ctx_bf, wt_ctx_bf, bd, kg, conv_w, conv_b)


def _attn_kernel(qr_ref, qp_ref, kr_ref, vt_ref, kc_ref, vct_ref, zat_ref, rexp_ref, o_ref, bias_ref, s_buf0, s_buf1):
    @pl.when(pl.program_id(1) == 0)
    def _():
        for h in range(2):
            _fill_bias_tables(rexp_ref, h, bias_ref)

    rows = qr_ref.shape[1] // GRID_W
    n_blk = rows // Q_ROWS
    chunk_rows = KEY_CHUNK // GRID_W
    n_tiles = Q_BLK // LANES
    lane_head = lax.broadcasted_iota(jnp.int32, (1, LANES), 1) // HEAD_DIM
    s_bufs = (s_buf0, s_buf1)
    ones_rows = jnp.ones((ONES_ROWS, KEY_CHUNK), BF16)

    def in_window(typ, jj, tile):
        if typ == 0:
            return jj < NA_ROWS
        if typ == 2:
            return jj >= BAND_ROWS - NA_ROWS
        return any(0 <= jj - ri < NA_ROWS for ri in range(tile * Q_ROWS // n_tiles, (tile + 1) * Q_ROWS // n_tiles))

    def local_chunks(typ):
        return [c for c in range(BAND // KEY_CHUNK)
                if any(in_window(typ, c * chunk_rows + r, t) for r in range(chunk_rows) for t in range(n_tiles))]

    def sub_blocks(c):
        for r in range(chunk_rows):
            for t in range(n_tiles):
                yield c * chunk_rows + r, t, slice(r * GRID_W, (r + 1) * GRID_W), slice(t * LANES, (t + 1) * LANES)

    def item(rb):
        rb = jnp.asarray(rb, jnp.int32)
        q0 = pl.multiple_of(rb * Q_BLK, Q_BLK)
        band0 = pl.multiple_of(jnp.clip(rb * Q_ROWS - NA_ROWS // 2, 0, rows - BAND_ROWS) * GRID_W, Q_BLK)
        return q0, band0

    def col_max(m, blk, t):
        m = list(m)
        m[t] = jnp.maximum(m[t], jnp.max(blk, axis=0, keepdims=True))
        return tuple(m)

    def score_steps(rb, h, typ):
        q0, band0 = item(rb)
        s_buf = s_bufs[h]
        mine = lane_head == h
        qr = jnp.where(mine, qr_ref[0, pl.ds(q0, Q_BLK), :], jnp.zeros((), BF16))
        qp = jnp.where(mine, qp_ref[0, pl.ds(q0, Q_BLK), :], jnp.zeros((), BF16))

        def local(c, m):
            n_rows = 1 + max(r for r in range(chunk_rows)
                             if any(in_window(typ, c * chunk_rows + r, t) for t in range(n_tiles)))
            s = _dot_nt(kr_ref[0, pl.ds(band0 + c * KEY_CHUNK, n_rows * GRID_W), :], qr)
            for jj, t, r_sl, l_sl in sub_blocks(c):
                if in_window(typ, jj, t):
                    k_sl = slice(jj * GRID_W, (jj + 1) * GRID_W)
                    blk = s[r_sl, l_sl] + bias_ref[h, typ, k_sl, l_sl]
                    s_buf[k_sl, l_sl] = blk
                    m = col_max(m, blk, t)
            return m

        def context(m):
            s = _dot_nt(kc_ref[0], qp)
            s_buf[pl.ds(BAND, KEY_CHUNK), :] = s
            for t in range(n_tiles):
                m = col_max(m, s[:, t * LANES:(t + 1) * LANES], t)
            return m

        return [functools.partial(local, c) for c in local_chunks(typ)] + [context]

    def value_steps(rb, h, typ, m):
        q0, band0 = item(rb)
        s_buf = s_bufs[h]
        hd = pl.ds(h * HEAD_DIM, HEAD_DIM)

        def probs(k_sl, t):
            return jnp.exp2(s_buf[k_sl, t * LANES:(t + 1) * LANES] - m[t]).astype(BF16)

        def local(c, acc):
            zero_blk = jnp.zeros((GRID_W, LANES), BF16)
            row_blks = []
            for r in range(chunk_rows):
                jj = c * chunk_rows + r
                k_sl = slice(jj * GRID_W, (jj + 1) * GRID_W)
                row_blks.append(jnp.concatenate(
                    [probs(k_sl, t) if in_window(typ, jj, t) else zero_blk for t in range(n_tiles)], axis=1))
            p = jnp.concatenate(row_blks, axis=0)
            vt = jnp.concatenate([vt_ref[0, hd, pl.ds(band0 + c * KEY_CHUNK, KEY_CHUNK)], ones_rows], axis=0)
            return acc + _dot(vt, p)

        def context(acc):
            k_sl = slice(BAND, BAND + KEY_CHUNK)
            p = jnp.concatenate([probs(k_sl, t) for t in range(n_tiles)], axis=1)
            vt = jnp.concatenate([vct_ref[0, hd, :], ones_rows], axis=0)
            return acc + _dot(vt, p)

        return [functools.partial(local, c) for c in local_chunks(typ)] + [context]

    def write_out(rb, h, acc):
        q0, _ = item(rb)
        hd = pl.ds(h * HEAD_DIM, HEAD_DIM)
        gated = (acc[:HEAD_DIM] / acc[HEAD_DIM:HEAD_DIM + 1]) * zat_ref[0, hd, pl.ds(q0, Q_BLK)].astype(F32)
        o_ref[0, hd, pl.ds(q0, Q_BLK)] = gated.astype(o_ref.dtype)

    m_init = (jnp.full((1, LANES), -jnp.inf, F32),) * n_tiles
    acc_init = jnp.zeros((HEAD_DIM + ONES_ROWS, Q_BLK), F32)

    def overlapped(score_item, value_item, out_item, carry):
        m_prev, acc_prev = carry
        s_steps = score_steps(*score_item) if score_item else []
        v_steps = value_steps(*value_item, m_prev) if value_item else []
        m, acc = m_init, acc_init
        for i, (s_step, v_step) in enumerate(itertools.zip_longest(s_steps, v_steps)):
            if s_step:
                m = s_step(m)
            if i == 0 and out_item:
                write_out(*out_item, acc_prev)
            if v_step:
                acc = v_step(acc)
        return m, acc

    first, mid, last = 0, 1, 2
    carry = overlapped((0, 0, first), None, None, (None, None))
    carry = overlapped((0, 1, first), (0, 0, first), None, carry)
    carry = overlapped((1, 0, mid), (0, 1, first), (0, 0), carry)

    def body(rb, carry):
        carry = overlapped((rb, 1, mid), (rb, 0, mid), (rb - 1, 1), carry)
        return overlapped((rb + 1, 0, mid), (rb, 1, mid), (rb, 0), carry)

    carry = lax.fori_loop(1, n_blk - 2, body, carry, unroll=3)
    carry = overlapped((n_blk - 2, 1, mid), (n_blk - 2, 0, mid), (n_blk - 3, 1), carry)
    carry = overlapped((n_blk - 1, 0, last), (n_blk - 2, 1, mid), (n_blk - 2, 0), carry)
    carry = overlapped((n_blk - 1, 1, last), (n_blk - 1, 0, last), (n_blk - 2, 1), carry)
    carry = overlapped(None, (n_blk - 1, 1, last), (n_blk - 1, 0), carry)
    write_out(n_blk - 1, 1, carry[1])


def _attention(qr, qp, kr, vt, kc, vct, zat, rexp):
    b, s, aw = qr.shape
    l = kc.shape[1]
    n_pairs = aw // LANES
    lat = pl.BlockSpec((1, s, LANES), lambda p, i: (i, 0, p))
    lat_t = pl.BlockSpec((1, LANES, s), lambda p, i: (i, p, 0))
    return pl.pallas_call(
        _attn_kernel,
        grid=(n_pairs, b),
        in_specs=[lat, lat, lat, lat_t,
                  pl.BlockSpec((1, l, LANES), lambda p, i: (i, 0, p)),
                  pl.BlockSpec((1, LANES, l), lambda p, i: (i, p, 0)),
                  lat_t,
                  pl.BlockSpec((2, N_BIAS_COLS, TOE_W), lambda p, i: (p, 0, 0))],
        out_specs=lat_t,
        out_shape=jax.ShapeDtypeStruct((b, aw, s), BF16),
        scratch_shapes=[pltpu.VMEM((2, 3, BAND, Q_BLK), F32),
                        pltpu.VMEM((BAND + l, Q_BLK), F32), pltpu.VMEM((BAND + l, Q_BLK), F32)],
        compiler_params=_params("arbitrary", "arbitrary"),
        name="attention",
    )(qr, qp, kr, vt, kc, vct, zat, rexp)


def _lru_pitch(tc):
    pitch = tc + SUBLANES
    assert pitch % (2 * SUBLANES) == SUBLANES
    return pitch


def _lru_gate_consts(ba, bx, lam):
    return 0.5 * ba, 0.5 * bx, jax.nn.softplus(-lam) * (-0.5 * LRU_C * LOG2E)


def _lru_fill(bi, uc, wg, consts, a_s, b_s):
    half_ba, half_bx, half_log2a = consts
    tc = uc.shape[0]
    ucb = uc.astype(BF16)

    def gate_tanh(gi, half_bias):
        pre = jnp.concatenate([_dot(ucb[:, :MXU_DIM], wg(gi, 0)), _dot(ucb[:, MXU_DIM:], wg(gi, 1))], axis=-1)
        return jnp.tanh(pre + half_bias)

    tr = gate_tanh(0, half_ba)
    ti = gate_tanh(1, half_bx)
    a = jnp.exp2(tr * half_log2a + half_log2a)
    gap = 1.0 - a * a
    mult = gap * lax.rsqrt(jnp.maximum(gap, 1e-30))
    bb = mult * ((0.5 * ti + 0.5) * uc)
    r0 = pl.multiple_of(bi * _lru_pitch(tc), SUBLANES)
    for j in range(LRU_SLABS):
        a_s[j, pl.ds(r0, tc), :] = a[:, j * LANES:(j + 1) * LANES]
        b_s[j, pl.ds(r0, tc), :] = bb[:, j * LANES:(j + 1) * LANES]


def _lru_scan(reverse, tc, nb, a_s, b_s, h_s, carry_s):
    pitch = _lru_pitch(tc)

    n_groups = tc // SUBLANES

    def body(g, hs):
        base = pl.multiple_of(((n_groups - 1 - g) if reverse else g) * SUBLANES, SUBLANES)
        hs = list(hs)
        for k in (reversed(range(SUBLANES)) if reverse else range(SUBLANES)):
            rows = pl.ds(base + k, nb, stride=pitch)
            for j in range(LRU_SLABS):
                hs[j] = a_s[j, rows, :] * hs[j] + b_s[j, rows, :]
                h_s[j, rows, :] = hs[j]
        return tuple(hs)

    hs = tuple(carry_s[:, j * LANES:(j + 1) * LANES] for j in range(LRU_SLABS))
    hs = lax.fori_loop(0, n_groups, body, hs)
    for j in range(LRU_SLABS):
        carry_s[:, j * LANES:(j + 1) * LANES] = hs[j]


def _lru_rows(bi, tc, h_s):
    r0 = pl.multiple_of(bi * _lru_pitch(tc), SUBLANES)
    return jnp.concatenate([h_s[j, pl.ds(r0, tc), :] for j in range(LRU_SLABS)], axis=-1)


def _lru_fwd_kernel(u_ref, uc_ref, wg_ref, ba_ref, bx_ref, lam_ref, hf_ref, hb0_ref, a_s, b_s, h_s, carry_s):
    nb, tc = u_ref.shape[0], u_ref.shape[1]
    consts = [_lru_gate_consts(ba_ref[d], bx_ref[d], lam_ref[d]) for d in range(2)]

    def fill_all(src_ref, d, unroll):
        def step(bi, carry):
            _lru_fill(bi, src_ref[bi], lambda gi, hf: wg_ref[d, gi, hf], consts[d], a_s, b_s)
            return carry

        lax.fori_loop(0, nb, step, 0, unroll=unroll)

    @pl.when(pl.program_id(0) == 0)
    def _():
        carry_s[...] = jnp.zeros_like(carry_s)
        fill_all(uc_ref, 1, 1)
        _lru_scan(True, tc, nb, a_s, b_s, h_s, carry_s)
        hb0_ref[...] = carry_s[...]
        carry_s[...] = jnp.zeros_like(carry_s)
        fill_all(uc_ref, 0, 1)
        _lru_scan(False, tc, nb, a_s, b_s, h_s, carry_s)

    fill_all(u_ref, 0, 4)
    _lru_scan(False, tc, nb, a_s, b_s, h_s, carry_s)

    def emit(bi, carry):
        hf_ref[bi] = _lru_rows(bi, tc, h_s).astype(hf_ref.dtype)
        return carry

    lax.fori_loop(0, nb, emit, 0)


def _lru_fwd(u, u_c, wg, ba, bx, lam):
    b, s, lw = u.shape
    tc = LRU_CHUNK
    assert u_c.shape[1] == tc and s % tc == 0
    full = lambda *shape: pl.BlockSpec(shape, lambda i: (0,) * len(shape))
    cblk = pl.BlockSpec((b, tc, lw), lambda i: (0, i, 0))
    slab = pltpu.VMEM((LRU_SLABS, b * _lru_pitch(tc), LANES), F32)
    return pl.pallas_call(
        _lru_fwd_kernel,
        grid=(s // tc,),
        in_specs=[cblk, full(b, tc, lw), full(2, 2, 2, MXU_DIM, MXU_DIM), full(2, 1, lw), full(2, 1, lw), full(2, 1, lw)],
        out_specs=[cblk, full(b, lw)],
        out_shape=[jax.ShapeDtypeStruct((b, s, lw), BF16), jax.ShapeDtypeStruct((b, lw), F32)],
        scratch_shapes=[slab, slab, slab, pltpu.VMEM((b, lw), F32)],
        compiler_params=_params("arbitrary"),
        name="rglru_fwd",
    )(u, u_c, wg, ba, bx, lam)


def _lru_bwd_out_kernel(u_ref, hf_ref, zl_ref, agt_ref, x_ref, hb0_ref, gate_ref, wg_ref, ba_ref, bx_ref, lam_ref,
                        wo_ref, o_ref, a_s, b_s, h_s, carry_s):
    nb, tc = u_ref.shape[0], u_ref.shape[1]
    consts = _lru_gate_consts(ba_ref[...], bx_ref[...], lam_ref[...])

    @pl.when(pl.program_id(0) == 0)
    def _():
        carry_s[...] = hb0_ref[...]

    def fill(bi, carry):
        _lru_fill(bi, u_ref[bi], lambda gi, hf: wg_ref[gi, hf], consts, a_s, b_s)
        return carry

    lax.fori_loop(0, nb, fill, 0, unroll=4)
    _lru_scan(True, tc, nb, a_s, b_s, h_s, carry_s)

    def emit(pi, carry):
        pair = (2 * pi, 2 * pi + 1)
        y = jnp.concatenate([((hf_ref[bi].astype(F32) + _lru_rows(bi, tc, h_s)) * zl_ref[bi].astype(F32)).astype(BF16)
                             for bi in pair], axis=0)
        at = jnp.concatenate([agt_ref[bi] for bi in pair], axis=1)
        mix = lax.dot_general(at, wo_ref[:ATTN_WIDTH, :], (((0,), (0,)), ((), ())), preferred_element_type=F32)
        mix = mix + _dot(y, wo_ref[ATTN_WIDTH:, :])
        for k, bi in enumerate(pair):
            o_ref[bi] = x_ref[bi] + gate_ref[bi] * mix[k * tc:(k + 1) * tc]
        return carry

    lax.fori_loop(0, nb // 2, emit, 0)


def _lru_bwd_out(u, hf, zl, agt, x, hb0, gate, wg, ba, bx, lam, wo_bf):
    b, s, lw = u.shape
    d = x.shape[2]
    tc = LRU_BWD_CHUNK
    n_chunks = s // tc
    assert s % tc == 0 and b % 2 == 0
    full = lambda *shape: pl.BlockSpec(shape, lambda i: (0,) * len(shape))
    rev = lambda i: n_chunks - 1 - i
    cblk = pl.BlockSpec((b, tc, lw), lambda i: (0, rev(i), 0))
    xblk = pl.BlockSpec((b, tc, d), lambda i: (0, rev(i), 0))
    slab = pltpu.VMEM((LRU_SLABS, b * _lru_pitch(tc), LANES), F32)
    return pl.pallas_call(
        _lru_bwd_out_kernel,
        grid=(n_chunks,),
        in_specs=[cblk, cblk, cblk,
                  pl.BlockSpec((b, ATTN_WIDTH, tc), lambda i: (0, 0, rev(i))),
                  xblk, full(b, lw), full(b, 1, d),
                  full(2, 2, MXU_DIM, MXU_DIM), full(1, lw), full(1, lw), full(1, lw),
                  full(ATTN_WIDTH + lw, d)],
        out_specs=xblk,
        out_shape=jax.ShapeDtypeStruct(x.shape, x.dtype),
        scratch_shapes=[slab, slab, slab, pltpu.VMEM((b, lw), F32)],
        compiler_params=_params("arbitrary"),
        name="rglru_bwd_outproj",
    )(u, hf, zl, agt, x, hb0, gate, wg, ba, bx, lam, wo_bf)


def _rope_tables(s):
    t = jnp.arange(s)
    half = HEAD_DIM // 4
    inv = ROPE_BASE ** (-jnp.arange(half, dtype=F32) / half)
    ang_r = (t // GRID_W).astype(F32)[:, None] * inv[None, :]
    ang_c = (t % GRID_W).astype(F32)[:, None] * inv[None, :]
    cos = jnp.concatenate([jnp.cos(ang_r)] * 2 + [jnp.cos(ang_c)] * 2, axis=-1)
    sin = jnp.concatenate([-jnp.sin(ang_r), jnp.sin(ang_r), -jnp.sin(ang_c), jnp.sin(ang_c)], axis=-1)
    return jnp.tile(cos, (1, LANES // HEAD_DIM)), jnp.tile(sin, (1, LANES // HEAD_DIM))


def _block_diag_gates(w):
    n = MXU_DIM // LRU_BLOCK
    lead = w.shape[:-3]
    w = w.reshape(lead + (2, n, LRU_BLOCK, LRU_BLOCK))
    on_diag = jnp.eye(n, dtype=bool)[:, None, :, None]
    blocks = jnp.where(on_diag, w[..., :, :, None, :], jnp.zeros((), w.dtype))
    return blocks.reshape(lead + (2, MXU_DIM, MXU_DIM))


def kernel(x, c, ctx, c_ctx, norm_g, w_mod, b_mod, w_in, w_out, q_norm_g, k_norm_g, rpb, conv_w, conv_b,
           lru_wa, lru_ba, lru_wx, lru_bx, lru_lam):
    bsz, s, d = x.shape
    assert w_in.shape[0] == 1 and d == D_MODEL and s % (GRID_W * Q_ROWS) == 0
    aw, lw = ATTN_WIDTH, LRU_WIDTH

    pad_rows = 2 * SUBLANES - bsz - 1
    cc = jnp.concatenate([c, c_ctx[None, :], jnp.zeros((pad_rows, d), F32)], axis=0)
    mod = _modulation(cc, w_mod[0], b_mod[0][None, :])
    shift, scale, gate = [mod[:bsz, i * d:(i + 1) * d][:, None, :] for i in range(3)]
    shift_c, scale_c = [mod[bsz:bsz + 1, i * d:(i + 1) * d][:, None, :] for i in range(2)]

    w = w_in[0]
    w_bf = w.astype(BF16)
    wt_bf = _transpose_cols(w_bf, COL_V * aw, (COL_ZA + 1) * aw)
    w_ctx_bf, wt_ctx_bf = w_bf, wt_bf
    g = norm_g[0][None, :]
    cos_t, sin_t = _rope_tables(s)
    blk = jnp.arange(MXU_DIM) // HEAD_DIM
    bd = jnp.where(blk[:, None] == blk[None, :], 1.0 / HEAD_DIM, 0.0).astype(BF16)
    qg = jnp.tile(q_norm_g[0] * (HEAD_DIM ** -0.5 * LOG2E), N_HEADS)[None, :]
    kg = jnp.tile(k_norm_g[0], N_HEADS)[None, :]

    cw, cb = conv_w[0], conv_b[0][None, :]
    qr, qp, kr, vt, zat, u, zl = _inproj(x, g, scale, shift, w_bf, wt_bf, cos_t, sin_t, bd, qg, kg, cw, cb)
    kc, vct, u_c = _ctxproj(ctx, g, scale_c, shift_c, w_ctx_bf, wt_ctx_bf, bd, kg, cw, cb)

    agt = _attention(qr, qp, kr, vt, kc, vct, zat, _bias_rows(rpb[0]))

    wg = _block_diag_gates(jnp.stack([lru_wa[0], lru_wx[0]], axis=1))
    wg = (0.5 * wg).astype(BF16)
    ba, bx, lam = lru_ba[0][:, None, :], lru_bx[0][:, None, :], lru_lam[0][:, None, :]
    hf, hb0 = _lru_fwd(u, u_c, wg, ba, bx, lam)
    return _lru_bwd_out(u, hf, zl, agt, x, hb0, gate, wg[1], ba[1], bx[1], lam[1], w_out[0].astype(BF16))
```

```python
import functools
import itertools
import math

import jax
import jax.numpy as jnp
from jax import lax
from jax.experimental import pallas as pl
from jax.experimental.pallas import tpu as pltpu

F32 = jnp.float32
BF16 = jnp.bfloat16

D_MODEL = 1024
GRID_W = 64
HEAD_DIM = 64
ATTN_WIDTH = 512
LRU_WIDTH = 512
N_HEADS = ATTN_WIDTH // HEAD_DIM
LRU_BLOCK = 64
NA_ROWS = 8
NA_COLS = 16
CONV_WIDTH = 4
LRU_C = 8.0
ROPE_BASE = 10000.0
EPS = 1e-6
NEG_INF = -1e30
LOG2E = math.log2(math.e)

LANES = 128
SUBLANES = 8
MXU_DIM = 256
VMEM_LIMIT_BYTES = 58 * 1024 * 1024

Q_ROWS = 4
BAND_ROWS = Q_ROWS + NA_ROWS
Q_BLK = Q_ROWS * GRID_W
BAND = BAND_ROWS * GRID_W
KEY_CHUNK = 256
ONES_ROWS = 16
N_BIAS_ROWS = 2 * NA_ROWS - 1
N_BIAS_COLS = 2 * NA_COLS - 1
TOE_W = 1024

LRU_CHUNK = 256
LRU_BWD_CHUNK = 128
LRU_SLABS = LRU_WIDTH // LANES
OUT_GROUP = 8
CONV_ROWS = 128
COL_Q, COL_K, COL_V, COL_ZA, COL_U, COL_ZL = range(6)
HALO = 16


def _dot(a, b):
    return jnp.dot(a, b, preferred_element_type=F32)


def _dot_nt(a, b):
    return lax.dot_general(a, b, (((1,), (1,)), ((), ())), preferred_element_type=F32)


def _params(*semantics):
    return pltpu.CompilerParams(dimension_semantics=semantics, vmem_limit_bytes=VMEM_LIMIT_BYTES)


def _split_bf16(t):
    hi = t.astype(BF16)
    lo = (t - hi.astype(F32)).astype(BF16)
    return hi, lo


def _mod_kernel(cc_ref, w_ref, b_ref, o_ref):
    cc = cc_ref[...]
    s = cc * jax.nn.sigmoid(cc)
    s_hi, s_lo = _split_bf16(s)
    w_hi, w_lo = _split_bf16(w_ref[...])
    o_ref[...] = _dot(s_hi, w_hi) + _dot(s_lo, w_hi) + _dot(s_hi, w_lo) + b_ref[...]


def _modulation(cc, w_mod, b_mod):
    rows, d = cc.shape
    n = w_mod.shape[1]
    tn = 512
    return pl.pallas_call(
        _mod_kernel,
        grid=(n // tn,),
        in_specs=[pl.BlockSpec((rows, d), lambda j: (0, 0)),
                  pl.BlockSpec((d, tn), lambda j: (0, j)),
                  pl.BlockSpec((1, tn), lambda j: (0, j))],
        out_specs=pl.BlockSpec((rows, tn), lambda j: (0, j)),
        out_shape=jax.ShapeDtypeStruct((rows, n), F32),
        compiler_params=_params("arbitrary"),
        name="modulation",
    )(cc, w_mod, b_mod)


def _fill_bias_tables(rexp_ref, h, o_ref):
    k = lax.broadcasted_iota(jnp.int32, (GRID_W, TOE_W), 0)
    q = lax.broadcasted_iota(jnp.int32, (GRID_W, TOE_W), 1) % GRID_W
    diff = k - q + (NA_COLS - 1)
    toe = jnp.zeros((GRID_W, TOE_W), F32)
    for d in range(N_BIAS_COLS):
        toe = jnp.where(diff == d, rexp_ref[h, d:d + 1, :], toe)
    col_start = jnp.clip(q - NA_COLS // 2, 0, GRID_W - NA_COLS)
    col_ok = (k >= col_start) & (k < col_start + NA_COLS)
    toe = jnp.where(col_ok, toe, NEG_INF)
    ri = lax.broadcasted_iota(jnp.int32, (GRID_W, Q_BLK), 1) // GRID_W
    masked = jnp.full((GRID_W, Q_BLK), NEG_INF, F32)

    for typ in range(3):
        off = (NA_ROWS - 1, NA_ROWS // 2 - 1, NA_ROWS - BAND_ROWS + Q_ROWS - 1)[typ]
        for jj in range(BAND_ROWS):
            if typ == 0:
                lo, hi = (0, Q_ROWS) if jj < NA_ROWS else (0, 0)
            elif typ == 1:
                lo, hi = max(jj - NA_ROWS + 1, 0), min(jj, Q_ROWS - 1) + 1
            else:
                lo, hi = (0, Q_ROWS) if jj >= BAND_ROWS - NA_ROWS else (0, 0)
            strip = masked
            if lo < hi:
                e0 = N_BIAS_ROWS - 1 - jj - off
                rolled = pltpu.roll(toe, (-e0 * GRID_W) % TOE_W, axis=1)[:, :Q_BLK]
                strip = jnp.where((ri >= lo) & (ri < hi), rolled, NEG_INF)
            o_ref[h, typ, jj * GRID_W:(jj + 1) * GRID_W, :] = strip


def _bias_rows(rpb):
    rexp = jnp.repeat(jnp.transpose(rpb[:, ::-1, :], (0, 2, 1)) * LOG2E, GRID_W, axis=2)
    return jnp.pad(rexp, ((0, 0), (0, 0), (0, TOE_W - N_BIAS_ROWS * GRID_W)))


def _transpose_kernel(w_ref, o_ref):
    o_ref[...] = w_ref[...].T


def _transpose_cols(w, lo, hi, tn=MXU_DIM):
    rows = w.shape[0]
    assert lo % tn == 0 and hi % tn == 0
    return pl.pallas_call(
        _transpose_kernel,
        grid=((hi - lo) // tn,),
        in_specs=[pl.BlockSpec((rows, tn), lambda i: (0, lo // tn + i))],
        out_specs=pl.BlockSpec((tn, rows), lambda i: (i, 0)),
        out_shape=jax.ShapeDtypeStruct((hi - lo, rows), w.dtype),
        compiler_params=_params("arbitrary"),
        name="weight_transpose",
    )(w)


def _adaln(x, g_ref, scale_ref, shift_ref):
    ms = jnp.mean(x * x, axis=-1, keepdims=True)
    gm = g_ref[...] * (1.0 + scale_ref[0])
    return ((x * lax.rsqrt(ms + EPS)) * gm + shift_ref[0]).astype(BF16)


def _head_rms(t, bd_ref, gain):
    sq = (t * t).astype(BF16)
    bd = bd_ref[...]
    m = jnp.concatenate([_dot(sq[:, :MXU_DIM], bd), _dot(sq[:, MXU_DIM:], bd)], axis=-1)
    return t * lax.rsqrt(m + EPS) * gain


def _rope(t, cos_ref, sin_ref):
    lane = lax.broadcasted_iota(jnp.int32, (1, LANES), 1)
    first = (lane % 32) < 16
    cos, sin = cos_ref[...], sin_ref[...]
    parts = []
    for j in range(t.shape[1] // LANES):
        c = t[:, j * LANES:(j + 1) * LANES]
        swapped = jnp.where(first, pltpu.roll(c, LANES - 16, axis=1), pltpu.roll(c, 16, axis=1))
        parts.append(c * cos + swapped * sin)
    return jnp.concatenate(parts, axis=-1)


def _silu(t):
    return t * jax.nn.sigmoid(t)


def _conv_window(ext, n, cw_ref, cb_ref):
    rows = ext.shape[0]

    def shifted(k):
        return pltpu.roll(ext, (rows - k) % rows, axis=0)[HALO:HALO + n]

    y = cb_ref[...] + cw_ref[0:1, :] * shifted(-1)
    y = y + cw_ref[1:2, :] * ext[HALO:HALO + n]
    y = y + cw_ref[2:3, :] * shifted(1)
    return y + cw_ref[3:4, :] * shifted(2)


def _inproj_kernel(x_ref, xp_ref, xn_ref, g_ref, scale_ref, shift_ref, w_ref, wt_ref, cos_ref, sin_ref, bd_ref,
                   qg_ref, kg_ref, cw_ref, cb_ref, qr_ref, qp_ref, kr_ref, vt_ref, zat_ref, uc_ref, zl_ref):
    tm = x_ref.shape[1]
    aw = ATTN_WIDTH
    i, n_tiles = pl.program_id(0), pl.num_programs(0)
    hb_ext = jnp.concatenate([_adaln(xp_ref[0], g_ref, scale_ref, shift_ref),
                              _adaln(x_ref[0], g_ref, scale_ref, shift_ref),
                              _adaln(xn_ref[0], g_ref, scale_ref, shift_ref)], axis=0)
    hb = hb_ext[HALO:HALO + tm]

    def cols(j):
        return w_ref[:, j * aw:(j + 1) * aw]

    u_ext = _dot(hb_ext, cols(COL_U))
    row = lax.broadcasted_iota(jnp.int32, (tm + 2 * HALO, 1), 0)
    inside = ((row >= HALO) | (i > 0)) & ((row < HALO + tm) | (i < n_tiles - 1))
    u_ext = jnp.where(inside, u_ext, 0.0)

    qn = _head_rms(_dot(hb, cols(COL_Q)), bd_ref, qg_ref[...])
    qp_ref[0] = qn.astype(BF16)
    qr_ref[0] = _rope(qn, cos_ref, sin_ref).astype(BF16)
    kn = _head_rms(_dot(hb, cols(COL_K)), bd_ref, kg_ref[...])
    kr_ref[0] = _rope(kn, cos_ref, sin_ref).astype(BF16)

    zl_ref[0] = _silu(_dot(hb, cols(COL_ZL))).astype(BF16)
    vt_ref[0] = _dot_nt(wt_ref[:aw, :], hb).astype(BF16)
    zat_ref[0] = _silu(_dot_nt(wt_ref[aw:, :], hb)).astype(BF16)
    for r in range(0, tm, CONV_ROWS):
        uc_ref[0, r:r + CONV_ROWS, :] = _conv_window(u_ext[r:r + CONV_ROWS + 2 * HALO], CONV_ROWS, cw_ref, cb_ref)


def _inproj(x, g, scale, shift, w_bf, wt_bf, cos_t, sin_t, bd, qg, kg, conv_w, conv_b, tm=1024):
    b, s, d = x.shape
    aw = ATTN_WIDTH
    tok = lambda i, j: (j, i, 0)
    per_b = lambda i, j: (j, 0, 0)
    const = lambda i, j: (0, 0)
    halo_per_tile = tm // HALO
    n_halo = s // HALO
    out_bf = jax.ShapeDtypeStruct((b, s, aw), BF16)
    out_t = jax.ShapeDtypeStruct((b, aw, s), BF16)
    out_f32 = jax.ShapeDtypeStruct((b, s, aw), F32)
    blk = pl.BlockSpec((1, tm, aw), tok)
    blk_t = pl.BlockSpec((1, aw, tm), lambda i, j: (j, 0, i))
    return pl.pallas_call(
        _inproj_kernel,
        grid=(s // tm, b),
        in_specs=[pl.BlockSpec((1, tm, d), tok),
                  pl.BlockSpec((1, HALO, d), lambda i, j: (j, jnp.maximum(i * halo_per_tile - 1, 0), 0)),
                  pl.BlockSpec((1, HALO, d), lambda i, j: (j, jnp.minimum((i + 1) * halo_per_tile, n_halo - 1), 0)),
                  pl.BlockSpec((1, d), const),
                  pl.BlockSpec((1, 1, d), per_b),
                  pl.BlockSpec((1, 1, d), per_b),
                  pl.BlockSpec(w_bf.shape, const),
                  pl.BlockSpec(wt_bf.shape, const),
                  pl.BlockSpec((tm, LANES), lambda i, j: (i, 0)),
                  pl.BlockSpec((tm, LANES), lambda i, j: (i, 0)),
                  pl.BlockSpec((MXU_DIM, MXU_DIM), const),
                  pl.BlockSpec((1, aw), const),
                  pl.BlockSpec((1, aw), const),
                  pl.BlockSpec((CONV_WIDTH, aw), const),
                  pl.BlockSpec((1, aw), const)],
        out_specs=[blk, blk, blk, blk_t, blk_t, blk, blk],
        out_shape=[out_bf, out_bf, out_bf, out_t, out_t, out_f32, out_bf],
        compiler_params=_params("arbitrary", "arbitrary"),
        name="inproj",
    )(x, x, x, g, scale, shift, w_bf, wt_bf, cos_t, sin_t, bd, qg, kg, conv_w, conv_b)


def _ctxproj_kernel(c_ref, g_ref, scale_ref, shift_ref, w_ref, wt_ref, bd_ref, kg_ref, cw_ref, cb_ref,
                    kc_ref, vct_ref, uc_ref):
    nb, l, d = c_ref.shape
    aw = ATTN_WIDTH
    hb = _adaln(c_ref[...].reshape(nb * l, d), g_ref, scale_ref, shift_ref)
    kc = _head_rms(_dot(hb, w_ref[:, COL_K * aw:(COL_K + 1) * aw]), bd_ref, kg_ref[...]).astype(BF16)
    u = _dot(hb, w_ref[:, COL_U * aw:(COL_U + 1) * aw])
    vct = _dot_nt(wt_ref[:aw, :], hb).astype(BF16)
    pad = jnp.zeros((HALO, aw), F32)
    for bi in range(nb):
        kc_ref[bi] = kc[bi * l:(bi + 1) * l]
        uc_ref[bi] = _conv_window(jnp.concatenate([pad, u[bi * l:(bi + 1) * l], pad], axis=0), l, cw_ref, cb_ref)
        vct_ref[bi] = vct[:, bi * l:(bi + 1) * l]


def _ctxproj(ctx, g, scale_c, shift_c, w_ctx_bf, wt_ctx_bf, bd, kg, conv_w, conv_b, nb=4):
    b, l, d = ctx.shape
    aw = ATTN_WIDTH
    const = lambda i: (0, 0)
    blk = pl.BlockSpec((nb, l, aw), lambda i: (i, 0, 0))
    return pl.pallas_call(
        _ctxproj_kernel,
        grid=(b // nb,),
        in_specs=[pl.BlockSpec((nb, l, d), lambda i: (i, 0, 0)),
                  pl.BlockSpec((1, d), const),
                  pl.BlockSpec((1, 1, d), lambda i: (0, 0, 0)),
                  pl.BlockSpec((1, 1, d), lambda i: (0, 0, 0)),
                  pl.BlockSpec(w_ctx_bf.shape, const),
                  pl.BlockSpec(wt_ctx_bf.shape, const),
                  pl.BlockSpec((MXU_DIM, MXU_DIM), const),
                  pl.BlockSpec((1, aw), const),
                  pl.BlockSpec((CONV_WIDTH, aw), const),
                  pl.BlockSpec((1, aw), const)],
        out_specs=[blk, pl.BlockSpec((nb, aw, l), lambda i: (i, 0, 0)), blk],
        out_shape=[jax.ShapeDtypeStruct((b, l, aw), BF16),
                   jax.ShapeDtypeStruct((b, aw, l), BF16),
                   jax.ShapeDtypeStruct((b, l, aw), F32)],
        compiler_params=_params("arbitrary"),
        name="ctxproj",
    )(ctx, g, scale_c, shift_c, w_ctx_bf, wt_ctx_bf, bd, kg, conv_w, conv_b)


def _attn_kernel(qr_ref, qp_ref, kr_ref, vt_ref, kc_ref, vct_ref, zat_ref, rexp_ref, o_ref, bias_ref, s_buf0, s_buf1):
    @pl.when(pl.program_id(1) == 0)
    def _():
        for h in range(2):
            _fill_bias_tables(rexp_ref, h, bias_ref)

    rows = qr_ref.shape[1] // GRID_W
    n_blk = rows // Q_ROWS
    chunk_rows = KEY_CHUNK // GRID_W
    n_tiles = Q_BLK // LANES
    lane_head = lax.broadcasted_iota(jnp.int32, (1, LANES), 1) // HEAD_DIM
    s_bufs = (s_buf0, s_buf1)
    ones_rows = jnp.ones((ONES_ROWS, KEY_CHUNK), BF16)

    def in_window(typ, jj, tile):
        if typ == 0:
            return jj < NA_ROWS
        if typ == 2:
            return jj >= BAND_ROWS - NA_ROWS
        return any(0 <= jj - ri < NA_ROWS for ri in range(tile * Q_ROWS // n_tiles, (tile + 1) * Q_ROWS // n_tiles))

    def local_chunks(typ):
        return [c for c in range(BAND // KEY_CHUNK)
                if any(in_window(typ, c * chunk_rows + r, t) for r in range(chunk_rows) for t in range(n_tiles))]

    def sub_blocks(c):
        for r in range(chunk_rows):
            for t in range(n_tiles):
                yield c * chunk_rows + r, t, slice(r * GRID_W, (r + 1) * GRID_W), slice(t * LANES, (t + 1) * LANES)

    def item(rb):
        rb = jnp.asarray(rb, jnp.int32)
        q0 = pl.multiple_of(rb * Q_BLK, Q_BLK)
        band0 = pl.multiple_of(jnp.clip(rb * Q_ROWS - NA_ROWS // 2, 0, rows - BAND_ROWS) * GRID_W, Q_BLK)
        return q0, band0

    def col_max(m, blk, t):
        m = list(m)
        m[t] = jnp.maximum(m[t], jnp.max(blk, axis=0, keepdims=True))
        return tuple(m)

    def score_steps(rb, h, typ):
        q0, band0 = item(rb)
        s_buf = s_bufs[h]
        mine = lane_head == h
        qr = jnp.where(mine, qr_ref[0, pl.ds(q0, Q_BLK), :], jnp.zeros((), BF16))
        qp = jnp.where(mine, qp_ref[0, pl.ds(q0, Q_BLK), :], jnp.zeros((), BF16))

        def local(c, m):
            n_rows = 1 + max(r for r in range(chunk_rows)
                             if any(in_window(typ, c * chunk_rows + r, t) for t in range(n_tiles)))
            s = _dot_nt(kr_ref[0, pl.ds(band0 + c * KEY_CHUNK, n_rows * GRID_W), :], qr)
            for jj, t, r_sl, l_sl in sub_blocks(c):
                if in_window(typ, jj, t):
                    k_sl = slice(jj * GRID_W, (jj + 1) * GRID_W)
                    blk = s[r_sl, l_sl] + bias_ref[h, typ, k_sl, l_sl]
                    s_buf[k_sl, l_sl] = blk
                    m = col_max(m, blk, t)
            return m

        def context(m):
            s = _dot_nt(kc_ref[0], qp)
            s_buf[pl.ds(BAND, KEY_CHUNK), :] = s
            for t in range(n_tiles):
                m = col_max(m, s[:, t * LANES:(t + 1) * LANES], t)
            return m

        return [functools.partial(local, c) for c in local_chunks(typ)] + [context]

    def value_steps(rb, h, typ, m):
        q0, band0 = item(rb)
        s_buf = s_bufs[h]
        hd = pl.ds(h * HEAD_DIM, HEAD_DIM)

        def probs(k_sl, t):
            return jnp.exp2(s_buf[k_sl, t * LANES:(t + 1) * LANES] - m[t]).astype(BF16)

        def local(c, acc):
            zero_blk = jnp.zeros((GRID_W, LANES), BF16)
            row_blks = []
            for r in range(chunk_rows):
                jj = c * chunk_rows + r
                k_sl = slice(jj * GRID_W, (jj + 1) * GRID_W)
                row_blks.append(jnp.concatenate(
                    [probs(k_sl, t) if in_window(typ, jj, t) else zero_blk for t in range(n_tiles)], axis=1))
            p = jnp.concatenate(row_blks, axis=0)
            vt = jnp.concatenate([vt_ref[0, hd, pl.ds(band0 + c * KEY_CHUNK, KEY_CHUNK)], ones_rows], axis=0)
            return acc + _dot(vt, p)

        def context(acc):
            k_sl = slice(BAND, BAND + KEY_CHUNK)
            p = jnp.concatenate([probs(k_sl, t) for t in range(n_tiles)], axis=1)
            vt = jnp.concatenate([vct_ref[0, hd, :], ones_rows], axis=0)
            return acc + _dot(vt, p)

        return [functools.partial(local, c) for c in local_chunks(typ)] + [context]

    def write_out(rb, h, acc):
        q0, _ = item(rb)
        hd = pl.ds(h * HEAD_DIM, HEAD_DIM)
        gated = (acc[:HEAD_DIM] / acc[HEAD_DIM:HEAD_DIM + 1]) * zat_ref[0, hd, pl.ds(q0, Q_BLK)].astype(F32)
        o_ref[0, hd, pl.ds(q0, Q_BLK)] = gated.astype(o_ref.dtype)

    m_init = (jnp.full((1, LANES), -jnp.inf, F32),) * n_tiles
    acc_init = jnp.zeros((HEAD_DIM + ONES_ROWS, Q_BLK), F32)

    def overlapped(score_item, value_item, out_item, carry):
        m_prev, acc_prev = carry
        s_steps = score_steps(*score_item) if score_item else []
        v_steps = value_steps(*value_item, m_prev) if value_item else []
        m, acc = m_init, acc_init
        for i, (s_step, v_step) in enumerate(itertools.zip_longest(s_steps, v_steps)):
            if s_step:
                m = s_step(m)
            if i == 0 and out_item:
                write_out(*out_item, acc_prev)
            if v_step:
                acc = v_step(acc)
        return m, acc

    first, mid, last = 0, 1, 2
    carry = overlapped((0, 0, first), None, None, (None, None))
    carry = overlapped((0, 1, first), (0, 0, first), None, carry)
    carry = overlapped((1, 0, mid), (0, 1, first), (0, 0), carry)

    def body(rb, carry):
        carry = overlapped((rb, 1, mid), (rb, 0, mid), (rb - 1, 1), carry)
        return overlapped((rb + 1, 0, mid), (rb, 1, mid), (rb, 0), carry)

    carry = lax.fori_loop(1, n_blk - 2, body, carry, unroll=3)
    carry = overlapped((n_blk - 2, 1, mid), (n_blk - 2, 0, mid), (n_blk - 3, 1), carry)
    carry = overlapped((n_blk - 1, 0, last), (n_blk - 2, 1, mid), (n_blk - 2, 0), carry)
    carry = overlapped((n_blk - 1, 1, last), (n_blk - 1, 0, last), (n_blk - 2, 1), carry)
    carry = overlapped(None, (n_blk - 1, 1, last), (n_blk - 1, 0), carry)
    write_out(n_blk - 1, 1, carry[1])


def _attention(qr, qp, kr, vt, kc, vct, zat, rexp):
    b, s, aw = qr.shape
    l = kc.shape[1]
    n_pairs = aw // LANES
    lat = pl.BlockSpec((1, s, LANES), lambda p, i: (i, 0, p))
    lat_t = pl.BlockSpec((1, LANES, s), lambda p, i: (i, p, 0))
    return pl.pallas_call(
        _attn_kernel,
        grid=(n_pairs, b),
        in_specs=[lat, lat, lat, lat_t,
                  pl.BlockSpec((1, l, LANES), lambda p, i: (i, 0, p)),
                  pl.BlockSpec((1, LANES, l), lambda p, i: (i, p, 0)),
                  lat_t,
                  pl.BlockSpec((2, N_BIAS_COLS, TOE_W), lambda p, i: (p, 0, 0))],
        out_specs=lat_t,
        out_shape=jax.ShapeDtypeStruct((b, aw, s), BF16),
        scratch_shapes=[pltpu.VMEM((2, 3, BAND, Q_BLK), F32),
                        pltpu.VMEM((BAND + l, Q_BLK), F32), pltpu.VMEM((BAND + l, Q_BLK), F32)],
        compiler_params=_params("arbitrary", "arbitrary"),
        name="attention",
    )(qr, qp, kr, vt, kc, vct, zat, rexp)


def _lru_pitch(tc):
    pitch = tc + SUBLANES
    assert pitch % (2 * SUBLANES) == SUBLANES
    return pitch


def _lru_gate_consts(ba, bx, lam):
    return 0.5 * ba, 0.5 * bx, jax.nn.softplus(-lam) * (-0.5 * LRU_C * LOG2E)


def _lru_fill(bi, uc, wg, consts, a_s, b_s):
    half_ba, half_bx, half_log2a = consts
    tc = uc.shape[0]
    ucb = uc.astype(BF16)

    def gate_tanh(gi, half_bias):
        pre = jnp.concatenate([_dot(ucb[:, :MXU_DIM], wg(gi, 0)), _dot(ucb[:, MXU_DIM:], wg(gi, 1))], axis=-1)
        return jnp.tanh(pre + half_bias)

    tr = gate_tanh(0, half_ba)
    ti = gate_tanh(1, half_bx)
    a = jnp.exp2(tr * half_log2a + half_log2a)
    gap = 1.0 - a * a
    mult = gap * lax.rsqrt(jnp.maximum(gap, 1e-30))
    bb = mult * ((0.5 * ti + 0.5) * uc)
    r0 = pl.multiple_of(bi * _lru_pitch(tc), SUBLANES)
    for j in range(LRU_SLABS):
        a_s[j, pl.ds(r0, tc), :] = a[:, j * LANES:(j + 1) * LANES]
        b_s[j, pl.ds(r0, tc), :] = bb[:, j * LANES:(j + 1) * LANES]


def _lru_scan(reverse, tc, nb, a_s, b_s, h_s, carry_s):
    pitch = _lru_pitch(tc)

    n_groups = tc // SUBLANES

    def body(g, hs):
        base = pl.multiple_of(((n_groups - 1 - g) if reverse else g) * SUBLANES, SUBLANES)
        hs = list(hs)
        for k in (reversed(range(SUBLANES)) if reverse else range(SUBLANES)):
            rows = pl.ds(base + k, nb, stride=pitch)
            for j in range(LRU_SLABS):
                hs[j] = a_s[j, rows, :] * hs[j] + b_s[j, rows, :]
                h_s[j, rows, :] = hs[j]
        return tuple(hs)

    hs = tuple(carry_s[:, j * LANES:(j + 1) * LANES] for j in range(LRU_SLABS))
    hs = lax.fori_loop(0, n_groups, body, hs)
    for j in range(LRU_SLABS):
        carry_s[:, j * LANES:(j + 1) * LANES] = hs[j]


def _lru_rows(bi, tc, h_s):
    r0 = pl.multiple_of(bi * _lru_pitch(tc), SUBLANES)
    return jnp.concatenate([h_s[j, pl.ds(r0, tc), :] for j in range(LRU_SLABS)], axis=-1)


def _lru_fwd_kernel(u_ref, uc_ref, wg_ref, ba_ref, bx_ref, lam_ref, hf_ref, hb0_ref, a_s, b_s, h_s, carry_s):
    nb, tc = u_ref.shape[0], u_ref.shape[1]
    consts = [_lru_gate_consts(ba_ref[d], bx_ref[d], lam_ref[d]) for d in range(2)]

    def fill_all(src_ref, d, unroll):
        def step(bi, carry):
            _lru_fill(bi, src_ref[bi], lambda gi, hf: wg_ref[d, gi, hf], consts[d], a_s, b_s)
            return carry

        lax.fori_loop(0, nb, step, 0, unroll=unroll)

    @pl.when(pl.program_id(0) == 0)
    def _():
        carry_s[...] = jnp.zeros_like(carry_s)
        fill_all(uc_ref, 1, 1)
        _lru_scan(True, tc, nb, a_s, b_s, h_s, carry_s)
        hb0_ref[...] = carry_s[...]
        carry_s[...] = jnp.zeros_like(carry_s)
        fill_all(uc_ref, 0, 1)
        _lru_scan(False, tc, nb, a_s, b_s, h_s, carry_s)

    fill_all(u_ref, 0, 4)
    _lru_scan(False, tc, nb, a_s, b_s, h_s, carry_s)

    def emit(bi, carry):
        hf_ref[bi] = _lru_rows(bi, tc, h_s).astype(hf_ref.dtype)
        return carry

    lax.fori_loop(0, nb, emit, 0)


def _lru_fwd(u, u_c, wg, ba, bx, lam):
    b, s, lw = u.shape
    tc = LRU_CHUNK
    assert u_c.shape[1] == tc and s % tc == 0
    full = lambda *shape: pl.BlockSpec(shape, lambda i: (0,) * len(shape))
    cblk = pl.BlockSpec((b, tc, lw), lambda i: (0, i, 0))
    slab = pltpu.VMEM((LRU_SLABS, b * _lru_pitch(tc), LANES), F32)
    return pl.pallas_call(
        _lru_fwd_kernel,
        grid=(s // tc,),
        in_specs=[cblk, full(b, tc, lw), full(2, 2, 2, MXU_DIM, MXU_DIM), full(2, 1, lw), full(2, 1, lw), full(2, 1, lw)],
        out_specs=[cblk, full(b, lw)],
        out_shape=[jax.ShapeDtypeStruct((b, s, lw), BF16), jax.ShapeDtypeStruct((b, lw), F32)],
        scratch_shapes=[slab, slab, slab, pltpu.VMEM((b, lw), F32)],
        compiler_params=_params("arbitrary"),
        name="rglru_fwd",
    )(u, u_c, wg, ba, bx, lam)


def _lru_bwd_out_kernel(u_ref, hf_ref, zl_ref, agt_ref, x_ref, hb0_ref, gate_ref, wg_ref, ba_ref, bx_ref, lam_ref,
                        wo_ref, o_ref, a_s, b_s, h_s, carry_s):
    nb, tc = u_ref.shape[0], u_ref.shape[1]
    consts = _lru_gate_consts(ba_ref[...], bx_ref[...], lam_ref[...])

    @pl.when(pl.program_id(0) == 0)
    def _():
        carry_s[...] = hb0_ref[...]

    def fill(bi, carry):
        _lru_fill(bi, u_ref[bi], lambda gi, hf: wg_ref[gi, hf], consts, a_s, b_s)
        return carry

    lax.fori_loop(0, nb, fill, 0, unroll=4)
    _lru_scan(True, tc, nb, a_s, b_s, h_s, carry_s)

    def emit(gi, carry):
        group = [OUT_GROUP * gi + k for k in range(OUT_GROUP)]
        y = jnp.concatenate([((hf_ref[bi].astype(F32) + _lru_rows(bi, tc, h_s)) * zl_ref[bi].astype(F32)).astype(BF16)
                             for bi in group], axis=0)
        at = jnp.concatenate([agt_ref[bi] for bi in group], axis=1)
        mix = lax.dot_general(at, wo_ref[:ATTN_WIDTH, :], (((0,), (0,)), ((), ())), preferred_element_type=F32)
        mix = mix + _dot(y, wo_ref[ATTN_WIDTH:, :])
        for k, bi in enumerate(group):
            o_ref[bi] = x_ref[bi] + gate_ref[bi] * mix[k * tc:(k + 1) * tc]
        return carry

    lax.fori_loop(0, nb // OUT_GROUP, emit, 0)


def _lru_bwd_out(u, hf, zl, agt, x, hb0, gate, wg, ba, bx, lam, wo_bf):
    b, s, lw = u.shape
    d = x.shape[2]
    tc = LRU_BWD_CHUNK
    n_chunks = s // tc
    assert s % tc == 0 and b % OUT_GROUP == 0
    full = lambda *shape: pl.BlockSpec(shape, lambda i: (0,) * len(shape))
    rev = lambda i: n_chunks - 1 - i
    cblk = pl.BlockSpec((b, tc, lw), lambda i: (0, rev(i), 0))
    xblk = pl.BlockSpec((b, tc, d), lambda i: (0, rev(i), 0))
    slab = pltpu.VMEM((LRU_SLABS, b * _lru_pitch(tc), LANES), F32)
    return pl.pallas_call(
        _lru_bwd_out_kernel,
        grid=(n_chunks,),
        in_specs=[cblk, cblk, cblk,
                  pl.BlockSpec((b, ATTN_WIDTH, tc), lambda i: (0, 0, rev(i))),
                  xblk, full(b, lw), full(b, 1, d),
                  full(2, 2, MXU_DIM, MXU_DIM), full(1, lw), full(1, lw), full(1, lw),
                  full(ATTN_WIDTH + lw, d)],
        out_specs=xblk,
        out_shape=jax.ShapeDtypeStruct(x.shape, x.dtype),
        scratch_shapes=[slab, slab, slab, pltpu.VMEM((b, lw), F32)],
        compiler_params=_params("arbitrary"),
        name="rglru_bwd_outproj",
    )(u, hf, zl, agt, x, hb0, gate, wg, ba, bx, lam, wo_bf)


def _rope_tables(s):
    t = jnp.arange(s)
    half = HEAD_DIM // 4
    inv = ROPE_BASE ** (-jnp.arange(half, dtype=F32) / half)
    ang_r = (t // GRID_W).astype(F32)[:, None] * inv[None, :]
    ang_c = (t % GRID_W).astype(F32)[:, None] * inv[None, :]
    cos = jnp.concatenate([jnp.cos(ang_r)] * 2 + [jnp.cos(ang_c)] * 2, axis=-1)
    sin = jnp.concatenate([-jnp.sin(ang_r), jnp.sin(ang_r), -jnp.sin(ang_c), jnp.sin(ang_c)], axis=-1)
    return jnp.tile(cos, (1, LANES // HEAD_DIM)), jnp.tile(sin, (1, LANES // HEAD_DIM))


def _block_diag_gates(w):
    n = MXU_DIM // LRU_BLOCK
    lead = w.shape[:-3]
    w = w.reshape(lead + (2, n, LRU_BLOCK, LRU_BLOCK))
    on_diag = jnp.eye(n, dtype=bool)[:, None, :, None]
    blocks = jnp.where(on_diag, w[..., :, :, None, :], jnp.zeros((), w.dtype))
    return blocks.reshape(lead + (2, MXU_DIM, MXU_DIM))


def kernel(x, c, ctx, c_ctx, norm_g, w_mod, b_mod, w_in, w_out, q_norm_g, k_norm_g, rpb, conv_w, conv_b,
           lru_wa, lru_ba, lru_wx, lru_bx, lru_lam):
    bsz, s, d = x.shape
    assert w_in.shape[0] == 1 and d == D_MODEL and s % (GRID_W * Q_ROWS) == 0
    aw, lw = ATTN_WIDTH, LRU_WIDTH

    pad_rows = 2 * SUBLANES - bsz - 1
    cc = jnp.concatenate([c, c_ctx[None, :], jnp.zeros((pad_rows, d), F32)], axis=0)
    mod = _modulation(cc, w_mod[0], b_mod[0][None, :])
    shift, scale, gate = [mod[:bsz, i * d:(i + 1) * d][:, None, :] for i in range(3)]
    shift_c, scale_c = [mod[bsz:bsz + 1, i * d:(i + 1) * d][:, None, :] for i in range(2)]

    w = w_in[0]
    w_bf = w.astype(BF16)
    wt_bf = _transpose_cols(w_bf, COL_V * aw, (COL_ZA + 1) * aw)
    w_ctx_bf, wt_ctx_bf = w_bf, wt_bf
    g = norm_g[0][None, :]
    cos_t, sin_t = _rope_tables(s)
    blk = jnp.arange(MXU_DIM) // HEAD_DIM
    bd = jnp.where(blk[:, None] == blk[None, :], 1.0 / HEAD_DIM, 0.0).astype(BF16)
    qg = jnp.tile(q_norm_g[0] * (HEAD_DIM ** -0.5 * LOG2E), N_HEADS)[None, :]
    kg = jnp.tile(k_norm_g[0], N_HEADS)[None, :]

    cw, cb = conv_w[0], conv_b[0][None, :]
    qr, qp, kr, vt, zat, u, zl = _inproj(x, g, scale, shift, w_bf, wt_bf, cos_t, sin_t, bd, qg, kg, cw, cb)
    kc, vct, u_c = _ctxproj(ctx, g, scale_c, shift_c, w_ctx_bf, wt_ctx_bf, bd, kg, cw, cb)

    agt = _attention(qr, qp, kr, vt, kc, vct, zat, _bias_rows(rpb[0]))

    wg = _block_diag_gates(jnp.stack([lru_wa[0], lru_wx[0]], axis=1))
    wg = (0.5 * wg).astype(BF16)
    ba, bx, lam = lru_ba[0][:, None, :], lru_bx[0][:, None, :], lru_lam[0][:, None, :]
    hf, hb0 = _lru_fwd(u, u_c, wg, ba, bx, lam)
    return _lru_bwd_out(u, hf, zl, agt, x, hb0, gate, wg[1], ba[1], bx[1], lam[1], w_out[0].astype(BF16))
```

```python
import functools
import itertools
import math

import jax
import jax.numpy as jnp
from jax import lax
from jax.experimental import pallas as pl
from jax.experimental.pallas import tpu as pltpu

F32 = jnp.float32
BF16 = jnp.bfloat16

D_MODEL = 1024
GRID_W = 64
HEAD_DIM = 64
ATTN_WIDTH = 512
LRU_WIDTH = 512
N_HEADS = ATTN_WIDTH // HEAD_DIM
LRU_BLOCK = 64
NA_ROWS = 8
NA_COLS = 16
CONV_WIDTH = 4
LRU_C = 8.0
ROPE_BASE = 10000.0
EPS = 1e-6
NEG_INF = -1e30
LOG2E = math.log2(math.e)

LANES = 128
SUBLANES = 8
MXU_DIM = 256
VMEM_LIMIT_BYTES = 58 * 1024 * 1024

Q_ROWS = 4
BAND_ROWS = Q_ROWS + NA_ROWS
Q_BLK = Q_ROWS * GRID_W
BAND = BAND_ROWS * GRID_W
KEY_CHUNK = 256
ONES_ROWS = 16
N_BIAS_ROWS = 2 * NA_ROWS - 1
N_BIAS_COLS = 2 * NA_COLS - 1
TOE_W = 1024

LRU_CHUNK = 256
LRU_BWD_CHUNK = 128
LRU_SLABS = LRU_WIDTH // LANES
OUT_GROUP = 8
CONV_ROWS = 128
COL_Q, COL_K, COL_V, COL_ZA, COL_U, COL_ZL = range(6)
HALO = 16


def _dot(a, b):
    return jnp.dot(a, b, preferred_element_type=F32)


def _dot_nt(a, b):
    return lax.dot_general(a, b, (((1,), (1,)), ((), ())), preferred_element_type=F32)


def _params(*semantics):
    return pltpu.CompilerParams(dimension_semantics=semantics, vmem_limit_bytes=VMEM_LIMIT_BYTES)


def _split_bf16(t):
    hi = t.astype(BF16)
    lo = (t - hi.astype(F32)).astype(BF16)
    return hi, lo


def _mod_kernel(cc_ref, w_ref, b_ref, o_ref):
    cc = cc_ref[...]
    s = cc * jax.nn.sigmoid(cc)
    s_hi, s_lo = _split_bf16(s)
    w_hi, w_lo = _split_bf16(w_ref[...])
    o_ref[...] = _dot(s_hi, w_hi) + _dot(s_lo, w_hi) + _dot(s_hi, w_lo) + b_ref[...]


def _modulation(cc, w_mod, b_mod):
    rows, d = cc.shape
    n = w_mod.shape[1]
    tn = 512
    return pl.pallas_call(
        _mod_kernel,
        grid=(n // tn,),
        in_specs=[pl.BlockSpec((rows, d), lambda j: (0, 0)),
                  pl.BlockSpec((d, tn), lambda j: (0, j)),
                  pl.BlockSpec((1, tn), lambda j: (0, j))],
        out_specs=pl.BlockSpec((rows, tn), lambda j: (0, j)),
        out_shape=jax.ShapeDtypeStruct((rows, n), F32),
        compiler_params=_params("arbitrary"),
        name="modulation",
    )(cc, w_mod, b_mod)


def _fill_bias_tables(rexp_ref, h, o_ref):
    k = lax.broadcasted_iota(jnp.int32, (GRID_W, TOE_W), 0)
    q = lax.broadcasted_iota(jnp.int32, (GRID_W, TOE_W), 1) % GRID_W
    diff = k - q + (NA_COLS - 1)
    toe = jnp.zeros((GRID_W, TOE_W), F32)
    for d in range(N_BIAS_COLS):
        toe = jnp.where(diff == d, rexp_ref[h, d:d + 1, :], toe)
    col_start = jnp.clip(q - NA_COLS // 2, 0, GRID_W - NA_COLS)
    col_ok = (k >= col_start) & (k < col_start + NA_COLS)
    toe = jnp.where(col_ok, toe, NEG_INF)
    ri = lax.broadcasted_iota(jnp.int32, (GRID_W, Q_BLK), 1) // GRID_W
    masked = jnp.full((GRID_W, Q_BLK), NEG_INF, F32)

    for typ in range(3):
        off = (NA_ROWS - 1, NA_ROWS // 2 - 1, NA_ROWS - BAND_ROWS + Q_ROWS - 1)[typ]
        for jj in range(BAND_ROWS):
            if typ == 0:
                lo, hi = (0, Q_ROWS) if jj < NA_ROWS else (0, 0)
            elif typ == 1:
                lo, hi = max(jj - NA_ROWS + 1, 0), min(jj, Q_ROWS - 1) + 1
            else:
                lo, hi = (0, Q_ROWS) if jj >= BAND_ROWS - NA_ROWS else (0, 0)
            strip = masked
            if lo < hi:
                e0 = N_BIAS_ROWS - 1 - jj - off
                rolled = pltpu.roll(toe, (-e0 * GRID_W) % TOE_W, axis=1)[:, :Q_BLK]
                strip = jnp.where((ri >= lo) & (ri < hi), rolled, NEG_INF)
            o_ref[h, typ, jj * GRID_W:(jj + 1) * GRID_W, :] = strip


def _bias_rows(rpb):
    rexp = jnp.repeat(jnp.transpose(rpb[:, ::-1, :], (0, 2, 1)) * LOG2E, GRID_W, axis=2)
    return jnp.pad(rexp, ((0, 0), (0, 0), (0, TOE_W - N_BIAS_ROWS * GRID_W)))


def _transpose_kernel(w_ref, o_ref):
    o_ref[...] = w_ref[...].T


def _transpose_cols(w, lo, hi, tn=MXU_DIM):
    rows = w.shape[0]
    assert lo % tn == 0 and hi % tn == 0
    return pl.pallas_call(
        _transpose_kernel,
        grid=((hi - lo) // tn,),
        in_specs=[pl.BlockSpec((rows, tn), lambda i: (0, lo // tn + i))],
        out_specs=pl.BlockSpec((tn, rows), lambda i: (i, 0)),
        out_shape=jax.ShapeDtypeStruct((hi - lo, rows), w.dtype),
        compiler_params=_params("arbitrary"),
        name="weight_transpose",
    )(w)


def _adaln(x, g_ref, scale_ref, shift_ref):
    ms = jnp.mean(x * x, axis=-1, keepdims=True)
    gm = g_ref[...] * (1.0 + scale_ref[0])
    return ((x * lax.rsqrt(ms + EPS)) * gm + shift_ref[0]).astype(BF16)


def _head_rms(t, bd_ref, gain):
    sq = (t * t).astype(BF16)
    bd = bd_ref[...]
    m = jnp.concatenate([_dot(sq[:, :MXU_DIM], bd), _dot(sq[:, MXU_DIM:], bd)], axis=-1)
    return t * lax.rsqrt(m + EPS) * gain


def _rope(t, cos_ref, sin_ref):
    lane = lax.broadcasted_iota(jnp.int32, (1, LANES), 1)
    first = (lane % 32) < 16
    cos, sin = cos_ref[...], sin_ref[...]
    parts = []
    for j in range(t.shape[1] // LANES):
        c = t[:, j * LANES:(j + 1) * LANES]
        swapped = jnp.where(first, pltpu.roll(c, LANES - 16, axis=1), pltpu.roll(c, 16, axis=1))
        parts.append(c * cos + swapped * sin)
    return jnp.concatenate(parts, axis=-1)


def _silu(t):
    return t * jax.nn.sigmoid(t)


def _conv_window(ext, n, cw_ref, cb_ref):
    rows = ext.shape[0]

    def shifted(k):
        return pltpu.roll(ext, (rows - k) % rows, axis=0)[HALO:HALO + n]

    y = cb_ref[...] + cw_ref[0:1, :] * shifted(-1)
    y = y + cw_ref[1:2, :] * ext[HALO:HALO + n]
    y = y + cw_ref[2:3, :] * shifted(1)
    return y + cw_ref[3:4, :] * shifted(2)


def _inproj_kernel(x_ref, xp_ref, xn_ref, g_ref, scale_ref, shift_ref, w_ref, wt_ref, cos_ref, sin_ref, bd_ref,
                   qg_ref, kg_ref, cw_ref, cb_ref, qr_ref, qp_ref, kr_ref, vt_ref, zat_ref, uc_ref, zl_ref):
    tm = x_ref.shape[1]
    aw = ATTN_WIDTH
    i, n_tiles = pl.program_id(0), pl.num_programs(0)
    hb_ext = jnp.concatenate([_adaln(xp_ref[0], g_ref, scale_ref, shift_ref),
                              _adaln(x_ref[0], g_ref, scale_ref, shift_ref),
                              _adaln(xn_ref[0], g_ref, scale_ref, shift_ref)], axis=0)
    hb = hb_ext[HALO:HALO + tm]

    def cols(j):
        return w_ref[:, j * aw:(j + 1) * aw]

    u_ext = _dot(hb_ext, cols(COL_U))
    row = lax.broadcasted_iota(jnp.int32, (tm + 2 * HALO, 1), 0)
    inside = ((row >= HALO) | (i > 0)) & ((row < HALO + tm) | (i < n_tiles - 1))
    u_ext = jnp.where(inside, u_ext, 0.0)

    qn = _head_rms(_dot(hb, cols(COL_Q)), bd_ref, qg_ref[...])
    qp_ref[0] = qn.astype(BF16)
    qr_ref[0] = _rope(qn, cos_ref, sin_ref).astype(BF16)
    kn = _head_rms(_dot(hb, cols(COL_K)), bd_ref, kg_ref[...])
    kr_ref[0] = _rope(kn, cos_ref, sin_ref).astype(BF16)

    zl_ref[0] = _silu(_dot(hb, cols(COL_ZL))).astype(BF16)
    vt_ref[0] = _dot_nt(wt_ref[:aw, :], hb).astype(BF16)
    zat_ref[0] = _silu(_dot_nt(wt_ref[aw:, :], hb)).astype(BF16)
    for r in range(0, tm, CONV_ROWS):
        uc_ref[0, r:r + CONV_ROWS, :] = _conv_window(u_ext[r:r + CONV_ROWS + 2 * HALO], CONV_ROWS, cw_ref, cb_ref)


def _inproj(x, g, scale, shift, w_bf, wt_bf, cos_t, sin_t, bd, qg, kg, conv_w, conv_b, tm=1024):
    b, s, d = x.shape
    aw = ATTN_WIDTH
    tok = lambda i, j: (j, i, 0)
    per_b = lambda i, j: (j, 0, 0)
    const = lambda i, j: (0, 0)
    halo_per_tile = tm // HALO
    n_halo = s // HALO
    out_bf = jax.ShapeDtypeStruct((b, s, aw), BF16)
    out_t = jax.ShapeDtypeStruct((b, aw, s), BF16)
    out_f32 = jax.ShapeDtypeStruct((b, s, aw), F32)
    blk = pl.BlockSpec((1, tm, aw), tok)
    blk_t = pl.BlockSpec((1, aw, tm), lambda i, j: (j, 0, i))
    return pl.pallas_call(
        _inproj_kernel,
        grid=(s // tm, b),
        in_specs=[pl.BlockSpec((1, tm, d), tok),
                  pl.BlockSpec((1, HALO, d), lambda i, j: (j, jnp.maximum(i * halo_per_tile - 1, 0), 0)),
                  pl.BlockSpec((1, HALO, d), lambda i, j: (j, jnp.minimum((i + 1) * halo_per_tile, n_halo - 1), 0)),
                  pl.BlockSpec((1, d), const),
                  pl.BlockSpec((1, 1, d), per_b),
                  pl.BlockSpec((1, 1, d), per_b),
                  pl.BlockSpec(w_bf.shape, const),
                  pl.BlockSpec(wt_bf.shape, const),
                  pl.BlockSpec((tm, LANES), lambda i, j: (i, 0)),
                  pl.BlockSpec((tm, LANES), lambda i, j: (i, 0)),
                  pl.BlockSpec((MXU_DIM, MXU_DIM), const),
                  pl.BlockSpec((1, aw), const),
                  pl.BlockSpec((1, aw), const),
                  pl.BlockSpec((CONV_WIDTH, aw), const),
                  pl.BlockSpec((1, aw), const)],
        out_specs=[blk, blk, blk, blk_t, blk_t, blk, blk],
        out_shape=[out_bf, out_bf, out_bf, out_t, out_t, out_f32, out_bf],
        compiler_params=_params("arbitrary", "arbitrary"),
        name="inproj",
    )(x, x, x, g, scale, shift, w_bf, wt_bf, cos_t, sin_t, bd, qg, kg, conv_w, conv_b)


def _ctxproj_kernel(c_ref, g_ref, scale_ref, shift_ref, wk_ref, wu_ref, wvt_ref, bd_ref, kg_ref, cw_ref, cb_ref,
                    kc_ref, vct_ref, uc_ref):
    nb, l, d = c_ref.shape
    aw = ATTN_WIDTH
    hb = _adaln(c_ref[...].reshape(nb * l, d), g_ref, scale_ref, shift_ref)
    kc = _head_rms(_dot(hb, wk_ref[...]), bd_ref, kg_ref[...]).astype(BF16)
    u = _dot(hb, wu_ref[...])
    vct = _dot_nt(wvt_ref[...], hb).astype(BF16)
    pad = jnp.zeros((HALO, aw), F32)
    for bi in range(nb):
        kc_ref[bi] = kc[bi * l:(bi + 1) * l]
        uc_ref[bi] = _conv_window(jnp.concatenate([pad, u[bi * l:(bi + 1) * l], pad], axis=0), l, cw_ref, cb_ref)
        vct_ref[bi] = vct[:, bi * l:(bi + 1) * l]


def _ctxproj(ctx, g, scale_c, shift_c, w_ctx_bf, wt_ctx_bf, bd, kg, conv_w, conv_b, nb=4):
    b, l, d = ctx.shape
    aw = ATTN_WIDTH
    const = lambda i: (0, 0)
    blk = pl.BlockSpec((nb, l, aw), lambda i: (i, 0, 0))
    return pl.pallas_call(
        _ctxproj_kernel,
        grid=(b // nb,),
        in_specs=[pl.BlockSpec((nb, l, d), lambda i: (i, 0, 0)),
                  pl.BlockSpec((1, d), const),
                  pl.BlockSpec((1, 1, d), lambda i: (0, 0, 0)),
                  pl.BlockSpec((1, 1, d), lambda i: (0, 0, 0)),
                  pl.BlockSpec((d, aw), lambda i: (0, COL_K)),
                  pl.BlockSpec((d, aw), lambda i: (0, COL_U)),
                  pl.BlockSpec((aw, d), const),
                  pl.BlockSpec((MXU_DIM, MXU_DIM), const),
                  pl.BlockSpec((1, aw), const),
                  pl.BlockSpec((CONV_WIDTH, aw), const),
                  pl.BlockSpec((1, aw), const)],
        out_specs=[blk, pl.BlockSpec((nb, aw, l), lambda i: (i, 0, 0)), blk],
        out_shape=[jax.ShapeDtypeStruct((b, l, aw), BF16),
                   jax.ShapeDtypeStruct((b, aw, l), BF16),
                   jax.ShapeDtypeStruct((b, l, aw), F32)],
        compiler_params=_params("arbitrary"),
        name="ctxproj",
    )(ctx, g, scale_c, shift_c, w_ctx_bf, w_ctx_bf, wt_ctx_bf, bd, kg, conv_w, conv_b)


def _attn_kernel(qr_ref, qp_ref, kr_ref, vt_ref, kc_ref, vct_ref, zat_ref, rexp_ref, o_ref, bias_ref, s_buf0, s_buf1):
    @pl.when(pl.program_id(1) == 0)
    def _():
        for h in range(2):
            _fill_bias_tables(rexp_ref, h, bias_ref)

    rows = qr_ref.shape[1] // GRID_W
    n_blk = rows // Q_ROWS
    chunk_rows = KEY_CHUNK // GRID_W
    n_tiles = Q_BLK // LANES
    lane_head = lax.broadcasted_iota(jnp.int32, (1, LANES), 1) // HEAD_DIM
    s_bufs = (s_buf0, s_buf1)
    ones_rows = jnp.ones((ONES_ROWS, KEY_CHUNK), BF16)

    def in_window(typ, jj, tile):
        if typ == 0:
            return jj < NA_ROWS
        if typ == 2:
            return jj >= BAND_ROWS - NA_ROWS
        return any(0 <= jj - ri < NA_ROWS for ri in range(tile * Q_ROWS // n_tiles, (tile + 1) * Q_ROWS // n_tiles))

    def local_chunks(typ):
        return [c for c in range(BAND // KEY_CHUNK)
                if any(in_window(typ, c * chunk_rows + r, t) for r in range(chunk_rows) for t in range(n_tiles))]

    def sub_blocks(c):
        for r in range(chunk_rows):
            for t in range(n_tiles):
                yield c * chunk_rows + r, t, slice(r * GRID_W, (r + 1) * GRID_W), slice(t * LANES, (t + 1) * LANES)

    def item(rb):
        rb = jnp.asarray(rb, jnp.int32)
        q0 = pl.multiple_of(rb * Q_BLK, Q_BLK)
        band0 = pl.multiple_of(jnp.clip(rb * Q_ROWS - NA_ROWS // 2, 0, rows - BAND_ROWS) * GRID_W, Q_BLK)
        return q0, band0

    def col_max(m, blk, t):
        m = list(m)
        m[t] = jnp.maximum(m[t], jnp.max(blk, axis=0, keepdims=True))
        return tuple(m)

    def score_steps(rb, h, typ):
        q0, band0 = item(rb)
        s_buf = s_bufs[h]
        mine = lane_head == h
        qr = jnp.where(mine, qr_ref[0, pl.ds(q0, Q_BLK), :], jnp.zeros((), BF16))
        qp = jnp.where(mine, qp_ref[0, pl.ds(q0, Q_BLK), :], jnp.zeros((), BF16))

        def local(c, m):
            n_rows = 1 + max(r for r in range(chunk_rows)
                             if any(in_window(typ, c * chunk_rows + r, t) for t in range(n_tiles)))
            s = _dot_nt(kr_ref[0, pl.ds(band0 + c * KEY_CHUNK, n_rows * GRID_W), :], qr)
            for jj, t, r_sl, l_sl in sub_blocks(c):
                if in_window(typ, jj, t):
                    k_sl = slice(jj * GRID_W, (jj + 1) * GRID_W)
                    blk = s[r_sl, l_sl] + bias_ref[h, typ, k_sl, l_sl]
                    s_buf[k_sl, l_sl] = blk
                    m = col_max(m, blk, t)
            return m

        def context(m):
            s = _dot_nt(kc_ref[0], qp)
            s_buf[pl.ds(BAND, KEY_CHUNK), :] = s
            for t in range(n_tiles):
                m = col_max(m, s[:, t * LANES:(t + 1) * LANES], t)
            return m

        return [functools.partial(local, c) for c in local_chunks(typ)] + [context]

    def value_steps(rb, h, typ, m):
        q0, band0 = item(rb)
        s_buf = s_bufs[h]
        hd = pl.ds(h * HEAD_DIM, HEAD_DIM)

        def probs(k_sl, t):
            return jnp.exp2(s_buf[k_sl, t * LANES:(t + 1) * LANES] - m[t]).astype(BF16)

        def local(c, acc):
            zero_blk = jnp.zeros((GRID_W, LANES), BF16)
            row_blks = []
            for r in range(chunk_rows):
                jj = c * chunk_rows + r
                k_sl = slice(jj * GRID_W, (jj + 1) * GRID_W)
                row_blks.append(jnp.concatenate(
                    [probs(k_sl, t) if in_window(typ, jj, t) else zero_blk for t in range(n_tiles)], axis=1))
            p = jnp.concatenate(row_blks, axis=0)
            vt = jnp.concatenate([vt_ref[0, hd, pl.ds(band0 + c * KEY_CHUNK, KEY_CHUNK)], ones_rows], axis=0)
            return acc + _dot(vt, p)

        def context(acc):
            k_sl = slice(BAND, BAND + KEY_CHUNK)
            p = jnp.concatenate([probs(k_sl, t) for t in range(n_tiles)], axis=1)
            vt = jnp.concatenate([vct_ref[0, hd, :], ones_rows], axis=0)
            return acc + _dot(vt, p)

        return [functools.partial(local, c) for c in local_chunks(typ)] + [context]

    def write_out(rb, h, acc):
        q0, _ = item(rb)
        hd = pl.ds(h * HEAD_DIM, HEAD_DIM)
        gated = (acc[:HEAD_DIM] / acc[HEAD_DIM:HEAD_DIM + 1]) * zat_ref[0, hd, pl.ds(q0, Q_BLK)].astype(F32)
        o_ref[0, hd, pl.ds(q0, Q_BLK)] = gated.astype(o_ref.dtype)

    m_init = (jnp.full((1, LANES), -jnp.inf, F32),) * n_tiles
    acc_init = jnp.zeros((HEAD_DIM + ONES_ROWS, Q_BLK), F32)

    def overlapped(score_item, value_item, out_item, carry):
        m_prev, acc_prev = carry
        s_steps = score_steps(*score_item) if score_item else []
        v_steps = value_steps(*value_item, m_prev) if value_item else []
        m, acc = m_init, acc_init
        for i, (s_step, v_step) in enumerate(itertools.zip_longest(s_steps, v_steps)):
            if s_step:
                m = s_step(m)
            if i == 0 and out_item:
                write_out(*out_item, acc_prev)
            if v_step:
                acc = v_step(acc)
        return m, acc

    first, mid, last = 0, 1, 2
    carry = overlapped((0, 0, first), None, None, (None, None))
    carry = overlapped((0, 1, first), (0, 0, first), None, carry)
    carry = overlapped((1, 0, mid), (0, 1, first), (0, 0), carry)

    def body(rb, carry):
        carry = overlapped((rb, 1, mid), (rb, 0, mid), (rb - 1, 1), carry)
        return overlapped((rb + 1, 0, mid), (rb, 1, mid), (rb, 0), carry)

    carry = lax.fori_loop(1, n_blk - 2, body, carry, unroll=3)
    carry = overlapped((n_blk - 2, 1, mid), (n_blk - 2, 0, mid), (n_blk - 3, 1), carry)
    carry = overlapped((n_blk - 1, 0, last), (n_blk - 2, 1, mid), (n_blk - 2, 0), carry)
    carry = overlapped((n_blk - 1, 1, last), (n_blk - 1, 0, last), (n_blk - 2, 1), carry)
    carry = overlapped(None, (n_blk - 1, 1, last), (n_blk - 1, 0), carry)
    write_out(n_blk - 1, 1, carry[1])


def _attention(qr, qp, kr, vt, kc, vct, zat, rexp):
    b, s, aw = qr.shape
    l = kc.shape[1]
    n_pairs = aw // LANES
    lat = pl.BlockSpec((1, s, LANES), lambda p, i: (i, 0, p))
    lat_t = pl.BlockSpec((1, LANES, s), lambda p, i: (i, p, 0))
    return pl.pallas_call(
        _attn_kernel,
        grid=(n_pairs, b),
        in_specs=[lat, lat, lat, lat_t,
                  pl.BlockSpec((1, l, LANES), lambda p, i: (i, 0, p)),
                  pl.BlockSpec((1, LANES, l), lambda p, i: (i, p, 0)),
                  lat_t,
                  pl.BlockSpec((2, N_BIAS_COLS, TOE_W), lambda p, i: (p, 0, 0))],
        out_specs=lat_t,
        out_shape=jax.ShapeDtypeStruct((b, aw, s), BF16),
        scratch_shapes=[pltpu.VMEM((2, 3, BAND, Q_BLK), F32),
                        pltpu.VMEM((BAND + l, Q_BLK), F32), pltpu.VMEM((BAND + l, Q_BLK), F32)],
        compiler_params=_params("arbitrary", "arbitrary"),
        name="attention",
    )(qr, qp, kr, vt, kc, vct, zat, rexp)


def _lru_pitch(tc):
    pitch = tc + SUBLANES
    assert pitch % (2 * SUBLANES) == SUBLANES
    return pitch


def _lru_gate_consts(ba, bx, lam):
    return 0.5 * ba, 0.5 * bx, jax.nn.softplus(-lam) * (-0.5 * LRU_C * LOG2E)


def _lru_fill(bi, uc, wg, consts, a_s, b_s):
    half_ba, half_bx, half_log2a = consts
    tc = uc.shape[0]
    ucb = uc.astype(BF16)

    def gate_tanh(gi, half_bias):
        pre = jnp.concatenate([_dot(ucb[:, :MXU_DIM], wg(gi, 0)), _dot(ucb[:, MXU_DIM:], wg(gi, 1))], axis=-1)
        return jnp.tanh(pre + half_bias)

    tr = gate_tanh(0, half_ba)
    ti = gate_tanh(1, half_bx)
    a = jnp.exp2(tr * half_log2a + half_log2a)
    gap = 1.0 - a * a
    mult = gap * lax.rsqrt(jnp.maximum(gap, 1e-30))
    bb = mult * ((0.5 * ti + 0.5) * uc)
    r0 = pl.multiple_of(bi * _lru_pitch(tc), SUBLANES)
    for j in range(LRU_SLABS):
        a_s[j, pl.ds(r0, tc), :] = a[:, j * LANES:(j + 1) * LANES]
        b_s[j, pl.ds(r0, tc), :] = bb[:, j * LANES:(j + 1) * LANES]


def _lru_scan(reverse, tc, nb, a_s, b_s, h_s, carry_s):
    pitch = _lru_pitch(tc)

    n_groups = tc // SUBLANES

    def body(g, hs):
        base = pl.multiple_of(((n_groups - 1 - g) if reverse else g) * SUBLANES, SUBLANES)
        hs = list(hs)
        for k in (reversed(range(SUBLANES)) if reverse else range(SUBLANES)):
            rows = pl.ds(base + k, nb, stride=pitch)
            for j in range(LRU_SLABS):
                hs[j] = a_s[j, rows, :] * hs[j] + b_s[j, rows, :]
                h_s[j, rows, :] = hs[j]
        return tuple(hs)

    hs = tuple(carry_s[:, j * LANES:(j + 1) * LANES] for j in range(LRU_SLABS))
    hs = lax.fori_loop(0, n_groups, body, hs)
    for j in range(LRU_SLABS):
        carry_s[:, j * LANES:(j + 1) * LANES] = hs[j]


def _lru_rows(bi, tc, h_s):
    r0 = pl.multiple_of(bi * _lru_pitch(tc), SUBLANES)
    return jnp.concatenate([h_s[j, pl.ds(r0, tc), :] for j in range(LRU_SLABS)], axis=-1)


def _lru_fwd_kernel(u_ref, uc_ref, wg_ref, ba_ref, bx_ref, lam_ref, hf_ref, hb0_ref, a_s, b_s, h_s, carry_s):
    nb, tc = u_ref.shape[0], u_ref.shape[1]
    consts = [_lru_gate_consts(ba_ref[d], bx_ref[d], lam_ref[d]) for d in range(2)]

    def fill_all(src_ref, d, unroll):
        def step(bi, carry):
            _lru_fill(bi, src_ref[bi], lambda gi, hf: wg_ref[d, gi, hf], consts[d], a_s, b_s)
            return carry

        lax.fori_loop(0, nb, step, 0, unroll=unroll)

    @pl.when(pl.program_id(0) == 0)
    def _():
        carry_s[...] = jnp.zeros_like(carry_s)
        fill_all(uc_ref, 1, 1)
        _lru_scan(True, tc, nb, a_s, b_s, h_s, carry_s)
        hb0_ref[...] = carry_s[...]
        carry_s[...] = jnp.zeros_like(carry_s)
        fill_all(uc_ref, 0, 1)
        _lru_scan(False, tc, nb, a_s, b_s, h_s, carry_s)

    fill_all(u_ref, 0, 4)
    _lru_scan(False, tc, nb, a_s, b_s, h_s, carry_s)

    def emit(bi, carry):
        hf_ref[bi] = _lru_rows(bi, tc, h_s).astype(hf_ref.dtype)
        return carry

    lax.fori_loop(0, nb, emit, 0)


def _lru_fwd(u, u_c, wg, ba, bx, lam):
    b, s, lw = u.shape
    tc = LRU_CHUNK
    assert u_c.shape[1] == tc and s % tc == 0
    full = lambda *shape: pl.BlockSpec(shape, lambda i: (0,) * len(shape))
    cblk = pl.BlockSpec((b, tc, lw), lambda i: (0, i, 0))
    slab = pltpu.VMEM((LRU_SLABS, b * _lru_pitch(tc), LANES), F32)
    return pl.pallas_call(
        _lru_fwd_kernel,
        grid=(s // tc,),
        in_specs=[cblk, full(b, tc, lw), full(2, 2, 2, MXU_DIM, MXU_DIM), full(2, 1, lw), full(2, 1, lw), full(2, 1, lw)],
        out_specs=[cblk, full(b, lw)],
        out_shape=[jax.ShapeDtypeStruct((b, s, lw), BF16), jax.ShapeDtypeStruct((b, lw), F32)],
        scratch_shapes=[slab, slab, slab, pltpu.VMEM((b, lw), F32)],
        compiler_params=_params("arbitrary"),
        name="rglru_fwd",
    )(u, u_c, wg, ba, bx, lam)


def _lru_bwd_out_kernel(u_ref, hf_ref, zl_ref, agt_ref, x_ref, hb0_ref, gate_ref, wg_ref, ba_ref, bx_ref, lam_ref,
                        wo_ref, o_ref, a_s, b_s, h_s, carry_s):
    nb, tc = u_ref.shape[0], u_ref.shape[1]
    consts = _lru_gate_consts(ba_ref[...], bx_ref[...], lam_ref[...])

    @pl.when(pl.program_id(0) == 0)
    def _():
        carry_s[...] = hb0_ref[...]

    def fill(bi, carry):
        _lru_fill(bi, u_ref[bi], lambda gi, hf: wg_ref[gi, hf], consts, a_s, b_s)
        return carry

    lax.fori_loop(0, nb, fill, 0, unroll=4)
    _lru_scan(True, tc, nb, a_s, b_s, h_s, carry_s)

    def emit(gi, carry):
        group = [OUT_GROUP * gi + k for k in range(OUT_GROUP)]
        y = jnp.concatenate([((hf_ref[bi].astype(F32) + _lru_rows(bi, tc, h_s)) * zl_ref[bi].astype(F32)).astype(BF16)
                             for bi in group], axis=0)
        at = jnp.concatenate([agt_ref[bi] for bi in group], axis=1)
        mix = lax.dot_general(at, wo_ref[:ATTN_WIDTH, :], (((0,), (0,)), ((), ())), preferred_element_type=F32)
        mix = mix + _dot(y, wo_ref[ATTN_WIDTH:, :])
        for k, bi in enumerate(group):
            o_ref[bi] = x_ref[bi] + gate_ref[bi] * mix[k * tc:(k + 1) * tc]
        return carry

    lax.fori_loop(0, nb // OUT_GROUP, emit, 0)


def _lru_bwd_out(u, hf, zl, agt, x, hb0, gate, wg, ba, bx, lam, wo_bf):
    b, s, lw = u.shape
    d = x.shape[2]
    tc = LRU_BWD_CHUNK
    n_chunks = s // tc
    assert s % tc == 0 and b % OUT_GROUP == 0
    full = lambda *shape: pl.BlockSpec(shape, lambda i: (0,) * len(shape))
    rev = lambda i: n_chunks - 1 - i
    cblk = pl.BlockSpec((b, tc, lw), lambda i: (0, rev(i), 0))
    xblk = pl.BlockSpec((b, tc, d), lambda i: (0, rev(i), 0))
    slab = pltpu.VMEM((LRU_SLABS, b * _lru_pitch(tc), LANES), F32)
    return pl.pallas_call(
        _lru_bwd_out_kernel,
        grid=(n_chunks,),
        in_specs=[cblk, cblk, cblk,
                  pl.BlockSpec((b, ATTN_WIDTH, tc), lambda i: (0, 0, rev(i))),
                  xblk, full(b, lw), full(b, 1, d),
                  full(2, 2, MXU_DIM, MXU_DIM), full(1, lw), full(1, lw), full(1, lw),
                  full(ATTN_WIDTH + lw, d)],
        out_specs=xblk,
        out_shape=jax.ShapeDtypeStruct(x.shape, x.dtype),
        scratch_shapes=[slab, slab, slab, pltpu.VMEM((b, lw), F32)],
        compiler_params=_params("arbitrary"),
        name="rglru_bwd_outproj",
    )(u, hf, zl, agt, x, hb0, gate, wg, ba, bx, lam, wo_bf)


def _rope_tables(s):
    half = HEAD_DIM // 4
    t = lax.broadcasted_iota(jnp.int32, (s, LANES), 0)
    j = lax.broadcasted_iota(jnp.int32, (s, LANES), 1) % HEAD_DIM
    pos = jnp.where(j < 2 * half, t // GRID_W, t % GRID_W).astype(F32)
    inv = ROPE_BASE ** (-(j % half).astype(F32) / half)
    ang = pos * inv
    sign = jnp.where((j // half) % 2 == 0, -1.0, 1.0)
    return jnp.cos(ang), sign * jnp.sin(ang)


def _block_diag_gates(w):
    n = MXU_DIM // LRU_BLOCK
    lead = w.shape[:-3]
    w = w.reshape(lead + (2, n, LRU_BLOCK, LRU_BLOCK))
    on_diag = jnp.eye(n, dtype=bool)[:, None, :, None]
    blocks = jnp.where(on_diag, w[..., :, :, None, :], jnp.zeros((), w.dtype))
    return blocks.reshape(lead + (2, MXU_DIM, MXU_DIM))


def kernel(x, c, ctx, c_ctx, norm_g, w_mod, b_mod, w_in, w_out, q_norm_g, k_norm_g, rpb, conv_w, conv_b,
           lru_wa, lru_ba, lru_wx, lru_bx, lru_lam):
    bsz, s, d = x.shape
    assert w_in.shape[0] == 1 and d == D_MODEL and s % (GRID_W * Q_ROWS) == 0
    aw, lw = ATTN_WIDTH, LRU_WIDTH

    pad_rows = 2 * SUBLANES - bsz - 1
    cc = jnp.concatenate([c, c_ctx[None, :], jnp.zeros((pad_rows, d), F32)], axis=0)
    mod = _modulation(cc, w_mod[0], b_mod[0][None, :])
    shift, scale, gate = [mod[:bsz, i * d:(i + 1) * d][:, None, :] for i in range(3)]
    shift_c, scale_c = [mod[bsz:bsz + 1, i * d:(i + 1) * d][:, None, :] for i in range(2)]

    w = w_in[0]
    w_bf = w.astype(BF16)
    wt_bf = _transpose_cols(w_bf, COL_V * aw, (COL_ZA + 1) * aw)
    w_ctx_bf, wt_ctx_bf = w_bf, wt_bf
    g = norm_g[0][None, :]
    cos_t, sin_t = _rope_tables(s)
    blk = jnp.arange(MXU_DIM) // HEAD_DIM
    bd = jnp.where(blk[:, None] == blk[None, :], 1.0 / HEAD_DIM, 0.0).astype(BF16)
    qg = jnp.tile(q_norm_g[0] * (HEAD_DIM ** -0.5 * LOG2E), N_HEADS)[None, :]
    kg = jnp.tile(k_norm_g[0], N_HEADS)[None, :]

    cw, cb = conv_w[0], conv_b[0][None, :]
    qr, qp, kr, vt, zat, u, zl = _inproj(x, g, scale, shift, w_bf, wt_bf, cos_t, sin_t, bd, qg, kg, cw, cb)
    kc, vct, u_c = _ctxproj(ctx, g, scale_c, shift_c, w_ctx_bf, wt_ctx_bf, bd, kg, cw, cb)

    agt = _attention(qr, qp, kr, vt, kc, vct, zat, _bias_rows(rpb[0]))

    wg = (0.5 * jnp.stack([lru_wa[0], lru_wx[0]], axis=1)).astype(BF16)
    wg = _block_diag_gates(wg)
    ba, bx, lam = lru_ba[0][:, None, :], lru_bx[0][:, None, :], lru_lam[0][:, None, :]
    hf, hb0 = _lru_fwd(u, u_c, wg, ba, bx, lam)
    return _lru_bwd_out(u, hf, zl, agt, x, hb0, gate, wg[1], ba[1], bx[1], lam[1], w_out[0].astype(BF16))
```

```python
import functools
import itertools
import math

import jax
import jax.numpy as jnp
from jax import lax
from jax.experimental import pallas as pl
from jax.experimental.pallas import tpu as pltpu

F32 = jnp.float32
BF16 = jnp.bfloat16

D_MODEL = 1024
GRID_W = 64
HEAD_DIM = 64
ATTN_WIDTH = 512
LRU_WIDTH = 512
N_HEADS = ATTN_WIDTH // HEAD_DIM
LRU_BLOCK = 64
NA_ROWS = 8
NA_COLS = 16
CONV_WIDTH = 4
LRU_C = 8.0
ROPE_BASE = 10000.0
EPS = 1e-6
NEG_INF = -1e30
LOG2E = math.log2(math.e)

LANES = 128
SUBLANES = 8
MXU_DIM = 256
VMEM_LIMIT_BYTES = 58 * 1024 * 1024

Q_ROWS = 4
BAND_ROWS = Q_ROWS + NA_ROWS
Q_BLK = Q_ROWS * GRID_W
BAND = BAND_ROWS * GRID_W
KEY_CHUNK = 256
ONES_ROWS = 16
N_BIAS_ROWS = 2 * NA_ROWS - 1
N_BIAS_COLS = 2 * NA_COLS - 1
TOE_W = 1024

LRU_CHUNK = 256
LRU_BWD_CHUNK = 128
LRU_SLABS = LRU_WIDTH // LANES
OUT_GROUP = 8
CONV_ROWS = 128
COL_Q, COL_K, COL_V, COL_ZA, COL_U, COL_ZL = range(6)
HALO = 16


def _dot(a, b):
    return jnp.dot(a, b, preferred_element_type=F32)


def _dot_nt(a, b):
    return lax.dot_general(a, b, (((1,), (1,)), ((), ())), preferred_element_type=F32)


def _params(*semantics):
    return pltpu.CompilerParams(dimension_semantics=semantics, vmem_limit_bytes=VMEM_LIMIT_BYTES)


def _split_bf16(t):
    hi = t.astype(BF16)
    lo = (t - hi.astype(F32)).astype(BF16)
    return hi, lo


def _mod_kernel(cc_ref, w_ref, b_ref, o_ref):
    cc = cc_ref[...]
    s = cc * jax.nn.sigmoid(cc)
    s_hi, s_lo = _split_bf16(s)
    w_hi, w_lo = _split_bf16(w_ref[...])
    o_ref[...] = _dot(s_hi, w_hi) + _dot(s_lo, w_hi) + _dot(s_hi, w_lo) + b_ref[...]


def _modulation(cc, w_mod, b_mod):
    rows, d = cc.shape
    n = w_mod.shape[1]
    tn = 512
    return pl.pallas_call(
        _mod_kernel,
        grid=(n // tn,),
        in_specs=[pl.BlockSpec((rows, d), lambda j: (0, 0)),
                  pl.BlockSpec((d, tn), lambda j: (0, j)),
                  pl.BlockSpec((1, tn), lambda j: (0, j))],
        out_specs=pl.BlockSpec((rows, tn), lambda j: (0, j)),
        out_shape=jax.ShapeDtypeStruct((rows, n), F32),
        compiler_params=_params("arbitrary"),
        name="modulation",
    )(cc, w_mod, b_mod)


def _fill_bias_tables(rexp_ref, h, o_ref):
    k = lax.broadcasted_iota(jnp.int32, (GRID_W, TOE_W), 0)
    q = lax.broadcasted_iota(jnp.int32, (GRID_W, TOE_W), 1) % GRID_W
    diff = k - q + (NA_COLS - 1)
    toe = jnp.zeros((GRID_W, TOE_W), F32)
    for d in range(N_BIAS_COLS):
        toe = jnp.where(diff == d, rexp_ref[h, d:d + 1, :], toe)
    col_start = jnp.clip(q - NA_COLS // 2, 0, GRID_W - NA_COLS)
    col_ok = (k >= col_start) & (k < col_start + NA_COLS)
    toe = jnp.where(col_ok, toe, NEG_INF)
    ri = lax.broadcasted_iota(jnp.int32, (GRID_W, Q_BLK), 1) // GRID_W
    masked = jnp.full((GRID_W, Q_BLK), NEG_INF, F32)

    for typ in range(3):
        off = (NA_ROWS - 1, NA_ROWS // 2 - 1, NA_ROWS - BAND_ROWS + Q_ROWS - 1)[typ]
        for jj in range(BAND_ROWS):
            if typ == 0:
                lo, hi = (0, Q_ROWS) if jj < NA_ROWS else (0, 0)
            elif typ == 1:
                lo, hi = max(jj - NA_ROWS + 1, 0), min(jj, Q_ROWS - 1) + 1
            else:
                lo, hi = (0, Q_ROWS) if jj >= BAND_ROWS - NA_ROWS else (0, 0)
            strip = masked
            if lo < hi:
                e0 = N_BIAS_ROWS - 1 - jj - off
                rolled = pltpu.roll(toe, (-e0 * GRID_W) % TOE_W, axis=1)[:, :Q_BLK]
                strip = jnp.where((ri >= lo) & (ri < hi), rolled, NEG_INF)
            o_ref[h, typ, jj * GRID_W:(jj + 1) * GRID_W, :] = strip


def _bias_rows(rpb):
    rexp = jnp.repeat(jnp.transpose(rpb[:, ::-1, :], (0, 2, 1)) * LOG2E, GRID_W, axis=2)
    return jnp.pad(rexp, ((0, 0), (0, 0), (0, TOE_W - N_BIAS_ROWS * GRID_W)))


def _transpose_kernel(w_ref, o_ref):
    o_ref[...] = w_ref[...].T


def _transpose_cols(w, lo, hi, tn=MXU_DIM):
    rows = w.shape[0]
    assert lo % tn == 0 and hi % tn == 0
    return pl.pallas_call(
        _transpose_kernel,
        grid=((hi - lo) // tn,),
        in_specs=[pl.BlockSpec((rows, tn), lambda i: (0, lo // tn + i))],
        out_specs=pl.BlockSpec((tn, rows), lambda i: (i, 0)),
        out_shape=jax.ShapeDtypeStruct((hi - lo, rows), w.dtype),
        compiler_params=_params("arbitrary"),
        name="weight_transpose",
    )(w)


def _adaln(x, g_ref, scale_ref, shift_ref):
    ms = jnp.mean(x * x, axis=-1, keepdims=True)
    gm = g_ref[...] * (1.0 + scale_ref[0])
    return ((x * lax.rsqrt(ms + EPS)) * gm + shift_ref[0]).astype(BF16)


def _head_rms(t, bd_ref, gain):
    sq = (t * t).astype(BF16)
    bd = bd_ref[...]
    m = jnp.concatenate([_dot(sq[:, :MXU_DIM], bd), _dot(sq[:, MXU_DIM:], bd)], axis=-1)
    return t * lax.rsqrt(m + EPS) * gain


def _rope(t, cos_ref, sin_ref):
    lane = lax.broadcasted_iota(jnp.int32, (1, LANES), 1)
    first = (lane % 32) < 16
    cos, sin = cos_ref[...], sin_ref[...]
    parts = []
    for j in range(t.shape[1] // LANES):
        c = t[:, j * LANES:(j + 1) * LANES]
        swapped = jnp.where(first, pltpu.roll(c, LANES - 16, axis=1), pltpu.roll(c, 16, axis=1))
        parts.append(c * cos + swapped * sin)
    return jnp.concatenate(parts, axis=-1)


def _silu(t):
    return t * jax.nn.sigmoid(t)


def _conv_window(ext, n, cw_ref, cb_ref):
    rows = ext.shape[0]

    def shifted(k):
        return pltpu.roll(ext, (rows - k) % rows, axis=0)[HALO:HALO + n]

    y = cb_ref[...] + cw_ref[0:1, :] * shifted(-1)
    y = y + cw_ref[1:2, :] * ext[HALO:HALO + n]
    y = y + cw_ref[2:3, :] * shifted(1)
    return y + cw_ref[3:4, :] * shifted(2)


def _inproj_kernel(x_ref, xp_ref, xn_ref, g_ref, scale_ref, shift_ref, w_ref, wt_ref, cos_ref, sin_ref, bd_ref,
                   qg_ref, kg_ref, cw_ref, cb_ref, qr_ref, qp_ref, kr_ref, vt_ref, zat_ref, uc_ref, zl_ref):
    tm = x_ref.shape[1]
    aw = ATTN_WIDTH
    i, n_tiles = pl.program_id(0), pl.num_programs(0)
    hb_ext = jnp.concatenate([_adaln(xp_ref[0], g_ref, scale_ref, shift_ref),
                              _adaln(x_ref[0], g_ref, scale_ref, shift_ref),
                              _adaln(xn_ref[0], g_ref, scale_ref, shift_ref)], axis=0)
    hb = hb_ext[HALO:HALO + tm]

    def cols(j):
        return w_ref[:, j * aw:(j + 1) * aw]

    u_ext = _dot(hb_ext, cols(COL_U))
    row = lax.broadcasted_iota(jnp.int32, (tm + 2 * HALO, 1), 0)
    inside = ((row >= HALO) | (i > 0)) & ((row < HALO + tm) | (i < n_tiles - 1))
    u_ext = jnp.where(inside, u_ext, 0.0)

    qn = _head_rms(_dot(hb, cols(COL_Q)), bd_ref, qg_ref[...])
    qp_ref[0] = qn.astype(BF16)
    qr_ref[0] = _rope(qn, cos_ref, sin_ref).astype(BF16)
    kn = _head_rms(_dot(hb, cols(COL_K)), bd_ref, kg_ref[...])
    kr_ref[0] = _rope(kn, cos_ref, sin_ref).astype(BF16)

    zl_ref[0] = _silu(_dot(hb, cols(COL_ZL))).astype(BF16)
    vt_ref[0] = _dot_nt(wt_ref[:aw, :], hb).astype(BF16)
    zat_ref[0] = _silu(_dot_nt(wt_ref[aw:, :], hb)).astype(BF16)
    for r in range(0, tm, CONV_ROWS):
        uc_ref[0, r:r + CONV_ROWS, :] = _conv_window(u_ext[r:r + CONV_ROWS + 2 * HALO], CONV_ROWS, cw_ref, cb_ref)


def _inproj(x, g, scale, shift, w_bf, wt_bf, cos_t, sin_t, bd, qg, kg, conv_w, conv_b, tm=1024):
    b, s, d = x.shape
    aw = ATTN_WIDTH
    tok = lambda i, j: (j, i, 0)
    per_b = lambda i, j: (j, 0, 0)
    const = lambda i, j: (0, 0)
    halo_per_tile = tm // HALO
    n_halo = s // HALO
    out_bf = jax.ShapeDtypeStruct((b, s, aw), BF16)
    out_t = jax.ShapeDtypeStruct((b, aw, s), BF16)
    out_f32 = jax.ShapeDtypeStruct((b, s, aw), F32)
    blk = pl.BlockSpec((1, tm, aw), tok)
    blk_t = pl.BlockSpec((1, aw, tm), lambda i, j: (j, 0, i))
    return pl.pallas_call(
        _inproj_kernel,
        grid=(s // tm, b),
        in_specs=[pl.BlockSpec((1, tm, d), tok),
                  pl.BlockSpec((1, HALO, d), lambda i, j: (j, jnp.maximum(i * halo_per_tile - 1, 0), 0)),
                  pl.BlockSpec((1, HALO, d), lambda i, j: (j, jnp.minimum((i + 1) * halo_per_tile, n_halo - 1), 0)),
                  pl.BlockSpec((1, d), const),
                  pl.BlockSpec((1, 1, d), per_b),
                  pl.BlockSpec((1, 1, d), per_b),
                  pl.BlockSpec(w_bf.shape, const),
                  pl.BlockSpec(wt_bf.shape, const),
                  pl.BlockSpec((tm, LANES), lambda i, j: (i, 0)),
                  pl.BlockSpec((tm, LANES), lambda i, j: (i, 0)),
                  pl.BlockSpec((MXU_DIM, MXU_DIM), const),
                  pl.BlockSpec((1, aw), const),
                  pl.BlockSpec((1, aw), const),
                  pl.BlockSpec((CONV_WIDTH, aw), const),
                  pl.BlockSpec((1, aw), const)],
        out_specs=[blk, blk, blk, blk_t, blk_t, blk, blk],
        out_shape=[out_bf, out_bf, out_bf, out_t, out_t, out_f32, out_bf],
        compiler_params=_params("arbitrary", "arbitrary"),
        name="inproj",
    )(x, x, x, g, scale, shift, w_bf, wt_bf, cos_t, sin_t, bd, qg, kg, conv_w, conv_b)


def _ctxproj_kernel(c_ref, g_ref, scale_ref, shift_ref, wk_ref, wu_ref, wvt_ref, bd_ref, kg_ref, cw_ref, cb_ref,
                    kc_ref, vct_ref, uc_ref):
    nb, l, d = c_ref.shape
    aw = ATTN_WIDTH
    hb = _adaln(c_ref[...].reshape(nb * l, d), g_ref, scale_ref, shift_ref)
    kc = _head_rms(_dot(hb, wk_ref[...]), bd_ref, kg_ref[...]).astype(BF16)
    u = _dot(hb, wu_ref[...])
    vct = _dot_nt(wvt_ref[...], hb).astype(BF16)
    pad = jnp.zeros((HALO, aw), F32)
    for bi in range(nb):
        kc_ref[bi] = kc[bi * l:(bi + 1) * l]
        uc_ref[bi] = _conv_window(jnp.concatenate([pad, u[bi * l:(bi + 1) * l], pad], axis=0), l, cw_ref, cb_ref)
        vct_ref[bi] = vct[:, bi * l:(bi + 1) * l]


def _ctxproj(ctx, g, scale_c, shift_c, w_ctx_bf, wt_ctx_bf, bd, kg, conv_w, conv_b, nb=4):
    b, l, d = ctx.shape
    aw = ATTN_WIDTH
    const = lambda i: (0, 0)
    blk = pl.BlockSpec((nb, l, aw), lambda i: (i, 0, 0))
    return pl.pallas_call(
        _ctxproj_kernel,
        grid=(b // nb,),
        in_specs=[pl.BlockSpec((nb, l, d), lambda i: (i, 0, 0)),
                  pl.BlockSpec((1, d), const),
                  pl.BlockSpec((1, 1, d), lambda i: (0, 0, 0)),
                  pl.BlockSpec((1, 1, d), lambda i: (0, 0, 0)),
                  pl.BlockSpec((d, aw), lambda i: (0, COL_K)),
                  pl.BlockSpec((d, aw), lambda i: (0, COL_U)),
                  pl.BlockSpec((aw, d), const),
                  pl.BlockSpec((MXU_DIM, MXU_DIM), const),
                  pl.BlockSpec((1, aw), const),
                  pl.BlockSpec((CONV_WIDTH, aw), const),
                  pl.BlockSpec((1, aw), const)],
        out_specs=[blk, pl.BlockSpec((nb, aw, l), lambda i: (i, 0, 0)), blk],
        out_shape=[jax.ShapeDtypeStruct((b, l, aw), BF16),
                   jax.ShapeDtypeStruct((b, aw, l), BF16),
                   jax.ShapeDtypeStruct((b, l, aw), F32)],
        compiler_params=_params("arbitrary"),
        name="ctxproj",
    )(ctx, g, scale_c, shift_c, w_ctx_bf, w_ctx_bf, wt_ctx_bf, bd, kg, conv_w, conv_b)


def _attn_kernel(qr_ref, qp_ref, kr_ref, vt_ref, kc_ref, vct_ref, zat_ref, rexp_ref, o_ref, bias_ref, s_buf0, s_buf1):
    @pl.when(pl.program_id(1) == 0)
    def _():
        for h in range(2):
            _fill_bias_tables(rexp_ref, h, bias_ref)

    rows = qr_ref.shape[1] // GRID_W
    n_blk = rows // Q_ROWS
    chunk_rows = KEY_CHUNK // GRID_W
    n_tiles = Q_BLK // LANES
    lane_head = lax.broadcasted_iota(jnp.int32, (1, LANES), 1) // HEAD_DIM
    s_bufs = (s_buf0, s_buf1)
    ones_rows = jnp.ones((ONES_ROWS, KEY_CHUNK), BF16)

    def in_window(typ, jj, tile):
        if typ == 0:
            return jj < NA_ROWS
        if typ == 2:
            return jj >= BAND_ROWS - NA_ROWS
        return any(0 <= jj - ri < NA_ROWS for ri in range(tile * Q_ROWS // n_tiles, (tile + 1) * Q_ROWS // n_tiles))

    def local_chunks(typ):
        return [c for c in range(BAND // KEY_CHUNK)
                if any(in_window(typ, c * chunk_rows + r, t) for r in range(chunk_rows) for t in range(n_tiles))]

    def sub_blocks(c):
        for r in range(chunk_rows):
            for t in range(n_tiles):
                yield c * chunk_rows + r, t, slice(r * GRID_W, (r + 1) * GRID_W), slice(t * LANES, (t + 1) * LANES)

    def item(rb):
        rb = jnp.asarray(rb, jnp.int32)
        q0 = pl.multiple_of(rb * Q_BLK, Q_BLK)
        band0 = pl.multiple_of(jnp.clip(rb * Q_ROWS - NA_ROWS // 2, 0, rows - BAND_ROWS) * GRID_W, Q_BLK)
        return q0, band0

    def col_max(m, blk, t):
        m = list(m)
        m[t] = jnp.maximum(m[t], jnp.max(blk, axis=0, keepdims=True))
        return tuple(m)

    def score_steps(rb, h, typ):
        q0, band0 = item(rb)
        s_buf = s_bufs[h]
        mine = lane_head == h
        qr = jnp.where(mine, qr_ref[0, pl.ds(q0, Q_BLK), :], jnp.zeros((), BF16))
        qp = jnp.where(mine, qp_ref[0, pl.ds(q0, Q_BLK), :], jnp.zeros((), BF16))

        def local(c, m):
            n_rows = 1 + max(r for r in range(chunk_rows)
                             if any(in_window(typ, c * chunk_rows + r, t) for t in range(n_tiles)))
            s = _dot_nt(kr_ref[0, pl.ds(band0 + c * KEY_CHUNK, n_rows * GRID_W), :], qr)
            for jj, t, r_sl, l_sl in sub_blocks(c):
                if in_window(typ, jj, t):
                    k_sl = slice(jj * GRID_W, (jj + 1) * GRID_W)
                    blk = s[r_sl, l_sl] + bias_ref[h, typ, k_sl, l_sl]
                    s_buf[k_sl, l_sl] = blk
                    m = col_max(m, blk, t)
            return m

        def context(m):
            s = _dot_nt(kc_ref[0], qp)
            s_buf[pl.ds(BAND, KEY_CHUNK), :] = s
            for t in range(n_tiles):
                m = col_max(m, s[:, t * LANES:(t + 1) * LANES], t)
            return m

        return [functools.partial(local, c) for c in local_chunks(typ)] + [context]

    def value_steps(rb, h, typ, m):
        q0, band0 = item(rb)
        s_buf = s_bufs[h]
        hd = pl.ds(h * HEAD_DIM, HEAD_DIM)

        def probs(k_sl, t):
            return jnp.exp2(s_buf[k_sl, t * LANES:(t + 1) * LANES] - m[t]).astype(BF16)

        def local(c, acc):
            zero_blk = jnp.zeros((GRID_W, LANES), BF16)
            row_blks = []
            for r in range(chunk_rows):
                jj = c * chunk_rows + r
                k_sl = slice(jj * GRID_W, (jj + 1) * GRID_W)
                row_blks.append(jnp.concatenate(
                    [probs(k_sl, t) if in_window(typ, jj, t) else zero_blk for t in range(n_tiles)], axis=1))
            p = jnp.concatenate(row_blks, axis=0)
            vt = jnp.concatenate([vt_ref[0, hd, pl.ds(band0 + c * KEY_CHUNK, KEY_CHUNK)], ones_rows], axis=0)
            return acc + _dot(vt, p)

        def context(acc):
            k_sl = slice(BAND, BAND + KEY_CHUNK)
            p = jnp.concatenate([probs(k_sl, t) for t in range(n_tiles)], axis=1)
            vt = jnp.concatenate([vct_ref[0, hd, :], ones_rows], axis=0)
            return acc + _dot(vt, p)

        return [functools.partial(local, c) for c in local_chunks(typ)] + [context]

    def write_out(rb, h, acc):
        q0, _ = item(rb)
        hd = pl.ds(h * HEAD_DIM, HEAD_DIM)
        gated = (acc[:HEAD_DIM] / acc[HEAD_DIM:HEAD_DIM + 1]) * zat_ref[0, hd, pl.ds(q0, Q_BLK)].astype(F32)
        o_ref[0, hd, pl.ds(q0, Q_BLK)] = gated.astype(o_ref.dtype)

    m_init = (jnp.full((1, LANES), -jnp.inf, F32),) * n_tiles
    acc_init = jnp.zeros((HEAD_DIM + ONES_ROWS, Q_BLK), F32)

    def overlapped(score_item, value_item, out_item, carry):
        m_prev, acc_prev = carry
        s_steps = score_steps(*score_item) if score_item else []
        v_steps = value_steps(*value_item, m_prev) if value_item else []
        m, acc = m_init, acc_init
        for i, (s_step, v_step) in enumerate(itertools.zip_longest(s_steps, v_steps)):
            if s_step:
                m = s_step(m)
            if i == 0 and out_item:
                write_out(*out_item, acc_prev)
            if v_step:
                acc = v_step(acc)
        return m, acc

    first, mid, last = 0, 1, 2
    carry = overlapped((0, 0, first), None, None, (None, None))
    carry = overlapped((0, 1, first), (0, 0, first), None, carry)
    carry = overlapped((1, 0, mid), (0, 1, first), (0, 0), carry)

    def body(rb, carry):
        carry = overlapped((rb, 1, mid), (rb, 0, mid), (rb - 1, 1), carry)
        return overlapped((rb + 1, 0, mid), (rb, 1, mid), (rb, 0), carry)

    carry = lax.fori_loop(1, n_blk - 2, body, carry, unroll=3)
    carry = overlapped((n_blk - 2, 1, mid), (n_blk - 2, 0, mid), (n_blk - 3, 1), carry)
    carry = overlapped((n_blk - 1, 0, last), (n_blk - 2, 1, mid), (n_blk - 2, 0), carry)
    carry = overlapped((n_blk - 1, 1, last), (n_blk - 1, 0, last), (n_blk - 2, 1), carry)
    carry = overlapped(None, (n_blk - 1, 1, last), (n_blk - 1, 0), carry)
    write_out(n_blk - 1, 1, carry[1])


def _attention(qr, qp, kr, vt, kc, vct, zat, rexp):
    b, s, aw = qr.shape
    l = kc.shape[1]
    n_pairs = aw // LANES
    lat = pl.BlockSpec((1, s, LANES), lambda p, i: (i, 0, p))
    lat_t = pl.BlockSpec((1, LANES, s), lambda p, i: (i, p, 0))
    return pl.pallas_call(
        _attn_kernel,
        grid=(n_pairs, b),
        in_specs=[lat, lat, lat, lat_t,
                  pl.BlockSpec((1, l, LANES), lambda p, i: (i, 0, p)),
                  pl.BlockSpec((1, LANES, l), lambda p, i: (i, p, 0)),
                  lat_t,
                  pl.BlockSpec((2, N_BIAS_COLS, TOE_W), lambda p, i: (p, 0, 0))],
        out_specs=lat_t,
        out_shape=jax.ShapeDtypeStruct((b, aw, s), BF16),
        scratch_shapes=[pltpu.VMEM((2, 3, BAND, Q_BLK), F32),
                        pltpu.VMEM((BAND + l, Q_BLK), F32), pltpu.VMEM((BAND + l, Q_BLK), F32)],
        compiler_params=_params("arbitrary", "arbitrary"),
        name="attention",
    )(qr, qp, kr, vt, kc, vct, zat, rexp)


def _lru_pitch(tc):
    pitch = tc + SUBLANES
    assert pitch % (2 * SUBLANES) == SUBLANES
    return pitch


def _lru_gate_consts(ba, bx, lam):
    return 0.5 * ba, 0.5 * bx, jax.nn.softplus(-lam) * (-0.5 * LRU_C * LOG2E)


def _lru_fill(bi, uc, wg, consts, a_s, b_s):
    half_ba, half_bx, half_log2a = consts
    tc = uc.shape[0]
    ucb = uc.astype(BF16)

    def gate_tanh(gi, half_bias):
        pre = jnp.concatenate([_dot(ucb[:, :MXU_DIM], wg(gi, 0)), _dot(ucb[:, MXU_DIM:], wg(gi, 1))], axis=-1)
        return jnp.tanh(pre + half_bias)

    tr = gate_tanh(0, half_ba)
    ti = gate_tanh(1, half_bx)
    a = jnp.exp2(tr * half_log2a + half_log2a)
    gap = 1.0 - a * a
    mult = gap * lax.rsqrt(jnp.maximum(gap, 1e-30))
    bb = mult * ((0.5 * ti + 0.5) * uc)
    r0 = pl.multiple_of(bi * _lru_pitch(tc), SUBLANES)
    for j in range(LRU_SLABS):
        a_s[j, pl.ds(r0, tc), :] = a[:, j * LANES:(j + 1) * LANES]
        b_s[j, pl.ds(r0, tc), :] = bb[:, j * LANES:(j + 1) * LANES]


def _lru_scan(reverse, tc, nb, a_s, b_s, h_s, carry_s):
    pitch = _lru_pitch(tc)

    n_groups = tc // SUBLANES

    def body(g, hs):
        base = pl.multiple_of(((n_groups - 1 - g) if reverse else g) * SUBLANES, SUBLANES)
        hs = list(hs)
        for k in (reversed(range(SUBLANES)) if reverse else range(SUBLANES)):
            rows = pl.ds(base + k, nb, stride=pitch)
            for j in range(LRU_SLABS):
                hs[j] = a_s[j, rows, :] * hs[j] + b_s[j, rows, :]
                h_s[j, rows, :] = hs[j]
        return tuple(hs)

    hs = tuple(carry_s[:, j * LANES:(j + 1) * LANES] for j in range(LRU_SLABS))
    hs = lax.fori_loop(0, n_groups, body, hs)
    for j in range(LRU_SLABS):
        carry_s[:, j * LANES:(j + 1) * LANES] = hs[j]


def _lru_rows(bi, tc, h_s):
    r0 = pl.multiple_of(bi * _lru_pitch(tc), SUBLANES)
    return jnp.concatenate([h_s[j, pl.ds(r0, tc), :] for j in range(LRU_SLABS)], axis=-1)


def _lru_fwd_kernel(u_ref, uc_ref, wg_ref, ba_ref, bx_ref, lam_ref, hf_ref, hb0_ref, a_s, b_s, h_s, carry_s):
    nb, tc = u_ref.shape[0], u_ref.shape[1]
    consts = [_lru_gate_consts(ba_ref[d], bx_ref[d], lam_ref[d]) for d in range(2)]

    def fill_all(src_ref, d, unroll):
        def step(bi, carry):
            _lru_fill(bi, src_ref[bi], lambda gi, hf: wg_ref[d, gi, hf], consts[d], a_s, b_s)
            return carry

        lax.fori_loop(0, nb, step, 0, unroll=unroll)

    @pl.when(pl.program_id(0) == 0)
    def _():
        carry_s[...] = jnp.zeros_like(carry_s)
        fill_all(uc_ref, 1, 1)
        _lru_scan(True, tc, nb, a_s, b_s, h_s, carry_s)
        hb0_ref[...] = carry_s[...]
        carry_s[...] = jnp.zeros_like(carry_s)
        fill_all(uc_ref, 0, 1)
        _lru_scan(False, tc, nb, a_s, b_s, h_s, carry_s)

    fill_all(u_ref, 0, 4)
    _lru_scan(False, tc, nb, a_s, b_s, h_s, carry_s)

    def emit(bi, carry):
        hf_ref[bi] = _lru_rows(bi, tc, h_s).astype(hf_ref.dtype)
        return carry

    lax.fori_loop(0, nb, emit, 0)


def _lru_fwd(u, u_c, wg, ba, bx, lam):
    b, s, lw = u.shape
    tc = LRU_CHUNK
    assert u_c.shape[1] == tc and s % tc == 0
    full = lambda *shape: pl.BlockSpec(shape, lambda i: (0,) * len(shape))
    cblk = pl.BlockSpec((b, tc, lw), lambda i: (0, i, 0))
    slab = pltpu.VMEM((LRU_SLABS, b * _lru_pitch(tc), LANES), F32)
    return pl.pallas_call(
        _lru_fwd_kernel,
        grid=(s // tc,),
        in_specs=[cblk, full(b, tc, lw), full(2, 2, 2, MXU_DIM, MXU_DIM), full(2, 1, lw), full(2, 1, lw), full(2, 1, lw)],
        out_specs=[cblk, full(b, lw)],
        out_shape=[jax.ShapeDtypeStruct((b, s, lw), BF16), jax.ShapeDtypeStruct((b, lw), F32)],
        scratch_shapes=[slab, slab, slab, pltpu.VMEM((b, lw), F32)],
        compiler_params=_params("arbitrary"),
        name="rglru_fwd",
    )(u, u_c, wg, ba, bx, lam)


def _lru_bwd_out_kernel(u_ref, hf_ref, zl_ref, agt_ref, x_ref, hb0_ref, gate_ref, wg_ref, ba_ref, bx_ref, lam_ref,
                        wo_ref, o_ref, a_s, b_s, h_s, carry_s):
    nb, tc = u_ref.shape[0], u_ref.shape[1]
    consts = _lru_gate_consts(ba_ref[...], bx_ref[...], lam_ref[...])

    @pl.when(pl.program_id(0) == 0)
    def _():
        carry_s[...] = hb0_ref[...]

    def fill(bi, carry):
        _lru_fill(bi, u_ref[bi], lambda gi, hf: wg_ref[gi, hf], consts, a_s, b_s)
        return carry

    lax.fori_loop(0, nb, fill, 0, unroll=4)
    _lru_scan(True, tc, nb, a_s, b_s, h_s, carry_s)

    def emit(gi, carry):
        group = [OUT_GROUP * gi + k for k in range(OUT_GROUP)]
        y = jnp.concatenate([((hf_ref[bi].astype(F32) + _lru_rows(bi, tc, h_s)) * zl_ref[bi].astype(F32)).astype(BF16)
                             for bi in group], axis=0)
        at = jnp.concatenate([agt_ref[bi] for bi in group], axis=1)
        mix = lax.dot_general(at, wo_ref[:ATTN_WIDTH, :], (((0,), (0,)), ((), ())), preferred_element_type=F32)
        mix = mix + _dot(y, wo_ref[ATTN_WIDTH:, :])
        for k, bi in enumerate(group):
            o_ref[bi] = x_ref[bi] + gate_ref[bi] * mix[k * tc:(k + 1) * tc]
        return carry

    lax.fori_loop(0, nb // OUT_GROUP, emit, 0)


def _lru_bwd_out(u, hf, zl, agt, x, hb0, gate, wg, ba, bx, lam, wo_bf):
    b, s, lw = u.shape
    d = x.shape[2]
    tc = LRU_BWD_CHUNK
    n_chunks = s // tc
    assert s % tc == 0 and b % OUT_GROUP == 0
    full = lambda *shape: pl.BlockSpec(shape, lambda i: (0,) * len(shape))
    rev = lambda i: n_chunks - 1 - i
    cblk = pl.BlockSpec((b, tc, lw), lambda i: (0, rev(i), 0))
    xblk = pl.BlockSpec((b, tc, d), lambda i: (0, rev(i), 0))
    slab = pltpu.VMEM((LRU_SLABS, b * _lru_pitch(tc), LANES), F32)
    return pl.pallas_call(
        _lru_bwd_out_kernel,
        grid=(n_chunks,),
        in_specs=[cblk, cblk, cblk,
                  pl.BlockSpec((b, ATTN_WIDTH, tc), lambda i: (0, 0, rev(i))),
                  xblk, full(b, lw), full(b, 1, d),
                  full(2, 2, MXU_DIM, MXU_DIM), full(1, lw), full(1, lw), full(1, lw),
                  full(ATTN_WIDTH + lw, d)],
        out_specs=xblk,
        out_shape=jax.ShapeDtypeStruct(x.shape, x.dtype),
        scratch_shapes=[slab, slab, slab, pltpu.VMEM((b, lw), F32)],
        compiler_params=_params("arbitrary"),
        name="rglru_bwd_outproj",
    )(u, hf, zl, agt, x, hb0, gate, wg, ba, bx, lam, wo_bf)


def _rope_tables(s):
    half = HEAD_DIM // 4
    rows = s // GRID_W
    assert rows <= GRID_W
    inv = ROPE_BASE ** (-jnp.arange(half, dtype=F32) / half)
    ang = jnp.arange(GRID_W, dtype=F32)[:, None] * inv[None, :]
    shape = (rows, GRID_W, half)

    def table(f, lo_sign):
        by_row = jnp.broadcast_to(f[:rows, None, :], shape)
        by_col = jnp.broadcast_to(f[None, :, :], shape)
        head = jnp.concatenate([lo_sign * by_row, by_row, lo_sign * by_col, by_col], axis=-1)
        return jnp.tile(head, (1, 1, LANES // HEAD_DIM)).reshape(s, LANES)

    return table(jnp.cos(ang), 1.0), table(jnp.sin(ang), -1.0)


def _block_diag_gates(w):
    n = MXU_DIM // LRU_BLOCK
    lead = w.shape[:-3]
    w = w.reshape(lead + (2, n, LRU_BLOCK, LRU_BLOCK))
    on_diag = jnp.eye(n, dtype=bool)[:, None, :, None]
    blocks = jnp.where(on_diag, w[..., :, :, None, :], jnp.zeros((), w.dtype))
    return blocks.reshape(lead + (2, MXU_DIM, MXU_DIM))


def kernel(x, c, ctx, c_ctx, norm_g, w_mod, b_mod, w_in, w_out, q_norm_g, k_norm_g, rpb, conv_w, conv_b,
           lru_wa, lru_ba, lru_wx, lru_bx, lru_lam):
    bsz, s, d = x.shape
    assert w_in.shape[0] == 1 and d == D_MODEL and s % (GRID_W * Q_ROWS) == 0
    aw, lw = ATTN_WIDTH, LRU_WIDTH

    pad_rows = 2 * SUBLANES - bsz - 1
    cc = jnp.concatenate([c, c_ctx[None, :], jnp.zeros((pad_rows, d), F32)], axis=0)
    mod = _modulation(cc, w_mod[0], b_mod[0][None, :])
    shift, scale, gate = [mod[:bsz, i * d:(i + 1) * d][:, None, :] for i in range(3)]
    shift_c, scale_c = [mod[bsz:bsz + 1, i * d:(i + 1) * d][:, None, :] for i in range(2)]

    w = w_in[0]
    w_bf = w.astype(BF16)
    wt_bf = _transpose_cols(w_bf, COL_V * aw, (COL_ZA + 1) * aw)
    w_ctx_bf, wt_ctx_bf = w_bf, wt_bf
    g = norm_g[0][None, :]
    cos_t, sin_t = _rope_tables(s)
    blk = jnp.arange(MXU_DIM) // HEAD_DIM
    bd = jnp.where(blk[:, None] == blk[None, :], 1.0 / HEAD_DIM, 0.0).astype(BF16)
    qg = jnp.tile(q_norm_g[0] * (HEAD_DIM ** -0.5 * LOG2E), N_HEADS)[None, :]
    kg = jnp.tile(k_norm_g[0], N_HEADS)[None, :]

    cw, cb = conv_w[0], conv_b[0][None, :]
    qr, qp, kr, vt, zat, u, zl = _inproj(x, g, scale, shift, w_bf, wt_bf, cos_t, sin_t, bd, qg, kg, cw, cb)
    kc, vct, u_c = _ctxproj(ctx, g, scale_c, shift_c, w_ctx_bf, wt_ctx_bf, bd, kg, cw, cb)

    agt = _attention(qr, qp, kr, vt, kc, vct, zat, _bias_rows(rpb[0]))

    wg = (0.5 * jnp.stack([lru_wa[0], lru_wx[0]], axis=1)).astype(BF16)
    wg = _block_diag_gates(wg)
    ba, bx, lam = lru_ba[0][:, None, :], lru_bx[0][:, None, :], lru_lam[0][:, None, :]
    hf, hb0 = _lru_fwd(u, u_c, wg, ba, bx, lam)
    return _lru_bwd_out(u, hf, zl, agt, x, hb0, gate, wg[1], ba[1], bx[1], lam[1], w_out[0].astype(BF16))
```

```python
import functools
import itertools
import math

import jax
import jax.numpy as jnp
from jax import lax
from jax.experimental import pallas as pl
from jax.experimental.pallas import tpu as pltpu

F32 = jnp.float32
BF16 = jnp.bfloat16

D_MODEL = 1024
GRID_W = 64
HEAD_DIM = 64
ATTN_WIDTH = 512
LRU_WIDTH = 512
N_HEADS = ATTN_WIDTH // HEAD_DIM
LRU_BLOCK = 64
NA_ROWS = 8
NA_COLS = 16
CONV_WIDTH = 4
LRU_C = 8.0
ROPE_BASE = 10000.0
EPS = 1e-6
NEG_INF = -1e30
LOG2E = math.log2(math.e)

LANES = 128
SUBLANES = 8
MXU_DIM = 256
VMEM_LIMIT_BYTES = 58 * 1024 * 1024

Q_ROWS = 4
BAND_ROWS = Q_ROWS + NA_ROWS
Q_BLK = Q_ROWS * GRID_W
BAND = BAND_ROWS * GRID_W
KEY_CHUNK = 256
ONES_ROWS = 16
N_BIAS_ROWS = 2 * NA_ROWS - 1
N_BIAS_COLS = 2 * NA_COLS - 1
TOE_W = 1024

LRU_CHUNK = 256
LRU_BWD_CHUNK = 128
LRU_SLABS = LRU_WIDTH // LANES
OUT_GROUP = 8
CONV_ROWS = 128
COL_Q, COL_K, COL_V, COL_ZA, COL_U, COL_ZL = range(6)
HALO = 16


def _dot(a, b):
    return jnp.dot(a, b, preferred_element_type=F32)


def _dot_nt(a, b):
    return lax.dot_general(a, b, (((1,), (1,)), ((), ())), preferred_element_type=F32)


def _params(*semantics):
    return pltpu.CompilerParams(dimension_semantics=semantics, vmem_limit_bytes=VMEM_LIMIT_BYTES)


def _split_bf16(t):
    hi = t.astype(BF16)
    lo = (t - hi.astype(F32)).astype(BF16)
    return hi, lo


def _mod_kernel(cc_ref, w_ref, b_ref, o_ref):
    cc = cc_ref[...]
    s = cc * jax.nn.sigmoid(cc)
    s_hi, s_lo = _split_bf16(s)
    w_hi, w_lo = _split_bf16(w_ref[...])
    o_ref[...] = _dot(s_hi, w_hi) + _dot(s_lo, w_hi) + _dot(s_hi, w_lo) + b_ref[...]


def _modulation(cc, w_mod, b_mod):
    rows, d = cc.shape
    n = w_mod.shape[1]
    tn = 512
    return pl.pallas_call(
        _mod_kernel,
        grid=(n // tn,),
        in_specs=[pl.BlockSpec((rows, d), lambda j: (0, 0)),
                  pl.BlockSpec((d, tn), lambda j: (0, j)),
                  pl.BlockSpec((1, tn), lambda j: (0, j))],
        out_specs=pl.BlockSpec((rows, tn), lambda j: (0, j)),
        out_shape=jax.ShapeDtypeStruct((rows, n), F32),
        compiler_params=_params("arbitrary"),
        name="modulation",
    )(cc, w_mod, b_mod)


def _fill_bias_tables(rexp_ref, h, o_ref):
    k = lax.broadcasted_iota(jnp.int32, (GRID_W, TOE_W), 0)
    q = lax.broadcasted_iota(jnp.int32, (GRID_W, TOE_W), 1) % GRID_W
    diff = k - q + (NA_COLS - 1)
    toe = jnp.zeros((GRID_W, TOE_W), F32)
    for d in range(N_BIAS_COLS):
        toe = jnp.where(diff == d, rexp_ref[h, d:d + 1, :], toe)
    col_start = jnp.clip(q - NA_COLS // 2, 0, GRID_W - NA_COLS)
    col_ok = (k >= col_start) & (k < col_start + NA_COLS)
    toe = jnp.where(col_ok, toe, NEG_INF)
    ri = lax.broadcasted_iota(jnp.int32, (GRID_W, Q_BLK), 1) // GRID_W
    masked = jnp.full((GRID_W, Q_BLK), NEG_INF, F32)

    for typ in range(3):
        off = (NA_ROWS - 1, NA_ROWS // 2 - 1, NA_ROWS - BAND_ROWS + Q_ROWS - 1)[typ]
        for jj in range(BAND_ROWS):
            if typ == 0:
                lo, hi = (0, Q_ROWS) if jj < NA_ROWS else (0, 0)
            elif typ == 1:
                lo, hi = max(jj - NA_ROWS + 1, 0), min(jj, Q_ROWS - 1) + 1
            else:
                lo, hi = (0, Q_ROWS) if jj >= BAND_ROWS - NA_ROWS else (0, 0)
            strip = masked
            if lo < hi:
                e0 = N_BIAS_ROWS - 1 - jj - off
                rolled = pltpu.roll(toe, (-e0 * GRID_W) % TOE_W, axis=1)[:, :Q_BLK]
                strip = jnp.where((ri >= lo) & (ri < hi), rolled, NEG_INF)
            o_ref[h, typ, jj * GRID_W:(jj + 1) * GRID_W, :] = strip


def _bias_rows(rpb):
    rexp = jnp.repeat(jnp.transpose(rpb[:, ::-1, :], (0, 2, 1)) * LOG2E, GRID_W, axis=2)
    return jnp.pad(rexp, ((0, 0), (0, 0), (0, TOE_W - N_BIAS_ROWS * GRID_W)))


def _transpose_kernel(w_ref, o_ref):
    o_ref[...] = w_ref[...].T


def _transpose_cols(w, lo, hi, tn=MXU_DIM):
    rows = w.shape[0]
    assert lo % tn == 0 and hi % tn == 0
    return pl.pallas_call(
        _transpose_kernel,
        grid=((hi - lo) // tn,),
        in_specs=[pl.BlockSpec((rows, tn), lambda i: (0, lo // tn + i))],
        out_specs=pl.BlockSpec((tn, rows), lambda i: (i, 0)),
        out_shape=jax.ShapeDtypeStruct((hi - lo, rows), w.dtype),
        compiler_params=_params("arbitrary"),
        name="weight_transpose",
    )(w)


def _adaln(x, g_ref, scale_ref, shift_ref):
    ms = jnp.mean(x * x, axis=-1, keepdims=True)
    gm = g_ref[...] * (1.0 + scale_ref[0])
    return ((x * lax.rsqrt(ms + EPS)) * gm + shift_ref[0]).astype(BF16)


def _head_rms(t, bd_ref, gain):
    sq = (t * t).astype(BF16)
    bd = bd_ref[...]
    m = jnp.concatenate([_dot(sq[:, :MXU_DIM], bd), _dot(sq[:, MXU_DIM:], bd)], axis=-1)
    return t * lax.rsqrt(m + EPS) * gain


def _rope_tile(rows_ref, cols_ref, f):
    lane = lax.broadcasted_iota(jnp.int32, (1, LANES), 1)
    by_row = (lane % HEAD_DIM) < HEAD_DIM // 2
    by_col = cols_ref[f]
    return jnp.concatenate([jnp.where(by_row, rows_ref[f, k:k + 1, :], by_col) for k in range(rows_ref.shape[1])],
                           axis=0)


def _rope(t, cos, sin):
    lane = lax.broadcasted_iota(jnp.int32, (1, LANES), 1)
    first = (lane % 32) < 16
    parts = []
    for j in range(t.shape[1] // LANES):
        c = t[:, j * LANES:(j + 1) * LANES]
        swapped = jnp.where(first, pltpu.roll(c, LANES - 16, axis=1), pltpu.roll(c, 16, axis=1))
        parts.append(c * cos + swapped * sin)
    return jnp.concatenate(parts, axis=-1)


def _silu(t):
    return t * jax.nn.sigmoid(t)


def _conv_window(ext, n, cw_ref, cb_ref):
    rows = ext.shape[0]

    def shifted(k):
        return pltpu.roll(ext, (rows - k) % rows, axis=0)[HALO:HALO + n]

    y = cb_ref[...] + cw_ref[0:1, :] * shifted(-1)
    y = y + cw_ref[1:2, :] * ext[HALO:HALO + n]
    y = y + cw_ref[2:3, :] * shifted(1)
    return y + cw_ref[3:4, :] * shifted(2)


def _inproj_kernel(x_ref, xp_ref, xn_ref, g_ref, scale_ref, shift_ref, w_ref, wt_ref, rope_rows_ref, rope_cols_ref,
                   bd_ref, qg_ref, kg_ref, cw_ref, cb_ref, qr_ref, qp_ref, kr_ref, vt_ref, zat_ref, uc_ref, zl_ref):
    tm = x_ref.shape[1]
    aw = ATTN_WIDTH
    i, n_tiles = pl.program_id(0), pl.num_programs(0)
    hb_ext = jnp.concatenate([_adaln(xp_ref[0], g_ref, scale_ref, shift_ref),
                              _adaln(x_ref[0], g_ref, scale_ref, shift_ref),
                              _adaln(xn_ref[0], g_ref, scale_ref, shift_ref)], axis=0)
    hb = hb_ext[HALO:HALO + tm]

    def cols(j):
        return w_ref[:, j * aw:(j + 1) * aw]

    u_ext = _dot(hb_ext, cols(COL_U))
    row = lax.broadcasted_iota(jnp.int32, (tm + 2 * HALO, 1), 0)
    inside = ((row >= HALO) | (i > 0)) & ((row < HALO + tm) | (i < n_tiles - 1))
    u_ext = jnp.where(inside, u_ext, 0.0)

    cos = _rope_tile(rope_rows_ref, rope_cols_ref, 0)
    sin = _rope_tile(rope_rows_ref, rope_cols_ref, 1)
    qn = _head_rms(_dot(hb, cols(COL_Q)), bd_ref, qg_ref[...])
    qp_ref[0] = qn.astype(BF16)
    qr_ref[0] = _rope(qn, cos, sin).astype(BF16)
    kn = _head_rms(_dot(hb, cols(COL_K)), bd_ref, kg_ref[...])
    kr_ref[0] = _rope(kn, cos, sin).astype(BF16)

    zl_ref[0] = _silu(_dot(hb, cols(COL_ZL))).astype(BF16)
    vt_ref[0] = _dot_nt(wt_ref[:aw, :], hb).astype(BF16)
    zat_ref[0] = _silu(_dot_nt(wt_ref[aw:, :], hb)).astype(BF16)
    for r in range(0, tm, CONV_ROWS):
        uc_ref[0, r:r + CONV_ROWS, :] = _conv_window(u_ext[r:r + CONV_ROWS + 2 * HALO], CONV_ROWS, cw_ref, cb_ref)


def _inproj(x, g, scale, shift, w_bf, wt_bf, rope_tab, bd, qg, kg, conv_w, conv_b, tm=1024):
    b, s, d = x.shape
    aw = ATTN_WIDTH
    tok = lambda i, j: (j, i, 0)
    per_b = lambda i, j: (j, 0, 0)
    const = lambda i, j: (0, 0)
    halo_per_tile = tm // HALO
    n_halo = s // HALO
    out_bf = jax.ShapeDtypeStruct((b, s, aw), BF16)
    out_t = jax.ShapeDtypeStruct((b, aw, s), BF16)
    out_f32 = jax.ShapeDtypeStruct((b, s, aw), F32)
    blk = pl.BlockSpec((1, tm, aw), tok)
    blk_t = pl.BlockSpec((1, aw, tm), lambda i, j: (j, 0, i))
    return pl.pallas_call(
        _inproj_kernel,
        grid=(s // tm, b),
        in_specs=[pl.BlockSpec((1, tm, d), tok),
                  pl.BlockSpec((1, HALO, d), lambda i, j: (j, jnp.maximum(i * halo_per_tile - 1, 0), 0)),
                  pl.BlockSpec((1, HALO, d), lambda i, j: (j, jnp.minimum((i + 1) * halo_per_tile, n_halo - 1), 0)),
                  pl.BlockSpec((1, d), const),
                  pl.BlockSpec((1, 1, d), per_b),
                  pl.BlockSpec((1, 1, d), per_b),
                  pl.BlockSpec(w_bf.shape, const),
                  pl.BlockSpec(wt_bf.shape, const),
                  pl.BlockSpec((2, tm // GRID_W, LANES), lambda i, j: (0, i, 0)),
                  pl.BlockSpec((2, GRID_W, LANES), lambda i, j: (0, 0, 0)),
                  pl.BlockSpec((MXU_DIM, MXU_DIM), const),
                  pl.BlockSpec((1, aw), const),
                  pl.BlockSpec((1, aw), const),
                  pl.BlockSpec((CONV_WIDTH, aw), const),
                  pl.BlockSpec((1, aw), const)],
        out_specs=[blk, blk, blk, blk_t, blk_t, blk, blk],
        out_shape=[out_bf, out_bf, out_bf, out_t, out_t, out_f32, out_bf],
        compiler_params=_params("arbitrary", "arbitrary"),
        name="inproj",
    )(x, x, x, g, scale, shift, w_bf, wt_bf, rope_tab, rope_tab, bd, qg, kg, conv_w, conv_b)


def _ctxproj_kernel(c_ref, g_ref, scale_ref, shift_ref, wk_ref, wu_ref, wvt_ref, bd_ref, kg_ref, cw_ref, cb_ref,
                    kc_ref, vct_ref, uc_ref):
    nb, l, d = c_ref.shape
    aw = ATTN_WIDTH
    hb = _adaln(c_ref[...].reshape(nb * l, d), g_ref, scale_ref, shift_ref)
    kc = _head_rms(_dot(hb, wk_ref[...]), bd_ref, kg_ref[...]).astype(BF16)
    u = _dot(hb, wu_ref[...])
    vct = _dot_nt(wvt_ref[...], hb).astype(BF16)
    pad = jnp.zeros((HALO, aw), F32)
    for bi in range(nb):
        kc_ref[bi] = kc[bi * l:(bi + 1) * l]
        uc_ref[bi] = _conv_window(jnp.concatenate([pad, u[bi * l:(bi + 1) * l], pad], axis=0), l, cw_ref, cb_ref)
        vct_ref[bi] = vct[:, bi * l:(bi + 1) * l]


def _ctxproj(ctx, g, scale_c, shift_c, w_ctx_bf, wt_ctx_bf, bd, kg, conv_w, conv_b, nb=4):
    b, l, d = ctx.shape
    aw = ATTN_WIDTH
    const = lambda i: (0, 0)
    blk = pl.BlockSpec((nb, l, aw), lambda i: (i, 0, 0))
    return pl.pallas_call(
        _ctxproj_kernel,
        grid=(b // nb,),
        in_specs=[pl.BlockSpec((nb, l, d), lambda i: (i, 0, 0)),
                  pl.BlockSpec((1, d), const),
                  pl.BlockSpec((1, 1, d), lambda i: (0, 0, 0)),
                  pl.BlockSpec((1, 1, d), lambda i: (0, 0, 0)),
                  pl.BlockSpec((d, aw), lambda i: (0, COL_K)),
                  pl.BlockSpec((d, aw), lambda i: (0, COL_U)),
                  pl.BlockSpec((aw, d), const),
                  pl.BlockSpec((MXU_DIM, MXU_DIM), const),
                  pl.BlockSpec((1, aw), const),
                  pl.BlockSpec((CONV_WIDTH, aw), const),
                  pl.BlockSpec((1, aw), const)],
        out_specs=[blk, pl.BlockSpec((nb, aw, l), lambda i: (i, 0, 0)), blk],
        out_shape=[jax.ShapeDtypeStruct((b, l, aw), BF16),
                   jax.ShapeDtypeStruct((b, aw, l), BF16),
                   jax.ShapeDtypeStruct((b, l, aw), F32)],
        compiler_params=_params("arbitrary"),
        name="ctxproj",
    )(ctx, g, scale_c, shift_c, w_ctx_bf, w_ctx_bf, wt_ctx_bf, bd, kg, conv_w, conv_b)


def _attn_kernel(qr_ref, qp_ref, kr_ref, vt_ref, kc_ref, vct_ref, zat_ref, rexp_ref, o_ref, bias_ref, s_buf0, s_buf1):
    @pl.when(pl.program_id(1) == 0)
    def _():
        for h in range(2):
            _fill_bias_tables(rexp_ref, h, bias_ref)

    rows = qr_ref.shape[1] // GRID_W
    n_blk = rows // Q_ROWS
    chunk_rows = KEY_CHUNK // GRID_W
    n_tiles = Q_BLK // LANES
    lane_head = lax.broadcasted_iota(jnp.int32, (1, LANES), 1) // HEAD_DIM
    s_bufs = (s_buf0, s_buf1)
    ones_rows = jnp.ones((ONES_ROWS, KEY_CHUNK), BF16)

    def in_window(typ, jj, tile):
        if typ == 0:
            return jj < NA_ROWS
        if typ == 2:
            return jj >= BAND_ROWS - NA_ROWS
        return any(0 <= jj - ri < NA_ROWS for ri in range(tile * Q_ROWS // n_tiles, (tile + 1) * Q_ROWS // n_tiles))

    def local_chunks(typ):
        return [c for c in range(BAND // KEY_CHUNK)
                if any(in_window(typ, c * chunk_rows + r, t) for r in range(chunk_rows) for t in range(n_tiles))]

    def sub_blocks(c):
        for r in range(chunk_rows):
            for t in range(n_tiles):
                yield c * chunk_rows + r, t, slice(r * GRID_W, (r + 1) * GRID_W), slice(t * LANES, (t + 1) * LANES)

    def item(rb):
        rb = jnp.asarray(rb, jnp.int32)
        q0 = pl.multiple_of(rb * Q_BLK, Q_BLK)
        band0 = pl.multiple_of(jnp.clip(rb * Q_ROWS - NA_ROWS // 2, 0, rows - BAND_ROWS) * GRID_W, Q_BLK)
        return q0, band0

    def col_max(m, blk, t):
        m = list(m)
        m[t] = jnp.maximum(m[t], jnp.max(blk, axis=0, keepdims=True))
        return tuple(m)

    def score_steps(rb, h, typ):
        q0, band0 = item(rb)
        s_buf = s_bufs[h]
        mine = lane_head == h
        qr = jnp.where(mine, qr_ref[0, pl.ds(q0, Q_BLK), :], jnp.zeros((), BF16))
        qp = jnp.where(mine, qp_ref[0, pl.ds(q0, Q_BLK), :], jnp.zeros((), BF16))

        def local(c, m):
            n_rows = 1 + max(r for r in range(chunk_rows)
                             if any(in_window(typ, c * chunk_rows + r, t) for t in range(n_tiles)))
            s = _dot_nt(kr_ref[0, pl.ds(band0 + c * KEY_CHUNK, n_rows * GRID_W), :], qr)
            for jj, t, r_sl, l_sl in sub_blocks(c):
                if in_window(typ, jj, t):
                    k_sl = slice(jj * GRID_W, (jj + 1) * GRID_W)
                    blk = s[r_sl, l_sl] + bias_ref[h, typ, k_sl, l_sl]
                    s_buf[k_sl, l_sl] = blk
                    m = col_max(m, blk, t)
            return m

        def context(m):
            s = _dot_nt(kc_ref[0], qp)
            s_buf[pl.ds(BAND, KEY_CHUNK), :] = s
            for t in range(n_tiles):
                m = col_max(m, s[:, t * LANES:(t + 1) * LANES], t)
            return m

        return [functools.partial(local, c) for c in local_chunks(typ)] + [context]

    def value_steps(rb, h, typ, m):
        q0, band0 = item(rb)
        s_buf = s_bufs[h]
        hd = pl.ds(h * HEAD_DIM, HEAD_DIM)

        def probs(k_sl, t):
            return jnp.exp2(s_buf[k_sl, t * LANES:(t + 1) * LANES] - m[t]).astype(BF16)

        def local(c, acc):
            zero_blk = jnp.zeros((GRID_W, LANES), BF16)
            row_blks = []
            for r in range(chunk_rows):
                jj = c * chunk_rows + r
                k_sl = slice(jj * GRID_W, (jj + 1) * GRID_W)
                row_blks.append(jnp.concatenate(
                    [probs(k_sl, t) if in_window(typ, jj, t) else zero_blk for t in range(n_tiles)], axis=1))
            p = jnp.concatenate(row_blks, axis=0)
            vt = jnp.concatenate([vt_ref[0, hd, pl.ds(band0 + c * KEY_CHUNK, KEY_CHUNK)], ones_rows], axis=0)
            return acc + _dot(vt, p)

        def context(acc):
            k_sl = slice(BAND, BAND + KEY_CHUNK)
            p = jnp.concatenate([probs(k_sl, t) for t in range(n_tiles)], axis=1)
            vt = jnp.concatenate([vct_ref[0, hd, :], ones_rows], axis=0)
            return acc + _dot(vt, p)

        return [functools.partial(local, c) for c in local_chunks(typ)] + [context]

    def write_out(rb, h, acc):
        q0, _ = item(rb)
        hd = pl.ds(h * HEAD_DIM, HEAD_DIM)
        gated = (acc[:HEAD_DIM] / acc[HEAD_DIM:HEAD_DIM + 1]) * zat_ref[0, hd, pl.ds(q0, Q_BLK)].astype(F32)
        o_ref[0, hd, pl.ds(q0, Q_BLK)] = gated.astype(o_ref.dtype)

    m_init = (jnp.full((1, LANES), -jnp.inf, F32),) * n_tiles
    acc_init = jnp.zeros((HEAD_DIM + ONES_ROWS, Q_BLK), F32)

    def overlapped(score_item, value_item, out_item, carry):
        m_prev, acc_prev = carry
        s_steps = score_steps(*score_item) if score_item else []
        v_steps = value_steps(*value_item, m_prev) if value_item else []
        m, acc = m_init, acc_init
        for i, (s_step, v_step) in enumerate(itertools.zip_longest(s_steps, v_steps)):
            if s_step:
                m = s_step(m)
            if i == 0 and out_item:
                write_out(*out_item, acc_prev)
            if v_step:
                acc = v_step(acc)
        return m, acc

    first, mid, last = 0, 1, 2
    carry = overlapped((0, 0, first), None, None, (None, None))
    carry = overlapped((0, 1, first), (0, 0, first), None, carry)
    carry = overlapped((1, 0, mid), (0, 1, first), (0, 0), carry)

    def body(rb, carry):
        carry = overlapped((rb, 1, mid), (rb, 0, mid), (rb - 1, 1), carry)
        return overlapped((rb + 1, 0, mid), (rb, 1, mid), (rb, 0), carry)

    carry = lax.fori_loop(1, n_blk - 2, body, carry, unroll=3)
    carry = overlapped((n_blk - 2, 1, mid), (n_blk - 2, 0, mid), (n_blk - 3, 1), carry)
    carry = overlapped((n_blk - 1, 0, last), (n_blk - 2, 1, mid), (n_blk - 2, 0), carry)
    carry = overlapped((n_blk - 1, 1, last), (n_blk - 1, 0, last), (n_blk - 2, 1), carry)
    carry = overlapped(None, (n_blk - 1, 1, last), (n_blk - 1, 0), carry)
    write_out(n_blk - 1, 1, carry[1])


def _attention(qr, qp, kr, vt, kc, vct, zat, rexp):
    b, s, aw = qr.shape
    l = kc.shape[1]
    n_pairs = aw // LANES
    lat = pl.BlockSpec((1, s, LANES), lambda p, i: (i, 0, p))
    lat_t = pl.BlockSpec((1, LANES, s), lambda p, i: (i, p, 0))
    return pl.pallas_call(
        _attn_kernel,
        grid=(n_pairs, b),
        in_specs=[lat, lat, lat, lat_t,
                  pl.BlockSpec((1, l, LANES), lambda p, i: (i, 0, p)),
                  pl.BlockSpec((1, LANES, l), lambda p, i: (i, p, 0)),
                  lat_t,
                  pl.BlockSpec((2, N_BIAS_COLS, TOE_W), lambda p, i: (p, 0, 0))],
        out_specs=lat_t,
        out_shape=jax.ShapeDtypeStruct((b, aw, s), BF16),
        scratch_shapes=[pltpu.VMEM((2, 3, BAND, Q_BLK), F32),
                        pltpu.VMEM((BAND + l, Q_BLK), F32), pltpu.VMEM((BAND + l, Q_BLK), F32)],
        compiler_params=_params("arbitrary", "arbitrary"),
        name="attention",
    )(qr, qp, kr, vt, kc, vct, zat, rexp)


def _lru_pitch(tc):
    pitch = tc + SUBLANES
    assert pitch % (2 * SUBLANES) == SUBLANES
    return pitch


def _lru_gate_consts(ba, bx, lam):
    return 0.5 * ba, 0.5 * bx, jax.nn.softplus(-lam) * (-0.5 * LRU_C * LOG2E)


def _lru_fill(bi, uc, wg, consts, a_s, b_s):
    half_ba, half_bx, half_log2a = consts
    tc = uc.shape[0]
    ucb = uc.astype(BF16)

    def gate_tanh(gi, half_bias):
        pre = jnp.concatenate([_dot(ucb[:, :MXU_DIM], wg(gi, 0)), _dot(ucb[:, MXU_DIM:], wg(gi, 1))], axis=-1)
        return jnp.tanh(pre + half_bias)

    tr = gate_tanh(0, half_ba)
    ti = gate_tanh(1, half_bx)
    a = jnp.exp2(tr * half_log2a + half_log2a)
    gap = 1.0 - a * a
    mult = gap * lax.rsqrt(jnp.maximum(gap, 1e-30))
    bb = mult * ((0.5 * ti + 0.5) * uc)
    r0 = pl.multiple_of(bi * _lru_pitch(tc), SUBLANES)
    for j in range(LRU_SLABS):
        a_s[j, pl.ds(r0, tc), :] = a[:, j * LANES:(j + 1) * LANES]
        b_s[j, pl.ds(r0, tc), :] = bb[:, j * LANES:(j + 1) * LANES]


def _lru_scan(reverse, tc, nb, a_s, b_s, h_s, carry_s):
    pitch = _lru_pitch(tc)

    n_groups = tc // SUBLANES

    def body(g, hs):
        base = pl.multiple_of(((n_groups - 1 - g) if reverse else g) * SUBLANES, SUBLANES)
        hs = list(hs)
        for k in (reversed(range(SUBLANES)) if reverse else range(SUBLANES)):
            rows = pl.ds(base + k, nb, stride=pitch)
            for j in range(LRU_SLABS):
                hs[j] = a_s[j, rows, :] * hs[j] + b_s[j, rows, :]
                h_s[j, rows, :] = hs[j]
        return tuple(hs)

    hs = tuple(carry_s[:, j * LANES:(j + 1) * LANES] for j in range(LRU_SLABS))
    hs = lax.fori_loop(0, n_groups, body, hs)
    for j in range(LRU_SLABS):
        carry_s[:, j * LANES:(j + 1) * LANES] = hs[j]


def _lru_rows(bi, tc, h_s):
    r0 = pl.multiple_of(bi * _lru_pitch(tc), SUBLANES)
    return jnp.concatenate([h_s[j, pl.ds(r0, tc), :] for j in range(LRU_SLABS)], axis=-1)


def _lru_fwd_kernel(u_ref, uc_ref, wg_ref, ba_ref, bx_ref, lam_ref, hf_ref, hb0_ref, a_s, b_s, h_s, carry_s):
    nb, tc = u_ref.shape[0], u_ref.shape[1]
    consts = [_lru_gate_consts(ba_ref[d], bx_ref[d], lam_ref[d]) for d in range(2)]

    def fill_all(src_ref, d, unroll):
        def step(bi, carry):
            _lru_fill(bi, src_ref[bi], lambda gi, hf: wg_ref[d, gi, hf], consts[d], a_s, b_s)
            return carry

        lax.fori_loop(0, nb, step, 0, unroll=unroll)

    @pl.when(pl.program_id(0) == 0)
    def _():
        carry_s[...] = jnp.zeros_like(carry_s)
        fill_all(uc_ref, 1, 1)
        _lru_scan(True, tc, nb, a_s, b_s, h_s, carry_s)
        hb0_ref[...] = carry_s[...]
        carry_s[...] = jnp.zeros_like(carry_s)
        fill_all(uc_ref, 0, 1)
        _lru_scan(False, tc, nb, a_s, b_s, h_s, carry_s)

    fill_all(u_ref, 0, 4)
    _lru_scan(False, tc, nb, a_s, b_s, h_s, carry_s)

    def emit(bi, carry):
        hf_ref[bi] = _lru_rows(bi, tc, h_s).astype(hf_ref.dtype)
        return carry

    lax.fori_loop(0, nb, emit, 0)


def _lru_fwd(u, u_c, wg, ba, bx, lam):
    b, s, lw = u.shape
    tc = LRU_CHUNK
    assert u_c.shape[1] == tc and s % tc == 0
    full = lambda *shape: pl.BlockSpec(shape, lambda i: (0,) * len(shape))
    cblk = pl.BlockSpec((b, tc, lw), lambda i: (0, i, 0))
    slab = pltpu.VMEM((LRU_SLABS, b * _lru_pitch(tc), LANES), F32)
    return pl.pallas_call(
        _lru_fwd_kernel,
        grid=(s // tc,),
        in_specs=[cblk, full(b, tc, lw), full(2, 2, 2, MXU_DIM, MXU_DIM), full(2, 1, lw), full(2, 1, lw), full(2, 1, lw)],
        out_specs=[cblk, full(b, lw)],
        out_shape=[jax.ShapeDtypeStruct((b, s, lw), BF16), jax.ShapeDtypeStruct((b, lw), F32)],
        scratch_shapes=[slab, slab, slab, pltpu.VMEM((b, lw), F32)],
        compiler_params=_params("arbitrary"),
        name="rglru_fwd",
    )(u, u_c, wg, ba, bx, lam)


def _lru_bwd_out_kernel(u_ref, hf_ref, zl_ref, agt_ref, x_ref, hb0_ref, gate_ref, wg_ref, ba_ref, bx_ref, lam_ref,
                        wo_ref, o_ref, a_s, b_s, h_s, carry_s):
    nb, tc = u_ref.shape[0], u_ref.shape[1]
    consts = _lru_gate_consts(ba_ref[...], bx_ref[...], lam_ref[...])

    @pl.when(pl.program_id(0) == 0)
    def _():
        carry_s[...] = hb0_ref[...]

    def fill(bi, carry):
        _lru_fill(bi, u_ref[bi], lambda gi, hf: wg_ref[gi, hf], consts, a_s, b_s)
        return carry

    lax.fori_loop(0, nb, fill, 0, unroll=4)
    _lru_scan(True, tc, nb, a_s, b_s, h_s, carry_s)

    def emit(gi, carry):
        group = [OUT_GROUP * gi + k for k in range(OUT_GROUP)]
        y = jnp.concatenate([((hf_ref[bi].astype(F32) + _lru_rows(bi, tc, h_s)) * zl_ref[bi].astype(F32)).astype(BF16)
                             for bi in group], axis=0)
        at = jnp.concatenate([agt_ref[bi] for bi in group], axis=1)
        mix = lax.dot_general(at, wo_ref[:ATTN_WIDTH, :], (((0,), (0,)), ((), ())), preferred_element_type=F32)
        mix = mix + _dot(y, wo_ref[ATTN_WIDTH:, :])
        for k, bi in enumerate(group):
            o_ref[bi] = x_ref[bi] + gate_ref[bi] * mix[k * tc:(k + 1) * tc]
        return carry

    lax.fori_loop(0, nb // OUT_GROUP, emit, 0)


def _lru_bwd_out(u, hf, zl, agt, x, hb0, gate, wg, ba, bx, lam, wo_bf):
    b, s, lw = u.shape
    d = x.shape[2]
    tc = LRU_BWD_CHUNK
    n_chunks = s // tc
    assert s % tc == 0 and b % OUT_GROUP == 0
    full = lambda *shape: pl.BlockSpec(shape, lambda i: (0,) * len(shape))
    rev = lambda i: n_chunks - 1 - i
    cblk = pl.BlockSpec((b, tc, lw), lambda i: (0, rev(i), 0))
    xblk = pl.BlockSpec((b, tc, d), lambda i: (0, rev(i), 0))
    slab = pltpu.VMEM((LRU_SLABS, b * _lru_pitch(tc), LANES), F32)
    return pl.pallas_call(
        _lru_bwd_out_kernel,
        grid=(n_chunks,),
        in_specs=[cblk, cblk, cblk,
                  pl.BlockSpec((b, ATTN_WIDTH, tc), lambda i: (0, 0, rev(i))),
                  xblk, full(b, lw), full(b, 1, d),
                  full(2, 2, MXU_DIM, MXU_DIM), full(1, lw), full(1, lw), full(1, lw),
                  full(ATTN_WIDTH + lw, d)],
        out_specs=xblk,
        out_shape=jax.ShapeDtypeStruct(x.shape, x.dtype),
        scratch_shapes=[slab, slab, slab, pltpu.VMEM((b, lw), F32)],
        compiler_params=_params("arbitrary"),
        name="rglru_bwd_outproj",
    )(u, hf, zl, agt, x, hb0, gate, wg, ba, bx, lam, wo_bf)


def _rope_table(s):
    half = HEAD_DIM // 4
    assert s // GRID_W <= GRID_W
    inv = ROPE_BASE ** (-jnp.arange(half, dtype=F32) / half)
    ang = jnp.arange(GRID_W, dtype=F32)[:, None] * inv[None, :]
    reps = LANES // (2 * half)
    cos = jnp.tile(jnp.cos(ang), (1, 2 * reps))
    sin = jnp.tile(jnp.concatenate([-jnp.sin(ang), jnp.sin(ang)], axis=-1), (1, reps))
    return jnp.stack([cos, sin])


def _block_diag_gates(w):
    n = MXU_DIM // LRU_BLOCK
    lead = w.shape[:-3]
    w = w.reshape(lead + (2, n, LRU_BLOCK, LRU_BLOCK))
    on_diag = jnp.eye(n, dtype=bool)[:, None, :, None]
    blocks = jnp.where(on_diag, w[..., :, :, None, :], jnp.zeros((), w.dtype))
    return blocks.reshape(lead + (2, MXU_DIM, MXU_DIM))


def kernel(x, c, ctx, c_ctx, norm_g, w_mod, b_mod, w_in, w_out, q_norm_g, k_norm_g, rpb, conv_w, conv_b,
           lru_wa, lru_ba, lru_wx, lru_bx, lru_lam):
    bsz, s, d = x.shape
    assert w_in.shape[0] == 1 and d == D_MODEL and s % (GRID_W * Q_ROWS) == 0
    aw, lw = ATTN_WIDTH, LRU_WIDTH

    pad_rows = 2 * SUBLANES - bsz - 1
    cc = jnp.concatenate([c, c_ctx[None, :], jnp.zeros((pad_rows, d), F32)], axis=0)
    mod = _modulation(cc, w_mod[0], b_mod[0][None, :])
    shift, scale, gate = [mod[:bsz, i * d:(i + 1) * d][:, None, :] for i in range(3)]
    shift_c, scale_c = [mod[bsz:bsz + 1, i * d:(i + 1) * d][:, None, :] for i in range(2)]

    w = w_in[0]
    w_bf = w.astype(BF16)
    wt_bf = _transpose_cols(w_bf, COL_V * aw, (COL_ZA + 1) * aw)
    w_ctx_bf, wt_ctx_bf = w_bf, wt_bf
    g = norm_g[0][None, :]
    rope_tab = _rope_table(s)
    blk = jnp.arange(MXU_DIM) // HEAD_DIM
    bd = jnp.where(blk[:, None] == blk[None, :], 1.0 / HEAD_DIM, 0.0).astype(BF16)
    qg = jnp.tile(q_norm_g[0] * (HEAD_DIM ** -0.5 * LOG2E), N_HEADS)[None, :]
    kg = jnp.tile(k_norm_g[0], N_HEADS)[None, :]

    cw, cb = conv_w[0], conv_b[0][None, :]
    qr, qp, kr, vt, zat, u, zl = _inproj(x, g, scale, shift, w_bf, wt_bf, rope_tab, bd, qg, kg, cw, cb)
    kc, vct, u_c = _ctxproj(ctx, g, scale_c, shift_c, w_ctx_bf, wt_ctx_bf, bd, kg, cw, cb)

    agt = _attention(qr, qp, kr, vt, kc, vct, zat, _bias_rows(rpb[0]))

    wg = (0.5 * jnp.stack([lru_wa[0], lru_wx[0]], axis=1)).astype(BF16)
    wg = _block_diag_gates(wg)
    ba, bx, lam = lru_ba[0][:, None, :], lru_bx[0][:, None, :], lru_lam[0][:, None, :]
    hf, hb0 = _lru_fwd(u, u_c, wg, ba, bx, lam)
    return _lru_bwd_out(u, hf, zl, agt, x, hb0, gate, wg[1], ba[1], bx[1], lam[1], w_out[0].astype(BF16))
```

```python
import functools
import itertools
import math

import jax
import jax.numpy as jnp
from jax import lax
from jax.experimental import pallas as pl
from jax.experimental.pallas import tpu as pltpu

F32 = jnp.float32
BF16 = jnp.bfloat16

D_MODEL = 1024
GRID_W = 64
HEAD_DIM = 64
ATTN_WIDTH = 512
LRU_WIDTH = 512
N_HEADS = ATTN_WIDTH // HEAD_DIM
LRU_BLOCK = 64
NA_ROWS = 8
NA_COLS = 16
CONV_WIDTH = 4
LRU_C = 8.0
ROPE_BASE = 10000.0
EPS = 1e-6
NEG_INF = -1e30
LOG2E = math.log2(math.e)

LANES = 128
SUBLANES = 8
MXU_DIM = 256
VMEM_LIMIT_BYTES = 58 * 1024 * 1024

Q_ROWS = 4
BAND_ROWS = Q_ROWS + NA_ROWS
Q_BLK = Q_ROWS * GRID_W
BAND = BAND_ROWS * GRID_W
KEY_CHUNK = 256
ONES_ROWS = 16
N_BIAS_ROWS = 2 * NA_ROWS - 1
N_BIAS_COLS = 2 * NA_COLS - 1
TOE_W = 1024

LRU_CHUNK = 256
LRU_BWD_CHUNK = 128
LRU_SLABS = LRU_WIDTH // LANES
OUT_GROUP = 8
CONV_ROWS = 128
COL_Q, COL_K, COL_V, COL_ZA, COL_U, COL_ZL = range(6)
HALO = 16


def _dot(a, b):
    return jnp.dot(a, b, preferred_element_type=F32)


def _dot_nt(a, b):
    return lax.dot_general(a, b, (((1,), (1,)), ((), ())), preferred_element_type=F32)


def _params(*semantics):
    return pltpu.CompilerParams(dimension_semantics=semantics, vmem_limit_bytes=VMEM_LIMIT_BYTES)


def _split_bf16(t):
    hi = t.astype(BF16)
    lo = (t - hi.astype(F32)).astype(BF16)
    return hi, lo


def _mod_kernel(cc_ref, w_ref, b_ref, o_ref):
    cc = cc_ref[...]
    s = cc * jax.nn.sigmoid(cc)
    s_hi, s_lo = _split_bf16(s)
    w_hi, w_lo = _split_bf16(w_ref[...])
    o_ref[...] = _dot(s_hi, w_hi) + _dot(s_lo, w_hi) + _dot(s_hi, w_lo) + b_ref[...]


def _modulation(cc, w_mod, b_mod):
    rows, d = cc.shape
    n = w_mod.shape[1]
    tn = 1024
    return pl.pallas_call(
        _mod_kernel,
        grid=(n // tn,),
        in_specs=[pl.BlockSpec((rows, d), lambda j: (0, 0)),
                  pl.BlockSpec((d, tn), lambda j: (0, j)),
                  pl.BlockSpec((1, tn), lambda j: (0, j))],
        out_specs=pl.BlockSpec((rows, tn), lambda j: (0, j)),
        out_shape=jax.ShapeDtypeStruct((rows, n), F32),
        compiler_params=_params("arbitrary"),
        name="modulation",
    )(cc, w_mod, b_mod)


def _fill_bias_tables(rexp_ref, h, o_ref):
    k = lax.broadcasted_iota(jnp.int32, (GRID_W, TOE_W), 0)
    q = lax.broadcasted_iota(jnp.int32, (GRID_W, TOE_W), 1) % GRID_W
    diff = k - q + (NA_COLS - 1)
    toe = jnp.zeros((GRID_W, TOE_W), F32)
    for d in range(N_BIAS_COLS):
        toe = jnp.where(diff == d, rexp_ref[h, d:d + 1, :], toe)
    col_start = jnp.clip(q - NA_COLS // 2, 0, GRID_W - NA_COLS)
    col_ok = (k >= col_start) & (k < col_start + NA_COLS)
    toe = jnp.where(col_ok, toe, NEG_INF)
    ri = lax.broadcasted_iota(jnp.int32, (GRID_W, Q_BLK), 1) // GRID_W
    masked = jnp.full((GRID_W, Q_BLK), NEG_INF, F32)

    for typ in range(3):
        off = (NA_ROWS - 1, NA_ROWS // 2 - 1, NA_ROWS - BAND_ROWS + Q_ROWS - 1)[typ]
        for jj in range(BAND_ROWS):
            if typ == 0:
                lo, hi = (0, Q_ROWS) if jj < NA_ROWS else (0, 0)
            elif typ == 1:
                lo, hi = max(jj - NA_ROWS + 1, 0), min(jj, Q_ROWS - 1) + 1
            else:
                lo, hi = (0, Q_ROWS) if jj >= BAND_ROWS - NA_ROWS else (0, 0)
            strip = masked
            if lo < hi:
                e0 = N_BIAS_ROWS - 1 - jj - off
                rolled = pltpu.roll(toe, (-e0 * GRID_W) % TOE_W, axis=1)[:, :Q_BLK]
                strip = jnp.where((ri >= lo) & (ri < hi), rolled, NEG_INF)
            o_ref[h, typ, jj * GRID_W:(jj + 1) * GRID_W, :] = strip


def _bias_rows(rpb):
    rexp = jnp.repeat(jnp.transpose(rpb[:, ::-1, :], (0, 2, 1)) * LOG2E, GRID_W, axis=2)
    return jnp.pad(rexp, ((0, 0), (0, 0), (0, TOE_W - N_BIAS_ROWS * GRID_W)))


def _transpose_kernel(w_ref, o_ref):
    o_ref[...] = w_ref[...].T


def _transpose_cols(w, lo, hi, tn=MXU_DIM):
    rows = w.shape[0]
    assert lo % tn == 0 and hi % tn == 0
    return pl.pallas_call(
        _transpose_kernel,
        grid=((hi - lo) // tn,),
        in_specs=[pl.BlockSpec((rows, tn), lambda i: (0, lo // tn + i))],
        out_specs=pl.BlockSpec((tn, rows), lambda i: (i, 0)),
        out_shape=jax.ShapeDtypeStruct((hi - lo, rows), w.dtype),
        compiler_params=_params("arbitrary"),
        name="weight_transpose",
    )(w)


def _adaln(x, g_ref, scale_ref, shift_ref):
    ms = jnp.mean(x * x, axis=-1, keepdims=True)
    gm = g_ref[...] * (1.0 + scale_ref[0])
    return ((x * lax.rsqrt(ms + EPS)) * gm + shift_ref[0]).astype(BF16)


def _head_rms(t, bd_ref, gain):
    sq = (t * t).astype(BF16)
    bd = bd_ref[...]
    m = jnp.concatenate([_dot(sq[:, :MXU_DIM], bd), _dot(sq[:, MXU_DIM:], bd)], axis=-1)
    return t * lax.rsqrt(m + EPS) * gain


def _rope_tile(rows_ref, cols_ref, f):
    lane = lax.broadcasted_iota(jnp.int32, (1, LANES), 1)
    by_row = (lane % HEAD_DIM) < HEAD_DIM // 2
    by_col = cols_ref[f]
    return jnp.concatenate([jnp.where(by_row, rows_ref[f, k:k + 1, :], by_col) for k in range(rows_ref.shape[1])],
                           axis=0)


def _rope(t, cos, sin):
    lane = lax.broadcasted_iota(jnp.int32, (1, LANES), 1)
    first = (lane % 32) < 16
    parts = []
    for j in range(t.shape[1] // LANES):
        c = t[:, j * LANES:(j + 1) * LANES]
        swapped = jnp.where(first, pltpu.roll(c, LANES - 16, axis=1), pltpu.roll(c, 16, axis=1))
        parts.append(c * cos + swapped * sin)
    return jnp.concatenate(parts, axis=-1)


def _silu(t):
    return t * jax.nn.sigmoid(t)


def _conv_window(ext, n, cw_ref, cb_ref):
    rows = ext.shape[0]

    def shifted(k):
        return pltpu.roll(ext, (rows - k) % rows, axis=0)[HALO:HALO + n]

    y = cb_ref[...] + cw_ref[0:1, :] * shifted(-1)
    y = y + cw_ref[1:2, :] * ext[HALO:HALO + n]
    y = y + cw_ref[2:3, :] * shifted(1)
    return y + cw_ref[3:4, :] * shifted(2)


def _inproj_kernel(x_ref, xp_ref, xn_ref, g_ref, scale_ref, shift_ref, w_ref, wt_ref, rope_rows_ref, rope_cols_ref,
                   bd_ref, qg_ref, kg_ref, cw_ref, cb_ref, qr_ref, qp_ref, kr_ref, vt_ref, zat_ref, uc_ref, zl_ref):
    tm = x_ref.shape[1]
    aw = ATTN_WIDTH
    i, n_tiles = pl.program_id(0), pl.num_programs(0)
    hb_ext = jnp.concatenate([_adaln(xp_ref[0], g_ref, scale_ref, shift_ref),
                              _adaln(x_ref[0], g_ref, scale_ref, shift_ref),
                              _adaln(xn_ref[0], g_ref, scale_ref, shift_ref)], axis=0)
    hb = hb_ext[HALO:HALO + tm]

    def cols(j):
        return w_ref[:, j * aw:(j + 1) * aw]

    u_ext = _dot(hb_ext, cols(COL_U))
    row = lax.broadcasted_iota(jnp.int32, (tm + 2 * HALO, 1), 0)
    inside = ((row >= HALO) | (i > 0)) & ((row < HALO + tm) | (i < n_tiles - 1))
    u_ext = jnp.where(inside, u_ext, 0.0)

    cos = _rope_tile(rope_rows_ref, rope_cols_ref, 0)
    sin = _rope_tile(rope_rows_ref, rope_cols_ref, 1)
    qn = _head_rms(_dot(hb, cols(COL_Q)), bd_ref, qg_ref[...])
    qp_ref[0] = qn.astype(BF16)
    qr_ref[0] = _rope(qn, cos, sin).astype(BF16)
    kn = _head_rms(_dot(hb, cols(COL_K)), bd_ref, kg_ref[...])
    kr_ref[0] = _rope(kn, cos, sin).astype(BF16)

    zl_ref[0] = _silu(_dot(hb, cols(COL_ZL))).astype(BF16)
    vt_ref[0] = _dot_nt(wt_ref[:aw, :], hb).astype(BF16)
    zat_ref[0] = _silu(_dot_nt(wt_ref[aw:, :], hb)).astype(BF16)
    for r in range(0, tm, CONV_ROWS):
        uc_ref[0, r:r + CONV_ROWS, :] = _conv_window(u_ext[r:r + CONV_ROWS + 2 * HALO], CONV_ROWS, cw_ref, cb_ref)


def _inproj(x, g, scale, shift, w_bf, wt_bf, rope_tab, bd, qg, kg, conv_w, conv_b, tm=1024):
    b, s, d = x.shape
    aw = ATTN_WIDTH
    tok = lambda i, j: (j, i, 0)
    per_b = lambda i, j: (j, 0, 0)
    const = lambda i, j: (0, 0)
    halo_per_tile = tm // HALO
    n_halo = s // HALO
    out_bf = jax.ShapeDtypeStruct((b, s, aw), BF16)
    out_t = jax.ShapeDtypeStruct((b, aw, s), BF16)
    out_f32 = jax.ShapeDtypeStruct((b, s, aw), F32)
    blk = pl.BlockSpec((1, tm, aw), tok)
    blk_t = pl.BlockSpec((1, aw, tm), lambda i, j: (j, 0, i))
    return pl.pallas_call(
        _inproj_kernel,
        grid=(s // tm, b),
        in_specs=[pl.BlockSpec((1, tm, d), tok),
                  pl.BlockSpec((1, HALO, d), lambda i, j: (j, jnp.maximum(i * halo_per_tile - 1, 0), 0)),
                  pl.BlockSpec((1, HALO, d), lambda i, j: (j, jnp.minimum((i + 1) * halo_per_tile, n_halo - 1), 0)),
                  pl.BlockSpec((1, d), const),
                  pl.BlockSpec((1, 1, d), per_b),
                  pl.BlockSpec((1, 1, d), per_b),
                  pl.BlockSpec(w_bf.shape, const),
                  pl.BlockSpec(wt_bf.shape, const),
                  pl.BlockSpec((2, tm // GRID_W, LANES), lambda i, j: (0, i, 0)),
                  pl.BlockSpec((2, GRID_W, LANES), lambda i, j: (0, 0, 0)),
                  pl.BlockSpec((MXU_DIM, MXU_DIM), const),
                  pl.BlockSpec((1, aw), const),
                  pl.BlockSpec((1, aw), const),
                  pl.BlockSpec((CONV_WIDTH, aw), const),
                  pl.BlockSpec((1, aw), const)],
        out_specs=[blk, blk, blk, blk_t, blk_t, blk, blk],
        out_shape=[out_bf, out_bf, out_bf, out_t, out_t, out_f32, out_bf],
        compiler_params=_params("arbitrary", "arbitrary"),
        name="inproj",
    )(x, x, x, g, scale, shift, w_bf, wt_bf, rope_tab, rope_tab, bd, qg, kg, conv_w, conv_b)


def _ctxproj_kernel(c_ref, g_ref, scale_ref, shift_ref, wk_ref, wu_ref, wvt_ref, bd_ref, kg_ref, cw_ref, cb_ref,
                    kc_ref, vct_ref, uc_ref):
    nb, l, d = c_ref.shape
    aw = ATTN_WIDTH
    hb = _adaln(c_ref[...].reshape(nb * l, d), g_ref, scale_ref, shift_ref)
    kc = _head_rms(_dot(hb, wk_ref[...]), bd_ref, kg_ref[...]).astype(BF16)
    u = _dot(hb, wu_ref[...])
    vct = _dot_nt(wvt_ref[...], hb).astype(BF16)
    pad = jnp.zeros((HALO, aw), F32)
    for bi in range(nb):
        kc_ref[bi] = kc[bi * l:(bi + 1) * l]
        uc_ref[bi] = _conv_window(jnp.concatenate([pad, u[bi * l:(bi + 1) * l], pad], axis=0), l, cw_ref, cb_ref)
        vct_ref[bi] = vct[:, bi * l:(bi + 1) * l]


def _ctxproj(ctx, g, scale_c, shift_c, w_ctx_bf, wt_ctx_bf, bd, kg, conv_w, conv_b, nb=4):
    b, l, d = ctx.shape
    aw = ATTN_WIDTH
    const = lambda i: (0, 0)
    blk = pl.BlockSpec((nb, l, aw), lambda i: (i, 0, 0))
    return pl.pallas_call(
        _ctxproj_kernel,
        grid=(b // nb,),
        in_specs=[pl.BlockSpec((nb, l, d), lambda i: (i, 0, 0)),
                  pl.BlockSpec((1, d), const),
                  pl.BlockSpec((1, 1, d), lambda i: (0, 0, 0)),
                  pl.BlockSpec((1, 1, d), lambda i: (0, 0, 0)),
                  pl.BlockSpec((d, aw), lambda i: (0, COL_K)),
                  pl.BlockSpec((d, aw), lambda i: (0, COL_U)),
                  pl.BlockSpec((aw, d), const),
                  pl.BlockSpec((MXU_DIM, MXU_DIM), const),
                  pl.BlockSpec((1, aw), const),
                  pl.BlockSpec((CONV_WIDTH, aw), const),
                  pl.BlockSpec((1, aw), const)],
        out_specs=[blk, pl.BlockSpec((nb, aw, l), lambda i: (i, 0, 0)), blk],
        out_shape=[jax.ShapeDtypeStruct((b, l, aw), BF16),
                   jax.ShapeDtypeStruct((b, aw, l), BF16),
                   jax.ShapeDtypeStruct((b, l, aw), F32)],
        compiler_params=_params("arbitrary"),
        name="ctxproj",
    )(ctx, g, scale_c, shift_c, w_ctx_bf, w_ctx_bf, wt_ctx_bf, bd, kg, conv_w, conv_b)


def _attn_kernel(qr_ref, qp_ref, kr_ref, vt_ref, kc_ref, vct_ref, zat_ref, rexp_ref, o_ref, bias_ref, s_buf0, s_buf1):
    @pl.when(pl.program_id(1) == 0)
    def _():
        for h in range(2):
            _fill_bias_tables(rexp_ref, h, bias_ref)

    rows = qr_ref.shape[1] // GRID_W
    n_blk = rows // Q_ROWS
    chunk_rows = KEY_CHUNK // GRID_W
    n_tiles = Q_BLK // LANES
    lane_head = lax.broadcasted_iota(jnp.int32, (1, LANES), 1) // HEAD_DIM
    s_bufs = (s_buf0, s_buf1)
    ones_rows = jnp.ones((ONES_ROWS, KEY_CHUNK), BF16)

    def in_window(typ, jj, tile):
        if typ == 0:
            return jj < NA_ROWS
        if typ == 2:
            return jj >= BAND_ROWS - NA_ROWS
        return any(0 <= jj - ri < NA_ROWS for ri in range(tile * Q_ROWS // n_tiles, (tile + 1) * Q_ROWS // n_tiles))

    def local_chunks(typ):
        return [c for c in range(BAND // KEY_CHUNK)
                if any(in_window(typ, c * chunk_rows + r, t) for r in range(chunk_rows) for t in range(n_tiles))]

    def sub_blocks(c):
        for r in range(chunk_rows):
            for t in range(n_tiles):
                yield c * chunk_rows + r, t, slice(r * GRID_W, (r + 1) * GRID_W), slice(t * LANES, (t + 1) * LANES)

    def item(rb):
        rb = jnp.asarray(rb, jnp.int32)
        q0 = pl.multiple_of(rb * Q_BLK, Q_BLK)
        band0 = pl.multiple_of(jnp.clip(rb * Q_ROWS - NA_ROWS // 2, 0, rows - BAND_ROWS) * GRID_W, Q_BLK)
        return q0, band0

    def col_max(m, blk, t):
        m = list(m)
        m[t] = jnp.maximum(m[t], jnp.max(blk.reshape(-1, SUBLANES, LANES), axis=0))
        return tuple(m)

    def score_steps(rb, h, typ):
        q0, band0 = item(rb)
        s_buf = s_bufs[h]
        mine = lane_head == h
        qr = jnp.where(mine, qr_ref[0, pl.ds(q0, Q_BLK), :], jnp.zeros((), BF16))
        qp = jnp.where(mine, qp_ref[0, pl.ds(q0, Q_BLK), :], jnp.zeros((), BF16))

        def local(c, m):
            n_rows = 1 + max(r for r in range(chunk_rows)
                             if any(in_window(typ, c * chunk_rows + r, t) for t in range(n_tiles)))
            s = _dot_nt(kr_ref[0, pl.ds(band0 + c * KEY_CHUNK, n_rows * GRID_W), :], qr)
            for jj, t, r_sl, l_sl in sub_blocks(c):
                if in_window(typ, jj, t):
                    k_sl = slice(jj * GRID_W, (jj + 1) * GRID_W)
                    blk = s[r_sl, l_sl] + bias_ref[h, typ, k_sl, l_sl]
                    s_buf[k_sl, l_sl] = blk
                    m = col_max(m, blk, t)
            return m

        def context(m):
            s = _dot_nt(kc_ref[0], qp)
            s_buf[pl.ds(BAND, KEY_CHUNK), :] = s
            for t in range(n_tiles):
                m = col_max(m, s[:, t * LANES:(t + 1) * LANES], t)
            return m

        return [functools.partial(local, c) for c in local_chunks(typ)] + [context]

    def value_steps(rb, h, typ, m):
        q0, band0 = item(rb)
        s_buf = s_bufs[h]
        hd = pl.ds(h * HEAD_DIM, HEAD_DIM)
        m = tuple(jnp.max(mt, axis=0, keepdims=True) for mt in m)

        def probs(k_sl, t):
            return jnp.exp2(s_buf[k_sl, t * LANES:(t + 1) * LANES] - m[t]).astype(BF16)

        def local(c, acc):
            zero_blk = jnp.zeros((GRID_W, LANES), BF16)
            row_blks = []
            for r in range(chunk_rows):
                jj = c * chunk_rows + r
                k_sl = slice(jj * GRID_W, (jj + 1) * GRID_W)
                row_blks.append(jnp.concatenate(
                    [probs(k_sl, t) if in_window(typ, jj, t) else zero_blk for t in range(n_tiles)], axis=1))
            p = jnp.concatenate(row_blks, axis=0)
            vt = jnp.concatenate([vt_ref[0, hd, pl.ds(band0 + c * KEY_CHUNK, KEY_CHUNK)], ones_rows], axis=0)
            return acc + _dot(vt, p)

        def context(acc):
            k_sl = slice(BAND, BAND + KEY_CHUNK)
            p = jnp.concatenate([probs(k_sl, t) for t in range(n_tiles)], axis=1)
            vt = jnp.concatenate([vct_ref[0, hd, :], ones_rows], axis=0)
            return acc + _dot(vt, p)

        return [functools.partial(local, c) for c in local_chunks(typ)] + [context]

    def write_out(rb, h, acc):
        q0, _ = item(rb)
        hd = pl.ds(h * HEAD_DIM, HEAD_DIM)
        inv_den = 1.0 / acc[HEAD_DIM:HEAD_DIM + 1]
        gated = (acc[:HEAD_DIM] * inv_den) * zat_ref[0, hd, pl.ds(q0, Q_BLK)].astype(F32)
        o_ref[0, hd, pl.ds(q0, Q_BLK)] = gated.astype(o_ref.dtype)

    m_init = (jnp.full((SUBLANES, LANES), -jnp.inf, F32),) * n_tiles
    acc_init = jnp.zeros((HEAD_DIM + ONES_ROWS, Q_BLK), F32)

    def overlapped(score_item, value_item, out_item, carry):
        m_prev, acc_prev = carry
        s_steps = score_steps(*score_item) if score_item else []
        v_steps = value_steps(*value_item, m_prev) if value_item else []
        m, acc = m_init, acc_init
        for i, (s_step, v_step) in enumerate(itertools.zip_longest(s_steps, v_steps)):
            if s_step:
                m = s_step(m)
            if i == 0 and out_item:
                write_out(*out_item, acc_prev)
            if v_step:
                acc = v_step(acc)
        return m, acc

    first, mid, last = 0, 1, 2
    carry = overlapped((0, 0, first), None, None, (None, None))
    carry = overlapped((0, 1, first), (0, 0, first), None, carry)
    carry = overlapped((1, 0, mid), (0, 1, first), (0, 0), carry)

    def body(rb, carry):
        carry = overlapped((rb, 1, mid), (rb, 0, mid), (rb - 1, 1), carry)
        return overlapped((rb + 1, 0, mid), (rb, 1, mid), (rb, 0), carry)

    carry = lax.fori_loop(1, n_blk - 2, body, carry, unroll=3)
    carry = overlapped((n_blk - 2, 1, mid), (n_blk - 2, 0, mid), (n_blk - 3, 1), carry)
    carry = overlapped((n_blk - 1, 0, last), (n_blk - 2, 1, mid), (n_blk - 2, 0), carry)
    carry = overlapped((n_blk - 1, 1, last), (n_blk - 1, 0, last), (n_blk - 2, 1), carry)
    carry = overlapped(None, (n_blk - 1, 1, last), (n_blk - 1, 0), carry)
    write_out(n_blk - 1, 1, carry[1])


def _attention(qr, qp, kr, vt, kc, vct, zat, rexp):
    b, s, aw = qr.shape
    l = kc.shape[1]
    n_pairs = aw // LANES
    lat = pl.BlockSpec((1, s, LANES), lambda p, i: (i, 0, p))
    lat_t = pl.BlockSpec((1, LANES, s), lambda p, i: (i, p, 0))
    return pl.pallas_call(
        _attn_kernel,
        grid=(n_pairs, b),
        in_specs=[lat, lat, lat, lat_t,
                  pl.BlockSpec((1, l, LANES), lambda p, i: (i, 0, p)),
                  pl.BlockSpec((1, LANES, l), lambda p, i: (i, p, 0)),
                  lat_t,
                  pl.BlockSpec((2, N_BIAS_COLS, TOE_W), lambda p, i: (p, 0, 0))],
        out_specs=lat_t,
        out_shape=jax.ShapeDtypeStruct((b, aw, s), BF16),
        scratch_shapes=[pltpu.VMEM((2, 3, BAND, Q_BLK), F32),
                        pltpu.VMEM((BAND + l, Q_BLK), F32), pltpu.VMEM((BAND + l, Q_BLK), F32)],
        compiler_params=_params("arbitrary", "arbitrary"),
        name="attention",
    )(qr, qp, kr, vt, kc, vct, zat, rexp)


def _lru_pitch(tc):
    pitch = tc + SUBLANES
    assert pitch % (2 * SUBLANES) == SUBLANES
    return pitch


def _lru_gate_consts(ba, bx, lam):
    return 0.5 * ba, 0.5 * bx, jax.nn.softplus(-lam) * (-0.5 * LRU_C * LOG2E)


def _lru_fill(bi, uc, wg, consts, a_s, b_s):
    half_ba, half_bx, half_log2a = consts
    tc = uc.shape[0]
    ucb = uc.astype(BF16)

    def gate_tanh(gi, half_bias):
        pre = jnp.concatenate([_dot(ucb[:, :MXU_DIM], wg(gi, 0)), _dot(ucb[:, MXU_DIM:], wg(gi, 1))], axis=-1)
        return jnp.tanh(pre + half_bias)

    tr = gate_tanh(0, half_ba)
    ti = gate_tanh(1, half_bx)
    a = jnp.exp2(tr * half_log2a + half_log2a)
    gap = 1.0 - a * a
    mult = gap * lax.rsqrt(jnp.maximum(gap, 1e-30))
    bb = mult * ((0.5 * ti + 0.5) * uc)
    r0 = pl.multiple_of(bi * _lru_pitch(tc), SUBLANES)
    for j in range(LRU_SLABS):
        a_s[j, pl.ds(r0, tc), :] = a[:, j * LANES:(j + 1) * LANES]
        b_s[j, pl.ds(r0, tc), :] = bb[:, j * LANES:(j + 1) * LANES]


def _lru_scan(reverse, tc, nb, a_s, b_s, h_s, carry_s):
    pitch = _lru_pitch(tc)

    n_groups = tc // SUBLANES

    def body(g, hs):
        base = pl.multiple_of(((n_groups - 1 - g) if reverse else g) * SUBLANES, SUBLANES)
        hs = list(hs)
        for k in (reversed(range(SUBLANES)) if reverse else range(SUBLANES)):
            rows = pl.ds(base + k, nb, stride=pitch)
            for j in range(LRU_SLABS):
                hs[j] = a_s[j, rows, :] * hs[j] + b_s[j, rows, :]
                h_s[j, rows, :] = hs[j]
        return tuple(hs)

    hs = tuple(carry_s[:, j * LANES:(j + 1) * LANES] for j in range(LRU_SLABS))
    hs = lax.fori_loop(0, n_groups, body, hs)
    for j in range(LRU_SLABS):
        carry_s[:, j * LANES:(j + 1) * LANES] = hs[j]


def _lru_rows(bi, tc, h_s):
    r0 = pl.multiple_of(bi * _lru_pitch(tc), SUBLANES)
    return jnp.concatenate([h_s[j, pl.ds(r0, tc), :] for j in range(LRU_SLABS)], axis=-1)


def _lru_fwd_kernel(u_ref, uc_ref, wg_ref, ba_ref, bx_ref, lam_ref, hf_ref, hb0_ref, a_s, b_s, h_s, carry_s):
    nb, tc = u_ref.shape[0], u_ref.shape[1]
    consts = [_lru_gate_consts(ba_ref[d], bx_ref[d], lam_ref[d]) for d in range(2)]

    def fill_all(src_ref, d, unroll):
        def step(bi, carry):
            _lru_fill(bi, src_ref[bi], lambda gi, hf: wg_ref[d, gi, hf], consts[d], a_s, b_s)
            return carry

        lax.fori_loop(0, nb, step, 0, unroll=unroll)

    @pl.when(pl.program_id(0) == 0)
    def _():
        carry_s[...] = jnp.zeros_like(carry_s)
        fill_all(uc_ref, 1, 1)
        _lru_scan(True, tc, nb, a_s, b_s, h_s, carry_s)
        hb0_ref[...] = carry_s[...]
        carry_s[...] = jnp.zeros_like(carry_s)
        fill_all(uc_ref, 0, 1)
        _lru_scan(False, tc, nb, a_s, b_s, h_s, carry_s)

    fill_all(u_ref, 0, 4)
    _lru_scan(False, tc, nb, a_s, b_s, h_s, carry_s)

    def emit(bi, carry):
        hf_ref[bi] = _lru_rows(bi, tc, h_s).astype(hf_ref.dtype)
        return carry

    lax.fori_loop(0, nb, emit, 0)


def _lru_fwd(u, u_c, wg, ba, bx, lam):
    b, s, lw = u.shape
    tc = LRU_CHUNK
    assert u_c.shape[1] == tc and s % tc == 0
    full = lambda *shape: pl.BlockSpec(shape, lambda i: (0,) * len(shape))
    cblk = pl.BlockSpec((b, tc, lw), lambda i: (0, i, 0))
    slab = pltpu.VMEM((LRU_SLABS, b * _lru_pitch(tc), LANES), F32)
    return pl.pallas_call(
        _lru_fwd_kernel,
        grid=(s // tc,),
        in_specs=[cblk, full(b, tc, lw), full(2, 2, 2, MXU_DIM, MXU_DIM), full(2, 1, lw), full(2, 1, lw), full(2, 1, lw)],
        out_specs=[cblk, full(b, lw)],
        out_shape=[jax.ShapeDtypeStruct((b, s, lw), BF16), jax.ShapeDtypeStruct((b, lw), F32)],
        scratch_shapes=[slab, slab, slab, pltpu.VMEM((b, lw), F32)],
        compiler_params=_params("arbitrary"),
        name="rglru_fwd",
    )(u, u_c, wg, ba, bx, lam)


def _lru_bwd_out_kernel(u_ref, hf_ref, zl_ref, agt_ref, x_ref, hb0_ref, gate_ref, wg_ref, ba_ref, bx_ref, lam_ref,
                        wo_ref, o_ref, a_s, b_s, h_s, carry_s):
    nb, tc = u_ref.shape[0], u_ref.shape[1]
    consts = _lru_gate_consts(ba_ref[...], bx_ref[...], lam_ref[...])

    @pl.when(pl.program_id(0) == 0)
    def _():
        carry_s[...] = hb0_ref[...]

    def fill(bi, carry):
        _lru_fill(bi, u_ref[bi], lambda gi, hf: wg_ref[gi, hf], consts, a_s, b_s)
        return carry

    lax.fori_loop(0, nb, fill, 0, unroll=4)
    _lru_scan(True, tc, nb, a_s, b_s, h_s, carry_s)

    def emit(gi, carry):
        group = [OUT_GROUP * gi + k for k in range(OUT_GROUP)]
        y = jnp.concatenate([((hf_ref[bi].astype(F32) + _lru_rows(bi, tc, h_s)) * zl_ref[bi].astype(F32)).astype(BF16)
                             for bi in group], axis=0)
        at = jnp.concatenate([agt_ref[bi] for bi in group], axis=1)
        mix = lax.dot_general(at, wo_ref[:ATTN_WIDTH, :], (((0,), (0,)), ((), ())), preferred_element_type=F32)
        mix = mix + _dot(y, wo_ref[ATTN_WIDTH:, :])
        for k, bi in enumerate(group):
            o_ref[bi] = x_ref[bi] + gate_ref[bi] * mix[k * tc:(k + 1) * tc]
        return carry

    lax.fori_loop(0, nb // OUT_GROUP, emit, 0)


def _lru_bwd_out(u, hf, zl, agt, x, hb0, gate, wg, ba, bx, lam, wo_bf):
    b, s, lw = u.shape
    d = x.shape[2]
    tc = LRU_BWD_CHUNK
    n_chunks = s // tc
    assert s % tc == 0 and b % OUT_GROUP == 0
    full = lambda *shape: pl.BlockSpec(shape, lambda i: (0,) * len(shape))
    rev = lambda i: n_chunks - 1 - i
    cblk = pl.BlockSpec((b, tc, lw), lambda i: (0, rev(i), 0))
    xblk = pl.BlockSpec((b, tc, d), lambda i: (0, rev(i), 0))
    slab = pltpu.VMEM((LRU_SLABS, b * _lru_pitch(tc), LANES), F32)
    return pl.pallas_call(
        _lru_bwd_out_kernel,
        grid=(n_chunks,),
        in_specs=[cblk, cblk, cblk,
                  pl.BlockSpec((b, ATTN_WIDTH, tc), lambda i: (0, 0, rev(i))),
                  xblk, full(b, lw), full(b, 1, d),
                  full(2, 2, MXU_DIM, MXU_DIM), full(1, lw), full(1, lw), full(1, lw),
                  full(ATTN_WIDTH + lw, d)],
        out_specs=xblk,
        out_shape=jax.ShapeDtypeStruct(x.shape, x.dtype),
        scratch_shapes=[slab, slab, slab, pltpu.VMEM((b, lw), F32)],
        compiler_params=_params("arbitrary"),
        name="rglru_bwd_outproj",
    )(u, hf, zl, agt, x, hb0, gate, wg, ba, bx, lam, wo_bf)


def _rope_table(s):
    half = HEAD_DIM // 4
    assert s // GRID_W <= GRID_W
    inv = ROPE_BASE ** (-jnp.arange(half, dtype=F32) / half)
    ang = jnp.arange(GRID_W, dtype=F32)[:, None] * inv[None, :]
    reps = LANES // (2 * half)
    cos = jnp.tile(jnp.cos(ang), (1, 2 * reps))
    sin = jnp.tile(jnp.concatenate([-jnp.sin(ang), jnp.sin(ang)], axis=-1), (1, reps))
    return jnp.stack([cos, sin])


def _block_diag_gates(w):
    n = MXU_DIM // LRU_BLOCK
    lead = w.shape[:-3]
    w = w.reshape(lead + (2, n, LRU_BLOCK, LRU_BLOCK))
    on_diag = jnp.eye(n, dtype=bool)[:, None, :, None]
    blocks = jnp.where(on_diag, w[..., :, :, None, :], jnp.zeros((), w.dtype))
    return blocks.reshape(lead + (2, MXU_DIM, MXU_DIM))


def kernel(x, c, ctx, c_ctx, norm_g, w_mod, b_mod, w_in, w_out, q_norm_g, k_norm_g, rpb, conv_w, conv_b,
           lru_wa, lru_ba, lru_wx, lru_bx, lru_lam):
    bsz, s, d = x.shape
    assert w_in.shape[0] == 1 and d == D_MODEL and s % (GRID_W * Q_ROWS) == 0
    aw, lw = ATTN_WIDTH, LRU_WIDTH

    pad_rows = 2 * SUBLANES - bsz - 1
    cc = jnp.concatenate([c, c_ctx[None, :], jnp.zeros((pad_rows, d), F32)], axis=0)
    mod = _modulation(cc, w_mod[0], b_mod[0][None, :])
    shift, scale, gate = [mod[:bsz, i * d:(i + 1) * d][:, None, :] for i in range(3)]
    shift_c, scale_c = [mod[bsz:bsz + 1, i * d:(i + 1) * d][:, None, :] for i in range(2)]

    w = w_in[0]
    w_bf = w.astype(BF16)
    wt_bf = _transpose_cols(w_bf, COL_V * aw, (COL_ZA + 1) * aw)
    w_ctx_bf, wt_ctx_bf = w_bf, wt_bf
    g = norm_g[0][None, :]
    rope_tab = _rope_table(s)
    blk = jnp.arange(MXU_DIM) // HEAD_DIM
    bd = jnp.where(blk[:, None] == blk[None, :], 1.0 / HEAD_DIM, 0.0).astype(BF16)
    qg = jnp.tile(q_norm_g[0] * (HEAD_DIM ** -0.5 * LOG2E), N_HEADS)[None, :]
    kg = jnp.tile(k_norm_g[0], N_HEADS)[None, :]

    cw, cb = conv_w[0], conv_b[0][None, :]
    qr, qp, kr, vt, zat, u, zl = _inproj(x, g, scale, shift, w_bf, wt_bf, rope_tab, bd, qg, kg, cw, cb)
    kc, vct, u_c = _ctxproj(ctx, g, scale_c, shift_c, w_ctx_bf, wt_ctx_bf, bd, kg, cw, cb)

    agt = _attention(qr, qp, kr, vt, kc, vct, zat, _bias_rows(rpb[0]))

    wg = (0.5 * jnp.stack([lru_wa[0], lru_wx[0]], axis=1)).astype(BF16)
    wg = _block_diag_gates(wg)
    ba, bx, lam = lru_ba[0][:, None, :], lru_bx[0][:, None, :], lru_lam[0][:, None, :]
    hf, hb0 = _lru_fwd(u, u_c, wg, ba, bx, lam)
    return _lru_bwd_out(u, hf, zl, agt, x, hb0, gate, wg[1], ba[1], bx[1], lam[1], w_out[0].astype(BF16))
```

```python
import functools
import itertools
import math

import jax
import jax.numpy as jnp
from jax import lax
from jax.experimental import pallas as pl
from jax.experimental.pallas import tpu as pltpu

F32 = jnp.float32
BF16 = jnp.bfloat16

D_MODEL = 1024
GRID_W = 64
HEAD_DIM = 64
ATTN_WIDTH = 512
LRU_WIDTH = 512
N_HEADS = ATTN_WIDTH // HEAD_DIM
LRU_BLOCK = 64
NA_ROWS = 8
NA_COLS = 16
CONV_WIDTH = 4
LRU_C = 8.0
ROPE_BASE = 10000.0
EPS = 1e-6
NEG_INF = -1e30
RSQRT_FLOOR = 1e-30
LOG2E = math.log2(math.e)

LANES = 128
SUBLANES = 8
MXU_DIM = 256
VMEM_LIMIT_BYTES = 58 * 1024 * 1024

Q_ROWS = 4
BAND_ROWS = Q_ROWS + NA_ROWS
Q_BLK = Q_ROWS * GRID_W
BAND = BAND_ROWS * GRID_W
KEY_CHUNK = 256
ONES_ROWS = 16
N_BIAS_ROWS = 2 * NA_ROWS - 1
N_BIAS_COLS = 2 * NA_COLS - 1
TOE_W = 1024

LRU_CHUNK = 256
LRU_BWD_CHUNK = 128
LRU_SLABS = LRU_WIDTH // LANES
OUT_GROUP = 8
CONV_ROWS = 128
COL_Q, COL_K, COL_V, COL_ZA, COL_U, COL_ZL = range(6)
HALO = 16


def _dot(a, b):
    return jnp.dot(a, b, preferred_element_type=F32)


def _dot_nt(a, b):
    return lax.dot_general(a, b, (((1,), (1,)), ((), ())), preferred_element_type=F32)


def _params(*semantics):
    return pltpu.CompilerParams(dimension_semantics=semantics, vmem_limit_bytes=VMEM_LIMIT_BYTES)


def _split_bf16(t):
    hi = t.astype(BF16)
    lo = (t - hi.astype(F32)).astype(BF16)
    return hi, lo


def _mod_kernel(cc_ref, w_ref, b_ref, o_ref):
    cc = cc_ref[...]
    s = cc * jax.nn.sigmoid(cc)
    s_hi, s_lo = _split_bf16(s)
    w_hi, w_lo = _split_bf16(w_ref[...])
    o_ref[...] = _dot(s_hi, w_hi) + _dot(s_lo, w_hi) + _dot(s_hi, w_lo) + b_ref[...]


def _modulation(cc, w_mod, b_mod):
    rows, d = cc.shape
    n = w_mod.shape[1]
    tn = 1024
    return pl.pallas_call(
        _mod_kernel,
        grid=(n // tn,),
        in_specs=[pl.BlockSpec((rows, d), lambda j: (0, 0)),
                  pl.BlockSpec((d, tn), lambda j: (0, j)),
                  pl.BlockSpec((1, tn), lambda j: (0, j))],
        out_specs=pl.BlockSpec((rows, tn), lambda j: (0, j)),
        out_shape=jax.ShapeDtypeStruct((rows, n), F32),
        compiler_params=_params("arbitrary"),
        name="modulation",
    )(cc, w_mod, b_mod)


def _fill_bias_tables(rexp_ref, h, o_ref):
    k = lax.broadcasted_iota(jnp.int32, (GRID_W, TOE_W), 0)
    q = lax.broadcasted_iota(jnp.int32, (GRID_W, TOE_W), 1) % GRID_W
    diff = k - q + (NA_COLS - 1)
    toe = jnp.zeros((GRID_W, TOE_W), F32)
    for d in range(N_BIAS_COLS):
        toe = jnp.where(diff == d, rexp_ref[h, d:d + 1, :], toe)
    col_start = jnp.clip(q - NA_COLS // 2, 0, GRID_W - NA_COLS)
    col_ok = (k >= col_start) & (k < col_start + NA_COLS)
    toe = jnp.where(col_ok, toe, NEG_INF)
    ri = lax.broadcasted_iota(jnp.int32, (GRID_W, Q_BLK), 1) // GRID_W
    masked = jnp.full((GRID_W, Q_BLK), NEG_INF, F32)

    for typ in range(3):
        off = (NA_ROWS - 1, NA_ROWS // 2 - 1, NA_ROWS - BAND_ROWS + Q_ROWS - 1)[typ]
        for jj in range(BAND_ROWS):
            if typ == 0:
                lo, hi = (0, Q_ROWS) if jj < NA_ROWS else (0, 0)
            elif typ == 1:
                lo, hi = max(jj - NA_ROWS + 1, 0), min(jj, Q_ROWS - 1) + 1
            else:
                lo, hi = (0, Q_ROWS) if jj >= BAND_ROWS - NA_ROWS else (0, 0)
            strip = masked
            if lo < hi:
                e0 = N_BIAS_ROWS - 1 - jj - off
                rolled = pltpu.roll(toe, (-e0 * GRID_W) % TOE_W, axis=1)[:, :Q_BLK]
                strip = jnp.where((ri >= lo) & (ri < hi), rolled, NEG_INF)
            o_ref[h, typ, jj * GRID_W:(jj + 1) * GRID_W, :] = strip


def _bias_rows(rpb):
    rexp = jnp.repeat(jnp.transpose(rpb[:, ::-1, :], (0, 2, 1)) * LOG2E, GRID_W, axis=2)
    return jnp.pad(rexp, ((0, 0), (0, 0), (0, TOE_W - N_BIAS_ROWS * GRID_W)))


def _transpose_kernel(w_ref, o_ref):
    o_ref[...] = w_ref[...].T


def _transpose_cols(w, lo, hi, tn=MXU_DIM):
    rows = w.shape[0]
    assert lo % tn == 0 and hi % tn == 0
    return pl.pallas_call(
        _transpose_kernel,
        grid=((hi - lo) // tn,),
        in_specs=[pl.BlockSpec((rows, tn), lambda i: (0, lo // tn + i))],
        out_specs=pl.BlockSpec((tn, rows), lambda i: (i, 0)),
        out_shape=jax.ShapeDtypeStruct((hi - lo, rows), w.dtype),
        compiler_params=_params("arbitrary"),
        name="weight_transpose",
    )(w)


def _adaln(x, g_ref, scale_ref, shift_ref):
    ms = jnp.mean(x * x, axis=-1, keepdims=True)
    gm = g_ref[...] * (1.0 + scale_ref[0])
    return ((x * lax.rsqrt(ms + EPS)) * gm + shift_ref[0]).astype(BF16)


def _head_rms(t, bd_ref, gain):
    sq = (t * t).astype(BF16)
    bd = bd_ref[...]
    m = jnp.concatenate([_dot(sq[:, :MXU_DIM], bd), _dot(sq[:, MXU_DIM:], bd)], axis=-1)
    return t * lax.rsqrt(m + EPS) * gain


def _rope_tile(rows_ref, cols_ref, f):
    lane = lax.broadcasted_iota(jnp.int32, (1, LANES), 1)
    by_row = (lane % HEAD_DIM) < HEAD_DIM // 2
    by_col = cols_ref[f]
    return jnp.concatenate([jnp.where(by_row, rows_ref[f, k:k + 1, :], by_col) for k in range(rows_ref.shape[1])],
                           axis=0)


def _rope(t, cos, sin):
    lane = lax.broadcasted_iota(jnp.int32, (1, LANES), 1)
    first = (lane % 32) < 16
    parts = []
    for j in range(t.shape[1] // LANES):
        c = t[:, j * LANES:(j + 1) * LANES]
        swapped = jnp.where(first, pltpu.roll(c, LANES - 16, axis=1), pltpu.roll(c, 16, axis=1))
        parts.append(c * cos + swapped * sin)
    return jnp.concatenate(parts, axis=-1)


def _silu(t):
    return t * jax.nn.sigmoid(t)


def _conv_window(ext, n, cw_ref, cb_ref):
    rows = ext.shape[0]

    def shifted(k):
        return pltpu.roll(ext, (rows - k) % rows, axis=0)[HALO:HALO + n]

    y = cb_ref[...] + cw_ref[0:1, :] * shifted(-1)
    y = y + cw_ref[1:2, :] * ext[HALO:HALO + n]
    y = y + cw_ref[2:3, :] * shifted(1)
    return y + cw_ref[3:4, :] * shifted(2)


def _inproj_kernel(x_ref, xp_ref, xn_ref, g_ref, scale_ref, shift_ref, w_ref, wt_ref, rope_rows_ref, rope_cols_ref,
                   bd_ref, qg_ref, kg_ref, cw_ref, cb_ref, qr_ref, qp_ref, kr_ref, vt_ref, zat_ref, uc_ref, zl_ref):
    tm = x_ref.shape[1]
    aw = ATTN_WIDTH
    i, n_tiles = pl.program_id(0), pl.num_programs(0)
    hb_ext = jnp.concatenate([_adaln(xp_ref[0], g_ref, scale_ref, shift_ref),
                              _adaln(x_ref[0], g_ref, scale_ref, shift_ref),
                              _adaln(xn_ref[0], g_ref, scale_ref, shift_ref)], axis=0)
    hb = hb_ext[HALO:HALO + tm]

    def cols(j):
        return w_ref[:, j * aw:(j + 1) * aw]

    u_ext = _dot(hb_ext, cols(COL_U))
    row = lax.broadcasted_iota(jnp.int32, (tm + 2 * HALO, 1), 0)
    inside = ((row >= HALO) | (i > 0)) & ((row < HALO + tm) | (i < n_tiles - 1))
    u_ext = jnp.where(inside, u_ext, 0.0)

    cos = _rope_tile(rope_rows_ref, rope_cols_ref, 0)
    sin = _rope_tile(rope_rows_ref, rope_cols_ref, 1)
    qn = _head_rms(_dot(hb, cols(COL_Q)), bd_ref, qg_ref[...])
    qp_ref[0] = qn.astype(BF16)
    qr_ref[0] = _rope(qn, cos, sin).astype(BF16)
    kn = _head_rms(_dot(hb, cols(COL_K)), bd_ref, kg_ref[...])
    kr_ref[0] = _rope(kn, cos, sin).astype(BF16)

    zl_ref[0] = _silu(_dot(hb, cols(COL_ZL))).astype(BF16)
    vt_ref[0] = _dot_nt(wt_ref[:aw, :], hb).astype(BF16)
    zat_ref[0] = _silu(_dot_nt(wt_ref[aw:, :], hb)).astype(BF16)
    for r in range(0, tm, CONV_ROWS):
        uc_ref[0, r:r + CONV_ROWS, :] = _conv_window(u_ext[r:r + CONV_ROWS + 2 * HALO], CONV_ROWS, cw_ref, cb_ref)


def _inproj(x, g, scale, shift, w_bf, wt_bf, rope_tab, bd, qg, kg, conv_w, conv_b, tm=1024):
    b, s, d = x.shape
    aw = ATTN_WIDTH
    tok = lambda i, j: (j, i, 0)
    per_b = lambda i, j: (j, 0, 0)
    const = lambda i, j: (0, 0)
    halo_per_tile = tm // HALO
    n_halo = s // HALO
    out_bf = jax.ShapeDtypeStruct((b, s, aw), BF16)
    out_t = jax.ShapeDtypeStruct((b, aw, s), BF16)
    out_f32 = jax.ShapeDtypeStruct((b, s, aw), F32)
    blk = pl.BlockSpec((1, tm, aw), tok)
    blk_t = pl.BlockSpec((1, aw, tm), lambda i, j: (j, 0, i))
    return pl.pallas_call(
        _inproj_kernel,
        grid=(s // tm, b),
        in_specs=[pl.BlockSpec((1, tm, d), tok),
                  pl.BlockSpec((1, HALO, d), lambda i, j: (j, jnp.maximum(i * halo_per_tile - 1, 0), 0)),
                  pl.BlockSpec((1, HALO, d), lambda i, j: (j, jnp.minimum((i + 1) * halo_per_tile, n_halo - 1), 0)),
                  pl.BlockSpec((1, d), const),
                  pl.BlockSpec((1, 1, d), per_b),
                  pl.BlockSpec((1, 1, d), per_b),
                  pl.BlockSpec(w_bf.shape, const),
                  pl.BlockSpec(wt_bf.shape, const),
                  pl.BlockSpec((2, tm // GRID_W, LANES), lambda i, j: (0, i, 0)),
                  pl.BlockSpec((2, GRID_W, LANES), lambda i, j: (0, 0, 0)),
                  pl.BlockSpec((MXU_DIM, MXU_DIM), const),
                  pl.BlockSpec((1, aw), const),
                  pl.BlockSpec((1, aw), const),
                  pl.BlockSpec((CONV_WIDTH, aw), const),
                  pl.BlockSpec((1, aw), const)],
        out_specs=[blk, blk, blk, blk_t, blk_t, blk, blk],
        out_shape=[out_bf, out_bf, out_bf, out_t, out_t, out_f32, out_bf],
        compiler_params=_params("arbitrary", "arbitrary"),
        name="inproj",
    )(x, x, x, g, scale, shift, w_bf, wt_bf, rope_tab, rope_tab, bd, qg, kg, conv_w, conv_b)


def _ctxproj_kernel(c_ref, g_ref, scale_ref, shift_ref, wk_ref, wu_ref, wvt_ref, bd_ref, kg_ref, cw_ref, cb_ref,
                    kc_ref, vct_ref, uc_ref):
    nb, l, d = c_ref.shape
    aw = ATTN_WIDTH
    hb = _adaln(c_ref[...].reshape(nb * l, d), g_ref, scale_ref, shift_ref)
    kc = _head_rms(_dot(hb, wk_ref[...]), bd_ref, kg_ref[...]).astype(BF16)
    u = _dot(hb, wu_ref[...])
    vct = _dot_nt(wvt_ref[...], hb).astype(BF16)
    pad = jnp.zeros((HALO, aw), F32)
    for bi in range(nb):
        kc_ref[bi] = kc[bi * l:(bi + 1) * l]
        uc_ref[bi] = _conv_window(jnp.concatenate([pad, u[bi * l:(bi + 1) * l], pad], axis=0), l, cw_ref, cb_ref)
        vct_ref[bi] = vct[:, bi * l:(bi + 1) * l]


def _ctxproj(ctx, g, scale_c, shift_c, w_ctx_bf, wt_ctx_bf, bd, kg, conv_w, conv_b, nb=4):
    b, l, d = ctx.shape
    aw = ATTN_WIDTH
    const = lambda i: (0, 0)
    blk = pl.BlockSpec((nb, l, aw), lambda i: (i, 0, 0))
    return pl.pallas_call(
        _ctxproj_kernel,
        grid=(b // nb,),
        in_specs=[pl.BlockSpec((nb, l, d), lambda i: (i, 0, 0)),
                  pl.BlockSpec((1, d), const),
                  pl.BlockSpec((1, 1, d), lambda i: (0, 0, 0)),
                  pl.BlockSpec((1, 1, d), lambda i: (0, 0, 0)),
                  pl.BlockSpec((d, aw), lambda i: (0, COL_K)),
                  pl.BlockSpec((d, aw), lambda i: (0, COL_U)),
                  pl.BlockSpec((aw, d), const),
                  pl.BlockSpec((MXU_DIM, MXU_DIM), const),
                  pl.BlockSpec((1, aw), const),
                  pl.BlockSpec((CONV_WIDTH, aw), const),
                  pl.BlockSpec((1, aw), const)],
        out_specs=[blk, pl.BlockSpec((nb, aw, l), lambda i: (i, 0, 0)), blk],
        out_shape=[jax.ShapeDtypeStruct((b, l, aw), BF16),
                   jax.ShapeDtypeStruct((b, aw, l), BF16),
                   jax.ShapeDtypeStruct((b, l, aw), F32)],
        compiler_params=_params("arbitrary"),
        name="ctxproj",
    )(ctx, g, scale_c, shift_c, w_ctx_bf, w_ctx_bf, wt_ctx_bf, bd, kg, conv_w, conv_b)


def _attn_kernel(qr_ref, qp_ref, kr_ref, vt_ref, kc_ref, vct_ref, zat_ref, rexp_ref, o_ref, bias_ref, s_buf0, s_buf1):
    @pl.when(pl.program_id(1) == 0)
    def _():
        for h in range(2):
            _fill_bias_tables(rexp_ref, h, bias_ref)

    rows = qr_ref.shape[1] // GRID_W
    n_blk = rows // Q_ROWS
    chunk_rows = KEY_CHUNK // GRID_W
    n_tiles = Q_BLK // LANES
    lane_head = lax.broadcasted_iota(jnp.int32, (1, LANES), 1) // HEAD_DIM
    s_bufs = (s_buf0, s_buf1)
    ones_rows = jnp.ones((ONES_ROWS, KEY_CHUNK), BF16)

    def in_window(typ, jj, tile):
        if typ == 0:
            return jj < NA_ROWS
        if typ == 2:
            return jj >= BAND_ROWS - NA_ROWS
        return any(0 <= jj - ri < NA_ROWS for ri in range(tile * Q_ROWS // n_tiles, (tile + 1) * Q_ROWS // n_tiles))

    def local_chunks(typ):
        return [c for c in range(BAND // KEY_CHUNK)
                if any(in_window(typ, c * chunk_rows + r, t) for r in range(chunk_rows) for t in range(n_tiles))]

    def sub_blocks(c):
        for r in range(chunk_rows):
            for t in range(n_tiles):
                yield c * chunk_rows + r, t, slice(r * GRID_W, (r + 1) * GRID_W), slice(t * LANES, (t + 1) * LANES)

    def item(rb):
        rb = jnp.asarray(rb, jnp.int32)
        q0 = pl.multiple_of(rb * Q_BLK, Q_BLK)
        band0 = pl.multiple_of(jnp.clip(rb * Q_ROWS - NA_ROWS // 2, 0, rows - BAND_ROWS) * GRID_W, Q_BLK)
        return q0, band0

    def col_max(m, blk, t):
        m = list(m)
        m[t] = jnp.maximum(m[t], jnp.max(blk.reshape(-1, SUBLANES, LANES), axis=0))
        return tuple(m)

    def score_steps(rb, h, typ):
        q0, band0 = item(rb)
        s_buf = s_bufs[h]
        mine = lane_head == h
        qr = jnp.where(mine, qr_ref[0, pl.ds(q0, Q_BLK), :], jnp.zeros((), BF16))
        qp = jnp.where(mine, qp_ref[0, pl.ds(q0, Q_BLK), :], jnp.zeros((), BF16))

        def local(c, m):
            n_rows = 1 + max(r for r in range(chunk_rows)
                             if any(in_window(typ, c * chunk_rows + r, t) for t in range(n_tiles)))
            s = _dot_nt(kr_ref[0, pl.ds(band0 + c * KEY_CHUNK, n_rows * GRID_W), :], qr)
            for jj, t, r_sl, l_sl in sub_blocks(c):
                if in_window(typ, jj, t):
                    k_sl = slice(jj * GRID_W, (jj + 1) * GRID_W)
                    blk = s[r_sl, l_sl] + bias_ref[h, typ, k_sl, l_sl]
                    s_buf[k_sl, l_sl] = blk
                    m = col_max(m, blk, t)
            return m

        def context(m):
            s = _dot_nt(kc_ref[0], qp)
            s_buf[pl.ds(BAND, KEY_CHUNK), :] = s
            for t in range(n_tiles):
                m = col_max(m, s[:, t * LANES:(t + 1) * LANES], t)
            return m

        return [functools.partial(local, c) for c in local_chunks(typ)] + [context]

    def value_steps(rb, h, typ, m):
        q0, band0 = item(rb)
        s_buf = s_bufs[h]
        hd = pl.ds(h * HEAD_DIM, HEAD_DIM)
        m = tuple(jnp.max(mt, axis=0, keepdims=True) for mt in m)

        def probs(k_sl, t):
            return jnp.exp2(s_buf[k_sl, t * LANES:(t + 1) * LANES] - m[t]).astype(BF16)

        def local(c, acc):
            zero_blk = jnp.zeros((GRID_W, LANES), BF16)
            row_blks = []
            for r in range(chunk_rows):
                jj = c * chunk_rows + r
                k_sl = slice(jj * GRID_W, (jj + 1) * GRID_W)
                row_blks.append(jnp.concatenate(
                    [probs(k_sl, t) if in_window(typ, jj, t) else zero_blk for t in range(n_tiles)], axis=1))
            p = jnp.concatenate(row_blks, axis=0)
            vt = jnp.concatenate([vt_ref[0, hd, pl.ds(band0 + c * KEY_CHUNK, KEY_CHUNK)], ones_rows], axis=0)
            return acc + _dot(vt, p)

        def context(acc):
            k_sl = slice(BAND, BAND + KEY_CHUNK)
            p = jnp.concatenate([probs(k_sl, t) for t in range(n_tiles)], axis=1)
            vt = jnp.concatenate([vct_ref[0, hd, :], ones_rows], axis=0)
            return acc + _dot(vt, p)

        return [functools.partial(local, c) for c in local_chunks(typ)] + [context]

    def write_out(rb, h, acc):
        q0, _ = item(rb)
        hd = pl.ds(h * HEAD_DIM, HEAD_DIM)
        inv_den = 1.0 / acc[HEAD_DIM:HEAD_DIM + 1]
        gated = (acc[:HEAD_DIM] * inv_den) * zat_ref[0, hd, pl.ds(q0, Q_BLK)].astype(F32)
        o_ref[0, hd, pl.ds(q0, Q_BLK)] = gated.astype(o_ref.dtype)

    m_init = (jnp.full((SUBLANES, LANES), -jnp.inf, F32),) * n_tiles
    acc_init = jnp.zeros((HEAD_DIM + ONES_ROWS, Q_BLK), F32)

    def overlapped(score_item, value_item, out_item, carry):
        m_prev, acc_prev = carry
        s_steps = score_steps(*score_item) if score_item else []
        v_steps = value_steps(*value_item, m_prev) if value_item else []
        m, acc = m_init, acc_init
        for i, (s_step, v_step) in enumerate(itertools.zip_longest(s_steps, v_steps)):
            if s_step:
                m = s_step(m)
            if i == 0 and out_item:
                write_out(*out_item, acc_prev)
            if v_step:
                acc = v_step(acc)
        return m, acc

    first, mid, last = 0, 1, 2
    carry = overlapped((0, 0, first), None, None, (None, None))
    carry = overlapped((0, 1, first), (0, 0, first), None, carry)
    carry = overlapped((1, 0, mid), (0, 1, first), (0, 0), carry)

    def body(rb, carry):
        carry = overlapped((rb, 1, mid), (rb, 0, mid), (rb - 1, 1), carry)
        return overlapped((rb + 1, 0, mid), (rb, 1, mid), (rb, 0), carry)

    carry = lax.fori_loop(1, n_blk - 2, body, carry, unroll=3)
    carry = overlapped((n_blk - 2, 1, mid), (n_blk - 2, 0, mid), (n_blk - 3, 1), carry)
    carry = overlapped((n_blk - 1, 0, last), (n_blk - 2, 1, mid), (n_blk - 2, 0), carry)
    carry = overlapped((n_blk - 1, 1, last), (n_blk - 1, 0, last), (n_blk - 2, 1), carry)
    carry = overlapped(None, (n_blk - 1, 1, last), (n_blk - 1, 0), carry)
    write_out(n_blk - 1, 1, carry[1])


def _attention(qr, qp, kr, vt, kc, vct, zat, rexp):
    b, s, aw = qr.shape
    l = kc.shape[1]
    n_pairs = aw // LANES
    lat = pl.BlockSpec((1, s, LANES), lambda p, i: (i, 0, p))
    lat_t = pl.BlockSpec((1, LANES, s), lambda p, i: (i, p, 0))
    return pl.pallas_call(
        _attn_kernel,
        grid=(n_pairs, b),
        in_specs=[lat, lat, lat, lat_t,
                  pl.BlockSpec((1, l, LANES), lambda p, i: (i, 0, p)),
                  pl.BlockSpec((1, LANES, l), lambda p, i: (i, p, 0)),
                  lat_t,
                  pl.BlockSpec((2, N_BIAS_COLS, TOE_W), lambda p, i: (p, 0, 0))],
        out_specs=lat_t,
        out_shape=jax.ShapeDtypeStruct((b, aw, s), BF16),
        scratch_shapes=[pltpu.VMEM((2, 3, BAND, Q_BLK), F32),
                        pltpu.VMEM((BAND + l, Q_BLK), F32), pltpu.VMEM((BAND + l, Q_BLK), F32)],
        compiler_params=_params("arbitrary", "arbitrary"),
        name="attention",
    )(qr, qp, kr, vt, kc, vct, zat, rexp)


def _lru_pitch(tc):
    pitch = tc + SUBLANES
    assert pitch % (2 * SUBLANES) == SUBLANES
    return pitch


def _lru_gate_consts(ba, bx, lam):
    return 0.5 * ba, 0.5 * bx, jax.nn.softplus(-lam) * (-0.5 * LRU_C * LOG2E)


def _lru_fill(bi, uc, wg, consts, a_s, b_s):
    half_ba, half_bx, half_log2a = consts
    tc = uc.shape[0]
    ucb = uc.astype(BF16)

    def gate_tanh(gi, half_bias):
        pre = jnp.concatenate([_dot(ucb[:, :MXU_DIM], wg(gi, 0)), _dot(ucb[:, MXU_DIM:], wg(gi, 1))], axis=-1)
        return jnp.tanh(pre + half_bias)

    tr = gate_tanh(0, half_ba)
    ti = gate_tanh(1, half_bx)
    a = jnp.exp2(tr * half_log2a + half_log2a)
    gap = 1.0 - a * a
    mult = gap * lax.rsqrt(jnp.maximum(gap, RSQRT_FLOOR))
    bb = mult * ((0.5 * ti + 0.5) * uc)
    r0 = pl.multiple_of(bi * _lru_pitch(tc), SUBLANES)
    for j in range(LRU_SLABS):
        a_s[j, pl.ds(r0, tc), :] = a[:, j * LANES:(j + 1) * LANES]
        b_s[j, pl.ds(r0, tc), :] = bb[:, j * LANES:(j + 1) * LANES]


def _lru_scan(reverse, tc, nb, a_s, b_s, h_s, carry_s):
    pitch = _lru_pitch(tc)

    n_groups = tc // SUBLANES

    def body(g, hs):
        base = pl.multiple_of(((n_groups - 1 - g) if reverse else g) * SUBLANES, SUBLANES)
        hs = list(hs)
        for k in (reversed(range(SUBLANES)) if reverse else range(SUBLANES)):
            rows = pl.ds(base + k, nb, stride=pitch)
            for j in range(LRU_SLABS):
                hs[j] = a_s[j, rows, :] * hs[j] + b_s[j, rows, :]
                h_s[j, rows, :] = hs[j]
        return tuple(hs)

    hs = tuple(carry_s[:, j * LANES:(j + 1) * LANES] for j in range(LRU_SLABS))
    hs = lax.fori_loop(0, n_groups, body, hs)
    for j in range(LRU_SLABS):
        carry_s[:, j * LANES:(j + 1) * LANES] = hs[j]


def _lru_rows(bi, tc, h_s):
    r0 = pl.multiple_of(bi * _lru_pitch(tc), SUBLANES)
    return jnp.concatenate([h_s[j, pl.ds(r0, tc), :] for j in range(LRU_SLABS)], axis=-1)


def _lru_fwd_kernel(u_ref, uc_ref, wg_ref, ba_ref, bx_ref, lam_ref, hf_ref, hb0_ref, a_s, b_s, h_s, carry_s):
    nb, tc = u_ref.shape[0], u_ref.shape[1]
    consts = [_lru_gate_consts(ba_ref[d], bx_ref[d], lam_ref[d]) for d in range(2)]

    def fill_all(src_ref, d, unroll):
        def step(bi, carry):
            _lru_fill(bi, src_ref[bi], lambda gi, hf: wg_ref[d, gi, hf], consts[d], a_s, b_s)
            return carry

        lax.fori_loop(0, nb, step, 0, unroll=unroll)

    @pl.when(pl.program_id(0) == 0)
    def _():
        carry_s[...] = jnp.zeros_like(carry_s)
        fill_all(uc_ref, 1, 1)
        _lru_scan(True, tc, nb, a_s, b_s, h_s, carry_s)
        hb0_ref[...] = carry_s[...]
        carry_s[...] = jnp.zeros_like(carry_s)
        fill_all(uc_ref, 0, 1)
        _lru_scan(False, tc, nb, a_s, b_s, h_s, carry_s)

    fill_all(u_ref, 0, 8)
    _lru_scan(False, tc, nb, a_s, b_s, h_s, carry_s)

    def emit(bi, carry):
        hf_ref[bi] = _lru_rows(bi, tc, h_s).astype(hf_ref.dtype)
        return carry

    lax.fori_loop(0, nb, emit, 0)


def _lru_fwd(u, u_c, wg, ba, bx, lam):
    b, s, lw = u.shape
    tc = LRU_CHUNK
    assert u_c.shape[1] == tc and s % tc == 0
    full = lambda *shape: pl.BlockSpec(shape, lambda i: (0,) * len(shape))
    cblk = pl.BlockSpec((b, tc, lw), lambda i: (0, i, 0))
    slab = pltpu.VMEM((LRU_SLABS, b * _lru_pitch(tc), LANES), F32)
    return pl.pallas_call(
        _lru_fwd_kernel,
        grid=(s // tc,),
        in_specs=[cblk, full(b, tc, lw), full(2, 2, 2, MXU_DIM, MXU_DIM), full(2, 1, lw), full(2, 1, lw), full(2, 1, lw)],
        out_specs=[cblk, full(b, lw)],
        out_shape=[jax.ShapeDtypeStruct((b, s, lw), BF16), jax.ShapeDtypeStruct((b, lw), F32)],
        scratch_shapes=[slab, slab, slab, pltpu.VMEM((b, lw), F32)],
        compiler_params=_params("arbitrary"),
        name="rglru_fwd",
    )(u, u_c, wg, ba, bx, lam)


def _lru_bwd_out_kernel(u_ref, hf_ref, zl_ref, agt_ref, x_ref, hb0_ref, gate_ref, wg_ref, ba_ref, bx_ref, lam_ref,
                        wo_ref, o_ref, a_s, b_s, h_s, carry_s):
    nb, tc = u_ref.shape[0], u_ref.shape[1]
    consts = _lru_gate_consts(ba_ref[...], bx_ref[...], lam_ref[...])

    @pl.when(pl.program_id(0) == 0)
    def _():
        carry_s[...] = hb0_ref[...]

    def fill(bi, carry):
        _lru_fill(bi, u_ref[bi], lambda gi, hf: wg_ref[gi, hf], consts, a_s, b_s)
        return carry

    lax.fori_loop(0, nb, fill, 0, unroll=8)
    _lru_scan(True, tc, nb, a_s, b_s, h_s, carry_s)

    def emit(gi, carry):
        group = [OUT_GROUP * gi + k for k in range(OUT_GROUP)]
        y = jnp.concatenate([((hf_ref[bi].astype(F32) + _lru_rows(bi, tc, h_s)) * zl_ref[bi].astype(F32)).astype(BF16)
                             for bi in group], axis=0)
        at = jnp.concatenate([agt_ref[bi] for bi in group], axis=1)
        mix = lax.dot_general(at, wo_ref[:ATTN_WIDTH, :], (((0,), (0,)), ((), ())), preferred_element_type=F32)
        mix = mix + _dot(y, wo_ref[ATTN_WIDTH:, :])
        for k, bi in enumerate(group):
            o_ref[bi] = x_ref[bi] + gate_ref[bi] * mix[k * tc:(k + 1) * tc]
        return carry

    lax.fori_loop(0, nb // OUT_GROUP, emit, 0)


def _lru_bwd_out(u, hf, zl, agt, x, hb0, gate, wg, ba, bx, lam, wo_bf):
    b, s, lw = u.shape
    d = x.shape[2]
    tc = LRU_BWD_CHUNK
    n_chunks = s // tc
    assert s % tc == 0 and b % OUT_GROUP == 0
    full = lambda *shape: pl.BlockSpec(shape, lambda i: (0,) * len(shape))
    rev = lambda i: n_chunks - 1 - i
    cblk = pl.BlockSpec((b, tc, lw), lambda i: (0, rev(i), 0))
    xblk = pl.BlockSpec((b, tc, d), lambda i: (0, rev(i), 0))
    slab = pltpu.VMEM((LRU_SLABS, b * _lru_pitch(tc), LANES), F32)
    return pl.pallas_call(
        _lru_bwd_out_kernel,
        grid=(n_chunks,),
        in_specs=[cblk, cblk, cblk,
                  pl.BlockSpec((b, ATTN_WIDTH, tc), lambda i: (0, 0, rev(i))),
                  xblk, full(b, lw), full(b, 1, d),
                  full(2, 2, MXU_DIM, MXU_DIM), full(1, lw), full(1, lw), full(1, lw),
                  full(ATTN_WIDTH + lw, d)],
        out_specs=xblk,
        out_shape=jax.ShapeDtypeStruct(x.shape, x.dtype),
        scratch_shapes=[slab, slab, slab, pltpu.VMEM((b, lw), F32)],
        compiler_params=_params("arbitrary"),
        name="rglru_bwd_outproj",
    )(u, hf, zl, agt, x, hb0, gate, wg, ba, bx, lam, wo_bf)


def _rope_table(s):
    half = HEAD_DIM // 4
    assert s // GRID_W <= GRID_W
    inv = ROPE_BASE ** (-jnp.arange(half, dtype=F32) / half)
    ang = jnp.arange(GRID_W, dtype=F32)[:, None] * inv[None, :]
    reps = LANES // (2 * half)
    cos = jnp.tile(jnp.cos(ang), (1, 2 * reps))
    sin = jnp.tile(jnp.concatenate([-jnp.sin(ang), jnp.sin(ang)], axis=-1), (1, reps))
    return jnp.stack([cos, sin])


def _block_diag_gates(w):
    n = MXU_DIM // LRU_BLOCK
    lead = w.shape[:-3]
    w = w.reshape(lead + (2, n, LRU_BLOCK, LRU_BLOCK))
    on_diag = jnp.eye(n, dtype=bool)[:, None, :, None]
    blocks = jnp.where(on_diag, w[..., :, :, None, :], jnp.zeros((), w.dtype))
    return blocks.reshape(lead + (2, MXU_DIM, MXU_DIM))


def kernel(x, c, ctx, c_ctx, norm_g, w_mod, b_mod, w_in, w_out, q_norm_g, k_norm_g, rpb, conv_w, conv_b,
           lru_wa, lru_ba, lru_wx, lru_bx, lru_lam):
    bsz, s, d = x.shape
    assert w_in.shape[0] == 1 and d == D_MODEL and s % (GRID_W * Q_ROWS) == 0
    aw, lw = ATTN_WIDTH, LRU_WIDTH

    pad_rows = 2 * SUBLANES - bsz - 1
    cc = jnp.concatenate([c, c_ctx[None, :], jnp.zeros((pad_rows, d), F32)], axis=0)
    mod = _modulation(cc, w_mod[0], b_mod[0][None, :])
    shift, scale, gate = [mod[:bsz, i * d:(i + 1) * d][:, None, :] for i in range(3)]
    shift_c, scale_c = [mod[bsz:bsz + 1, i * d:(i + 1) * d][:, None, :] for i in range(2)]

    w = w_in[0]
    w_bf = w.astype(BF16)
    wt_bf = _transpose_cols(w_bf, COL_V * aw, (COL_ZA + 1) * aw)
    w_ctx_bf, wt_ctx_bf = w_bf, wt_bf
    g = norm_g[0][None, :]
    rope_tab = _rope_table(s)
    blk = jnp.arange(MXU_DIM) // HEAD_DIM
    bd = jnp.where(blk[:, None] == blk[None, :], 1.0 / HEAD_DIM, 0.0).astype(BF16)
    qg = jnp.tile(q_norm_g[0] * (HEAD_DIM ** -0.5 * LOG2E), N_HEADS)[None, :]
    kg = jnp.tile(k_norm_g[0], N_HEADS)[None, :]

    cw, cb = conv_w[0], conv_b[0][None, :]
    qr, qp, kr, vt, zat, u, zl = _inproj(x, g, scale, shift, w_bf, wt_bf, rope_tab, bd, qg, kg, cw, cb)
    kc, vct, u_c = _ctxproj(ctx, g, scale_c, shift_c, w_ctx_bf, wt_ctx_bf, bd, kg, cw, cb)

    agt = _attention(qr, qp, kr, vt, kc, vct, zat, _bias_rows(rpb[0]))

    wg = (0.5 * jnp.stack([lru_wa[0], lru_wx[0]], axis=1)).astype(BF16)
    wg = _block_diag_gates(wg)
    ba, bx, lam = lru_ba[0][:, None, :], lru_bx[0][:, None, :], lru_lam[0][:, None, :]
    hf, hb0 = _lru_fwd(u, u_c, wg, ba, bx, lam)
    return _lru_bwd_out(u, hf, zl, agt, x, hb0, gate, wg[1], ba[1], bx[1], lam[1], w_out[0].astype(BF16))
```

```python
import functools
import itertools
import math

import jax
import jax.numpy as jnp
from jax import lax
from jax.experimental import pallas as pl
from jax.experimental.pallas import tpu as pltpu

F32 = jnp.float32
BF16 = jnp.bfloat16

D_MODEL = 1024
GRID_W = 64
HEAD_DIM = 64
ATTN_WIDTH = 512
LRU_WIDTH = 512
N_HEADS = ATTN_WIDTH // HEAD_DIM
LRU_BLOCK = 64
NA_ROWS = 8
NA_COLS = 16
CONV_WIDTH = 4
LRU_C = 8.0
ROPE_BASE = 10000.0
EPS = 1e-6
NEG_INF = -1e30
RSQRT_FLOOR = 1e-30
LOG2E = math.log2(math.e)

LANES = 128
SUBLANES = 8
MXU_DIM = 256
VMEM_LIMIT_BYTES = 58 * 1024 * 1024

Q_ROWS = 4
BAND_ROWS = Q_ROWS + NA_ROWS
Q_BLK = Q_ROWS * GRID_W
BAND = BAND_ROWS * GRID_W
KEY_CHUNK = 256
ONES_ROWS = 16
N_BIAS_ROWS = 2 * NA_ROWS - 1
N_BIAS_COLS = 2 * NA_COLS - 1
TOE_W = 1024

LRU_CHUNK = 256
LRU_BWD_CHUNK = 128
LRU_SLABS = LRU_WIDTH // LANES
OUT_GROUP = 8
CONV_ROWS = 128
COL_Q, COL_K, COL_V, COL_ZA, COL_U, COL_ZL = range(6)
HALO = 16


def _dot(a, b):
    return jnp.dot(a, b, preferred_element_type=F32)


def _dot_nt(a, b):
    return lax.dot_general(a, b, (((1,), (1,)), ((), ())), preferred_element_type=F32)


def _params(*semantics):
    return pltpu.CompilerParams(dimension_semantics=semantics, vmem_limit_bytes=VMEM_LIMIT_BYTES)


def _split_bf16(t):
    hi = t.astype(BF16)
    lo = (t - hi.astype(F32)).astype(BF16)
    return hi, lo


def _mod_kernel(cc_ref, w_ref, b_ref, o_ref):
    cc = cc_ref[...]
    s = cc * jax.nn.sigmoid(cc)
    s_hi, s_lo = _split_bf16(s)
    w_hi, w_lo = _split_bf16(w_ref[...])
    o_ref[...] = _dot(s_hi, w_hi) + _dot(s_lo, w_hi) + _dot(s_hi, w_lo) + b_ref[...]


def _modulation(cc, w_mod, b_mod):
    rows, d = cc.shape
    n = w_mod.shape[1]
    tn = 1024
    return pl.pallas_call(
        _mod_kernel,
        grid=(n // tn,),
        in_specs=[pl.BlockSpec((rows, d), lambda j: (0, 0)),
                  pl.BlockSpec((d, tn), lambda j: (0, j)),
                  pl.BlockSpec((1, tn), lambda j: (0, j))],
        out_specs=pl.BlockSpec((rows, tn), lambda j: (0, j)),
        out_shape=jax.ShapeDtypeStruct((rows, n), F32),
        compiler_params=_params("arbitrary"),
        name="modulation",
    )(cc, w_mod, b_mod)


def _fill_bias_tables(rexp_ref, h, o_ref):
    k = lax.broadcasted_iota(jnp.int32, (GRID_W, TOE_W), 0)
    q = lax.broadcasted_iota(jnp.int32, (GRID_W, TOE_W), 1) % GRID_W
    diff = k - q + (NA_COLS - 1)
    toe = jnp.zeros((GRID_W, TOE_W), F32)
    for d in range(N_BIAS_COLS):
        toe = jnp.where(diff == d, rexp_ref[h, d:d + 1, :], toe)
    col_start = jnp.clip(q - NA_COLS // 2, 0, GRID_W - NA_COLS)
    col_ok = (k >= col_start) & (k < col_start + NA_COLS)
    toe = jnp.where(col_ok, toe, NEG_INF)
    ri = lax.broadcasted_iota(jnp.int32, (GRID_W, Q_BLK), 1) // GRID_W
    masked = jnp.full((GRID_W, Q_BLK), NEG_INF, F32)

    for typ in range(3):
        off = (NA_ROWS - 1, NA_ROWS // 2 - 1, NA_ROWS - BAND_ROWS + Q_ROWS - 1)[typ]
        for jj in range(BAND_ROWS):
            if typ == 0:
                lo, hi = (0, Q_ROWS) if jj < NA_ROWS else (0, 0)
            elif typ == 1:
                lo, hi = max(jj - NA_ROWS + 1, 0), min(jj, Q_ROWS - 1) + 1
            else:
                lo, hi = (0, Q_ROWS) if jj >= BAND_ROWS - NA_ROWS else (0, 0)
            strip = masked
            if lo < hi:
                e0 = N_BIAS_ROWS - 1 - jj - off
                rolled = pltpu.roll(toe, (-e0 * GRID_W) % TOE_W, axis=1)[:, :Q_BLK]
                strip = jnp.where((ri >= lo) & (ri < hi), rolled, NEG_INF)
            o_ref[h, typ, jj * GRID_W:(jj + 1) * GRID_W, :] = strip


def _bias_rows(rpb):
    rexp = jnp.repeat(jnp.transpose(rpb[:, ::-1, :], (0, 2, 1)) * LOG2E, GRID_W, axis=2)
    return jnp.pad(rexp, ((0, 0), (0, 0), (0, TOE_W - N_BIAS_ROWS * GRID_W)))


def _transpose_kernel(w_ref, o_ref):
    o_ref[...] = w_ref[...].T


def _transpose_cols(w, lo, hi, tn=MXU_DIM):
    rows = w.shape[0]
    assert lo % tn == 0 and hi % tn == 0
    return pl.pallas_call(
        _transpose_kernel,
        grid=((hi - lo) // tn,),
        in_specs=[pl.BlockSpec((rows, tn), lambda i: (0, lo // tn + i))],
        out_specs=pl.BlockSpec((tn, rows), lambda i: (i, 0)),
        out_shape=jax.ShapeDtypeStruct((hi - lo, rows), w.dtype),
        compiler_params=_params("arbitrary"),
        name="weight_transpose",
    )(w)


def _adaln(x, g_ref, scale_ref, shift_ref):
    ms = jnp.mean(x * x, axis=-1, keepdims=True)
    gm = g_ref[...] * (1.0 + scale_ref[0])
    return ((x * lax.rsqrt(ms + EPS)) * gm + shift_ref[0]).astype(BF16)


def _head_rms(t, bd_ref, gain):
    sq = (t * t).astype(BF16)
    bd = bd_ref[...]
    m = jnp.concatenate([_dot(sq[:, :MXU_DIM], bd), _dot(sq[:, MXU_DIM:], bd)], axis=-1)
    return t * lax.rsqrt(m + EPS) * gain


def _rope_tile(rows_ref, cols_ref, f):
    lane = lax.broadcasted_iota(jnp.int32, (1, LANES), 1)
    by_row = (lane % HEAD_DIM) < HEAD_DIM // 2
    by_col = cols_ref[f]
    return jnp.concatenate([jnp.where(by_row, rows_ref[f, k:k + 1, :], by_col) for k in range(rows_ref.shape[1])],
                           axis=0)


def _rope(t, cos, sin):
    lane = lax.broadcasted_iota(jnp.int32, (1, LANES), 1)
    first = (lane % 32) < 16
    parts = []
    for j in range(t.shape[1] // LANES):
        c = t[:, j * LANES:(j + 1) * LANES]
        swapped = jnp.where(first, pltpu.roll(c, LANES - 16, axis=1), pltpu.roll(c, 16, axis=1))
        parts.append(c * cos + swapped * sin)
    return jnp.concatenate(parts, axis=-1)


def _silu(t):
    return t * jax.nn.sigmoid(t)


def _conv_window(ext, n, cw_ref, cb_ref):
    rows = ext.shape[0]

    def shifted(k):
        return pltpu.roll(ext, (rows - k) % rows, axis=0)[HALO:HALO + n]

    y = cb_ref[...] + cw_ref[0:1, :] * shifted(-1)
    y = y + cw_ref[1:2, :] * ext[HALO:HALO + n]
    y = y + cw_ref[2:3, :] * shifted(1)
    return y + cw_ref[3:4, :] * shifted(2)


def _inproj_kernel(x_ref, xp_ref, xn_ref, g_ref, scale_ref, shift_ref, w_ref, wt_ref, rope_rows_ref, rope_cols_ref,
                   bd_ref, qg_ref, kg_ref, cw_ref, cb_ref, qr_ref, qp_ref, kr_ref, vt_ref, zat_ref, uc_ref, zl_ref):
    tm = x_ref.shape[1]
    aw = ATTN_WIDTH
    i, n_tiles = pl.program_id(0), pl.num_programs(0)
    hb_ext = jnp.concatenate([_adaln(xp_ref[0], g_ref, scale_ref, shift_ref),
                              _adaln(x_ref[0], g_ref, scale_ref, shift_ref),
                              _adaln(xn_ref[0], g_ref, scale_ref, shift_ref)], axis=0)
    hb = hb_ext[HALO:HALO + tm]

    def cols(j):
        return w_ref[:, j * aw:(j + 1) * aw]

    u_ext = _dot(hb_ext, cols(COL_U))
    row = lax.broadcasted_iota(jnp.int32, (tm + 2 * HALO, 1), 0)
    inside = ((row >= HALO) | (i > 0)) & ((row < HALO + tm) | (i < n_tiles - 1))
    u_ext = jnp.where(inside, u_ext, 0.0)

    cos = _rope_tile(rope_rows_ref, rope_cols_ref, 0)
    sin = _rope_tile(rope_rows_ref, rope_cols_ref, 1)
    qn = _head_rms(_dot(hb, cols(COL_Q)), bd_ref, qg_ref[...])
    qp_ref[0] = qn.astype(BF16)
    qr_ref[0] = _rope(qn, cos, sin).astype(BF16)
    kn = _head_rms(_dot(hb, cols(COL_K)), bd_ref, kg_ref[...])
    kr_ref[0] = _rope(kn, cos, sin).astype(BF16)

    zl_ref[0] = _silu(_dot(hb, cols(COL_ZL))).astype(BF16)
    vt_ref[0] = _dot_nt(wt_ref[:aw, :], hb).astype(BF16)
    zat_ref[0] = _silu(_dot_nt(wt_ref[aw:, :], hb)).astype(BF16)
    for r in range(0, tm, CONV_ROWS):
        uc_ref[0, r:r + CONV_ROWS, :] = _conv_window(u_ext[r:r + CONV_ROWS + 2 * HALO], CONV_ROWS, cw_ref, cb_ref)


def _inproj(x, g, scale, shift, w_bf, wt_bf, rope_tab, bd, qg, kg, conv_w, conv_b, tm=1024):
    b, s, d = x.shape
    aw = ATTN_WIDTH
    tok = lambda i, j: (j, i, 0)
    per_b = lambda i, j: (j, 0, 0)
    const = lambda i, j: (0, 0)
    halo_per_tile = tm // HALO
    n_halo = s // HALO
    out_bf = jax.ShapeDtypeStruct((b, s, aw), BF16)
    out_t = jax.ShapeDtypeStruct((b, aw, s), BF16)
    out_f32 = jax.ShapeDtypeStruct((b, s, aw), F32)
    blk = pl.BlockSpec((1, tm, aw), tok)
    blk_t = pl.BlockSpec((1, aw, tm), lambda i, j: (j, 0, i))
    return pl.pallas_call(
        _inproj_kernel,
        grid=(s // tm, b),
        in_specs=[pl.BlockSpec((1, tm, d), tok),
                  pl.BlockSpec((1, HALO, d), lambda i, j: (j, jnp.maximum(i * halo_per_tile - 1, 0), 0)),
                  pl.BlockSpec((1, HALO, d), lambda i, j: (j, jnp.minimum((i + 1) * halo_per_tile, n_halo - 1), 0)),
                  pl.BlockSpec((1, d), const),
                  pl.BlockSpec((1, 1, d), per_b),
                  pl.BlockSpec((1, 1, d), per_b),
                  pl.BlockSpec(w_bf.shape, const),
                  pl.BlockSpec(wt_bf.shape, const),
                  pl.BlockSpec((2, tm // GRID_W, LANES), lambda i, j: (0, i, 0)),
                  pl.BlockSpec((2, GRID_W, LANES), lambda i, j: (0, 0, 0)),
                  pl.BlockSpec((MXU_DIM, MXU_DIM), const),
                  pl.BlockSpec((1, aw), const),
                  pl.BlockSpec((1, aw), const),
                  pl.BlockSpec((CONV_WIDTH, aw), const),
                  pl.BlockSpec((1, aw), const)],
        out_specs=[blk, blk, blk, blk_t, blk_t, blk, blk],
        out_shape=[out_bf, out_bf, out_bf, out_t, out_t, out_f32, out_bf],
        compiler_params=_params("arbitrary", "arbitrary"),
        name="inproj",
    )(x, x, x, g, scale, shift, w_bf, wt_bf, rope_tab, rope_tab, bd, qg, kg, conv_w, conv_b)


def _ctxproj_kernel(c_ref, g_ref, scale_ref, shift_ref, wk_ref, wu_ref, wvt_ref, bd_ref, kg_ref, cw_ref, cb_ref,
                    kc_ref, vct_ref, uc_ref):
    nb, l, d = c_ref.shape
    aw = ATTN_WIDTH
    hb = _adaln(c_ref[...].reshape(nb * l, d), g_ref, scale_ref, shift_ref)
    kc = _head_rms(_dot(hb, wk_ref[...]), bd_ref, kg_ref[...]).astype(BF16)
    u = _dot(hb, wu_ref[...])
    vct = _dot_nt(wvt_ref[...], hb).astype(BF16)
    pad = jnp.zeros((HALO, aw), F32)
    for bi in range(nb):
        kc_ref[bi] = kc[bi * l:(bi + 1) * l]
        uc_ref[bi] = _conv_window(jnp.concatenate([pad, u[bi * l:(bi + 1) * l], pad], axis=0), l, cw_ref, cb_ref)
        vct_ref[bi] = vct[:, bi * l:(bi + 1) * l]


def _ctxproj(ctx, g, scale_c, shift_c, w_ctx_bf, wt_ctx_bf, bd, kg, conv_w, conv_b, nb=4):
    b, l, d = ctx.shape
    aw = ATTN_WIDTH
    const = lambda i: (0, 0)
    blk = pl.BlockSpec((nb, l, aw), lambda i: (i, 0, 0))
    return pl.pallas_call(
        _ctxproj_kernel,
        grid=(b // nb,),
        in_specs=[pl.BlockSpec((nb, l, d), lambda i: (i, 0, 0)),
                  pl.BlockSpec((1, d), const),
                  pl.BlockSpec((1, 1, d), lambda i: (0, 0, 0)),
                  pl.BlockSpec((1, 1, d), lambda i: (0, 0, 0)),
                  pl.BlockSpec((d, aw), lambda i: (0, COL_K)),
                  pl.BlockSpec((d, aw), lambda i: (0, COL_U)),
                  pl.BlockSpec((aw, d), const),
                  pl.BlockSpec((MXU_DIM, MXU_DIM), const),
                  pl.BlockSpec((1, aw), const),
                  pl.BlockSpec((CONV_WIDTH, aw), const),
                  pl.BlockSpec((1, aw), const)],
        out_specs=[blk, pl.BlockSpec((nb, aw, l), lambda i: (i, 0, 0)), blk],
        out_shape=[jax.ShapeDtypeStruct((b, l, aw), BF16),
                   jax.ShapeDtypeStruct((b, aw, l), BF16),
                   jax.ShapeDtypeStruct((b, l, aw), F32)],
        compiler_params=_params("arbitrary"),
        name="ctxproj",
    )(ctx, g, scale_c, shift_c, w_ctx_bf, w_ctx_bf, wt_ctx_bf, bd, kg, conv_w, conv_b)


def _attn_kernel(qr_ref, qp_ref, kr_ref, vt_ref, kc_ref, vct_ref, zat_ref, rexp_ref, o_ref, bias_ref, s_buf0, s_buf1):
    @pl.when(pl.program_id(1) == 0)
    def _():
        for h in range(2):
            _fill_bias_tables(rexp_ref, h, bias_ref)

    rows = qr_ref.shape[1] // GRID_W
    n_blk = rows // Q_ROWS
    chunk_rows = KEY_CHUNK // GRID_W
    n_tiles = Q_BLK // LANES
    lane_head = lax.broadcasted_iota(jnp.int32, (1, LANES), 1) // HEAD_DIM
    s_bufs = (s_buf0, s_buf1)
    ones_rows = jnp.ones((ONES_ROWS, KEY_CHUNK), BF16)

    def in_window(typ, jj, tile):
        if typ == 0:
            return jj < NA_ROWS
        if typ == 2:
            return jj >= BAND_ROWS - NA_ROWS
        return any(0 <= jj - ri < NA_ROWS for ri in range(tile * Q_ROWS // n_tiles, (tile + 1) * Q_ROWS // n_tiles))

    def local_chunks(typ):
        return [c for c in range(BAND // KEY_CHUNK)
                if any(in_window(typ, c * chunk_rows + r, t) for r in range(chunk_rows) for t in range(n_tiles))]

    def sub_blocks(c):
        for r in range(chunk_rows):
            for t in range(n_tiles):
                yield c * chunk_rows + r, t, slice(r * GRID_W, (r + 1) * GRID_W), slice(t * LANES, (t + 1) * LANES)

    def item(rb):
        rb = jnp.asarray(rb, jnp.int32)
        q0 = pl.multiple_of(rb * Q_BLK, Q_BLK)
        band0 = pl.multiple_of(jnp.clip(rb * Q_ROWS - NA_ROWS // 2, 0, rows - BAND_ROWS) * GRID_W, Q_BLK)
        return q0, band0

    def col_max(m, blk, t):
        m = list(m)
        m[t] = jnp.maximum(m[t], jnp.max(blk.reshape(-1, SUBLANES, LANES), axis=0))
        return tuple(m)

    def score_steps(rb, h, typ):
        q0, band0 = item(rb)
        s_buf = s_bufs[h]
        mine = lane_head == h
        qr = jnp.where(mine, qr_ref[0, pl.ds(q0, Q_BLK), :], jnp.zeros((), BF16))
        qp = jnp.where(mine, qp_ref[0, pl.ds(q0, Q_BLK), :], jnp.zeros((), BF16))

        def local(c, m):
            n_rows = 1 + max(r for r in range(chunk_rows)
                             if any(in_window(typ, c * chunk_rows + r, t) for t in range(n_tiles)))
            s = _dot_nt(kr_ref[0, pl.ds(band0 + c * KEY_CHUNK, n_rows * GRID_W), :], qr)
            for jj, t, r_sl, l_sl in sub_blocks(c):
                if in_window(typ, jj, t):
                    k_sl = slice(jj * GRID_W, (jj + 1) * GRID_W)
                    blk = s[r_sl, l_sl] + bias_ref[h, typ, k_sl, l_sl]
                    s_buf[k_sl, l_sl] = blk
                    m = col_max(m, blk, t)
            return m

        def context(m):
            s = _dot_nt(kc_ref[0], qp)
            s_buf[pl.ds(BAND, KEY_CHUNK), :] = s
            for t in range(n_tiles):
                m = col_max(m, s[:, t * LANES:(t + 1) * LANES], t)
            return m

        return [functools.partial(local, c) for c in local_chunks(typ)] + [context]

    def value_steps(rb, h, typ, m):
        q0, band0 = item(rb)
        s_buf = s_bufs[h]
        hd = pl.ds(h * HEAD_DIM, HEAD_DIM)
        m = tuple(jnp.max(mt, axis=0, keepdims=True) for mt in m)

        def probs(k_sl, t):
            return jnp.exp2(s_buf[k_sl, t * LANES:(t + 1) * LANES] - m[t]).astype(BF16)

        def local(c, acc):
            zero_blk = jnp.zeros((GRID_W, LANES), BF16)
            row_blks = []
            for r in range(chunk_rows):
                jj = c * chunk_rows + r
                k_sl = slice(jj * GRID_W, (jj + 1) * GRID_W)
                row_blks.append(jnp.concatenate(
                    [probs(k_sl, t) if in_window(typ, jj, t) else zero_blk for t in range(n_tiles)], axis=1))
            p = jnp.concatenate(row_blks, axis=0)
            vt = jnp.concatenate([vt_ref[0, hd, pl.ds(band0 + c * KEY_CHUNK, KEY_CHUNK)], ones_rows], axis=0)
            return acc + _dot(vt, p)

        def context(acc):
            k_sl = slice(BAND, BAND + KEY_CHUNK)
            p = jnp.concatenate([probs(k_sl, t) for t in range(n_tiles)], axis=1)
            vt = jnp.concatenate([vct_ref[0, hd, :], ones_rows], axis=0)
            return acc + _dot(vt, p)

        return [functools.partial(local, c) for c in local_chunks(typ)] + [context]

    def write_out(rb, h, acc):
        q0, _ = item(rb)
        hd = pl.ds(h * HEAD_DIM, HEAD_DIM)
        inv_den = 1.0 / acc[HEAD_DIM:HEAD_DIM + 1]
        gated = (acc[:HEAD_DIM] * inv_den) * zat_ref[0, hd, pl.ds(q0, Q_BLK)].astype(F32)
        o_ref[0, hd, pl.ds(q0, Q_BLK)] = gated.astype(o_ref.dtype)

    m_init = (jnp.full((SUBLANES, LANES), -jnp.inf, F32),) * n_tiles
    acc_init = jnp.zeros((HEAD_DIM + ONES_ROWS, Q_BLK), F32)

    def overlapped(score_item, value_item, out_item, carry):
        m_prev, acc_prev = carry
        s_steps = score_steps(*score_item) if score_item else []
        v_steps = value_steps(*value_item, m_prev) if value_item else []
        m, acc = m_init, acc_init
        for i, (s_step, v_step) in enumerate(itertools.zip_longest(s_steps, v_steps)):
            if s_step:
                m = s_step(m)
            if i == 0 and out_item:
                write_out(*out_item, acc_prev)
            if v_step:
                acc = v_step(acc)
        return m, acc

    first, mid, last = 0, 1, 2
    carry = overlapped((0, 0, first), None, None, (None, None))
    carry = overlapped((0, 1, first), (0, 0, first), None, carry)
    carry = overlapped((1, 0, mid), (0, 1, first), (0, 0), carry)

    def body(rb, carry):
        carry = overlapped((rb, 1, mid), (rb, 0, mid), (rb - 1, 1), carry)
        return overlapped((rb + 1, 0, mid), (rb, 1, mid), (rb, 0), carry)

    carry = lax.fori_loop(1, n_blk - 2, body, carry, unroll=4)
    carry = overlapped((n_blk - 2, 1, mid), (n_blk - 2, 0, mid), (n_blk - 3, 1), carry)
    carry = overlapped((n_blk - 1, 0, last), (n_blk - 2, 1, mid), (n_blk - 2, 0), carry)
    carry = overlapped((n_blk - 1, 1, last), (n_blk - 1, 0, last), (n_blk - 2, 1), carry)
    carry = overlapped(None, (n_blk - 1, 1, last), (n_blk - 1, 0), carry)
    write_out(n_blk - 1, 1, carry[1])


def _attention(qr, qp, kr, vt, kc, vct, zat, rexp):
    b, s, aw = qr.shape
    l = kc.shape[1]
    n_pairs = aw // LANES
    lat = pl.BlockSpec((1, s, LANES), lambda p, i: (i, 0, p))
    lat_t = pl.BlockSpec((1, LANES, s), lambda p, i: (i, p, 0))
    return pl.pallas_call(
        _attn_kernel,
        grid=(n_pairs, b),
        in_specs=[lat, lat, lat, lat_t,
                  pl.BlockSpec((1, l, LANES), lambda p, i: (i, 0, p)),
                  pl.BlockSpec((1, LANES, l), lambda p, i: (i, p, 0)),
                  lat_t,
                  pl.BlockSpec((2, N_BIAS_COLS, TOE_W), lambda p, i: (p, 0, 0))],
        out_specs=lat_t,
        out_shape=jax.ShapeDtypeStruct((b, aw, s), BF16),
        scratch_shapes=[pltpu.VMEM((2, 3, BAND, Q_BLK), F32),
                        pltpu.VMEM((BAND + l, Q_BLK), F32), pltpu.VMEM((BAND + l, Q_BLK), F32)],
        compiler_params=_params("arbitrary", "arbitrary"),
        name="attention",
    )(qr, qp, kr, vt, kc, vct, zat, rexp)


def _lru_pitch(tc):
    assert tc % SUBLANES == 0
    return tc + SUBLANES // 2


def _lru_gate_consts(ba, bx, lam):
    return 0.5 * ba, 0.5 * bx, jax.nn.softplus(-lam) * (-0.5 * LRU_C * LOG2E)


def _lru_fill(bi, uc, wg, consts, a_s, b_s):
    half_ba, half_bx, half_log2a = consts
    tc = uc.shape[0]
    ucb = uc.astype(BF16)

    def gate_tanh(gi, half_bias):
        pre = jnp.concatenate([_dot(ucb[:, :MXU_DIM], wg(gi, 0)), _dot(ucb[:, MXU_DIM:], wg(gi, 1))], axis=-1)
        return jnp.tanh(pre + half_bias)

    tr = gate_tanh(0, half_ba)
    ti = gate_tanh(1, half_bx)
    a = jnp.exp2(tr * half_log2a + half_log2a)
    gap = 1.0 - a * a
    mult = gap * lax.rsqrt(jnp.maximum(gap, RSQRT_FLOOR))
    bb = mult * ((0.5 * ti + 0.5) * uc)
    r0 = bi * _lru_pitch(tc)
    for j in range(LRU_SLABS):
        a_s[j, r0:r0 + tc, :] = a[:, j * LANES:(j + 1) * LANES]
        b_s[j, r0:r0 + tc, :] = bb[:, j * LANES:(j + 1) * LANES]


def _lru_scan(reverse, tc, nb, a_s, b_s, h_s, carry_s):
    pitch = _lru_pitch(tc)

    n_groups = tc // SUBLANES

    def body(g, hs):
        base = pl.multiple_of(((n_groups - 1 - g) if reverse else g) * SUBLANES, SUBLANES)
        hs = list(hs)
        for k in (reversed(range(SUBLANES)) if reverse else range(SUBLANES)):
            rows = pl.ds(base + k, nb, stride=pitch)
            for j in range(LRU_SLABS):
                hs[j] = a_s[j, rows, :] * hs[j] + b_s[j, rows, :]
                h_s[j, rows, :] = hs[j]
        return tuple(hs)

    hs = tuple(carry_s[:, j * LANES:(j + 1) * LANES] for j in range(LRU_SLABS))
    hs = lax.fori_loop(0, n_groups, body, hs)
    for j in range(LRU_SLABS):
        carry_s[:, j * LANES:(j + 1) * LANES] = hs[j]


def _lru_rows(bi, tc, h_s):
    r0 = bi * _lru_pitch(tc)
    return jnp.concatenate([h_s[j, r0:r0 + tc, :] for j in range(LRU_SLABS)], axis=-1)


def _lru_fwd_kernel(u_ref, uc_ref, wg_ref, ba_ref, bx_ref, lam_ref, hf_ref, hb0_ref, a_s, b_s, h_s, carry_s):
    nb, tc = u_ref.shape[0], u_ref.shape[1]
    consts = [_lru_gate_consts(ba_ref[d], bx_ref[d], lam_ref[d]) for d in range(2)]

    def fill_all(src_ref, d):
        for bi in range(nb):
            _lru_fill(bi, src_ref[bi], lambda gi, hf: wg_ref[d, gi, hf], consts[d], a_s, b_s)

    @pl.when(pl.program_id(0) == 0)
    def _():
        carry_s[...] = jnp.zeros_like(carry_s)
        fill_all(uc_ref, 1)
        _lru_scan(True, tc, nb, a_s, b_s, h_s, carry_s)
        hb0_ref[...] = carry_s[...]
        carry_s[...] = jnp.zeros_like(carry_s)
        fill_all(uc_ref, 0)
        _lru_scan(False, tc, nb, a_s, b_s, h_s, carry_s)

    fill_all(u_ref, 0)
    _lru_scan(False, tc, nb, a_s, b_s, h_s, carry_s)
    for bi in range(nb):
        hf_ref[bi] = _lru_rows(bi, tc, h_s).astype(hf_ref.dtype)


def _lru_fwd(u, u_c, wg, ba, bx, lam):
    b, s, lw = u.shape
    tc = LRU_CHUNK
    assert u_c.shape[1] == tc and s % tc == 0
    full = lambda *shape: pl.BlockSpec(shape, lambda i: (0,) * len(shape))
    cblk = pl.BlockSpec((b, tc, lw), lambda i: (0, i, 0))
    slab = pltpu.VMEM((LRU_SLABS, b * _lru_pitch(tc), LANES), F32)
    return pl.pallas_call(
        _lru_fwd_kernel,
        grid=(s // tc,),
        in_specs=[cblk, full(b, tc, lw), full(2, 2, 2, MXU_DIM, MXU_DIM), full(2, 1, lw), full(2, 1, lw), full(2, 1, lw)],
        out_specs=[cblk, full(b, lw)],
        out_shape=[jax.ShapeDtypeStruct((b, s, lw), BF16), jax.ShapeDtypeStruct((b, lw), F32)],
        scratch_shapes=[slab, slab, slab, pltpu.VMEM((b, lw), F32)],
        compiler_params=_params("arbitrary"),
        name="rglru_fwd",
    )(u, u_c, wg, ba, bx, lam)


def _lru_bwd_out_kernel(u_ref, hf_ref, zl_ref, agt_ref, x_ref, hb0_ref, gate_ref, wg_ref, ba_ref, bx_ref, lam_ref,
                        wo_ref, o_ref, a_s, b_s, h_s, carry_s):
    nb, tc = u_ref.shape[0], u_ref.shape[1]
    consts = _lru_gate_consts(ba_ref[...], bx_ref[...], lam_ref[...])

    @pl.when(pl.program_id(0) == 0)
    def _():
        carry_s[...] = hb0_ref[...]

    for bi in range(nb):
        _lru_fill(bi, u_ref[bi], lambda gi, hf: wg_ref[gi, hf], consts, a_s, b_s)
    _lru_scan(True, tc, nb, a_s, b_s, h_s, carry_s)

    for g0 in range(0, nb, OUT_GROUP):
        group = range(g0, g0 + OUT_GROUP)
        y = jnp.concatenate([((hf_ref[bi].astype(F32) + _lru_rows(bi, tc, h_s)) * zl_ref[bi].astype(F32)).astype(BF16)
                             for bi in group], axis=0)
        at = jnp.concatenate([agt_ref[bi] for bi in group], axis=1)
        mix = lax.dot_general(at, wo_ref[:ATTN_WIDTH, :], (((0,), (0,)), ((), ())), preferred_element_type=F32)
        mix = mix + _dot(y, wo_ref[ATTN_WIDTH:, :])
        for k, bi in enumerate(group):
            o_ref[bi] = x_ref[bi] + gate_ref[bi] * mix[k * tc:(k + 1) * tc]


def _lru_bwd_out(u, hf, zl, agt, x, hb0, gate, wg, ba, bx, lam, wo_bf):
    b, s, lw = u.shape
    d = x.shape[2]
    tc = LRU_BWD_CHUNK
    n_chunks = s // tc
    assert s % tc == 0 and b % OUT_GROUP == 0
    full = lambda *shape: pl.BlockSpec(shape, lambda i: (0,) * len(shape))
    rev = lambda i: n_chunks - 1 - i
    cblk = pl.BlockSpec((b, tc, lw), lambda i: (0, rev(i), 0))
    xblk = pl.BlockSpec((b, tc, d), lambda i: (0, rev(i), 0))
    slab = pltpu.VMEM((LRU_SLABS, b * _lru_pitch(tc), LANES), F32)
    return pl.pallas_call(
        _lru_bwd_out_kernel,
        grid=(n_chunks,),
        in_specs=[cblk, cblk, cblk,
                  pl.BlockSpec((b, ATTN_WIDTH, tc), lambda i: (0, 0, rev(i))),
                  xblk, full(b, lw), full(b, 1, d),
                  full(2, 2, MXU_DIM, MXU_DIM), full(1, lw), full(1, lw), full(1, lw),
                  full(ATTN_WIDTH + lw, d)],
        out_specs=xblk,
        out_shape=jax.ShapeDtypeStruct(x.shape, x.dtype),
        scratch_shapes=[slab, slab, slab, pltpu.VMEM((b, lw), F32)],
        compiler_params=_params("arbitrary"),
        name="rglru_bwd_outproj",
    )(u, hf, zl, agt, x, hb0, gate, wg, ba, bx, lam, wo_bf)


def _rope_table(s):
    half = HEAD_DIM // 4
    assert s // GRID_W <= GRID_W
    inv = ROPE_BASE ** (-jnp.arange(half, dtype=F32) / half)
    ang = jnp.arange(GRID_W, dtype=F32)[:, None] * inv[None, :]
    reps = LANES // (2 * half)
    cos = jnp.tile(jnp.cos(ang), (1, 2 * reps))
    sin = jnp.tile(jnp.concatenate([-jnp.sin(ang), jnp.sin(ang)], axis=-1), (1, reps))
    return jnp.stack([cos, sin])


def _block_diag_gates(w):
    n = MXU_DIM // LRU_BLOCK
    lead = w.shape[:-3]
    w = w.reshape(lead + (2, n, LRU_BLOCK, LRU_BLOCK))
    on_diag = jnp.eye(n, dtype=bool)[:, None, :, None]
    blocks = jnp.where(on_diag, w[..., :, :, None, :], jnp.zeros((), w.dtype))
    return blocks.reshape(lead + (2, MXU_DIM, MXU_DIM))


def kernel(x, c, ctx, c_ctx, norm_g, w_mod, b_mod, w_in, w_out, q_norm_g, k_norm_g, rpb, conv_w, conv_b,
           lru_wa, lru_ba, lru_wx, lru_bx, lru_lam):
    bsz, s, d = x.shape
    assert w_in.shape[0] == 1 and d == D_MODEL and s % (GRID_W * Q_ROWS) == 0
    aw, lw = ATTN_WIDTH, LRU_WIDTH

    pad_rows = 2 * SUBLANES - bsz - 1
    cc = jnp.concatenate([c, c_ctx[None, :], jnp.zeros((pad_rows, d), F32)], axis=0)
    mod = _modulation(cc, w_mod[0], b_mod[0][None, :])
    shift, scale, gate = [mod[:bsz, i * d:(i + 1) * d][:, None, :] for i in range(3)]
    shift_c, scale_c = [mod[bsz:bsz + 1, i * d:(i + 1) * d][:, None, :] for i in range(2)]

    w = w_in[0]
    w_bf = w.astype(BF16)
    wt_bf = _transpose_cols(w_bf, COL_V * aw, (COL_ZA + 1) * aw)
    w_ctx_bf, wt_ctx_bf = w_bf, wt_bf
    g = norm_g[0][None, :]
    rope_tab = _rope_table(s)
    blk = jnp.arange(MXU_DIM) // HEAD_DIM
    bd = jnp.where(blk[:, None] == blk[None, :], 1.0 / HEAD_DIM, 0.0).astype(BF16)
    qg = jnp.tile(q_norm_g[0] * (HEAD_DIM ** -0.5 * LOG2E), N_HEADS)[None, :]
    kg = jnp.tile(k_norm_g[0], N_HEADS)[None, :]

    cw, cb = conv_w[0], conv_b[0][None, :]
    qr, qp, kr, vt, zat, u, zl = _inproj(x, g, scale, shift, w_bf, wt_bf, rope_tab, bd, qg, kg, cw, cb)
    kc, vct, u_c = _ctxproj(ctx, g, scale_c, shift_c, w_ctx_bf, wt_ctx_bf, bd, kg, cw, cb)

    agt = _attention(qr, qp, kr, vt, kc, vct, zat, _bias_rows(rpb[0]))

    wg = (0.5 * jnp.stack([lru_wa[0], lru_wx[0]], axis=1)).astype(BF16)
    wg = _block_diag_gates(wg)
    ba, bx, lam = lru_ba[0][:, None, :], lru_bx[0][:, None, :], lru_lam[0][:, None, :]
    hf, hb0 = _lru_fwd(u, u_c, wg, ba, bx, lam)
    return _lru_bwd_out(u, hf, zl, agt, x, hb0, gate, wg[1], ba[1], bx[1], lam[1], w_out[0].astype(BF16))
```

```python
import functools
import itertools
import math

import jax
import jax.numpy as jnp
from jax import lax
from jax.experimental import pallas as pl
from jax.experimental.pallas import tpu as pltpu

F32 = jnp.float32
BF16 = jnp.bfloat16

D_MODEL = 1024
GRID_W = 64
HEAD_DIM = 64
ATTN_WIDTH = 512
LRU_WIDTH = 512
N_HEADS = ATTN_WIDTH // HEAD_DIM
LRU_BLOCK = 64
NA_ROWS = 8
NA_COLS = 16
CONV_WIDTH = 4
LRU_C = 8.0
ROPE_BASE = 10000.0
EPS = 1e-6
NEG_INF = -1e30
RSQRT_FLOOR = 1e-30
LOG2E = math.log2(math.e)

LANES = 128
SUBLANES = 8
MXU_DIM = 256
VMEM_LIMIT_BYTES = 58 * 1024 * 1024

Q_ROWS = 4
BAND_ROWS = Q_ROWS + NA_ROWS
Q_BLK = Q_ROWS * GRID_W
BAND = BAND_ROWS * GRID_W
KEY_CHUNK = 256
ONES_ROWS = 16
N_BIAS_ROWS = 2 * NA_ROWS - 1
N_BIAS_COLS = 2 * NA_COLS - 1
TOE_W = 1024

LRU_CHUNK = 256
LRU_BWD_CHUNK = 128
LRU_SLABS = LRU_WIDTH // LANES
OUT_GROUP = 8
CONV_ROWS = 128
COL_Q, COL_K, COL_V, COL_ZA, COL_U, COL_ZL = range(6)
HALO = 16


def _dot(a, b):
    return jnp.dot(a, b, preferred_element_type=F32)


def _dot_nt(a, b):
    return lax.dot_general(a, b, (((1,), (1,)), ((), ())), preferred_element_type=F32)


def _params(*semantics):
    return pltpu.CompilerParams(dimension_semantics=semantics, vmem_limit_bytes=VMEM_LIMIT_BYTES)


def _split_bf16(t):
    hi = t.astype(BF16)
    lo = (t - hi.astype(F32)).astype(BF16)
    return hi, lo


def _mod_kernel(cc_ref, w_ref, b_ref, o_ref):
    cc = cc_ref[...]
    s = cc * jax.nn.sigmoid(cc)
    s_hi, s_lo = _split_bf16(s)
    w_hi, w_lo = _split_bf16(w_ref[...])
    o_ref[...] = _dot(s_hi, w_hi) + _dot(s_lo, w_hi) + _dot(s_hi, w_lo) + b_ref[...]


def _modulation(cc, w_mod, b_mod):
    rows, d = cc.shape
    n = w_mod.shape[1]
    tn = 1024
    return pl.pallas_call(
        _mod_kernel,
        grid=(n // tn,),
        in_specs=[pl.BlockSpec((rows, d), lambda j: (0, 0)),
                  pl.BlockSpec((d, tn), lambda j: (0, j)),
                  pl.BlockSpec((1, tn), lambda j: (0, j))],
        out_specs=pl.BlockSpec((rows, tn), lambda j: (0, j)),
        out_shape=jax.ShapeDtypeStruct((rows, n), F32),
        compiler_params=_params("arbitrary"),
        name="modulation",
    )(cc, w_mod, b_mod)


def _fill_bias_tables(rexp_ref, h, o_ref):
    k = lax.broadcasted_iota(jnp.int32, (GRID_W, TOE_W), 0)
    q = lax.broadcasted_iota(jnp.int32, (GRID_W, TOE_W), 1) % GRID_W
    diff = k - q + (NA_COLS - 1)
    toe = jnp.zeros((GRID_W, TOE_W), F32)
    for d in range(N_BIAS_COLS):
        toe = jnp.where(diff == d, rexp_ref[h, d:d + 1, :], toe)
    col_start = jnp.clip(q - NA_COLS // 2, 0, GRID_W - NA_COLS)
    col_ok = (k >= col_start) & (k < col_start + NA_COLS)
    toe = jnp.where(col_ok, toe, NEG_INF)
    ri = lax.broadcasted_iota(jnp.int32, (GRID_W, Q_BLK), 1) // GRID_W
    masked = jnp.full((GRID_W, Q_BLK), NEG_INF, F32)

    for typ in range(3):
        off = (NA_ROWS - 1, NA_ROWS // 2 - 1, NA_ROWS - BAND_ROWS + Q_ROWS - 1)[typ]
        for jj in range(BAND_ROWS):
            if typ == 0:
                lo, hi = (0, Q_ROWS) if jj < NA_ROWS else (0, 0)
            elif typ == 1:
                lo, hi = max(jj - NA_ROWS + 1, 0), min(jj, Q_ROWS - 1) + 1
            else:
                lo, hi = (0, Q_ROWS) if jj >= BAND_ROWS - NA_ROWS else (0, 0)
            strip = masked
            if lo < hi:
                e0 = N_BIAS_ROWS - 1 - jj - off
                rolled = pltpu.roll(toe, (-e0 * GRID_W) % TOE_W, axis=1)[:, :Q_BLK]
                strip = jnp.where((ri >= lo) & (ri < hi), rolled, NEG_INF)
            o_ref[h, typ, jj * GRID_W:(jj + 1) * GRID_W, :] = strip


def _bias_rows(rpb):
    rexp = jnp.repeat(jnp.transpose(rpb[:, ::-1, :], (0, 2, 1)) * LOG2E, GRID_W, axis=2)
    return jnp.pad(rexp, ((0, 0), (0, 0), (0, TOE_W - N_BIAS_ROWS * GRID_W)))


def _transpose_kernel(w_ref, o_ref):
    o_ref[...] = w_ref[...].T


def _transpose_cols(w, lo, hi, tn=MXU_DIM):
    rows = w.shape[0]
    assert lo % tn == 0 and hi % tn == 0
    return pl.pallas_call(
        _transpose_kernel,
        grid=((hi - lo) // tn,),
        in_specs=[pl.BlockSpec((rows, tn), lambda i: (0, lo // tn + i))],
        out_specs=pl.BlockSpec((tn, rows), lambda i: (i, 0)),
        out_shape=jax.ShapeDtypeStruct((hi - lo, rows), w.dtype),
        compiler_params=_params("arbitrary"),
        name="weight_transpose",
    )(w)


def _adaln(x, g_ref, scale_ref, shift_ref):
    ms = jnp.mean(x * x, axis=-1, keepdims=True)
    gm = g_ref[...] * (1.0 + scale_ref[0])
    return ((x * lax.rsqrt(ms + EPS)) * gm + shift_ref[0]).astype(BF16)


def _head_rms(t, bd_ref, gain):
    sq = (t * t).astype(BF16)
    bd = bd_ref[...]
    m = jnp.concatenate([_dot(sq[:, :MXU_DIM], bd), _dot(sq[:, MXU_DIM:], bd)], axis=-1)
    return t * lax.rsqrt(m + EPS) * gain


def _rope_tile(rows_ref, cols_ref, f):
    lane = lax.broadcasted_iota(jnp.int32, (1, LANES), 1)
    by_row = (lane % HEAD_DIM) < HEAD_DIM // 2
    by_col = cols_ref[f]
    return jnp.concatenate([jnp.where(by_row, rows_ref[f, k:k + 1, :], by_col) for k in range(rows_ref.shape[1])],
                           axis=0)


def _rope(t, cos, sin):
    lane = lax.broadcasted_iota(jnp.int32, (1, LANES), 1)
    first = (lane % 32) < 16
    parts = []
    for j in range(t.shape[1] // LANES):
        c = t[:, j * LANES:(j + 1) * LANES]
        swapped = jnp.where(first, pltpu.roll(c, LANES - 16, axis=1), pltpu.roll(c, 16, axis=1))
        parts.append(c * cos + swapped * sin)
    return jnp.concatenate(parts, axis=-1)


def _silu(t):
    return t * jax.nn.sigmoid(t)


def _conv_window(ext, n, cw_ref, cb_ref):
    rows = ext.shape[0]

    def shifted(k):
        return pltpu.roll(ext, (rows - k) % rows, axis=0)[HALO:HALO + n]

    y = cb_ref[...] + cw_ref[0:1, :] * shifted(-1)
    y = y + cw_ref[1:2, :] * ext[HALO:HALO + n]
    y = y + cw_ref[2:3, :] * shifted(1)
    return y + cw_ref[3:4, :] * shifted(2)


def _inproj_kernel(x_ref, xp_ref, xn_ref, g_ref, scale_ref, shift_ref, w_ref, wt_ref, rope_rows_ref, rope_cols_ref,
                   bd_ref, qg_ref, kg_ref, cw_ref, cb_ref, qr_ref, qp_ref, kr_ref, vt_ref, zat_ref, uc_ref, zl_ref):
    tm = x_ref.shape[1]
    aw = ATTN_WIDTH
    i, n_tiles = pl.program_id(0), pl.num_programs(0)
    hb_ext = jnp.concatenate([_adaln(xp_ref[0], g_ref, scale_ref, shift_ref),
                              _adaln(x_ref[0], g_ref, scale_ref, shift_ref),
                              _adaln(xn_ref[0], g_ref, scale_ref, shift_ref)], axis=0)
    hb = hb_ext[HALO:HALO + tm]

    def cols(j):
        return w_ref[:, j * aw:(j + 1) * aw]

    u_ext = _dot(hb_ext, cols(COL_U))
    row = lax.broadcasted_iota(jnp.int32, (tm + 2 * HALO, 1), 0)
    inside = ((row >= HALO) | (i > 0)) & ((row < HALO + tm) | (i < n_tiles - 1))
    u_ext = jnp.where(inside, u_ext, 0.0)

    cos = _rope_tile(rope_rows_ref, rope_cols_ref, 0)
    sin = _rope_tile(rope_rows_ref, rope_cols_ref, 1)
    qn = _head_rms(_dot(hb, cols(COL_Q)), bd_ref, qg_ref[...])
    qp_ref[0] = qn.astype(BF16)
    qr_ref[0] = _rope(qn, cos, sin).astype(BF16)
    kn = _head_rms(_dot(hb, cols(COL_K)), bd_ref, kg_ref[...])
    kr_ref[0] = _rope(kn, cos, sin).astype(BF16)

    zl_ref[0] = _silu(_dot(hb, cols(COL_ZL))).astype(BF16)
    vt_ref[0] = _dot_nt(wt_ref[:aw, :], hb).astype(BF16)
    zat_ref[0] = _silu(_dot_nt(wt_ref[aw:, :], hb)).astype(BF16)
    for r in range(0, tm, CONV_ROWS):
        uc_ref[0, r:r + CONV_ROWS, :] = _conv_window(u_ext[r:r + CONV_ROWS + 2 * HALO], CONV_ROWS, cw_ref, cb_ref)


def _inproj(x, g, scale, shift, w_bf, wt_bf, rope_tab, bd, qg, kg, conv_w, conv_b, tm=1024):
    b, s, d = x.shape
    aw = ATTN_WIDTH
    tok = lambda i, j: (j, i, 0)
    per_b = lambda i, j: (j, 0, 0)
    const = lambda i, j: (0, 0)
    halo_per_tile = tm // HALO
    n_halo = s // HALO
    out_bf = jax.ShapeDtypeStruct((b, s, aw), BF16)
    out_t = jax.ShapeDtypeStruct((b, aw, s), BF16)
    out_f32 = jax.ShapeDtypeStruct((b, s, aw), F32)
    blk = pl.BlockSpec((1, tm, aw), tok)
    blk_t = pl.BlockSpec((1, aw, tm), lambda i, j: (j, 0, i))
    return pl.pallas_call(
        _inproj_kernel,
        grid=(s // tm, b),
        in_specs=[pl.BlockSpec((1, tm, d), tok),
                  pl.BlockSpec((1, HALO, d), lambda i, j: (j, jnp.maximum(i * halo_per_tile - 1, 0), 0)),
                  pl.BlockSpec((1, HALO, d), lambda i, j: (j, jnp.minimum((i + 1) * halo_per_tile, n_halo - 1), 0)),
                  pl.BlockSpec((1, d), const),
                  pl.BlockSpec((1, 1, d), per_b),
                  pl.BlockSpec((1, 1, d), per_b),
                  pl.BlockSpec(w_bf.shape, const),
                  pl.BlockSpec(wt_bf.shape, const),
                  pl.BlockSpec((2, tm // GRID_W, LANES), lambda i, j: (0, i, 0)),
                  pl.BlockSpec((2, GRID_W, LANES), lambda i, j: (0, 0, 0)),
                  pl.BlockSpec((MXU_DIM, MXU_DIM), const),
                  pl.BlockSpec((1, aw), const),
                  pl.BlockSpec((1, aw), const),
                  pl.BlockSpec((CONV_WIDTH, aw), const),
                  pl.BlockSpec((1, aw), const)],
        out_specs=[blk, blk, blk, blk_t, blk_t, blk, blk],
        out_shape=[out_bf, out_bf, out_bf, out_t, out_t, out_f32, out_bf],
        compiler_params=_params("arbitrary", "arbitrary"),
        name="inproj",
    )(x, x, x, g, scale, shift, w_bf, wt_bf, rope_tab, rope_tab, bd, qg, kg, conv_w, conv_b)


def _ctxproj_kernel(c_ref, g_ref, scale_ref, shift_ref, wk_ref, wu_ref, wvt_ref, bd_ref, kg_ref, cw_ref, cb_ref,
                    kc_ref, vct_ref, uc_ref):
    nb, l, d = c_ref.shape
    aw = ATTN_WIDTH
    hb = _adaln(c_ref[...].reshape(nb * l, d), g_ref, scale_ref, shift_ref)
    kc = _head_rms(_dot(hb, wk_ref[...]), bd_ref, kg_ref[...]).astype(BF16)
    u = _dot(hb, wu_ref[...])
    vct = _dot_nt(wvt_ref[...], hb).astype(BF16)
    pad = jnp.zeros((HALO, aw), F32)
    for bi in range(nb):
        kc_ref[bi] = kc[bi * l:(bi + 1) * l]
        uc_ref[bi] = _conv_window(jnp.concatenate([pad, u[bi * l:(bi + 1) * l], pad], axis=0), l, cw_ref, cb_ref)
        vct_ref[bi] = vct[:, bi * l:(bi + 1) * l]


def _ctxproj(ctx, g, scale_c, shift_c, w_ctx_bf, wt_ctx_bf, bd, kg, conv_w, conv_b, nb=4):
    b, l, d = ctx.shape
    aw = ATTN_WIDTH
    const = lambda i: (0, 0)
    blk = pl.BlockSpec((nb, l, aw), lambda i: (i, 0, 0))
    return pl.pallas_call(
        _ctxproj_kernel,
        grid=(b // nb,),
        in_specs=[pl.BlockSpec((nb, l, d), lambda i: (i, 0, 0)),
                  pl.BlockSpec((1, d), const),
                  pl.BlockSpec((1, 1, d), lambda i: (0, 0, 0)),
                  pl.BlockSpec((1, 1, d), lambda i: (0, 0, 0)),
                  pl.BlockSpec((d, aw), lambda i: (0, COL_K)),
                  pl.BlockSpec((d, aw), lambda i: (0, COL_U)),
                  pl.BlockSpec((aw, d), const),
                  pl.BlockSpec((MXU_DIM, MXU_DIM), const),
                  pl.BlockSpec((1, aw), const),
                  pl.BlockSpec((CONV_WIDTH, aw), const),
                  pl.BlockSpec((1, aw), const)],
        out_specs=[blk, pl.BlockSpec((nb, aw, l), lambda i: (i, 0, 0)), blk],
        out_shape=[jax.ShapeDtypeStruct((b, l, aw), BF16),
                   jax.ShapeDtypeStruct((b, aw, l), BF16),
                   jax.ShapeDtypeStruct((b, l, aw), F32)],
        compiler_params=_params("arbitrary"),
        name="ctxproj",
    )(ctx, g, scale_c, shift_c, w_ctx_bf, w_ctx_bf, wt_ctx_bf, bd, kg, conv_w, conv_b)


def _attn_kernel(qr_ref, qp_ref, kr_ref, vt_ref, kc_ref, vct_ref, zat_ref, rexp_ref, o_ref, bias_ref, s_buf0, s_buf1):
    @pl.when(pl.program_id(1) == 0)
    def _():
        for h in range(2):
            _fill_bias_tables(rexp_ref, h, bias_ref)

    rows = qr_ref.shape[1] // GRID_W
    n_blk = rows // Q_ROWS
    chunk_rows = KEY_CHUNK // GRID_W
    n_tiles = Q_BLK // LANES
    lane_head = lax.broadcasted_iota(jnp.int32, (1, LANES), 1) // HEAD_DIM
    s_bufs = (s_buf0, s_buf1)
    ones_rows = jnp.ones((ONES_ROWS, KEY_CHUNK), BF16)

    def in_window(typ, jj, tile):
        if typ == 0:
            return jj < NA_ROWS
        if typ == 2:
            return jj >= BAND_ROWS - NA_ROWS
        return any(0 <= jj - ri < NA_ROWS for ri in range(tile * Q_ROWS // n_tiles, (tile + 1) * Q_ROWS // n_tiles))

    def local_chunks(typ):
        return [c for c in range(BAND // KEY_CHUNK)
                if any(in_window(typ, c * chunk_rows + r, t) for r in range(chunk_rows) for t in range(n_tiles))]

    def sub_blocks(c):
        for r in range(chunk_rows):
            for t in range(n_tiles):
                yield c * chunk_rows + r, t, slice(r * GRID_W, (r + 1) * GRID_W), slice(t * LANES, (t + 1) * LANES)

    def item(rb):
        rb = jnp.asarray(rb, jnp.int32)
        q0 = pl.multiple_of(rb * Q_BLK, Q_BLK)
        band0 = pl.multiple_of(jnp.clip(rb * Q_ROWS - NA_ROWS // 2, 0, rows - BAND_ROWS) * GRID_W, Q_BLK)
        return q0, band0

    def col_max(m, blk, t):
        m = list(m)
        m[t] = jnp.maximum(m[t], jnp.max(blk.reshape(-1, SUBLANES, LANES), axis=0))
        return tuple(m)

    def score_steps(rb, h, typ):
        q0, band0 = item(rb)
        s_buf = s_bufs[h]
        mine = lane_head == h
        qr = jnp.where(mine, qr_ref[0, pl.ds(q0, Q_BLK), :], jnp.zeros((), BF16))
        qp = jnp.where(mine, qp_ref[0, pl.ds(q0, Q_BLK), :], jnp.zeros((), BF16))

        def local(c, m):
            n_rows = 1 + max(r for r in range(chunk_rows)
                             if any(in_window(typ, c * chunk_rows + r, t) for t in range(n_tiles)))
            s = _dot_nt(kr_ref[0, pl.ds(band0 + c * KEY_CHUNK, n_rows * GRID_W), :], qr)
            for jj, t, r_sl, l_sl in sub_blocks(c):
                if in_window(typ, jj, t):
                    k_sl = slice(jj * GRID_W, (jj + 1) * GRID_W)
                    blk = s[r_sl, l_sl] + bias_ref[h, typ, k_sl, l_sl]
                    s_buf[k_sl, l_sl] = blk
                    m = col_max(m, blk, t)
            return m

        def context(m):
            s = _dot_nt(kc_ref[0], qp)
            s_buf[pl.ds(BAND, KEY_CHUNK), :] = s
            for t in range(n_tiles):
                m = col_max(m, s[:, t * LANES:(t + 1) * LANES], t)
            return m

        return [context] + [functools.partial(local, c) for c in local_chunks(typ)]

    def value_steps(rb, h, typ, m):
        q0, band0 = item(rb)
        s_buf = s_bufs[h]
        hd = pl.ds(h * HEAD_DIM, HEAD_DIM)
        m = tuple(jnp.max(mt, axis=0, keepdims=True) for mt in m)

        def probs(k_sl, t):
            return jnp.exp2(s_buf[k_sl, t * LANES:(t + 1) * LANES] - m[t]).astype(BF16)

        def local(c, acc):
            zero_blk = jnp.zeros((GRID_W, LANES), BF16)
            row_blks = []
            for r in range(chunk_rows):
                jj = c * chunk_rows + r
                k_sl = slice(jj * GRID_W, (jj + 1) * GRID_W)
                row_blks.append(jnp.concatenate(
                    [probs(k_sl, t) if in_window(typ, jj, t) else zero_blk for t in range(n_tiles)], axis=1))
            p = jnp.concatenate(row_blks, axis=0)
            vt = jnp.concatenate([vt_ref[0, hd, pl.ds(band0 + c * KEY_CHUNK, KEY_CHUNK)], ones_rows], axis=0)
            return acc + _dot(vt, p)

        def context(acc):
            k_sl = slice(BAND, BAND + KEY_CHUNK)
            p = jnp.concatenate([probs(k_sl, t) for t in range(n_tiles)], axis=1)
            vt = jnp.concatenate([vct_ref[0, hd, :], ones_rows], axis=0)
            return acc + _dot(vt, p)

        return [context] + [functools.partial(local, c) for c in local_chunks(typ)]

    def write_out(rb, h, acc):
        q0, _ = item(rb)
        hd = pl.ds(h * HEAD_DIM, HEAD_DIM)
        inv_den = 1.0 / acc[HEAD_DIM:HEAD_DIM + 1]
        gated = (acc[:HEAD_DIM] * inv_den) * zat_ref[0, hd, pl.ds(q0, Q_BLK)].astype(F32)
        o_ref[0, hd, pl.ds(q0, Q_BLK)] = gated.astype(o_ref.dtype)

    m_init = (jnp.full((SUBLANES, LANES), -jnp.inf, F32),) * n_tiles
    acc_init = jnp.zeros((HEAD_DIM + ONES_ROWS, Q_BLK), F32)

    def overlapped(score_item, value_item, out_item, carry):
        m_prev, acc_prev = carry
        s_steps = score_steps(*score_item) if score_item else []
        v_steps = value_steps(*value_item, m_prev) if value_item else []
        m, acc = m_init, acc_init
        for i, (s_step, v_step) in enumerate(itertools.zip_longest(s_steps, v_steps)):
            if s_step:
                m = s_step(m)
            if i == 0 and out_item:
                write_out(*out_item, acc_prev)
            if v_step:
                acc = v_step(acc)
        return m, acc

    first, mid, last = 0, 1, 2
    carry = overlapped((0, 0, first), None, None, (None, None))
    carry = overlapped((0, 1, first), (0, 0, first), None, carry)
    carry = overlapped((1, 0, mid), (0, 1, first), (0, 0), carry)

    def body(rb, carry):
        carry = overlapped((rb, 1, mid), (rb, 0, mid), (rb - 1, 1), carry)
        return overlapped((rb + 1, 0, mid), (rb, 1, mid), (rb, 0), carry)

    carry = lax.fori_loop(1, n_blk - 2, body, carry, unroll=4)
    carry = overlapped((n_blk - 2, 1, mid), (n_blk - 2, 0, mid), (n_blk - 3, 1), carry)
    carry = overlapped((n_blk - 1, 0, last), (n_blk - 2, 1, mid), (n_blk - 2, 0), carry)
    carry = overlapped((n_blk - 1, 1, last), (n_blk - 1, 0, last), (n_blk - 2, 1), carry)
    carry = overlapped(None, (n_blk - 1, 1, last), (n_blk - 1, 0), carry)
    write_out(n_blk - 1, 1, carry[1])


def _attention(qr, qp, kr, vt, kc, vct, zat, rexp):
    b, s, aw = qr.shape
    l = kc.shape[1]
    n_pairs = aw // LANES
    lat = pl.BlockSpec((1, s, LANES), lambda p, i: (i, 0, p))
    lat_t = pl.BlockSpec((1, LANES, s), lambda p, i: (i, p, 0))
    return pl.pallas_call(
        _attn_kernel,
        grid=(n_pairs, b),
        in_specs=[lat, lat, lat, lat_t,
                  pl.BlockSpec((1, l, LANES), lambda p, i: (i, 0, p)),
                  pl.BlockSpec((1, LANES, l), lambda p, i: (i, p, 0)),
                  lat_t,
                  pl.BlockSpec((2, N_BIAS_COLS, TOE_W), lambda p, i: (p, 0, 0))],
        out_specs=lat_t,
        out_shape=jax.ShapeDtypeStruct((b, aw, s), BF16),
        scratch_shapes=[pltpu.VMEM((2, 3, BAND, Q_BLK), F32),
                        pltpu.VMEM((BAND + l, Q_BLK), F32), pltpu.VMEM((BAND + l, Q_BLK), F32)],
        compiler_params=_params("arbitrary", "arbitrary"),
        name="attention",
    )(qr, qp, kr, vt, kc, vct, zat, rexp)


def _lru_pitch(tc):
    assert tc % SUBLANES == 0
    return tc + SUBLANES // 2


def _lru_gate_consts(ba, bx, lam):
    return 0.5 * ba, 0.5 * bx, jax.nn.softplus(-lam) * (-0.5 * LRU_C * LOG2E)


def _lru_fill(bi, uc, wg, consts, a_s, b_s):
    half_ba, half_bx, half_log2a = consts
    tc = uc.shape[0]
    ucb = uc.astype(BF16)

    def gate_tanh(gi, half_bias):
        pre = jnp.concatenate([_dot(ucb[:, :MXU_DIM], wg(gi, 0)), _dot(ucb[:, MXU_DIM:], wg(gi, 1))], axis=-1)
        return jnp.tanh(pre + half_bias)

    tr = gate_tanh(0, half_ba)
    ti = gate_tanh(1, half_bx)
    a = jnp.exp2(tr * half_log2a + half_log2a)
    gap = 1.0 - a * a
    mult = gap * lax.rsqrt(jnp.maximum(gap, RSQRT_FLOOR))
    bb = mult * ((0.5 * ti + 0.5) * uc)
    r0 = bi * _lru_pitch(tc)
    for j in range(LRU_SLABS):
        a_s[j, r0:r0 + tc, :] = a[:, j * LANES:(j + 1) * LANES]
        b_s[j, r0:r0 + tc, :] = bb[:, j * LANES:(j + 1) * LANES]


def _lru_scan(reverse, tc, nb, a_s, b_s, h_s, carry_s):
    pitch = _lru_pitch(tc)

    n_groups = tc // SUBLANES

    def body(g, hs):
        base = pl.multiple_of(((n_groups - 1 - g) if reverse else g) * SUBLANES, SUBLANES)
        hs = list(hs)
        for k in (reversed(range(SUBLANES)) if reverse else range(SUBLANES)):
            rows = pl.ds(base + k, nb, stride=pitch)
            for j in range(LRU_SLABS):
                hs[j] = a_s[j, rows, :] * hs[j] + b_s[j, rows, :]
                h_s[j, rows, :] = hs[j]
        return tuple(hs)

    hs = tuple(carry_s[:, j * LANES:(j + 1) * LANES] for j in range(LRU_SLABS))
    hs = lax.fori_loop(0, n_groups, body, hs)
    for j in range(LRU_SLABS):
        carry_s[:, j * LANES:(j + 1) * LANES] = hs[j]


def _lru_rows(bi, tc, h_s):
    r0 = bi * _lru_pitch(tc)
    return jnp.concatenate([h_s[j, r0:r0 + tc, :] for j in range(LRU_SLABS)], axis=-1)


def _lru_fwd_kernel(u_ref, uc_ref, wg_ref, ba_ref, bx_ref, lam_ref, hf_ref, hb0_ref, a_s, b_s, h_s, carry_s):
    nb, tc = u_ref.shape[0], u_ref.shape[1]
    consts = [_lru_gate_consts(ba_ref[d], bx_ref[d], lam_ref[d]) for d in range(2)]

    def fill_all(src_ref, d):
        for bi in range(nb):
            _lru_fill(bi, src_ref[bi], lambda gi, hf: wg_ref[d, gi, hf], consts[d], a_s, b_s)

    @pl.when(pl.program_id(0) == 0)
    def _():
        carry_s[...] = jnp.zeros_like(carry_s)
        fill_all(uc_ref, 1)
        _lru_scan(True, tc, nb, a_s, b_s, h_s, carry_s)
        hb0_ref[...] = carry_s[...]
        carry_s[...] = jnp.zeros_like(carry_s)
        fill_all(uc_ref, 0)
        _lru_scan(False, tc, nb, a_s, b_s, h_s, carry_s)

    fill_all(u_ref, 0)
    _lru_scan(False, tc, nb, a_s, b_s, h_s, carry_s)
    for bi in range(nb):
        hf_ref[bi] = _lru_rows(bi, tc, h_s).astype(hf_ref.dtype)


def _lru_fwd(u, u_c, wg, ba, bx, lam):
    b, s, lw = u.shape
    tc = LRU_CHUNK
    assert u_c.shape[1] == tc and s % tc == 0
    full = lambda *shape: pl.BlockSpec(shape, lambda i: (0,) * len(shape))
    cblk = pl.BlockSpec((b, tc, lw), lambda i: (0, i, 0))
    slab = pltpu.VMEM((LRU_SLABS, b * _lru_pitch(tc), LANES), F32)
    return pl.pallas_call(
        _lru_fwd_kernel,
        grid=(s // tc,),
        in_specs=[cblk, full(b, tc, lw), full(2, 2, 2, MXU_DIM, MXU_DIM), full(2, 1, lw), full(2, 1, lw), full(2, 1, lw)],
        out_specs=[cblk, full(b, lw)],
        out_shape=[jax.ShapeDtypeStruct((b, s, lw), BF16), jax.ShapeDtypeStruct((b, lw), F32)],
        scratch_shapes=[slab, slab, slab, pltpu.VMEM((b, lw), F32)],
        compiler_params=_params("arbitrary"),
        name="rglru_fwd",
    )(u, u_c, wg, ba, bx, lam)


def _lru_bwd_out_kernel(u_ref, hf_ref, zl_ref, agt_ref, x_ref, hb0_ref, gate_ref, wg_ref, ba_ref, bx_ref, lam_ref,
                        wo_ref, o_ref, a_s, b_s, h_s, carry_s):
    nb, tc = u_ref.shape[0], u_ref.shape[1]
    consts = _lru_gate_consts(ba_ref[...], bx_ref[...], lam_ref[...])

    @pl.when(pl.program_id(0) == 0)
    def _():
        carry_s[...] = hb0_ref[...]

    for bi in range(nb):
        _lru_fill(bi, u_ref[bi], lambda gi, hf: wg_ref[gi, hf], consts, a_s, b_s)
    _lru_scan(True, tc, nb, a_s, b_s, h_s, carry_s)

    for g0 in range(0, nb, OUT_GROUP):
        group = range(g0, g0 + OUT_GROUP)
        y = jnp.concatenate([((hf_ref[bi].astype(F32) + _lru_rows(bi, tc, h_s)) * zl_ref[bi].astype(F32)).astype(BF16)
                             for bi in group], axis=0)
        at = jnp.concatenate([agt_ref[bi] for bi in group], axis=1)
        mix = lax.dot_general(at, wo_ref[:ATTN_WIDTH, :], (((0,), (0,)), ((), ())), preferred_element_type=F32)
        mix = mix + _dot(y, wo_ref[ATTN_WIDTH:, :])
        for k, bi in enumerate(group):
            o_ref[bi] = x_ref[bi] + gate_ref[bi] * mix[k * tc:(k + 1) * tc]


def _lru_bwd_out(u, hf, zl, agt, x, hb0, gate, wg, ba, bx, lam, wo_bf):
    b, s, lw = u.shape
    d = x.shape[2]
    tc = LRU_BWD_CHUNK
    n_chunks = s // tc
    assert s % tc == 0 and b % OUT_GROUP == 0
    full = lambda *shape: pl.BlockSpec(shape, lambda i: (0,) * len(shape))
    rev = lambda i: n_chunks - 1 - i
    cblk = pl.BlockSpec((b, tc, lw), lambda i: (0, rev(i), 0))
    xblk = pl.BlockSpec((b, tc, d), lambda i: (0, rev(i), 0))
    slab = pltpu.VMEM((LRU_SLABS, b * _lru_pitch(tc), LANES), F32)
    return pl.pallas_call(
        _lru_bwd_out_kernel,
        grid=(n_chunks,),
        in_specs=[cblk, cblk, cblk,
                  pl.BlockSpec((b, ATTN_WIDTH, tc), lambda i: (0, 0, rev(i))),
                  xblk, full(b, lw), full(b, 1, d),
                  full(2, 2, MXU_DIM, MXU_DIM), full(1, lw), full(1, lw), full(1, lw),
                  full(ATTN_WIDTH + lw, d)],
        out_specs=xblk,
        out_shape=jax.ShapeDtypeStruct(x.shape, x.dtype),
        scratch_shapes=[slab, slab, slab, pltpu.VMEM((b, lw), F32)],
        compiler_params=_params("arbitrary"),
        name="rglru_bwd_outproj",
    )(u, hf, zl, agt, x, hb0, gate, wg, ba, bx, lam, wo_bf)


def _rope_table(s):
    half = HEAD_DIM // 4
    assert s // GRID_W <= GRID_W
    inv = ROPE_BASE ** (-jnp.arange(half, dtype=F32) / half)
    ang = jnp.arange(GRID_W, dtype=F32)[:, None] * inv[None, :]
    reps = LANES // (2 * half)
    cos = jnp.tile(jnp.cos(ang), (1, 2 * reps))
    sin = jnp.tile(jnp.concatenate([-jnp.sin(ang), jnp.sin(ang)], axis=-1), (1, reps))
    return jnp.stack([cos, sin])


def _block_diag_gates(w):
    n = MXU_DIM // LRU_BLOCK
    lead = w.shape[:-3]
    w = w.reshape(lead + (2, n, LRU_BLOCK, LRU_BLOCK))
    on_diag = jnp.eye(n, dtype=bool)[:, None, :, None]
    blocks = jnp.where(on_diag, w[..., :, :, None, :], jnp.zeros((), w.dtype))
    return blocks.reshape(lead + (2, MXU_DIM, MXU_DIM))


def kernel(x, c, ctx, c_ctx, norm_g, w_mod, b_mod, w_in, w_out, q_norm_g, k_norm_g, rpb, conv_w, conv_b,
           lru_wa, lru_ba, lru_wx, lru_bx, lru_lam):
    bsz, s, d = x.shape
    assert w_in.shape[0] == 1 and d == D_MODEL and s % (GRID_W * Q_ROWS) == 0
    aw, lw = ATTN_WIDTH, LRU_WIDTH

    pad_rows = 2 * SUBLANES - bsz - 1
    cc = jnp.concatenate([c, c_ctx[None, :], jnp.zeros((pad_rows, d), F32)], axis=0)
    mod = _modulation(cc, w_mod[0], b_mod[0][None, :])
    shift, scale, gate = [mod[:bsz, i * d:(i + 1) * d][:, None, :] for i in range(3)]
    shift_c, scale_c = [mod[bsz:bsz + 1, i * d:(i + 1) * d][:, None, :] for i in range(2)]

    w = w_in[0]
    w_bf = w.astype(BF16)
    wt_bf = _transpose_cols(w_bf, COL_V * aw, (COL_ZA + 1) * aw)
    w_ctx_bf, wt_ctx_bf = w_bf, wt_bf
    g = norm_g[0][None, :]
    rope_tab = _rope_table(s)
    blk = jnp.arange(MXU_DIM) // HEAD_DIM
    bd = jnp.where(blk[:, None] == blk[None, :], 1.0 / HEAD_DIM, 0.0).astype(BF16)
    qg = jnp.tile(q_norm_g[0] * (HEAD_DIM ** -0.5 * LOG2E), N_HEADS)[None, :]
    kg = jnp.tile(k_norm_g[0], N_HEADS)[None, :]

    cw, cb = conv_w[0], conv_b[0][None, :]
    qr, qp, kr, vt, zat, u, zl = _inproj(x, g, scale, shift, w_bf, wt_bf, rope_tab, bd, qg, kg, cw, cb)
    kc, vct, u_c = _ctxproj(ctx, g, scale_c, shift_c, w_ctx_bf, wt_ctx_bf, bd, kg, cw, cb)

    agt = _attention(qr, qp, kr, vt, kc, vct, zat, _bias_rows(rpb[0]))

    wg = (0.5 * jnp.stack([lru_wa[0], lru_wx[0]], axis=1)).astype(BF16)
    wg = _block_diag_gates(wg)
    ba, bx, lam = lru_ba[0][:, None, :], lru_bx[0][:, None, :], lru_lam[0][:, None, :]
    hf, hb0 = _lru_fwd(u, u_c, wg, ba, bx, lam)
    return _lru_bwd_out(u, hf, zl, agt, x, hb0, gate, wg[1], ba[1], bx[1], lam[1], w_out[0].astype(BF16))
```

```python
import functools
import itertools
import math

import jax
import jax.numpy as jnp
from jax import lax
from jax.experimental import pallas as pl
from jax.experimental.pallas import tpu as pltpu

F32 = jnp.float32
BF16 = jnp.bfloat16

D_MODEL = 1024
GRID_W = 64
HEAD_DIM = 64
ATTN_WIDTH = 512
LRU_WIDTH = 512
N_HEADS = ATTN_WIDTH // HEAD_DIM
LRU_BLOCK = 64
NA_ROWS = 8
NA_COLS = 16
CONV_WIDTH = 4
LRU_C = 8.0
ROPE_BASE = 10000.0
EPS = 1e-6
NEG_INF = -1e30
RSQRT_FLOOR = 1e-30
LOG2E = math.log2(math.e)

LANES = 128
SUBLANES = 8
MXU_DIM = 256
VMEM_LIMIT_BYTES = 58 * 1024 * 1024

Q_ROWS = 4
BAND_ROWS = Q_ROWS + NA_ROWS
Q_BLK = Q_ROWS * GRID_W
BAND = BAND_ROWS * GRID_W
KEY_CHUNK = 256
ONES_ROWS = 16
N_BIAS_ROWS = 2 * NA_ROWS - 1
N_BIAS_COLS = 2 * NA_COLS - 1
TOE_W = 1024

LRU_CHUNK = 256
LRU_BWD_CHUNK = 128
LRU_SLABS = LRU_WIDTH // LANES
OUT_GROUP = 8
CONV_ROWS = 128
COL_Q, COL_K, COL_V, COL_ZA, COL_U, COL_ZL = range(6)
HALO = 16


def _dot(a, b):
    return jnp.dot(a, b, preferred_element_type=F32)


def _dot_nt(a, b):
    return lax.dot_general(a, b, (((1,), (1,)), ((), ())), preferred_element_type=F32)


def _params(*semantics):
    return pltpu.CompilerParams(dimension_semantics=semantics, vmem_limit_bytes=VMEM_LIMIT_BYTES)


def _split_bf16(t):
    hi = t.astype(BF16)
    lo = (t - hi.astype(F32)).astype(BF16)
    return hi, lo


def _mod_kernel(cc_ref, w_ref, b_ref, o_ref):
    cc = cc_ref[...]
    s = cc * jax.nn.sigmoid(cc)
    s_hi, s_lo = _split_bf16(s)
    w_hi, w_lo = _split_bf16(w_ref[...])
    o_ref[...] = _dot(s_hi, w_hi) + _dot(s_lo, w_hi) + _dot(s_hi, w_lo) + b_ref[...]


def _modulation(cc, w_mod, b_mod):
    rows, d = cc.shape
    n = w_mod.shape[1]
    tn = 1024
    return pl.pallas_call(
        _mod_kernel,
        grid=(n // tn,),
        in_specs=[pl.BlockSpec((rows, d), lambda j: (0, 0)),
                  pl.BlockSpec((d, tn), lambda j: (0, j)),
                  pl.BlockSpec((1, tn), lambda j: (0, j))],
        out_specs=pl.BlockSpec((rows, tn), lambda j: (0, j)),
        out_shape=jax.ShapeDtypeStruct((rows, n), F32),
        compiler_params=_params("arbitrary"),
        name="modulation",
    )(cc, w_mod, b_mod)


def _fill_bias_tables(rexp_ref, h, o_ref):
    k = lax.broadcasted_iota(jnp.int32, (GRID_W, TOE_W), 0)
    q = lax.broadcasted_iota(jnp.int32, (GRID_W, TOE_W), 1) % GRID_W
    diff = k - q + (NA_COLS - 1)
    toe = jnp.zeros((GRID_W, TOE_W), F32)
    for d in range(N_BIAS_COLS):
        toe = jnp.where(diff == d, rexp_ref[h, d:d + 1, :], toe)
    col_start = jnp.clip(q - NA_COLS // 2, 0, GRID_W - NA_COLS)
    col_ok = (k >= col_start) & (k < col_start + NA_COLS)
    toe = jnp.where(col_ok, toe, NEG_INF)
    ri = lax.broadcasted_iota(jnp.int32, (GRID_W, Q_BLK), 1) // GRID_W
    masked = jnp.full((GRID_W, Q_BLK), NEG_INF, F32)

    for typ in range(3):
        off = (NA_ROWS - 1, NA_ROWS // 2 - 1, NA_ROWS - BAND_ROWS + Q_ROWS - 1)[typ]
        for jj in range(BAND_ROWS):
            if typ == 0:
                lo, hi = (0, Q_ROWS) if jj < NA_ROWS else (0, 0)
            elif typ == 1:
                lo, hi = max(jj - NA_ROWS + 1, 0), min(jj, Q_ROWS - 1) + 1
            else:
                lo, hi = (0, Q_ROWS) if jj >= BAND_ROWS - NA_ROWS else (0, 0)
            strip = masked
            if lo < hi:
                e0 = N_BIAS_ROWS - 1 - jj - off
                rolled = pltpu.roll(toe, (-e0 * GRID_W) % TOE_W, axis=1)[:, :Q_BLK]
                strip = jnp.where((ri >= lo) & (ri < hi), rolled, NEG_INF)
            o_ref[h, typ, jj * GRID_W:(jj + 1) * GRID_W, :] = strip


def _bias_rows(rpb):
    rexp = jnp.repeat(jnp.transpose(rpb[:, ::-1, :], (0, 2, 1)) * LOG2E, GRID_W, axis=2)
    return jnp.pad(rexp, ((0, 0), (0, 0), (0, TOE_W - N_BIAS_ROWS * GRID_W)))


def _transpose_kernel(w_ref, o_ref):
    o_ref[...] = w_ref[...].T


def _transpose_cols(w, lo, hi, tn=MXU_DIM):
    rows = w.shape[0]
    assert lo % tn == 0 and hi % tn == 0
    return pl.pallas_call(
        _transpose_kernel,
        grid=((hi - lo) // tn,),
        in_specs=[pl.BlockSpec((rows, tn), lambda i: (0, lo // tn + i))],
        out_specs=pl.BlockSpec((tn, rows), lambda i: (i, 0)),
        out_shape=jax.ShapeDtypeStruct((hi - lo, rows), w.dtype),
        compiler_params=_params("arbitrary"),
        name="weight_transpose",
    )(w)


def _adaln(x, g_ref, scale_ref, shift_ref):
    ms = jnp.mean(x * x, axis=-1, keepdims=True)
    gm = g_ref[...] * (1.0 + scale_ref[0])
    return ((x * lax.rsqrt(ms + EPS)) * gm + shift_ref[0]).astype(BF16)


def _head_rms(t, bd_ref, gain):
    sq = (t * t).astype(BF16)
    bd = bd_ref[...]
    m = jnp.concatenate([_dot(sq[:, :MXU_DIM], bd), _dot(sq[:, MXU_DIM:], bd)], axis=-1)
    return t * lax.rsqrt(m + EPS) * gain


def _rope_tile(rows_ref, cols_ref, f):
    lane = lax.broadcasted_iota(jnp.int32, (1, LANES), 1)
    by_row = (lane % HEAD_DIM) < HEAD_DIM // 2
    by_col = cols_ref[f]
    return jnp.concatenate([jnp.where(by_row, rows_ref[f, k:k + 1, :], by_col) for k in range(rows_ref.shape[1])],
                           axis=0)


def _rope(t, cos, sin):
    lane = lax.broadcasted_iota(jnp.int32, (1, LANES), 1)
    first = (lane % 32) < 16
    parts = []
    for j in range(t.shape[1] // LANES):
        c = t[:, j * LANES:(j + 1) * LANES]
        swapped = jnp.where(first, pltpu.roll(c, LANES - 16, axis=1), pltpu.roll(c, 16, axis=1))
        parts.append(c * cos + swapped * sin)
    return jnp.concatenate(parts, axis=-1)


def _silu(t):
    return t * jax.nn.sigmoid(t)


def _conv_window(ext, n, cw_ref, cb_ref):
    rows = ext.shape[0]

    def shifted(k):
        return pltpu.roll(ext, (rows - k) % rows, axis=0)[HALO:HALO + n]

    y = cb_ref[...] + cw_ref[0:1, :] * shifted(-1)
    y = y + cw_ref[1:2, :] * ext[HALO:HALO + n]
    y = y + cw_ref[2:3, :] * shifted(1)
    return y + cw_ref[3:4, :] * shifted(2)


def _inproj_kernel(x_ref, xp_ref, xn_ref, g_ref, scale_ref, shift_ref, w_ref, wt_ref, rope_rows_ref, rope_cols_ref,
                   bd_ref, qg_ref, kg_ref, cw_ref, cb_ref, qr_ref, qp_ref, kr_ref, vt_ref, zat_ref, uc_ref, zl_ref):
    tm = x_ref.shape[1]
    aw = ATTN_WIDTH
    i, n_tiles = pl.program_id(0), pl.num_programs(0)
    hb_ext = jnp.concatenate([_adaln(xp_ref[0], g_ref, scale_ref, shift_ref),
                              _adaln(x_ref[0], g_ref, scale_ref, shift_ref),
                              _adaln(xn_ref[0], g_ref, scale_ref, shift_ref)], axis=0)
    hb = hb_ext[HALO:HALO + tm]

    def cols(j):
        return w_ref[:, j * aw:(j + 1) * aw]

    u_ext = _dot(hb_ext, cols(COL_U))
    row = lax.broadcasted_iota(jnp.int32, (tm + 2 * HALO, 1), 0)
    inside = ((row >= HALO) | (i > 0)) & ((row < HALO + tm) | (i < n_tiles - 1))
    u_ext = jnp.where(inside, u_ext, 0.0)

    cos = _rope_tile(rope_rows_ref, rope_cols_ref, 0)
    sin = _rope_tile(rope_rows_ref, rope_cols_ref, 1)
    qn = _head_rms(_dot(hb, cols(COL_Q)), bd_ref, qg_ref[...])
    qp_ref[0] = qn.astype(BF16)
    qr_ref[0] = _rope(qn, cos, sin).astype(BF16)
    kn = _head_rms(_dot(hb, cols(COL_K)), bd_ref, kg_ref[...])
    kr_ref[0] = _rope(kn, cos, sin).astype(BF16)

    zl_ref[0] = _silu(_dot(hb, cols(COL_ZL))).astype(BF16)
    vt_ref[0] = _dot_nt(wt_ref[:aw, :], hb).astype(BF16)
    zat_ref[0] = _silu(_dot_nt(wt_ref[aw:, :], hb)).astype(BF16)
    for r in range(0, tm, CONV_ROWS):
        uc_ref[0, r:r + CONV_ROWS, :] = _conv_window(u_ext[r:r + CONV_ROWS + 2 * HALO], CONV_ROWS, cw_ref, cb_ref)


def _inproj(x, g, scale, shift, w_bf, wt_bf, rope_tab, bd, qg, kg, conv_w, conv_b, tm=1024):
    b, s, d = x.shape
    aw = ATTN_WIDTH
    tok = lambda i, j: (j, i, 0)
    per_b = lambda i, j: (j, 0, 0)
    const = lambda i, j: (0, 0)
    halo_per_tile = tm // HALO
    n_halo = s // HALO
    out_bf = jax.ShapeDtypeStruct((b, s, aw), BF16)
    out_t = jax.ShapeDtypeStruct((b, aw, s), BF16)
    out_f32 = jax.ShapeDtypeStruct((b, s, aw), F32)
    blk = pl.BlockSpec((1, tm, aw), tok)
    blk_t = pl.BlockSpec((1, aw, tm), lambda i, j: (j, 0, i))
    return pl.pallas_call(
        _inproj_kernel,
        grid=(s // tm, b),
        in_specs=[pl.BlockSpec((1, tm, d), tok),
                  pl.BlockSpec((1, HALO, d), lambda i, j: (j, jnp.maximum(i * halo_per_tile - 1, 0), 0)),
                  pl.BlockSpec((1, HALO, d), lambda i, j: (j, jnp.minimum((i + 1) * halo_per_tile, n_halo - 1), 0)),
                  pl.BlockSpec((1, d), const),
                  pl.BlockSpec((1, 1, d), per_b),
                  pl.BlockSpec((1, 1, d), per_b),
                  pl.BlockSpec(w_bf.shape, const),
                  pl.BlockSpec(wt_bf.shape, const),
                  pl.BlockSpec((2, tm // GRID_W, LANES), lambda i, j: (0, i, 0)),
                  pl.BlockSpec((2, GRID_W, LANES), lambda i, j: (0, 0, 0)),
                  pl.BlockSpec((MXU_DIM, MXU_DIM), const),
                  pl.BlockSpec((1, aw), const),
                  pl.BlockSpec((1, aw), const),
                  pl.BlockSpec((CONV_WIDTH, aw), const),
                  pl.BlockSpec((1, aw), const)],
        out_specs=[blk, blk, blk, blk_t, blk_t, blk, blk],
        out_shape=[out_bf, out_bf, out_bf, out_t, out_t, out_f32, out_bf],
        compiler_params=_params("arbitrary", "arbitrary"),
        name="inproj",
    )(x, x, x, g, scale, shift, w_bf, wt_bf, rope_tab, rope_tab, bd, qg, kg, conv_w, conv_b)


def _ctxproj_kernel(c_ref, g_ref, scale_ref, shift_ref, wk_ref, wu_ref, wvt_ref, bd_ref, kg_ref, cw_ref, cb_ref,
                    kc_ref, vct_ref, uc_ref):
    nb, l, d = c_ref.shape
    aw = ATTN_WIDTH
    hb = _adaln(c_ref[...].reshape(nb * l, d), g_ref, scale_ref, shift_ref)
    kc = _head_rms(_dot(hb, wk_ref[...]), bd_ref, kg_ref[...]).astype(BF16)
    u = _dot(hb, wu_ref[...])
    vct = _dot_nt(wvt_ref[...], hb).astype(BF16)
    pad = jnp.zeros((HALO, aw), F32)
    for bi in range(nb):
        kc_ref[bi] = kc[bi * l:(bi + 1) * l]
        uc_ref[bi] = _conv_window(jnp.concatenate([pad, u[bi * l:(bi + 1) * l], pad], axis=0), l, cw_ref, cb_ref)
        vct_ref[bi] = vct[:, bi * l:(bi + 1) * l]


def _ctxproj(ctx, g, scale_c, shift_c, w_ctx_bf, wt_ctx_bf, bd, kg, conv_w, conv_b, nb=4):
    b, l, d = ctx.shape
    aw = ATTN_WIDTH
    const = lambda i: (0, 0)
    blk = pl.BlockSpec((nb, l, aw), lambda i: (i, 0, 0))
    return pl.pallas_call(
        _ctxproj_kernel,
        grid=(b // nb,),
        in_specs=[pl.BlockSpec((nb, l, d), lambda i: (i, 0, 0)),
                  pl.BlockSpec((1, d), const),
                  pl.BlockSpec((1, 1, d), lambda i: (0, 0, 0)),
                  pl.BlockSpec((1, 1, d), lambda i: (0, 0, 0)),
                  pl.BlockSpec((d, aw), lambda i: (0, COL_K)),
                  pl.BlockSpec((d, aw), lambda i: (0, COL_U)),
                  pl.BlockSpec((aw, d), const),
                  pl.BlockSpec((MXU_DIM, MXU_DIM), const),
                  pl.BlockSpec((1, aw), const),
                  pl.BlockSpec((CONV_WIDTH, aw), const),
                  pl.BlockSpec((1, aw), const)],
        out_specs=[blk, pl.BlockSpec((nb, aw, l), lambda i: (i, 0, 0)), blk],
        out_shape=[jax.ShapeDtypeStruct((b, l, aw), BF16),
                   jax.ShapeDtypeStruct((b, aw, l), BF16),
                   jax.ShapeDtypeStruct((b, l, aw), F32)],
        compiler_params=_params("arbitrary"),
        name="ctxproj",
    )(ctx, g, scale_c, shift_c, w_ctx_bf, w_ctx_bf, wt_ctx_bf, bd, kg, conv_w, conv_b)


def _attn_kernel(qr_ref, qp_ref, kr_ref, vt_ref, kc_ref, vct_ref, zat_ref, rexp_ref, o_ref, bias_ref, s_buf0, s_buf1):
    @pl.when(pl.program_id(1) == 0)
    def _():
        for h in range(2):
            _fill_bias_tables(rexp_ref, h, bias_ref)

    rows = qr_ref.shape[1] // GRID_W
    n_blk = rows // Q_ROWS
    chunk_rows = KEY_CHUNK // GRID_W
    n_tiles = Q_BLK // LANES
    lane_head = lax.broadcasted_iota(jnp.int32, (1, LANES), 1) // HEAD_DIM
    s_bufs = (s_buf0, s_buf1)
    ones_rows = jnp.ones((ONES_ROWS, KEY_CHUNK), BF16)

    def in_window(typ, jj, tile):
        if typ == 0:
            return jj < NA_ROWS
        if typ == 2:
            return jj >= BAND_ROWS - NA_ROWS
        return any(0 <= jj - ri < NA_ROWS for ri in range(tile * Q_ROWS // n_tiles, (tile + 1) * Q_ROWS // n_tiles))

    def local_chunks(typ):
        return [c for c in range(BAND // KEY_CHUNK)
                if any(in_window(typ, c * chunk_rows + r, t) for r in range(chunk_rows) for t in range(n_tiles))]

    def sub_blocks(c):
        for r in range(chunk_rows):
            for t in range(n_tiles):
                yield c * chunk_rows + r, t, slice(r * GRID_W, (r + 1) * GRID_W), slice(t * LANES, (t + 1) * LANES)

    def item(rb):
        rb = jnp.asarray(rb, jnp.int32)
        q0 = pl.multiple_of(rb * Q_BLK, Q_BLK)
        band0 = pl.multiple_of(jnp.clip(rb * Q_ROWS - NA_ROWS // 2, 0, rows - BAND_ROWS) * GRID_W, Q_BLK)
        return q0, band0

    def col_max(m, blk, t):
        m = list(m)
        m[t] = jnp.maximum(m[t], jnp.max(blk.reshape(-1, SUBLANES, LANES), axis=0))
        return tuple(m)

    def score_steps(rb, h, typ):
        q0, band0 = item(rb)
        s_buf = s_bufs[h]
        mine = lane_head == h
        qr = jnp.where(mine, qr_ref[0, pl.ds(q0, Q_BLK), :], jnp.zeros((), BF16))
        qp = jnp.where(mine, qp_ref[0, pl.ds(q0, Q_BLK), :], jnp.zeros((), BF16))

        def local(c, m):
            n_rows = 1 + max(r for r in range(chunk_rows)
                             if any(in_window(typ, c * chunk_rows + r, t) for t in range(n_tiles)))
            s = _dot_nt(kr_ref[0, pl.ds(band0 + c * KEY_CHUNK, n_rows * GRID_W), :], qr)
            for jj, t, r_sl, l_sl in sub_blocks(c):
                if in_window(typ, jj, t):
                    k_sl = slice(jj * GRID_W, (jj + 1) * GRID_W)
                    blk = s[r_sl, l_sl] + bias_ref[h, typ, k_sl, l_sl]
                    s_buf[k_sl, l_sl] = blk
                    m = col_max(m, blk, t)
            return m

        def context(m):
            s = _dot_nt(kc_ref[0], qp)
            s_buf[pl.ds(BAND, KEY_CHUNK), :] = s
            for t in range(n_tiles):
                m = col_max(m, s[:, t * LANES:(t + 1) * LANES], t)
            return m

        return [context] + [functools.partial(local, c) for c in local_chunks(typ)]

    def value_steps(rb, h, typ, m):
        q0, band0 = item(rb)
        s_buf = s_bufs[h]
        hd = pl.ds(h * HEAD_DIM, HEAD_DIM)
        m = tuple(jnp.max(mt, axis=0, keepdims=True) for mt in m)

        def probs(k_sl, t):
            return jnp.exp2(s_buf[k_sl, t * LANES:(t + 1) * LANES] - m[t]).astype(BF16)

        def local(c, acc):
            zero_blk = jnp.zeros((GRID_W, LANES), BF16)
            row_blks = []
            for r in range(chunk_rows):
                jj = c * chunk_rows + r
                k_sl = slice(jj * GRID_W, (jj + 1) * GRID_W)
                row_blks.append(jnp.concatenate(
                    [probs(k_sl, t) if in_window(typ, jj, t) else zero_blk for t in range(n_tiles)], axis=1))
            p = jnp.concatenate(row_blks, axis=0)
            vt = jnp.concatenate([vt_ref[0, hd, pl.ds(band0 + c * KEY_CHUNK, KEY_CHUNK)], ones_rows], axis=0)
            return acc + _dot(vt, p)

        def context(acc):
            k_sl = slice(BAND, BAND + KEY_CHUNK)
            p = jnp.concatenate([probs(k_sl, t) for t in range(n_tiles)], axis=1)
            vt = jnp.concatenate([vct_ref[0, hd, :], ones_rows], axis=0)
            return acc + _dot(vt, p)

        return [context] + [functools.partial(local, c) for c in local_chunks(typ)]

    def write_out(rb, h, acc):
        q0, _ = item(rb)
        hd = pl.ds(h * HEAD_DIM, HEAD_DIM)
        inv_den = 1.0 / acc[HEAD_DIM:HEAD_DIM + 1]
        gated = (acc[:HEAD_DIM] * inv_den) * zat_ref[0, hd, pl.ds(q0, Q_BLK)].astype(F32)
        o_ref[0, hd, pl.ds(q0, Q_BLK)] = gated.astype(o_ref.dtype)

    m_init = (jnp.full((SUBLANES, LANES), -jnp.inf, F32),) * n_tiles
    acc_init = jnp.zeros((HEAD_DIM + ONES_ROWS, Q_BLK), F32)

    def overlapped(score_item, value_item, out_item, carry):
        m_prev, acc_prev = carry
        s_steps = score_steps(*score_item) if score_item else []
        v_steps = value_steps(*value_item, m_prev) if value_item else []
        m, acc = m_init, acc_init
        for i, (s_step, v_step) in enumerate(itertools.zip_longest(s_steps, [None] + v_steps)):
            if s_step:
                m = s_step(m)
            if i == 0 and out_item:
                write_out(*out_item, acc_prev)
            if v_step:
                acc = v_step(acc)
        return m, acc

    first, mid, last = 0, 1, 2
    carry = overlapped((0, 0, first), None, None, (None, None))
    carry = overlapped((0, 1, first), (0, 0, first), None, carry)
    carry = overlapped((1, 0, mid), (0, 1, first), (0, 0), carry)

    def body(rb, carry):
        carry = overlapped((rb, 1, mid), (rb, 0, mid), (rb - 1, 1), carry)
        return overlapped((rb + 1, 0, mid), (rb, 1, mid), (rb, 0), carry)

    carry = lax.fori_loop(1, n_blk - 2, body, carry, unroll=4)
    carry = overlapped((n_blk - 2, 1, mid), (n_blk - 2, 0, mid), (n_blk - 3, 1), carry)
    carry = overlapped((n_blk - 1, 0, last), (n_blk - 2, 1, mid), (n_blk - 2, 0), carry)
    carry = overlapped((n_blk - 1, 1, last), (n_blk - 1, 0, last), (n_blk - 2, 1), carry)
    carry = overlapped(None, (n_blk - 1, 1, last), (n_blk - 1, 0), carry)
    write_out(n_blk - 1, 1, carry[1])


def _attention(qr, qp, kr, vt, kc, vct, zat, rexp):
    b, s, aw = qr.shape
    l = kc.shape[1]
    n_pairs = aw // LANES
    lat = pl.BlockSpec((1, s, LANES), lambda p, i: (i, 0, p))
    lat_t = pl.BlockSpec((1, LANES, s), lambda p, i: (i, p, 0))
    return pl.pallas_call(
        _attn_kernel,
        grid=(n_pairs, b),
        in_specs=[lat, lat, lat, lat_t,
                  pl.BlockSpec((1, l, LANES), lambda p, i: (i, 0, p)),
                  pl.BlockSpec((1, LANES, l), lambda p, i: (i, p, 0)),
                  lat_t,
                  pl.BlockSpec((2, N_BIAS_COLS, TOE_W), lambda p, i: (p, 0, 0))],
        out_specs=lat_t,
        out_shape=jax.ShapeDtypeStruct((b, aw, s), BF16),
        scratch_shapes=[pltpu.VMEM((2, 3, BAND, Q_BLK), F32),
                        pltpu.VMEM((BAND + l, Q_BLK), F32), pltpu.VMEM((BAND + l, Q_BLK), F32)],
        compiler_params=_params("arbitrary", "arbitrary"),
        name="attention",
    )(qr, qp, kr, vt, kc, vct, zat, rexp)


def _lru_pitch(tc):
    assert tc % SUBLANES == 0
    return tc + SUBLANES // 2


def _lru_gate_consts(ba, bx, lam):
    return 0.5 * ba, 0.5 * bx, jax.nn.softplus(-lam) * (-0.5 * LRU_C * LOG2E)


def _lru_fill(bi, uc, wg, consts, a_s, b_s):
    half_ba, half_bx, half_log2a = consts
    tc = uc.shape[0]
    ucb = uc.astype(BF16)

    def gate_tanh(gi, half_bias):
        pre = jnp.concatenate([_dot(ucb[:, :MXU_DIM], wg(gi, 0)), _dot(ucb[:, MXU_DIM:], wg(gi, 1))], axis=-1)
        return jnp.tanh(pre + half_bias)

    tr = gate_tanh(0, half_ba)
    ti = gate_tanh(1, half_bx)
    a = jnp.exp2(tr * half_log2a + half_log2a)
    gap = 1.0 - a * a
    mult = gap * lax.rsqrt(jnp.maximum(gap, RSQRT_FLOOR))
    bb = mult * ((0.5 * ti + 0.5) * uc)
    r0 = bi * _lru_pitch(tc)
    for j in range(LRU_SLABS):
        a_s[j, r0:r0 + tc, :] = a[:, j * LANES:(j + 1) * LANES]
        b_s[j, r0:r0 + tc, :] = bb[:, j * LANES:(j + 1) * LANES]


def _lru_scan(reverse, tc, nb, a_s, b_s, h_s, carry_s):
    pitch = _lru_pitch(tc)

    n_groups = tc // SUBLANES

    def body(g, hs):
        base = pl.multiple_of(((n_groups - 1 - g) if reverse else g) * SUBLANES, SUBLANES)
        hs = list(hs)
        for k in (reversed(range(SUBLANES)) if reverse else range(SUBLANES)):
            rows = pl.ds(base + k, nb, stride=pitch)
            for j in range(LRU_SLABS):
                hs[j] = a_s[j, rows, :] * hs[j] + b_s[j, rows, :]
                h_s[j, rows, :] = hs[j]
        return tuple(hs)

    hs = tuple(carry_s[:, j * LANES:(j + 1) * LANES] for j in range(LRU_SLABS))
    hs = lax.fori_loop(0, n_groups, body, hs)
    for j in range(LRU_SLABS):
        carry_s[:, j * LANES:(j + 1) * LANES] = hs[j]


def _lru_rows(bi, tc, h_s):
    r0 = bi * _lru_pitch(tc)
    return jnp.concatenate([h_s[j, r0:r0 + tc, :] for j in range(LRU_SLABS)], axis=-1)


def _lru_fwd_kernel(u_ref, uc_ref, wg_ref, ba_ref, bx_ref, lam_ref, hf_ref, hb0_ref, a_s, b_s, h_s, carry_s):
    nb, tc = u_ref.shape[0], u_ref.shape[1]
    consts = [_lru_gate_consts(ba_ref[d], bx_ref[d], lam_ref[d]) for d in range(2)]

    def fill_all(src_ref, d):
        for bi in range(nb):
            _lru_fill(bi, src_ref[bi], lambda gi, hf: wg_ref[d, gi, hf], consts[d], a_s, b_s)

    @pl.when(pl.program_id(0) == 0)
    def _():
        carry_s[...] = jnp.zeros_like(carry_s)
        fill_all(uc_ref, 1)
        _lru_scan(True, tc, nb, a_s, b_s, h_s, carry_s)
        hb0_ref[...] = carry_s[...]
        carry_s[...] = jnp.zeros_like(carry_s)
        fill_all(uc_ref, 0)
        _lru_scan(False, tc, nb, a_s, b_s, h_s, carry_s)

    fill_all(u_ref, 0)
    _lru_scan(False, tc, nb, a_s, b_s, h_s, carry_s)
    for bi in range(nb):
        hf_ref[bi] = _lru_rows(bi, tc, h_s).astype(hf_ref.dtype)


def _lru_fwd(u, u_c, wg, ba, bx, lam):
    b, s, lw = u.shape
    tc = LRU_CHUNK
    assert u_c.shape[1] == tc and s % tc == 0
    full = lambda *shape: pl.BlockSpec(shape, lambda i: (0,) * len(shape))
    cblk = pl.BlockSpec((b, tc, lw), lambda i: (0, i, 0))
    slab = pltpu.VMEM((LRU_SLABS, b * _lru_pitch(tc), LANES), F32)
    return pl.pallas_call(
        _lru_fwd_kernel,
        grid=(s // tc,),
        in_specs=[cblk, full(b, tc, lw), full(2, 2, 2, MXU_DIM, MXU_DIM), full(2, 1, lw), full(2, 1, lw), full(2, 1, lw)],
        out_specs=[cblk, full(b, lw)],
        out_shape=[jax.ShapeDtypeStruct((b, s, lw), BF16), jax.ShapeDtypeStruct((b, lw), F32)],
        scratch_shapes=[slab, slab, slab, pltpu.VMEM((b, lw), F32)],
        compiler_params=_params("arbitrary"),
        name="rglru_fwd",
    )(u, u_c, wg, ba, bx, lam)


def _lru_bwd_out_kernel(u_ref, hf_ref, zl_ref, agt_ref, x_ref, hb0_ref, gate_ref, wg_ref, ba_ref, bx_ref, lam_ref,
                        wo_ref, o_ref, a_s, b_s, h_s, carry_s):
    nb, tc = u_ref.shape[0], u_ref.shape[1]
    consts = _lru_gate_consts(ba_ref[...], bx_ref[...], lam_ref[...])

    @pl.when(pl.program_id(0) == 0)
    def _():
        carry_s[...] = hb0_ref[...]

    for bi in range(nb):
        _lru_fill(bi, u_ref[bi], lambda gi, hf: wg_ref[gi, hf], consts, a_s, b_s)
    _lru_scan(True, tc, nb, a_s, b_s, h_s, carry_s)

    for g0 in range(0, nb, OUT_GROUP):
        group = range(g0, g0 + OUT_GROUP)
        y = jnp.concatenate([((hf_ref[bi].astype(F32) + _lru_rows(bi, tc, h_s)) * zl_ref[bi].astype(F32)).astype(BF16)
                             for bi in group], axis=0)
        at = jnp.concatenate([agt_ref[bi] for bi in group], axis=1)
        mix = lax.dot_general(at, wo_ref[:ATTN_WIDTH, :], (((0,), (0,)), ((), ())), preferred_element_type=F32)
        mix = mix + _dot(y, wo_ref[ATTN_WIDTH:, :])
        for k, bi in enumerate(group):
            o_ref[bi] = x_ref[bi] + gate_ref[bi] * mix[k * tc:(k + 1) * tc]


def _lru_bwd_out(u, hf, zl, agt, x, hb0, gate, wg, ba, bx, lam, wo_bf):
    b, s, lw = u.shape
    d = x.shape[2]
    tc = LRU_BWD_CHUNK
    n_chunks = s // tc
    assert s % tc == 0 and b % OUT_GROUP == 0
    full = lambda *shape: pl.BlockSpec(shape, lambda i: (0,) * len(shape))
    rev = lambda i: n_chunks - 1 - i
    cblk = pl.BlockSpec((b, tc, lw), lambda i: (0, rev(i), 0))
    xblk = pl.BlockSpec((b, tc, d), lambda i: (0, rev(i), 0))
    slab = pltpu.VMEM((LRU_SLABS, b * _lru_pitch(tc), LANES), F32)
    return pl.pallas_call(
        _lru_bwd_out_kernel,
        grid=(n_chunks,),
        in_specs=[cblk, cblk, cblk,
                  pl.BlockSpec((b, ATTN_WIDTH, tc), lambda i: (0, 0, rev(i))),
                  xblk, full(b, lw), full(b, 1, d),
                  full(2, 2, MXU_DIM, MXU_DIM), full(1, lw), full(1, lw), full(1, lw),
                  full(ATTN_WIDTH + lw, d)],
        out_specs=xblk,
        out_shape=jax.ShapeDtypeStruct(x.shape, x.dtype),
        scratch_shapes=[slab, slab, slab, pltpu.VMEM((b, lw), F32)],
        compiler_params=_params("arbitrary"),
        name="rglru_bwd_outproj",
    )(u, hf, zl, agt, x, hb0, gate, wg, ba, bx, lam, wo_bf)


def _rope_table(s):
    half = HEAD_DIM // 4
    assert s // GRID_W <= GRID_W
    inv = ROPE_BASE ** (-jnp.arange(half, dtype=F32) / half)
    ang = jnp.arange(GRID_W, dtype=F32)[:, None] * inv[None, :]
    reps = LANES // (2 * half)
    cos = jnp.tile(jnp.cos(ang), (1, 2 * reps))
    sin = jnp.tile(jnp.concatenate([-jnp.sin(ang), jnp.sin(ang)], axis=-1), (1, reps))
    return jnp.stack([cos, sin])


def _block_diag_gates(w):
    n = MXU_DIM // LRU_BLOCK
    lead = w.shape[:-3]
    w = w.reshape(lead + (2, n, LRU_BLOCK, LRU_BLOCK))
    on_diag = jnp.eye(n, dtype=bool)[:, None, :, None]
    blocks = jnp.where(on_diag, w[..., :, :, None, :], jnp.zeros((), w.dtype))
    return blocks.reshape(lead + (2, MXU_DIM, MXU_DIM))


def kernel(x, c, ctx, c_ctx, norm_g, w_mod, b_mod, w_in, w_out, q_norm_g, k_norm_g, rpb, conv_w, conv_b,
           lru_wa, lru_ba, lru_wx, lru_bx, lru_lam):
    bsz, s, d = x.shape
    assert w_in.shape[0] == 1 and d == D_MODEL and s % (GRID_W * Q_ROWS) == 0
    aw, lw = ATTN_WIDTH, LRU_WIDTH

    pad_rows = 2 * SUBLANES - bsz - 1
    cc = jnp.concatenate([c, c_ctx[None, :], jnp.zeros((pad_rows, d), F32)], axis=0)
    mod = _modulation(cc, w_mod[0], b_mod[0][None, :])
    shift, scale, gate = [mod[:bsz, i * d:(i + 1) * d][:, None, :] for i in range(3)]
    shift_c, scale_c = [mod[bsz:bsz + 1, i * d:(i + 1) * d][:, None, :] for i in range(2)]

    w = w_in[0]
    w_bf = w.astype(BF16)
    wt_bf = _transpose_cols(w_bf, COL_V * aw, (COL_ZA + 1) * aw)
    w_ctx_bf, wt_ctx_bf = w_bf, wt_bf
    g = norm_g[0][None, :]
    rope_tab = _rope_table(s)
    blk = jnp.arange(MXU_DIM) // HEAD_DIM
    bd = jnp.where(blk[:, None] == blk[None, :], 1.0 / HEAD_DIM, 0.0).astype(BF16)
    qg = jnp.tile(q_norm_g[0] * (HEAD_DIM ** -0.5 * LOG2E), N_HEADS)[None, :]
    kg = jnp.tile(k_norm_g[0], N_HEADS)[None, :]

    cw, cb = conv_w[0], conv_b[0][None, :]
    qr, qp, kr, vt, zat, u, zl = _inproj(x, g, scale, shift, w_bf, wt_bf, rope_tab, bd, qg, kg, cw, cb)
    kc, vct, u_c = _ctxproj(ctx, g, scale_c, shift_c, w_ctx_bf, wt_ctx_bf, bd, kg, cw, cb)

    agt = _attention(qr, qp, kr, vt, kc, vct, zat, _bias_rows(rpb[0]))

    wg = (0.5 * jnp.stack([lru_wa[0], lru_wx[0]], axis=1)).astype(BF16)
    wg = _block_diag_gates(wg)
    ba, bx, lam = lru_ba[0][:, None, :], lru_bx[0][:, None, :], lru_lam[0][:, None, :]
    hf, hb0 = _lru_fwd(u, u_c, wg, ba, bx, lam)
    return _lru_bwd_out(u, hf, zl, agt, x, hb0, gate, wg[1], ba[1], bx[1], lam[1], w_out[0].astype(BF16))
```
